```python
import math
import jax, jax.numpy as jnp
from jax import lax
import numpy as np

D_MODEL = 2048
BATCH = 4
SEQ = 2048
DEPTH = 1
DEC_BATCH = 128
DEC_SEQ = 8
PAST_LEN = 16384
PAGE_SIZE = 128

M_W = D_MODEL // 2
M_H = 4
M_DK = M_W // M_H
M_DV = M_W // M_H
CONV_K = 4
M_CHUNK = 64
HG_W = D_MODEL // 2
HG_DK = 128
HG_H = HG_W // HG_DK
HG_DV = HG_W // HG_H
HG_CHUNK = 64
MEM_LEN = 256
XA_H = 4
XA_D = D_MODEL // XA_H
N_GROUPS = 4
EXP_PER_GROUP = 8
N_EXPERTS = N_GROUPS * EXP_PER_GROUP
TOP_K = 2
EXP_FF = D_MODEL // 4
N_IN = 4 * M_W + 2 * M_H + 4 * HG_W + 2 * D_MODEL
ALPHA = (2 * DEPTH) ** 0.25
BETA = (8 * DEPTH) ** -0.25
LN_EPS = 1e-5
F32 = jnp.float32

kernel_name = "mlstm_hgrn2_gated_merge_memxattn_hiermoe_step"


def layer_norm(x, g, b):
    xf = x.astype(F32)
    mu = xf.mean(-1, keepdims=True)
    var = jnp.mean(jnp.square(xf - mu), -1, keepdims=True)
    return ((xf - mu) * lax.rsqrt(var + LN_EPS) * g.astype(F32) + b.astype(F32)).astype(x.dtype)


def head_norm(h, gain, center):
    hf = h.astype(F32)
    if center:
        hf = hf - hf.mean(-1, keepdims=True)
    return hf * lax.rsqrt(jnp.mean(jnp.square(hf), -1, keepdims=True) + LN_EPS) * gain.astype(F32)


def to_chunks(a, L):
    B, T = a.shape[:2]
    return jnp.moveaxis(a.reshape(B, T // L, L, *a.shape[2:]), 1, 0)


def from_chunks(a):
    nc, B, L = a.shape[:3]
    return jnp.moveaxis(a, 0, 1).reshape(B, nc * L, *a.shape[3:])


def causal_conv(u, buf, w):
    T = u.shape[1]
    full = jnp.concatenate([buf.astype(u.dtype), u], axis=1)
    out = sum(full[:, j:j + T] * w[j] for j in range(CONV_K))
    return out, full[:, T:]


def mlstm_recurrence(q, k, v, ig, lf, C0, n0, m0):
    T = q.shape[1]
    L = math.gcd(T, M_CHUNK)
    causal = jnp.tril(jnp.ones((L, L), dtype=bool))[None, :, :, None]

    def step(carry, inp):
        C, n, m = carry
        qc, kc, vc, ic, fc = inp
        F = jnp.cumsum(fc, axis=1)
        Dm = F[:, :, None, :] - F[:, None, :, :] + ic[:, None, :, :]
        Dm = jnp.where(causal, Dm, -jnp.inf)
        init_w = F + m[:, None, :]
        m_t = jnp.maximum(init_w, Dm.max(axis=2))
        P = jnp.exp(Dm - m_t[:, :, None, :])
        a0 = jnp.exp(init_w - m_t)
        Sc = jnp.einsum('bthd,bshd->btsh', qc, kc) * P
        num = jnp.einsum('btsh,bshv->bthv', Sc, vc) + a0[..., None] * jnp.einsum('bthd,bhdv->bthv', qc, C)
        den = Sc.sum(axis=2) + a0 * jnp.einsum('bthd,bhd->bth', qc, n)
        h = num / jnp.maximum(jnp.abs(den), jnp.exp(-m_t))[..., None]
        mL = m_t[:, -1]
        wL = jnp.exp(F[:, -1:, :] - F + ic - mL[:, None, :])
        decay = jnp.exp(F[:, -1] + m - mL)
        C_new = decay[:, :, None, None] * C + jnp.einsum('bshd,bshv->bhdv', wL[..., None] * kc, vc)
        n_new = decay[..., None] * n + jnp.einsum('bsh,bshd->bhd', wL, kc)
        return (C_new, n_new, mL), h

    xs = tuple(to_chunks(a, L) for a in (q, k, v, ig, lf))
    (C, n, m), h = lax.scan(step, (C0, n0, m0), xs)
    return from_chunks(h), C, n, m


def hgrn2_recurrence(q, k, lf, v, S0):
    T = q.shape[1]
    L = math.gcd(T, HG_CHUNK)
    causal = jnp.tril(jnp.ones((L, L), dtype=bool))[None, :, :, None, None]

    def step(S, inp):
        qc, kc, fc, vc = inp
        b = jnp.cumsum(fc, axis=1)
        rel = jnp.exp(jnp.where(causal, b[:, :, None] - b[:, None], -jnp.inf))
        A = jnp.einsum('bthc,btshc->btsh', qc, rel * kc[:, None])
        o = jnp.einsum('btsh,bshv->bthv', A, vc) + jnp.einsum('bthc,bhcv->bthv', qc * jnp.exp(b), S)
        bL = b[:, -1]
        S_new = jnp.exp(bL)[..., None] * S + jnp.einsum('bshc,bshv->bhcv', kc * jnp.exp(bL[:, None] - b), vc)
        return S_new, o

    xs = tuple(to_chunks(a, L) for a in (q, k, lf, v))
    S, o = lax.scan(step, S0, xs)
    return from_chunks(o), S


def token_mixer(x, C0, n0, m0, conv0, S0, lb, w_in, b_in, conv_w, mlstm_gn, hgrn_gn, w_bm, w_bh, w_out):
    B, T, _ = x.shape
    u = x @ w_in + b_in
    sizes = (2 * M_W, M_W, M_W, M_H, M_H, HG_W, HG_W, HG_W, HG_W, D_MODEL, D_MODEL)
    offs = np.cumsum(sizes)[:-1].tolist()
    qk_m, v_m, o_m, i_m, f_m, q_h, f_h, i_h, g_h, gate_m, gate_h = jnp.split(u, offs, axis=-1)

    qk_c, conv_new = causal_conv(qk_m, conv0, conv_w)
    q_m, k_m = jnp.split(jax.nn.silu(qk_c).astype(F32), 2, axis=-1)
    q_m = q_m.reshape(B, T, M_H, M_DK) * (M_DK ** -0.5)
    k_m = k_m.reshape(B, T, M_H, M_DK)
    v_m = v_m.astype(F32).reshape(B, T, M_H, M_DV)
    h_m, C, n, m = mlstm_recurrence(q_m, k_m, v_m, i_m.astype(F32), jax.nn.log_sigmoid(f_m.astype(F32)),
                                    C0.astype(F32), n0.astype(F32), m0.astype(F32))
    h_m = jax.nn.sigmoid(o_m.astype(F32)).reshape(B, T, M_H, M_DV) * h_m
    h_m = head_norm(h_m, mlstm_gn.reshape(M_H, M_DV), True).reshape(B, T, M_W).astype(x.dtype)

    f = lb + (1.0 - lb) * jax.nn.sigmoid(f_h.astype(F32))
    q_g = jax.nn.silu(q_h.astype(F32)).reshape(B, T, HG_H, HG_DK)
    k_g = (1.0 - f).reshape(B, T, HG_H, HG_DK)
    lf_g = jnp.log(f).reshape(B, T, HG_H, HG_DK)
    i_g = i_h.astype(F32).reshape(B, T, HG_H, HG_DV)
    o_g, S = hgrn2_recurrence(q_g, k_g, lf_g, i_g, S0.astype(F32))
    o_g = head_norm(o_g, hgrn_gn.reshape(HG_H, HG_DV), False).reshape(B, T, HG_W) * jax.nn.silu(g_h.astype(F32))
    o_g = o_g.astype(x.dtype)

    merged = jax.nn.sigmoid(gate_m) * (h_m @ w_bm) + jax.nn.sigmoid(gate_h) * (o_g @ w_bh)
    return merged @ w_out, (C, n, m, conv_new, S)


def memory_kv(mem, wk, wv):
    B, M, _ = mem.shape
    return (mem @ wk).reshape(B, M, XA_H, XA_D), (mem @ wv).reshape(B, M, XA_H, XA_D)


def cross_attention(h, mk, mv, wq, wo):
    B, T, _ = h.shape
    q = (h @ wq).reshape(B, T, XA_H, XA_D)
    s = jnp.einsum('bthd,bmhd->bhtm', q, mk.astype(q.dtype)).astype(F32) * (XA_D ** -0.5)
    p = jax.nn.softmax(s, axis=-1).astype(h.dtype)
    o = jnp.einsum('bhtm,bmhd->bthd', p, mv.astype(h.dtype))
    return o.reshape(B, T, D_MODEL) @ wo


def hier_moe(h, r1_w, r1_b, r2_w, r2_b, e_wg, e_wu, e_wd):
    B, T, D = h.shape
    x = h.reshape(B * T, D)
    lg_group = (x @ r1_w).astype(F32) + r1_b.astype(F32)
    grp = jnp.argmax(lg_group, axis=-1)
    p_grp = jnp.take_along_axis(jax.nn.softmax(lg_group, axis=-1), grp[:, None], axis=-1)
    lg_exp = jnp.einsum('nd,gde->nge', x, r2_w).astype(F32) + r2_b.astype(F32)
    lg_exp = jnp.take_along_axis(lg_exp, grp[:, None, None], axis=1)[:, 0]
    top_val, top_idx = lax.top_k(lg_exp, TOP_K)
    w_top = jax.nn.softmax(top_val, axis=-1) * p_grp
    w_in_group = jnp.einsum('nk,nke->ne', w_top, jax.nn.one_hot(top_idx, EXP_PER_GROUP, dtype=F32))
    combine = (jax.nn.one_hot(grp, N_GROUPS, dtype=F32)[:, :, None] * w_in_group[:, None, :]).reshape(-1, N_EXPERTS)
    hid = jax.nn.silu(jnp.einsum('nd,edf->nef', x, e_wg)) * jnp.einsum('nd,edf->nef', x, e_wu)
    out = jnp.einsum('nef,efd->nd', hid * combine[:, :, None].astype(hid.dtype), e_wd)
    return out.reshape(B, T, D)


def decoder_layer(x, mem_k, mem_v, C0, n0, m0, conv0, S0, lb, w_in, b_in, conv_w, mlstm_gn, hgrn_gn,
                  w_bm, w_bh, w_out, ln1_g, ln1_b, xa_wq, xa_wo, ln2_g, ln2_b,
                  r1_w, r1_b, r2_w, r2_b, e_wg, e_wu, e_wd, ln3_g, ln3_b):
    mix, state = token_mixer(x, C0, n0, m0, conv0, S0, lb, w_in, b_in, conv_w, mlstm_gn, hgrn_gn, w_bm, w_bh, w_out)
    h = layer_norm(ALPHA * x + mix, ln1_g, ln1_b)
    h = layer_norm(ALPHA * h + cross_attention(h, mem_k, mem_v, xa_wq, xa_wo), ln2_g, ln2_b)
    y = layer_norm(ALPHA * h + hier_moe(h, r1_w, r1_b, r2_w, r2_b, e_wg, e_wu, e_wd), ln3_g, ln3_b)
    return y, state


def setup_inputs(seed: int = 0) -> dict:
    key = jax.random.key(seed)
    ks = iter(jax.random.split(key, 48))

    def nrm(shape, scale):
        return jax.random.normal(next(ks), shape, F32) * scale

    f_lo = 4 * M_W + M_H
    forget_offset = jnp.zeros((N_IN,), F32).at[f_lo:f_lo + M_H].set(jnp.linspace(3.0, 6.0, M_H))
    return {
        "x_prompt": nrm((BATCH, SEQ, D_MODEL), 1.0),
        "x_sample": nrm((DEC_BATCH, DEC_SEQ, D_MODEL), 1.0),
        "mem_prompt": nrm((BATCH, MEM_LEN, D_MODEL), 1.0),
        "cache_mem_k": nrm((DEPTH, DEC_BATCH, MEM_LEN, XA_H, XA_D), 1.0),
        "cache_mem_v": nrm((DEPTH, DEC_BATCH, MEM_LEN, XA_H, XA_D), 1.0),
        "state_mlstm_C": nrm((DEPTH, DEC_BATCH, M_H, M_DK, M_DV), 0.5),
        "state_mlstm_n": nrm((DEPTH, DEC_BATCH, M_H, M_DK), 0.5),
        "state_mlstm_m": nrm((DEPTH, DEC_BATCH, M_H), 1.0),
        "state_mlstm_conv": nrm((DEPTH, DEC_BATCH, CONV_K - 1, 2 * M_W), 1.0),
        "state_hgrn_S": nrm((DEPTH, DEC_BATCH, HG_H, HG_DK, HG_DV), 0.5),
        "w_in": nrm((DEPTH, D_MODEL, N_IN), D_MODEL ** -0.5),
        "b_in": nrm((DEPTH, N_IN), 0.02) + forget_offset[None],
        "conv_w": nrm((DEPTH, CONV_K, 2 * M_W), CONV_K ** -0.5),
        "mlstm_gn": 1.0 + nrm((DEPTH, M_W), 0.02),
        "lb_logits": nrm((DEPTH + 1, HG_W), 0.1),
        "hgrn_gn": 1.0 + nrm((DEPTH, HG_W), 0.02),
        "w_bm": nrm((DEPTH, M_W, D_MODEL), M_W ** -0.5),
        "w_bh": nrm((DEPTH, HG_W, D_MODEL), HG_W ** -0.5),
        "w_out": nrm((DEPTH, D_MODEL, D_MODEL), BETA * D_MODEL ** -0.5),
        "ln1_g": 1.0 + nrm((DEPTH, D_MODEL), 0.02),
        "ln1_b": nrm((DEPTH, D_MODEL), 0.02),
        "xa_wq": nrm((DEPTH, D_MODEL, D_MODEL), D_MODEL ** -0.5),
        "xa_wk": nrm((DEPTH, D_MODEL, D_MODEL), D_MODEL ** -0.5),
        "xa_wv": nrm((DEPTH, D_MODEL, D_MODEL), BETA * D_MODEL ** -0.5),
        "xa_wo": nrm((DEPTH, D_MODEL, D_MODEL), BETA * D_MODEL ** -0.5),
        "ln2_g": 1.0 + nrm((DEPTH, D_MODEL), 0.02),
        "ln2_b": nrm((DEPTH, D_MODEL), 0.02),
        "r1_w": nrm((DEPTH, D_MODEL, N_GROUPS), D_MODEL ** -0.5),
        "r1_b": nrm((DEPTH, N_GROUPS), 0.01),
        "r2_w": nrm((DEPTH, N_GROUPS, D_MODEL, EXP_PER_GROUP), D_MODEL ** -0.5),
        "r2_b": nrm((DEPTH, N_GROUPS, EXP_PER_GROUP), 0.01),
        "e_wg": nrm((DEPTH, N_EXPERTS, D_MODEL, EXP_FF), D_MODEL ** -0.5),
        "e_wu": nrm((DEPTH, N_EXPERTS, D_MODEL, EXP_FF), BETA * D_MODEL ** -0.5),
        "e_wd": nrm((DEPTH, N_EXPERTS, EXP_FF, D_MODEL), BETA * EXP_FF ** -0.5),
        "ln3_g": 1.0 + nrm((DEPTH, D_MODEL), 0.02),
        "ln3_b": nrm((DEPTH, D_MODEL), 0.02),
    }


def reference(x_prompt, x_sample, mem_prompt, cache_mem_k, cache_mem_v, state_mlstm_C, state_mlstm_n,
              state_mlstm_m, state_mlstm_conv, state_hgrn_S, w_in, b_in, conv_w, mlstm_gn, lb_logits, hgrn_gn,
              w_bm, w_bh, w_out, ln1_g, ln1_b, xa_wq, xa_wk, xa_wv, xa_wo, ln2_g, ln2_b,
              r1_w, r1_b, r2_w, r2_b, e_wg, e_wu, e_wd, ln3_g, ln3_b):
    lb_all = jnp.cumsum(jax.nn.softmax(lb_logits.astype(F32), axis=0), axis=0)
    Bp = x_prompt.shape[0]
    hp, hs = x_prompt, x_sample
    p_states, s_states, p_mk, p_mv = [], [], [], []
    for l in range(DEPTH):
        lw = (w_in[l], b_in[l], conv_w[l], mlstm_gn[l], hgrn_gn[l], w_bm[l], w_bh[l], w_out[l],
              ln1_g[l], ln1_b[l], xa_wq[l], xa_wo[l], ln2_g[l], ln2_b[l],
              r1_w[l], r1_b[l], r2_w[l], r2_b[l], e_wg[l], e_wu[l], e_wd[l], ln3_g[l], ln3_b[l])
        mk, mv = memory_kv(mem_prompt, xa_wk[l], xa_wv[l])
        zero_state = (jnp.zeros((Bp, M_H, M_DK, M_DV), F32), jnp.zeros((Bp, M_H, M_DK), F32),
                      jnp.zeros((Bp, M_H), F32), jnp.zeros((Bp, CONV_K - 1, 2 * M_W), hp.dtype),
                      jnp.zeros((Bp, HG_H, HG_DK, HG_DV), F32))
        hp, sp = decoder_layer(hp, mk, mv, *zero_state, lb_all[l], *lw)
        hs, ss = decoder_layer(hs, cache_mem_k[l], cache_mem_v[l], state_mlstm_C[l], state_mlstm_n[l],
                               state_mlstm_m[l], state_mlstm_conv[l], state_hgrn_S[l], lb_all[l], *lw)
        p_mk.append(mk)
        p_mv.append(mv)
        p_states.append(sp)
        s_states.append(ss)
    mem_k_p = jnp.stack(p_mk)
    mem_v_p = jnp.stack(p_mv)
    C_p, n_p, m_p, conv_p, S_p = [jnp.stack([s[i] for s in p_states]) for i in range(5)]
    C_s, n_s, m_s, conv_s, S_s = [jnp.stack([s[i] for s in s_states]) for i in range(5)]
    return (hp, hs, mem_k_p, mem_v_p, C_p, n_p, m_p, conv_p, S_p, C_s, n_s, m_s, conv_s, S_s)
```

```python
import functools
import math

import jax
import jax.numpy as jnp
from jax import lax
from jax.experimental import pallas as pl
from jax.experimental.pallas import tpu as pltpu

F32 = jnp.float32
BF16 = jnp.bfloat16

D_MODEL = 2048
M_W = 1024
M_H = 4
M_DK = 256
M_DV = 256
CONV_K = 4
HG_W = 1024
HG_H = 8
HG_DK = 128
HG_DV = 128
XA_H = 4
XA_D = 512
N_GROUPS = 4
EXP_PER_GROUP = 8
N_EXPERTS = 32
TOP_K = 2
EXP_FF = 512
DEPTH = 1
ALPHA = (2 * DEPTH) ** 0.25
LN_EPS = 1e-5

COL_QK, COL_V, COL_O = 0, 2048, 3072
COL_QH, COL_FH, COL_IH, COL_GH = 4096, 5120, 6144, 7168
COL_GM, COL_GHH = 8192, 10240
N_MAIN = 12288
GATE_LO = 4 * M_W
LANES = 128
SUBLANES = 8
ROUTE_LANE0 = N_GROUPS

VMEM_LIMIT = 56 << 20
MOE_ROW_TILE = 256


def _cparams(sem, vmem=VMEM_LIMIT):
    return pltpu.CompilerParams(dimension_semantics=sem, vmem_limit_bytes=vmem)


def _tile(n, pref):
    t = math.gcd(n, pref)
    assert t % SUBLANES == 0, (n, pref)
    return t


def _bdot(a, b):
    return jnp.dot(a.astype(BF16), b.astype(BF16), preferred_element_type=F32)


def _bdot_nt(a, b):
    return lax.dot_general(a.astype(BF16), b.astype(BF16), (((1,), (1,)), ((), ())),
                           preferred_element_type=F32)


def _bdot_tn(a, b):
    return lax.dot_general(a.astype(BF16), b.astype(BF16), (((0,), (0,)), ((), ())),
                           preferred_element_type=F32)


def _sigmoid(x):
    return 1.0 / (1.0 + jnp.exp(-x))


def _cumsum_rows(x):
    n = x.shape[0]
    row = lax.broadcasted_iota(jnp.int32, x.shape, 0)
    d = 1
    while d < n:
        x = x + jnp.where(row >= d, pltpu.roll(x, d, axis=0), 0.0)
        d *= 2
    return x


def _col_to_row(col, eye):
    return jnp.sum(jnp.where(eye, col, 0.0), axis=0, keepdims=True)


def _row_to_col(row, eye):
    return jnp.sum(jnp.where(eye, row, 0.0), axis=1, keepdims=True)


def _layer_norm(x, g, b):
    mu = jnp.mean(x, axis=-1, keepdims=True)
    xc = x - mu
    var = jnp.mean(xc * xc, axis=-1, keepdims=True)
    return xc * lax.rsqrt(var + LN_EPS) * g + b


def _mm_kernel(x_ref, w_ref, b_ref, o_ref):
    acc = jnp.dot(x_ref[...], w_ref[...], preferred_element_type=F32)
    o_ref[...] = (acc + b_ref[...]).astype(o_ref.dtype)


def _matmul_bias(x, w, b, *, tm, tn, out_dtype=F32, name):
    M, K = x.shape
    N = w.shape[1]
    tm = _tile(M, tm)
    tn = _tile(N, tn)
    return pl.pallas_call(
        _mm_kernel,
        grid=(N // tn, M // tm),
        in_specs=[pl.BlockSpec((tm, K), lambda j, i: (i, 0)),
                  pl.BlockSpec((K, tn), lambda j, i: (0, j)),
                  pl.BlockSpec((1, tn), lambda j, i: (0, j))],
        out_specs=pl.BlockSpec((tm, tn), lambda j, i: (i, j)),
        out_shape=jax.ShapeDtypeStruct((M, N), out_dtype),
        compiler_params=_cparams(("parallel", "parallel")),
        name=name,
    )(x, w, b)


def _mlstm_kernel(*refs, L, NC):
    (qk_ref, v_ref, o_ref, g_ref, conv0_ref, C0_ref, n0_ref, m0_ref, cw_ref, gn_ref,
     h_ref, Co_ref, no_ref, mo_ref, C_scr, n_scr, m_scr, tail_scr) = refs
    c = pl.program_id(1)

    @pl.when(c == 0)
    def _init():
        C_scr[...] = C0_ref[0]
        n_scr[...] = n0_ref[0]
        m_scr[...] = m0_ref[0]
        tail_scr[...] = conv0_ref[0]

    qk_pre = qk_ref[...]
    ext = jnp.concatenate([tail_scr[...], qk_pre], axis=0)
    cw = cw_ref[...]
    acc = qk_pre * cw[CONV_K - 1:CONV_K, :]
    for j in range(1, CONV_K):
        acc = acc + pltpu.roll(ext, j, axis=0)[SUBLANES:, :] * cw[CONV_K - 1 - j:CONV_K - j, :]
    tail_scr[...] = qk_pre[L - SUBLANES:, :]
    qk = acc * _sigmoid(acc)

    g = g_ref[...]
    lf_all = jnp.minimum(g, 0.0) - jnp.log(1.0 + jnp.exp(-jnp.abs(g)))
    F_all = _cumsum_rows(lf_all)
    ti = lax.broadcasted_iota(jnp.int32, (L, L), 0)
    si = lax.broadcasted_iota(jnp.int32, (L, L), 1)
    eye = ti == si
    causal = si <= ti
    gn = gn_ref[...]

    for h in range(M_H):
        ks = slice(h * M_DK, (h + 1) * M_DK)
        q = qk[:, ks] * (M_DK ** -0.5)
        k = qk[:, M_W + h * M_DK:M_W + (h + 1) * M_DK]
        v = v_ref[:, ks]
        ig = g[:, h:h + 1]
        F = F_all[:, M_H + h:M_H + h + 1]
        m_prev = m_scr[:, h:h + 1]
        C = C_scr[h]
        n_row = n_scr[h:h + 1, :]

        r_row = _col_to_row(ig - F, eye)
        Dm = jnp.where(causal, F + r_row, -jnp.inf)
        init_w = F + m_prev
        m_t = jnp.maximum(init_w, jnp.max(Dm, axis=1, keepdims=True))
        P = jnp.exp(Dm - m_t)
        a0 = jnp.exp(init_w - m_t)
        Sc = _bdot_nt(q, k) * P
        num = _bdot(Sc, v) + a0 * _bdot(q, C)
        den = jnp.sum(Sc, axis=1, keepdims=True) + a0 * jnp.sum(q * n_row, axis=1, keepdims=True)
        hh = num * (1.0 / jnp.maximum(jnp.abs(den), jnp.exp(-m_t)))

        FL = F[L - 1:L, :]
        mL = m_t[L - 1:L, :]
        wL = jnp.exp(FL - F + ig - mL)
        decay = jnp.exp(FL + m_prev - mL)
        kw = wL * k
        C_scr[h] = decay * C + _bdot_tn(kw, v)
        n_scr[h:h + 1, :] = decay * n_row + jnp.sum(kw, axis=0, keepdims=True)
        m_scr[:, h:h + 1] = mL

        hm = _sigmoid(o_ref[:, ks]) * hh
        hm = hm - jnp.mean(hm, axis=1, keepdims=True)
        hm = hm * lax.rsqrt(jnp.mean(hm * hm, axis=1, keepdims=True) + LN_EPS) * gn[:, ks]
        h_ref[:, ks] = hm

    @pl.when(c == NC - 1)
    def _fin():
        Co_ref[0] = C_scr[...]
        no_ref[0] = n_scr[...]
        mo_ref[0] = m_scr[...]


def _mlstm(u_main, u_gate, conv0p, C0, n0, m0, conv_w, gn, *, B, T, L, row0):
    NC = T // L
    rb0 = row0 // L
    tok = lambda b, c: rb0 + b * NC + c
    in_specs = [
        pl.BlockSpec((L, 2 * M_W), lambda b, c: (tok(b, c), COL_QK // (2 * M_W))),
        pl.BlockSpec((L, M_W), lambda b, c: (tok(b, c), COL_V // M_W)),
        pl.BlockSpec((L, M_W), lambda b, c: (tok(b, c), COL_O // M_W)),
        pl.BlockSpec((L, LANES), lambda b, c: (tok(b, c), 0)),
        pl.BlockSpec((1, SUBLANES, 2 * M_W), lambda b, c: (b, 0, 0)),
        pl.BlockSpec((1, M_H, M_DK, M_DV), lambda b, c: (b, 0, 0, 0)),
        pl.BlockSpec((1, M_H, M_DK), lambda b, c: (b, 0, 0)),
        pl.BlockSpec((1, 1, M_H), lambda b, c: (b, 0, 0)),
        pl.BlockSpec((CONV_K, 2 * M_W), lambda b, c: (0, 0)),
        pl.BlockSpec((1, M_W), lambda b, c: (0, 0)),
    ]
    args = [u_main, u_main, u_main, u_gate, conv0p, C0, n0, m0, conv_w, gn]
    out_specs = [
        pl.BlockSpec((L, M_W), lambda b, c: (b * NC + c, 0)),
        pl.BlockSpec((1, M_H, M_DK, M_DV), lambda b, c: (b, 0, 0, 0)),
        pl.BlockSpec((1, M_H, M_DK), lambda b, c: (b, 0, 0)),
        pl.BlockSpec((1, 1, M_H), lambda b, c: (b, 0, 0)),
    ]
    out_shape = [
        jax.ShapeDtypeStruct((B * T, M_W), F32),
        jax.ShapeDtypeStruct((B, M_H, M_DK, M_DV), F32),
        jax.ShapeDtypeStruct((B, M_H, M_DK), F32),
        jax.ShapeDtypeStruct((B, 1, M_H), F32),
    ]
    return pl.pallas_call(
        functools.partial(_mlstm_kernel, L=L, NC=NC),
        grid=(B, NC),
        in_specs=in_specs,
        out_specs=out_specs,
        out_shape=out_shape,
        scratch_shapes=[pltpu.VMEM((M_H, M_DK, M_DV), F32), pltpu.VMEM((M_H, M_DK), F32),
                        pltpu.VMEM((1, M_H), F32), pltpu.VMEM((SUBLANES, 2 * M_W), F32)],
        compiler_params=_cparams(("parallel", "arbitrary")),
        name=f"mlstm_L{L}",
    )(*args)


def _hgrn_kernel(*refs, L, NC):
    (q_ref, f_ref, i_ref, g_ref, S0_ref, lb_ref, gn_ref, o_ref, So_ref, S_scr) = refs
    c = pl.program_id(1)

    @pl.when(c == 0)
    def _init():
        S_scr[...] = S0_ref[0]

    lb = lb_ref[...]
    f = lb + (1.0 - lb) * _sigmoid(f_ref[...])
    kk = 1.0 - f
    b = _cumsum_rows(jnp.log(f))
    qh = q_ref[...]
    q = qh * _sigmoid(qh)
    v = i_ref[...]
    gh = g_ref[...]
    gsilu = gh * _sigmoid(gh)
    gn = gn_ref[...]
    bL = b[L - 1:L, :]
    mid = max(L // 2 - 1, 0)
    bm = b[mid:mid + 1, :]
    q_in = q * jnp.exp(b)
    q_t = q * jnp.exp(b - bm)
    k_t = kk * jnp.exp(bm - b)
    k_st = kk * jnp.exp(bL - b)
    ti = lax.broadcasted_iota(jnp.int32, (L, L), 0)
    si = lax.broadcasted_iota(jnp.int32, (L, L), 1)
    causal = si <= ti
    ci = lax.broadcasted_iota(jnp.int32, (HG_DK, HG_DK), 0)
    cj = lax.broadcasted_iota(jnp.int32, (HG_DK, HG_DK), 1)
    eye = ci == cj

    for h in range(HG_H):
        hs = slice(h * HG_DK, (h + 1) * HG_DK)
        S = S_scr[h]
        A = jnp.where(causal, _bdot_nt(q_t[:, hs], k_t[:, hs]), 0.0)
        o = _bdot(A, v[:, hs]) + _bdot(q_in[:, hs], S)
        dec = jnp.exp(_row_to_col(bL[:, hs], eye))
        S_scr[h] = dec * S + _bdot_tn(k_st[:, hs], v[:, hs])
        o = o * lax.rsqrt(jnp.mean(o * o, axis=1, keepdims=True) + LN_EPS) * gn[:, hs]
        o_ref[:, hs] = o * gsilu[:, hs]

    @pl.when(c == NC - 1)
    def _fin():
        So_ref[0] = S_scr[...]


def _hgrn(u_main, S0, lb, gn, *, B, T, L, row0):
    NC = T // L
    rb0 = row0 // L
    tok = lambda b, c: rb0 + b * NC + c
    col = lambda off: (lambda b, c: (tok(b, c), off // HG_W))
    in_specs = [
        pl.BlockSpec((L, HG_W), col(COL_QH)),
        pl.BlockSpec((L, HG_W), col(COL_FH)),
        pl.BlockSpec((L, HG_W), col(COL_IH)),
        pl.BlockSpec((L, HG_W), col(COL_GH)),
        pl.BlockSpec((1, HG_H, HG_DK, HG_DV), lambda b, c: (b, 0, 0, 0)),
        pl.BlockSpec((1, HG_W), lambda b, c: (0, 0)),
        pl.BlockSpec((1, HG_W), lambda b, c: (0, 0)),
    ]
    args = [u_main, u_main, u_main, u_main, S0, lb, gn]
    return pl.pallas_call(
        functools.partial(_hgrn_kernel, L=L, NC=NC),
        grid=(B, NC),
        in_specs=in_specs,
        out_specs=[pl.BlockSpec((L, HG_W), lambda b, c: (b * NC + c, 0)),
                   pl.BlockSpec((1, HG_H, HG_DK, HG_DV), lambda b, c: (b, 0, 0, 0))],
        out_shape=[jax.ShapeDtypeStruct((B * T, HG_W), F32),
                   jax.ShapeDtypeStruct((B, HG_H, HG_DK, HG_DV), F32)],
        scratch_shapes=[pltpu.VMEM((HG_H, HG_DK, HG_DV), F32)],
        compiler_params=_cparams(("parallel", "arbitrary")),
        name=f"hgrn_L{L}",
    )(*args)


def _merge_kernel(hmp_ref, hms_ref, ogp_ref, ogs_ref, gm_ref, gh_ref, xp_ref, xs_ref, wbm_ref, wbh_ref,
                  wo_ref, g_ref, b_ref, h1_ref, h1b_ref, *, n_prompt_tiles):
    is_prompt = pl.program_id(0) < n_prompt_tiles
    a = _bdot(jnp.where(is_prompt, hmp_ref[...], hms_ref[...]), wbm_ref[...])
    bb = _bdot(jnp.where(is_prompt, ogp_ref[...], ogs_ref[...]), wbh_ref[...])
    merged = _sigmoid(gm_ref[...]) * a + _sigmoid(gh_ref[...]) * bb
    mix = _bdot(merged, wo_ref[...])
    x = jnp.where(is_prompt, xp_ref[...], xs_ref[...])
    h1 = _layer_norm(ALPHA * x + mix, g_ref[...], b_ref[...])
    h1_ref[...] = h1
    h1b_ref[...] = h1.astype(BF16)


def _merge(hm_p, hm_s, og_p, og_s, u_main, x_p, x_s, w_bm, w_bh, w_out, ln_g, ln_b, *, tm):
    Np, Ns = x_p.shape[0], x_s.shape[0]
    NT = Np + Ns
    tm = _tile(math.gcd(Np, Ns), tm)
    npt = Np // tm
    const = lambda i: (0, 0)
    prompt_rows = lambda i: (jnp.minimum(i, npt - 1), 0)
    sample_rows = lambda i: (jnp.maximum(i - npt, 0), 0)
    return pl.pallas_call(
        functools.partial(_merge_kernel, n_prompt_tiles=npt),
        grid=(NT // tm,),
        in_specs=[
            pl.BlockSpec((tm, M_W), prompt_rows),
            pl.BlockSpec((tm, M_W), sample_rows),
            pl.BlockSpec((tm, HG_W), prompt_rows),
            pl.BlockSpec((tm, HG_W), sample_rows),
            pl.BlockSpec((tm, D_MODEL), lambda i: (i, COL_GM // D_MODEL)),
            pl.BlockSpec((tm, D_MODEL), lambda i: (i, COL_GHH // D_MODEL)),
            pl.BlockSpec((tm, D_MODEL), prompt_rows),
            pl.BlockSpec((tm, D_MODEL), sample_rows),
            pl.BlockSpec((M_W, D_MODEL), const, pipeline_mode=pl.Buffered(1)),
            pl.BlockSpec((HG_W, D_MODEL), const, pipeline_mode=pl.Buffered(1)),
            pl.BlockSpec((D_MODEL, D_MODEL), const, pipeline_mode=pl.Buffered(1)),
            pl.BlockSpec((1, D_MODEL), const),
            pl.BlockSpec((1, D_MODEL), const),
        ],
        out_specs=[pl.BlockSpec((tm, D_MODEL), lambda i: (i, 0)),
                   pl.BlockSpec((tm, D_MODEL), lambda i: (i, 0))],
        out_shape=[jax.ShapeDtypeStruct((NT, D_MODEL), F32),
                   jax.ShapeDtypeStruct((NT, D_MODEL), BF16)],
        compiler_params=_cparams(("parallel",)),
        name="merge_ln1",
    )(hm_p, hm_s, og_p, og_s, u_main, u_main, x_p, x_s, w_bm, w_bh, w_out, ln_g, ln_b)


def _attn_kernel(q_ref, k_ref, v_ref, o_ref):
    for h in range(XA_H):
        sl = slice(h * XA_D, (h + 1) * XA_D)
        s = _bdot_nt(q_ref[:, sl], k_ref[0, :, sl]) * (XA_D ** -0.5)
        e = jnp.exp(s - jnp.max(s, axis=1, keepdims=True))
        p = e / jnp.sum(e, axis=1, keepdims=True)
        o_ref[:, sl] = _bdot(p, v_ref[0, :, sl])


def _attention(q, mem_k, mem_v, *, B, T, tq, row0):
    tq = _tile(T, tq)
    nq = T // tq
    rb0 = row0 // tq
    M = mem_k.shape[1]
    in_specs = [pl.BlockSpec((tq, D_MODEL), lambda b, t: (rb0 + b * nq + t, 0)),
                pl.BlockSpec((1, M, D_MODEL), lambda b, t: (b, 0, 0)),
                pl.BlockSpec((1, M, D_MODEL), lambda b, t: (b, 0, 0))]
    args = [q, mem_k, mem_v]
    return pl.pallas_call(
        _attn_kernel,
        grid=(B, nq),
        in_specs=in_specs,
        out_specs=pl.BlockSpec((tq, D_MODEL), lambda b, t: (b * nq + t, 0)),
        out_shape=jax.ShapeDtypeStruct((B * T, D_MODEL), F32),
        compiler_params=_cparams(("parallel", "parallel")),
        name=f"xattn_T{T}",
    )(*args)


def _oln_kernel(op_ref, os_ref, h1_ref, wo_ref, g_ref, b_ref, wr_ref, br_ref,
                h2_ref, route_ref, cnt_ref, carry_scr, *, n_prompt_tiles):
    i = pl.program_id(0)

    @pl.when(i == 0)
    def _init():
        carry_scr[...] = jnp.zeros_like(carry_scr)

    att = jnp.where(i < n_prompt_tiles, op_ref[...], os_ref[...])
    h2 = _layer_norm(ALPHA * h1_ref[...] + _bdot(att, wo_ref[...]), g_ref[...], b_ref[...])
    h2_ref[...] = h2

    logits = _bdot(h2, wr_ref[...]) + br_ref[...]
    tm = logits.shape[0]
    lane = lax.broadcasted_iota(jnp.int32, logits.shape, 1)
    lane_f = lane.astype(F32)
    neg = -jnp.inf

    def first_argmax(vals):
        mx = jnp.max(vals, axis=1, keepdims=True)
        idx = jnp.min(jnp.where(vals == mx, lane_f, float(LANES)), axis=1, keepdims=True)
        return mx, idx.astype(jnp.int32)

    gl = jnp.where(lane < N_GROUPS, logits, neg)
    gmax, grp = first_argmax(gl)
    p_grp = 1.0 / jnp.sum(jnp.exp(gl - gmax), axis=1, keepdims=True)
    lo = ROUTE_LANE0 + grp * EXP_PER_GROUP
    el = jnp.where((lane >= lo) & (lane < lo + EXP_PER_GROUP), logits, neg)
    v1, i1 = first_argmax(el)
    v2, i2 = first_argmax(jnp.where(lane == i1, neg, el))
    e21 = jnp.exp(v2 - v1)
    w1 = p_grp / (1.0 + e21)
    w2 = p_grp * e21 / (1.0 + e21)

    pick1 = lane == i1
    pick2 = lane == i2
    onehot = jnp.where(pick1 | pick2, 1.0, 0.0)
    ti = lax.broadcasted_iota(jnp.int32, (tm, tm), 0)
    si = lax.broadcasted_iota(jnp.int32, (tm, tm), 1)
    tri = jnp.where(si <= ti, 1.0, 0.0)
    cnt = _bdot(tri, onehot) + carry_scr[...]
    r1 = jnp.sum(jnp.where(pick1, cnt, 0.0), axis=1, keepdims=True) - 1.0
    r2 = jnp.sum(jnp.where(pick2, cnt, 0.0), axis=1, keepdims=True) - 1.0
    last = cnt[tm - 1:tm, :]
    carry_scr[...] = last
    cnt_ref[...] = jnp.broadcast_to(last, cnt_ref.shape)

    e1 = (i1 - ROUTE_LANE0).astype(F32)
    e2 = (i2 - ROUTE_LANE0).astype(F32)
    packed = jnp.zeros_like(logits)
    for idx, val in enumerate((e1, e2, w1, w2, r1, r2)):
        packed = jnp.where(lane == idx, val, packed)
    route_ref[...] = packed


def _oln(o_p, o_s, h1, wo, ln_g, ln_b, wr, br, *, tm):
    Np, Ns = o_p.shape[0], o_s.shape[0]
    NT = Np + Ns
    tm = _tile(math.gcd(Np, Ns), tm)
    npt = Np // tm
    const = lambda i: (0, 0)
    return pl.pallas_call(
        functools.partial(_oln_kernel, n_prompt_tiles=npt),
        grid=(NT // tm,),
        in_specs=[pl.BlockSpec((tm, D_MODEL), lambda i: (jnp.minimum(i, npt - 1), 0)),
                  pl.BlockSpec((tm, D_MODEL), lambda i: (jnp.maximum(i - npt, 0), 0)),
                  pl.BlockSpec((tm, D_MODEL), lambda i: (i, 0)),
                  pl.BlockSpec((D_MODEL, D_MODEL), const, pipeline_mode=pl.Buffered(1)),
                  pl.BlockSpec((1, D_MODEL), const),
                  pl.BlockSpec((1, D_MODEL), const),
                  pl.BlockSpec((D_MODEL, LANES), const),
                  pl.BlockSpec((1, LANES), const)],
        out_specs=[pl.BlockSpec((tm, D_MODEL), lambda i: (i, 0)),
                   pl.BlockSpec((tm, LANES), lambda i: (i, 0)),
                   pl.BlockSpec((SUBLANES, LANES), const)],
        out_shape=[jax.ShapeDtypeStruct((NT, D_MODEL), F32),
                   jax.ShapeDtypeStruct((NT, LANES), F32),
                   jax.ShapeDtypeStruct((SUBLANES, LANES), F32)],
        scratch_shapes=[pltpu.VMEM((1, LANES), F32)],
        compiler_params=_cparams(("arbitrary",)),
        name="oproj_ln2_router",
    )(o_p, o_s, h1, wo, ln_g, ln_b, wr, br)


def _row_copy(src_ref, src_row, dst_ref, dst_row, sem):
    return pltpu.make_async_copy(src_ref.at[pl.ds(src_row, 1), :], dst_ref.at[pl.ds(dst_row, 1), :], sem)


def _dispatch_kernel(pos_ref, h_ref, xs_ref, sem, *, tm):
    base = pl.program_id(0) * tm

    def start(r, carry):
        for k in range(TOP_K):
            _row_copy(h_ref, r, xs_ref, pos_ref[TOP_K * (base + r) + k], sem).start()
        return carry

    def wait(r, carry):
        for k in range(TOP_K):
            _row_copy(h_ref, r, xs_ref, pos_ref[TOP_K * (base + r) + k], sem).wait()
        return carry

    lax.fori_loop(0, tm, start, 0)
    lax.fori_loop(0, tm, wait, 0)


def _dispatch(pos_flat, h2, cap, *, tm):
    NT = h2.shape[0]
    tm = _tile(NT, tm)
    return pl.pallas_call(
        functools.partial(_dispatch_kernel, tm=tm),
        grid_spec=pltpu.PrefetchScalarGridSpec(
            num_scalar_prefetch=1,
            grid=(NT // tm,),
            in_specs=[pl.BlockSpec((tm, D_MODEL), lambda i, pos: (i, 0))],
            out_specs=pl.BlockSpec(memory_space=pl.ANY),
            scratch_shapes=[pltpu.SemaphoreType.DMA(())],
        ),
        out_shape=jax.ShapeDtypeStruct((cap, D_MODEL), F32),
        compiler_params=_cparams(("arbitrary",)),
        name="moe_dispatch",
    )(pos_flat, h2)


def _expert_kernel(it_ref, ie_ref, lo_ref, hi_ref, first_ref, n_ref, x_ref, wg_ref, wu_ref, wd_ref, y_ref,
                   wg_s, wu_s, wd_s):
    i = pl.program_id(0)

    @pl.when(i < n_ref[0])
    def _compute():
        fresh = jnp.logical_or(i == 0, ie_ref[i] != ie_ref[jnp.maximum(i - 1, 0)])

        @pl.when(fresh)
        def _cast():
            wg_s[...] = wg_ref[0].astype(BF16)
            wu_s[...] = wu_ref[0].astype(BF16)
            wd_s[...] = wd_ref[0].astype(BF16)

        x = x_ref[...].astype(BF16)
        gate = jnp.dot(x, wg_s[...], preferred_element_type=F32)
        up = jnp.dot(x, wu_s[...], preferred_element_type=F32)
        hid = gate * _sigmoid(gate) * up
        y = jnp.dot(hid.astype(BF16), wd_s[...], preferred_element_type=F32)
        TR = y.shape[0]
        rows = it_ref[i] * TR + lax.broadcasted_iota(jnp.int32, (TR, 1), 0)
        mine = (rows >= lo_ref[i]) & (rows < hi_ref[i])

        @pl.when(first_ref[i] == 1)
        def _first():
            y_ref[...] = jnp.where(mine, y, 0.0)

        @pl.when(first_ref[i] == 0)
        def _later():
            y_ref[...] = jnp.where(mine, y, y_ref[...])


def _experts(items, xs, e_wg, e_wu, e_wd):
    TR = MOE_ROW_TILE
    n_work = items[0].shape[0]
    rows = lambda i, it, ie, lo, hi, fi, n: (it[i], 0)
    wsel = lambda i, it, ie, lo, hi, fi, n: (ie[i], 0, 0)
    return pl.pallas_call(
        _expert_kernel,
        grid_spec=pltpu.PrefetchScalarGridSpec(
            num_scalar_prefetch=6,
            grid=(n_work,),
            in_specs=[pl.BlockSpec((TR, D_MODEL), rows),
                      pl.BlockSpec((1, D_MODEL, EXP_FF), wsel),
                      pl.BlockSpec((1, D_MODEL, EXP_FF), wsel),
                      pl.BlockSpec((1, EXP_FF, D_MODEL), wsel)],
            out_specs=pl.BlockSpec((TR, D_MODEL), rows),
            scratch_shapes=[pltpu.VMEM((D_MODEL, EXP_FF), BF16), pltpu.VMEM((D_MODEL, EXP_FF), BF16),
                            pltpu.VMEM((EXP_FF, D_MODEL), BF16)],
        ),
        out_shape=jax.ShapeDtypeStruct(xs.shape, F32),
        compiler_params=_cparams(("arbitrary",)),
        name="moe_experts",
    )(*items, xs, e_wg, e_wu, e_wd)


def _moe_schedule(route, counts, n_rows):
    TR = MOE_ROW_TILE
    assert n_rows % TR == 0
    n_work = n_rows // TR + N_EXPERTS - 1
    e12 = route[:, 0:2].astype(jnp.int32)
    r12 = route[:, 4:6].astype(jnp.int32)
    cnt = counts[0, ROUTE_LANE0:ROUTE_LANE0 + N_EXPERTS].astype(jnp.int32)
    g_end = jnp.cumsum(cnt)
    g_start = g_end - cnt
    pos_flat = (g_start[e12] + r12).reshape(-1)
    first_tile = g_start // TR
    n_items_e = jnp.where(cnt > 0, (g_end - 1) // TR - first_tile + 1, 0)
    item_end = jnp.cumsum(n_items_e)
    n_items = item_end[-1:]
    idx = jnp.minimum(jnp.arange(n_work, dtype=jnp.int32), n_items[0] - 1)
    ie = jnp.minimum(jnp.searchsorted(item_end, idx, side="right"), N_EXPERTS - 1).astype(jnp.int32)
    it = first_tile[ie] + idx - (item_end - n_items_e)[ie]
    first = jnp.concatenate([jnp.ones((1,), jnp.int32), (it[1:] != it[:-1]).astype(jnp.int32)])
    items = (it.astype(jnp.int32), ie, g_start[ie], g_end[ie], first, n_items.astype(jnp.int32))
    return pos_flat, items


def _combine_kernel(pos_ref, h2_ref, route_ref, ys_ref, g_ref, b_ref, yp_ref, ysm_ref,
                    buf, sem, *, tm, n_prompt_tiles):
    i = pl.program_id(0)
    base = i * tm

    def start(r, carry):
        for k in range(TOP_K):
            _row_copy(ys_ref, pos_ref[TOP_K * (base + r) + k], buf.at[k], r, sem).start()
        return carry

    def wait(r, carry):
        for k in range(TOP_K):
            _row_copy(ys_ref, pos_ref[TOP_K * (base + r) + k], buf.at[k], r, sem).wait()
        return carry

    lax.fori_loop(0, tm, start, 0)
    lax.fori_loop(0, tm, wait, 0)
    route = route_ref[...]
    moe = route[:, 2:3] * buf[0] + route[:, 3:4] * buf[1]
    y = _layer_norm(ALPHA * h2_ref[...] + moe, g_ref[...], b_ref[...])

    @pl.when(i < n_prompt_tiles)
    def _prompt():
        yp_ref[...] = y

    @pl.when(i >= n_prompt_tiles)
    def _sample():
        ysm_ref[...] = y


def _combine(pos_flat, h2, route, ys, ln_g, ln_b, *, Np, Ns, tm):
    NT = h2.shape[0]
    tm = _tile(math.gcd(Np, Ns), tm)
    npt = Np // tm
    const = lambda i, pos: (0, 0)
    return pl.pallas_call(
        functools.partial(_combine_kernel, tm=tm, n_prompt_tiles=npt),
        grid_spec=pltpu.PrefetchScalarGridSpec(
            num_scalar_prefetch=1,
            grid=(NT // tm,),
            in_specs=[pl.BlockSpec((tm, D_MODEL), lambda i, pos: (i, 0)),
                      pl.BlockSpec((tm, LANES), lambda i, pos: (i, 0)),
                      pl.BlockSpec(memory_space=pl.ANY),
                      pl.BlockSpec((1, D_MODEL), const),
                      pl.BlockSpec((1, D_MODEL), const)],
            out_specs=[pl.BlockSpec((tm, D_MODEL), lambda i, pos: (jnp.minimum(i, npt - 1), 0)),
                       pl.BlockSpec((tm, D_MODEL), lambda i, pos: (jnp.maximum(i - npt, 0), 0))],
            scratch_shapes=[pltpu.VMEM((TOP_K, tm, D_MODEL), F32), pltpu.SemaphoreType.DMA(())],
        ),
        out_shape=[jax.ShapeDtypeStruct((Np, D_MODEL), F32),
                   jax.ShapeDtypeStruct((Ns, D_MODEL), F32)],
        compiler_params=_cparams(("arbitrary",)),
        name="moe_combine_ln3",
    )(pos_flat, h2, route, ys, ln_g, ln_b)


def kernel(x_prompt, x_sample, mem_prompt, cache_mem_k, cache_mem_v, state_mlstm_C, state_mlstm_n,
           state_mlstm_m, state_mlstm_conv, state_hgrn_S, w_in, b_in, conv_w, mlstm_gn, lb_logits, hgrn_gn,
           w_bm, w_bh, w_out, ln1_g, ln1_b, xa_wq, xa_wk, xa_wv, xa_wo, ln2_g, ln2_b,
           r1_w, r1_b, r2_w, r2_b, e_wg, e_wu, e_wd, ln3_g, ln3_b):
    Bp, Tp, _ = x_prompt.shape
    Bs, Ts, _ = x_sample.shape
    MEM = mem_prompt.shape[1]
    Np, Ns = Bp * Tp, Bs * Ts
    NT = Np + Ns
    Lp_m = math.gcd(Tp, 256)
    Lp_h = math.gcd(Tp, 64)
    Ls = Ts
    assert Ts % SUBLANES == 0 and Np % Ls == 0

    xp2 = x_prompt.reshape(Np, D_MODEL)
    xs2 = x_sample.reshape(Ns, D_MODEL)
    xb = jnp.concatenate([xp2, xs2], axis=0).astype(BF16)

    w = w_in[0]
    bi = b_in[0]
    w_main = jnp.concatenate([w[:, :GATE_LO], w[:, GATE_LO + 2 * M_H:]], axis=1).astype(BF16)
    b_main = jnp.concatenate([bi[:GATE_LO], bi[GATE_LO + 2 * M_H:]])[None]
    w_gate = jnp.pad(w[:, GATE_LO:GATE_LO + 2 * M_H], ((0, 0), (0, LANES - 2 * M_H))).astype(BF16)
    b_gate = jnp.pad(bi[GATE_LO:GATE_LO + 2 * M_H], (0, LANES - 2 * M_H))[None]
    lb = jnp.cumsum(jax.nn.softmax(lb_logits.astype(F32), axis=0), axis=0)[0][None]
    wr = jnp.zeros((D_MODEL, LANES), F32)
    wr = wr.at[:, :N_GROUPS].set(r1_w[0])
    wr = wr.at[:, ROUTE_LANE0:ROUTE_LANE0 + N_EXPERTS].set(
        jnp.transpose(r2_w[0], (1, 0, 2)).reshape(D_MODEL, N_EXPERTS)).astype(BF16)
    br = jnp.zeros((1, LANES), F32)
    br = br.at[0, :N_GROUPS].set(r1_b[0])
    br = br.at[0, ROUTE_LANE0:ROUTE_LANE0 + N_EXPERTS].set(r2_b[0].reshape(N_EXPERTS))
    zeros_d = jnp.zeros((1, D_MODEL), F32)

    memb = mem_prompt.reshape(Bp * MEM, D_MODEL).astype(BF16)
    mk = _matmul_bias(memb, xa_wk[0].astype(BF16), zeros_d, tm=1024, tn=1024, name="mem_k")
    mv = _matmul_bias(memb, xa_wv[0].astype(BF16), zeros_d, tm=1024, tn=1024, name="mem_v")

    u_main = _matmul_bias(xb, w_main, b_main, tm=1024, tn=1024, name="in_proj")
    u_gate = _matmul_bias(xb, w_gate, b_gate, tm=1024, tn=LANES, name="in_proj_gates")

    padc = lambda c: jnp.pad(c, ((0, 0), (SUBLANES - (CONV_K - 1), 0), (0, 0)))
    zC = jnp.zeros((Bp, M_H, M_DK, M_DV), F32)
    zn = jnp.zeros((Bp, M_H, M_DK), F32)
    zm = jnp.zeros((Bp, 1, M_H), F32)
    zconv = jnp.zeros((Bp, SUBLANES, 2 * M_W), F32)
    zS = jnp.zeros((Bp, HG_H, HG_DK, HG_DV), F32)
    cw = conv_w[0]
    mgn = mlstm_gn[0][None]
    hgn = hgrn_gn[0][None]
    hm_p, C_p, n_p, m_p = _mlstm(u_main, u_gate, zconv, zC, zn, zm, cw, mgn,
                                 B=Bp, T=Tp, L=Lp_m, row0=0)
    hm_s, C_s, n_s, m_s = _mlstm(u_main, u_gate, padc(state_mlstm_conv[0]), state_mlstm_C[0],
                                 state_mlstm_n[0], state_mlstm_m[0][:, None, :], cw, mgn,
                                 B=Bs, T=Ts, L=Ls, row0=Np)
    og_p, S_p = _hgrn(u_main, zS, lb, hgn, B=Bp, T=Tp, L=Lp_h, row0=0)
    og_s, S_s = _hgrn(u_main, state_hgrn_S[0], lb, hgn, B=Bs, T=Ts, L=Ls, row0=Np)
    qk_pre_p = u_main[:Np, COL_QK:COL_QK + 2 * M_W].reshape(Bp, Tp, 2 * M_W)
    qk_pre_s = u_main[Np:, COL_QK:COL_QK + 2 * M_W].reshape(Bs, Ts, 2 * M_W)
    conv_p = qk_pre_p[:, Tp - (CONV_K - 1):]
    conv_s = qk_pre_s[:, Ts - (CONV_K - 1):]

    h1, h1b = _merge(hm_p, hm_s, og_p, og_s, u_main, xp2, xs2, w_bm[0].astype(BF16), w_bh[0].astype(BF16),
                     w_out[0].astype(BF16), ln1_g, ln1_b, tm=256)

    q = _matmul_bias(h1b, xa_wq[0].astype(BF16), zeros_d, tm=1024, tn=1024, name="xa_q")
    att_p = _attention(q, mk.reshape(Bp, MEM, D_MODEL), mv.reshape(Bp, MEM, D_MODEL),
                       B=Bp, T=Tp, tq=512, row0=0)
    att_s = _attention(q, cache_mem_k[0].reshape(Bs, MEM, D_MODEL), cache_mem_v[0].reshape(Bs, MEM, D_MODEL),
                       B=Bs, T=Ts, tq=Ts, row0=Np)
    h2, route, counts = _oln(att_p, att_s, h1, xa_wo[0].astype(BF16), ln2_g, ln2_b, wr, br, tm=512)

    pos_flat, items = _moe_schedule(route, counts, TOP_K * NT)
    xs_sorted = _dispatch(pos_flat, h2, TOP_K * NT, tm=256)
    ys_sorted = _experts(items, xs_sorted, e_wg[0], e_wu[0], e_wd[0])
    y_p, y_s = _combine(pos_flat, h2, route, ys_sorted, ln3_g, ln3_b, Np=Np, Ns=Ns, tm=256)

    shp = lambda a, B: a.reshape(1, B, MEM, XA_H, XA_D)
    return (y_p.reshape(Bp, Tp, D_MODEL), y_s.reshape(Bs, Ts, D_MODEL),
            shp(mk, Bp), shp(mv, Bp),
            C_p[None], n_p[None], m_p.reshape(1, Bp, M_H), conv_p[None], S_p[None],
            C_s[None], n_s[None], m_s.reshape(1, Bs, M_H), conv_s[None], S_s[None])
```

```python
import functools
import math

import jax
import jax.numpy as jnp
from jax import lax
from jax.experimental import pallas as pl
from jax.experimental.pallas import tpu as pltpu

F32 = jnp.float32
BF16 = jnp.bfloat16

D_MODEL = 2048
M_W = 1024
M_H = 4
M_DK = 256
M_DV = 256
CONV_K = 4
HG_W = 1024
HG_H = 8
HG_DK = 128
HG_DV = 128
XA_H = 4
XA_D = 512
N_GROUPS = 4
EXP_PER_GROUP = 8
N_EXPERTS = 32
TOP_K = 2
EXP_FF = 512
DEPTH = 1
ALPHA = (2 * DEPTH) ** 0.25
LN_EPS = 1e-5

COL_QK, COL_V, COL_O = 0, 2048, 3072
COL_QH, COL_FH, COL_IH, COL_GH = 4096, 5120, 6144, 7168
COL_GM, COL_GHH = 8192, 10240
N_MAIN = 12288
GATE_LO = 4 * M_W
LANES = 128
SUBLANES = 8
ROUTE_LANE0 = N_GROUPS

VMEM_LIMIT = 56 << 20
MOE_ROW_TILE = 256


def _cparams(sem, vmem=VMEM_LIMIT):
    return pltpu.CompilerParams(dimension_semantics=sem, vmem_limit_bytes=vmem)


def _tile(n, pref):
    t = math.gcd(n, pref)
    assert t % SUBLANES == 0, (n, pref)
    return t


def _bdot(a, b):
    return jnp.dot(a.astype(BF16), b.astype(BF16), preferred_element_type=F32)


def _bdot_nt(a, b):
    return lax.dot_general(a.astype(BF16), b.astype(BF16), (((1,), (1,)), ((), ())),
                           preferred_element_type=F32)


def _bdot_tn(a, b):
    return lax.dot_general(a.astype(BF16), b.astype(BF16), (((0,), (0,)), ((), ())),
                           preferred_element_type=F32)


def _sigmoid(x):
    return 1.0 / (1.0 + jnp.exp(-x))


def _cumsum_rows(x):
    n = x.shape[0]
    row = lax.broadcasted_iota(jnp.int32, x.shape, 0)
    d = 1
    while d < n:
        x = x + jnp.where(row >= d, pltpu.roll(x, d, axis=0), 0.0)
        d *= 2
    return x


def _col_to_row(col, eye):
    return jnp.sum(jnp.where(eye, col, 0.0), axis=0, keepdims=True)


def _row_to_col(row, eye):
    return jnp.sum(jnp.where(eye, row, 0.0), axis=1, keepdims=True)


def _layer_norm(x, g, b):
    mu = jnp.mean(x, axis=-1, keepdims=True)
    xc = x - mu
    var = jnp.mean(xc * xc, axis=-1, keepdims=True)
    return xc * lax.rsqrt(var + LN_EPS) * g + b


def _mm_kernel(x_ref, w_ref, b_ref, o_ref):
    acc = jnp.dot(x_ref[...], w_ref[...], preferred_element_type=F32)
    o_ref[...] = (acc + b_ref[...]).astype(o_ref.dtype)


def _matmul_bias(x, w, b, *, tm, tn, out_dtype=F32, name):
    M, K = x.shape
    N = w.shape[1]
    tm = _tile(M, tm)
    tn = _tile(N, tn)
    return pl.pallas_call(
        _mm_kernel,
        grid=(N // tn, M // tm),
        in_specs=[pl.BlockSpec((tm, K), lambda j, i: (i, 0)),
                  pl.BlockSpec((K, tn), lambda j, i: (0, j)),
                  pl.BlockSpec((1, tn), lambda j, i: (0, j))],
        out_specs=pl.BlockSpec((tm, tn), lambda j, i: (i, j)),
        out_shape=jax.ShapeDtypeStruct((M, N), out_dtype),
        compiler_params=_cparams(("parallel", "parallel")),
        name=name,
    )(x, w, b)


def _inproj_kernel(xp_ref, xs_ref, w_ref, b_ref, wg_ref, bg_ref, u_ref, ug_ref, xb_scr, *, n_prompt_tiles):
    i = pl.program_id(0)

    @pl.when(pl.program_id(1) == 0)
    def _row_tile():
        xb = jnp.where(i < n_prompt_tiles, xp_ref[...], xs_ref[...]).astype(BF16)
        xb_scr[...] = xb
        ug_ref[...] = jnp.dot(xb, wg_ref[...], preferred_element_type=F32) + bg_ref[...]

    u_ref[...] = jnp.dot(xb_scr[...], w_ref[...], preferred_element_type=F32) + b_ref[...]


def _in_proj(x_p, x_s, w_main, b_main, w_gate, b_gate, *, tm, tn):
    Np, Ns = x_p.shape[0], x_s.shape[0]
    NT = Np + Ns
    K, N = w_main.shape
    tm = _tile(math.gcd(Np, Ns), tm)
    tn = _tile(N, tn)
    npt = Np // tm
    return pl.pallas_call(
        functools.partial(_inproj_kernel, n_prompt_tiles=npt),
        grid=(NT // tm, N // tn),
        in_specs=[pl.BlockSpec((tm, K), lambda i, j: (jnp.minimum(i, npt - 1), 0), pipeline_mode=pl.Buffered(1)),
                  pl.BlockSpec((tm, K), lambda i, j: (jnp.maximum(i - npt, 0), 0), pipeline_mode=pl.Buffered(1)),
                  pl.BlockSpec((K, tn), lambda i, j: (0, j)),
                  pl.BlockSpec((1, tn), lambda i, j: (0, j)),
                  pl.BlockSpec((K, LANES), lambda i, j: (0, 0)),
                  pl.BlockSpec((1, LANES), lambda i, j: (0, 0))],
        out_specs=[pl.BlockSpec((tm, tn), lambda i, j: (i, j)),
                   pl.BlockSpec((tm, LANES), lambda i, j: (i, 0))],
        out_shape=[jax.ShapeDtypeStruct((NT, N), F32), jax.ShapeDtypeStruct((NT, LANES), F32)],
        scratch_shapes=[pltpu.VMEM((tm, K), BF16)],
        compiler_params=_cparams(("parallel", "arbitrary")),
        name="in_proj",
    )(x_p, x_s, w_main, b_main, w_gate, b_gate)


def _mlstm_kernel(*refs, L, NC):
    (qk_ref, v_ref, o_ref, g_ref, conv0_ref, C0_ref, n0_ref, m0_ref, cw_ref, gn_ref,
     h_ref, Co_ref, no_ref, mo_ref, convo_ref, C_scr, n_scr, m_scr, tail_scr) = refs
    c = pl.program_id(1)

    @pl.when(c == 0)
    def _init():
        C_scr[...] = C0_ref[0]
        n_scr[...] = n0_ref[0]
        m_scr[...] = m0_ref[0]
        tail_scr[...] = conv0_ref[0]

    qk_pre = qk_ref[...]
    ext = jnp.concatenate([tail_scr[...], qk_pre], axis=0)
    cw = cw_ref[...]
    acc = qk_pre * cw[CONV_K - 1:CONV_K, :]
    for j in range(1, CONV_K):
        acc = acc + pltpu.roll(ext, j, axis=0)[SUBLANES:, :] * cw[CONV_K - 1 - j:CONV_K - j, :]
    tail_scr[...] = qk_pre[L - SUBLANES:, :]
    qk = acc * _sigmoid(acc)

    g = g_ref[...]
    lf_all = jnp.minimum(g, 0.0) - jnp.log(1.0 + jnp.exp(-jnp.abs(g)))
    F_all = _cumsum_rows(lf_all)
    ti = lax.broadcasted_iota(jnp.int32, (L, L), 0)
    si = lax.broadcasted_iota(jnp.int32, (L, L), 1)
    eye = ti == si
    causal = si <= ti
    gn = gn_ref[...]

    for h in range(M_H):
        ks = slice(h * M_DK, (h + 1) * M_DK)
        q = qk[:, ks] * (M_DK ** -0.5)
        k = qk[:, M_W + h * M_DK:M_W + (h + 1) * M_DK]
        v = v_ref[:, ks]
        ig = g[:, h:h + 1]
        F = F_all[:, M_H + h:M_H + h + 1]
        m_prev = m_scr[:, h:h + 1]
        C = C_scr[h]
        n_row = n_scr[h:h + 1, :]

        r_row = _col_to_row(ig - F, eye)
        Dm = jnp.where(causal, F + r_row, -jnp.inf)
        init_w = F + m_prev
        m_t = jnp.maximum(init_w, jnp.max(Dm, axis=1, keepdims=True))
        P = jnp.exp(Dm - m_t)
        a0 = jnp.exp(init_w - m_t)
        Sc = _bdot_nt(q, k) * P
        num = _bdot(Sc, v) + a0 * _bdot(q, C)
        den = jnp.sum(Sc, axis=1, keepdims=True) + a0 * jnp.sum(q * n_row, axis=1, keepdims=True)
        hh = num * (1.0 / jnp.maximum(jnp.abs(den), jnp.exp(-m_t)))

        FL = F[L - 1:L, :]
        mL = m_t[L - 1:L, :]
        wL = jnp.exp(FL - F + ig - mL)
        decay = jnp.exp(FL + m_prev - mL)
        kw = wL * k
        C_scr[h] = decay * C + _bdot_tn(kw, v)
        n_scr[h:h + 1, :] = decay * n_row + jnp.sum(kw, axis=0, keepdims=True)
        m_scr[:, h:h + 1] = mL

        hm = _sigmoid(o_ref[:, ks]) * hh
        hm = hm - jnp.mean(hm, axis=1, keepdims=True)
        hm = hm * lax.rsqrt(jnp.mean(hm * hm, axis=1, keepdims=True) + LN_EPS) * gn[:, ks]
        h_ref[:, ks] = hm

    @pl.when(c == NC - 1)
    def _fin():
        Co_ref[0] = C_scr[...]
        no_ref[0] = n_scr[...]
        mo_ref[0] = m_scr[...]
        convo_ref[0] = qk_pre[L - (CONV_K - 1):, :]


def _mlstm(u_main, u_gate, conv0p, C0, n0, m0, conv_w, gn, *, B, T, L, row0):
    NC = T // L
    rb0 = row0 // L
    tok = lambda b, c: rb0 + b * NC + c
    in_specs = [
        pl.BlockSpec((L, 2 * M_W), lambda b, c: (tok(b, c), COL_QK // (2 * M_W))),
        pl.BlockSpec((L, M_W), lambda b, c: (tok(b, c), COL_V // M_W)),
        pl.BlockSpec((L, M_W), lambda b, c: (tok(b, c), COL_O // M_W)),
        pl.BlockSpec((L, LANES), lambda b, c: (tok(b, c), 0)),
        pl.BlockSpec((1, SUBLANES, 2 * M_W), lambda b, c: (b, 0, 0)),
        pl.BlockSpec((1, M_H, M_DK, M_DV), lambda b, c: (b, 0, 0, 0)),
        pl.BlockSpec((1, M_H, M_DK), lambda b, c: (b, 0, 0)),
        pl.BlockSpec((1, 1, M_H), lambda b, c: (b, 0, 0)),
        pl.BlockSpec((CONV_K, 2 * M_W), lambda b, c: (0, 0)),
        pl.BlockSpec((1, M_W), lambda b, c: (0, 0)),
    ]
    args = [u_main, u_main, u_main, u_gate, conv0p, C0, n0, m0, conv_w, gn]
    out_specs = [
        pl.BlockSpec((L, M_W), lambda b, c: (b * NC + c, 0)),
        pl.BlockSpec((1, M_H, M_DK, M_DV), lambda b, c: (b, 0, 0, 0)),
        pl.BlockSpec((1, M_H, M_DK), lambda b, c: (b, 0, 0)),
        pl.BlockSpec((1, 1, M_H), lambda b, c: (b, 0, 0)),
        pl.BlockSpec((1, CONV_K - 1, 2 * M_W), lambda b, c: (b, 0, 0)),
    ]
    out_shape = [
        jax.ShapeDtypeStruct((B * T, M_W), F32),
        jax.ShapeDtypeStruct((B, M_H, M_DK, M_DV), F32),
        jax.ShapeDtypeStruct((B, M_H, M_DK), F32),
        jax.ShapeDtypeStruct((B, 1, M_H), F32),
        jax.ShapeDtypeStruct((B, CONV_K - 1, 2 * M_W), F32),
    ]
    return pl.pallas_call(
        functools.partial(_mlstm_kernel, L=L, NC=NC),
        grid=(B, NC),
        in_specs=in_specs,
        out_specs=out_specs,
        out_shape=out_shape,
        scratch_shapes=[pltpu.VMEM((M_H, M_DK, M_DV), F32), pltpu.VMEM((M_H, M_DK), F32),
                        pltpu.VMEM((1, M_H), F32), pltpu.VMEM((SUBLANES, 2 * M_W), F32)],
        compiler_params=_cparams(("parallel", "arbitrary")),
        name=f"mlstm_L{L}",
    )(*args)


def _hgrn_kernel(*refs, L, NC):
    (q_ref, f_ref, i_ref, g_ref, S0_ref, lb_ref, gn_ref, o_ref, So_ref, S_scr) = refs
    c = pl.program_id(1)

    @pl.when(c == 0)
    def _init():
        S_scr[...] = S0_ref[0]

    lb = lb_ref[...]
    f = lb + (1.0 - lb) * _sigmoid(f_ref[...])
    kk = 1.0 - f
    b = _cumsum_rows(jnp.log(f))
    qh = q_ref[...]
    q = qh * _sigmoid(qh)
    v = i_ref[...]
    gh = g_ref[...]
    gsilu = gh * _sigmoid(gh)
    gn = gn_ref[...]
    bL = b[L - 1:L, :]
    mid = max(L // 2 - 1, 0)
    bm = b[mid:mid + 1, :]
    q_in = q * jnp.exp(b)
    q_t = q * jnp.exp(b - bm)
    k_t = kk * jnp.exp(bm - b)
    k_st = kk * jnp.exp(bL - b)
    ti = lax.broadcasted_iota(jnp.int32, (L, L), 0)
    si = lax.broadcasted_iota(jnp.int32, (L, L), 1)
    causal = si <= ti
    ci = lax.broadcasted_iota(jnp.int32, (HG_DK, HG_DK), 0)
    cj = lax.broadcasted_iota(jnp.int32, (HG_DK, HG_DK), 1)
    eye = ci == cj

    for h in range(HG_H):
        hs = slice(h * HG_DK, (h + 1) * HG_DK)
        S = S_scr[h]
        A = jnp.where(causal, _bdot_nt(q_t[:, hs], k_t[:, hs]), 0.0)
        o = _bdot(A, v[:, hs]) + _bdot(q_in[:, hs], S)
        dec = jnp.exp(_row_to_col(bL[:, hs], eye))
        S_scr[h] = dec * S + _bdot_tn(k_st[:, hs], v[:, hs])
        o = o * lax.rsqrt(jnp.mean(o * o, axis=1, keepdims=True) + LN_EPS) * gn[:, hs]
        o_ref[:, hs] = o * gsilu[:, hs]

    @pl.when(c == NC - 1)
    def _fin():
        So_ref[0] = S_scr[...]


def _hgrn(u_main, S0, lb, gn, *, B, T, L, row0):
    NC = T // L
    rb0 = row0 // L
    tok = lambda b, c: rb0 + b * NC + c
    col = lambda off: (lambda b, c: (tok(b, c), off // HG_W))
    in_specs = [
        pl.BlockSpec((L, HG_W), col(COL_QH)),
        pl.BlockSpec((L, HG_W), col(COL_FH)),
        pl.BlockSpec((L, HG_W), col(COL_IH)),
        pl.BlockSpec((L, HG_W), col(COL_GH)),
        pl.BlockSpec((1, HG_H, HG_DK, HG_DV), lambda b, c: (b, 0, 0, 0)),
        pl.BlockSpec((1, HG_W), lambda b, c: (0, 0)),
        pl.BlockSpec((1, HG_W), lambda b, c: (0, 0)),
    ]
    args = [u_main, u_main, u_main, u_main, S0, lb, gn]
    return pl.pallas_call(
        functools.partial(_hgrn_kernel, L=L, NC=NC),
        grid=(B, NC),
        in_specs=in_specs,
        out_specs=[pl.BlockSpec((L, HG_W), lambda b, c: (b * NC + c, 0)),
                   pl.BlockSpec((1, HG_H, HG_DK, HG_DV), lambda b, c: (b, 0, 0, 0))],
        out_shape=[jax.ShapeDtypeStruct((B * T, HG_W), F32),
                   jax.ShapeDtypeStruct((B, HG_H, HG_DK, HG_DV), F32)],
        scratch_shapes=[pltpu.VMEM((HG_H, HG_DK, HG_DV), F32)],
        compiler_params=_cparams(("parallel", "arbitrary")),
        name=f"hgrn_L{L}",
    )(*args)


def _merge_kernel(hmp_ref, hms_ref, ogp_ref, ogs_ref, gm_ref, gh_ref, xp_ref, xs_ref, wbm_ref, wbh_ref,
                  wo_ref, g_ref, b_ref, h1_ref, h1b_ref, *, n_prompt_tiles):
    is_prompt = pl.program_id(0) < n_prompt_tiles
    a = _bdot(jnp.where(is_prompt, hmp_ref[...], hms_ref[...]), wbm_ref[...])
    bb = _bdot(jnp.where(is_prompt, ogp_ref[...], ogs_ref[...]), wbh_ref[...])
    merged = _sigmoid(gm_ref[...]) * a + _sigmoid(gh_ref[...]) * bb
    mix = _bdot(merged, wo_ref[...])
    x = jnp.where(is_prompt, xp_ref[...], xs_ref[...])
    h1 = _layer_norm(ALPHA * x + mix, g_ref[...], b_ref[...])
    h1_ref[...] = h1
    h1b_ref[...] = h1.astype(BF16)


def _merge(hm_p, hm_s, og_p, og_s, u_main, x_p, x_s, w_bm, w_bh, w_out, ln_g, ln_b, *, tm):
    Np, Ns = x_p.shape[0], x_s.shape[0]
    NT = Np + Ns
    tm = _tile(math.gcd(Np, Ns), tm)
    npt = Np // tm
    const = lambda i: (0, 0)
    prompt_rows = lambda i: (jnp.minimum(i, npt - 1), 0)
    sample_rows = lambda i: (jnp.maximum(i - npt, 0), 0)
    return pl.pallas_call(
        functools.partial(_merge_kernel, n_prompt_tiles=npt),
        grid=(NT // tm,),
        in_specs=[
            pl.BlockSpec((tm, M_W), prompt_rows),
            pl.BlockSpec((tm, M_W), sample_rows),
            pl.BlockSpec((tm, HG_W), prompt_rows),
            pl.BlockSpec((tm, HG_W), sample_rows),
            pl.BlockSpec((tm, D_MODEL), lambda i: (i, COL_GM // D_MODEL)),
            pl.BlockSpec((tm, D_MODEL), lambda i: (i, COL_GHH // D_MODEL)),
            pl.BlockSpec((tm, D_MODEL), prompt_rows),
            pl.BlockSpec((tm, D_MODEL), sample_rows),
            pl.BlockSpec((M_W, D_MODEL), const, pipeline_mode=pl.Buffered(1)),
            pl.BlockSpec((HG_W, D_MODEL), const, pipeline_mode=pl.Buffered(1)),
            pl.BlockSpec((D_MODEL, D_MODEL), const, pipeline_mode=pl.Buffered(1)),
            pl.BlockSpec((1, D_MODEL), const),
            pl.BlockSpec((1, D_MODEL), const),
        ],
        out_specs=[pl.BlockSpec((tm, D_MODEL), lambda i: (i, 0)),
                   pl.BlockSpec((tm, D_MODEL), lambda i: (i, 0))],
        out_shape=[jax.ShapeDtypeStruct((NT, D_MODEL), F32),
                   jax.ShapeDtypeStruct((NT, D_MODEL), BF16)],
        compiler_params=_cparams(("parallel",)),
        name="merge_ln1",
    )(hm_p, hm_s, og_p, og_s, u_main, u_main, x_p, x_s, w_bm, w_bh, w_out, ln_g, ln_b)


def _attn_kernel(q_ref, k_ref, v_ref, o_ref):
    for h in range(XA_H):
        sl = slice(h * XA_D, (h + 1) * XA_D)
        s = _bdot_nt(q_ref[:, sl], k_ref[0, 0, :, h, :]) * (XA_D ** -0.5)
        e = jnp.exp(s - jnp.max(s, axis=1, keepdims=True))
        p = e / jnp.sum(e, axis=1, keepdims=True)
        o_ref[:, sl] = _bdot(p, v_ref[0, 0, :, h, :])


def _attention(q, mem_k, mem_v, *, B, T, tq, row0):
    tq = _tile(T, tq)
    nq = T // tq
    rb0 = row0 // tq
    M = mem_k.shape[2]
    kv_spec = pl.BlockSpec((1, 1, M, XA_H, XA_D), lambda b, t: (0, b, 0, 0, 0))
    return pl.pallas_call(
        _attn_kernel,
        grid=(B, nq),
        in_specs=[pl.BlockSpec((tq, D_MODEL), lambda b, t: (rb0 + b * nq + t, 0)), kv_spec, kv_spec],
        out_specs=pl.BlockSpec((tq, D_MODEL), lambda b, t: (b * nq + t, 0)),
        out_shape=jax.ShapeDtypeStruct((B * T, D_MODEL), F32),
        compiler_params=_cparams(("parallel", "parallel")),
        name=f"xattn_T{T}",
    )(q, mem_k, mem_v)


def _oln_kernel(op_ref, os_ref, h1_ref, wo_ref, g_ref, b_ref, wr_ref, br_ref,
                h2_ref, route_ref, cnt_ref, carry_scr, *, n_prompt_tiles):
    i = pl.program_id(0)

    @pl.when(i == 0)
    def _init():
        carry_scr[...] = jnp.zeros_like(carry_scr)

    att = jnp.where(i < n_prompt_tiles, op_ref[...], os_ref[...])
    h2 = _layer_norm(ALPHA * h1_ref[...] + _bdot(att, wo_ref[...]), g_ref[...], b_ref[...])
    h2_ref[...] = h2

    logits = _bdot(h2, wr_ref[...]) + br_ref[...]
    tm = logits.shape[0]
    lane = lax.broadcasted_iota(jnp.int32, logits.shape, 1)
    lane_f = lane.astype(F32)
    neg = -jnp.inf

    def first_argmax(vals):
        mx = jnp.max(vals, axis=1, keepdims=True)
        idx = jnp.min(jnp.where(vals == mx, lane_f, float(LANES)), axis=1, keepdims=True)
        return mx, idx.astype(jnp.int32)

    gl = jnp.where(lane < N_GROUPS, logits, neg)
    gmax, grp = first_argmax(gl)
    p_grp = 1.0 / jnp.sum(jnp.exp(gl - gmax), axis=1, keepdims=True)
    lo = ROUTE_LANE0 + grp * EXP_PER_GROUP
    el = jnp.where((lane >= lo) & (lane < lo + EXP_PER_GROUP), logits, neg)
    v1, i1 = first_argmax(el)
    v2, i2 = first_argmax(jnp.where(lane == i1, neg, el))
    e21 = jnp.exp(v2 - v1)
    w1 = p_grp / (1.0 + e21)
    w2 = p_grp * e21 / (1.0 + e21)

    pick1 = lane == i1
    pick2 = lane == i2
    onehot = jnp.where(pick1 | pick2, 1.0, 0.0)
    ti = lax.broadcasted_iota(jnp.int32, (tm, tm), 0)
    si = lax.broadcasted_iota(jnp.int32, (tm, tm), 1)
    tri = jnp.where(si <= ti, 1.0, 0.0)
    cnt = _bdot(tri, onehot) + carry_scr[...]
    r1 = jnp.sum(jnp.where(pick1, cnt, 0.0), axis=1, keepdims=True) - 1.0
    r2 = jnp.sum(jnp.where(pick2, cnt, 0.0), axis=1, keepdims=True) - 1.0
    last = cnt[tm - 1:tm, :]
    carry_scr[...] = last
    cnt_ref[...] = jnp.broadcast_to(last, cnt_ref.shape)

    e1 = (i1 - ROUTE_LANE0).astype(F32)
    e2 = (i2 - ROUTE_LANE0).astype(F32)
    packed = jnp.zeros_like(logits)
    for idx, val in enumerate((e1, e2, w1, w2, r1, r2)):
        packed = jnp.where(lane == idx, val, packed)
    route_ref[...] = packed


def _oln(o_p, o_s, h1, wo, ln_g, ln_b, wr, br, *, tm):
    Np, Ns = o_p.shape[0], o_s.shape[0]
    NT = Np + Ns
    tm = _tile(math.gcd(Np, Ns), tm)
    npt = Np // tm
    const = lambda i: (0, 0)
    return pl.pallas_call(
        functools.partial(_oln_kernel, n_prompt_tiles=npt),
        grid=(NT // tm,),
        in_specs=[pl.BlockSpec((tm, D_MODEL), lambda i: (jnp.minimum(i, npt - 1), 0)),
                  pl.BlockSpec((tm, D_MODEL), lambda i: (jnp.maximum(i - npt, 0), 0)),
                  pl.BlockSpec((tm, D_MODEL), lambda i: (i, 0)),
                  pl.BlockSpec((D_MODEL, D_MODEL), const, pipeline_mode=pl.Buffered(1)),
                  pl.BlockSpec((1, D_MODEL), const),
                  pl.BlockSpec((1, D_MODEL), const),
                  pl.BlockSpec((D_MODEL, LANES), const),
                  pl.BlockSpec((1, LANES), const)],
        out_specs=[pl.BlockSpec((tm, D_MODEL), lambda i: (i, 0)),
                   pl.BlockSpec((tm, LANES), lambda i: (i, 0)),
                   pl.BlockSpec((SUBLANES, LANES), const)],
        out_shape=[jax.ShapeDtypeStruct((NT, D_MODEL), F32),
                   jax.ShapeDtypeStruct((NT, LANES), F32),
                   jax.ShapeDtypeStruct((SUBLANES, LANES), F32)],
        scratch_shapes=[pltpu.VMEM((1, LANES), F32)],
        compiler_params=_cparams(("arbitrary",)),
        name="oproj_ln2_router",
    )(o_p, o_s, h1, wo, ln_g, ln_b, wr, br)


def _row_copy(src_ref, src_row, dst_ref, dst_row, sem):
    return pltpu.make_async_copy(src_ref.at[pl.ds(src_row, 1), :], dst_ref.at[pl.ds(dst_row, 1), :], sem)


def _dispatch_kernel(pos_ref, h_ref, xs_ref, sem, *, tm):
    base = pl.program_id(0) * tm

    def start(r, carry):
        for k in range(TOP_K):
            _row_copy(h_ref, r, xs_ref, pos_ref[TOP_K * (base + r) + k], sem).start()
        return carry

    def wait(r, carry):
        for k in range(TOP_K):
            _row_copy(h_ref, r, xs_ref, pos_ref[TOP_K * (base + r) + k], sem).wait()
        return carry

    lax.fori_loop(0, tm, start, 0)
    lax.fori_loop(0, tm, wait, 0)


def _dispatch(pos_flat, h2, cap, *, tm):
    NT = h2.shape[0]
    tm = _tile(NT, tm)
    return pl.pallas_call(
        functools.partial(_dispatch_kernel, tm=tm),
        grid_spec=pltpu.PrefetchScalarGridSpec(
            num_scalar_prefetch=1,
            grid=(NT // tm,),
            in_specs=[pl.BlockSpec((tm, D_MODEL), lambda i, pos: (i, 0))],
            out_specs=pl.BlockSpec(memory_space=pl.ANY),
            scratch_shapes=[pltpu.SemaphoreType.DMA(())],
        ),
        out_shape=jax.ShapeDtypeStruct((cap, D_MODEL), F32),
        compiler_params=_cparams(("arbitrary",)),
        name="moe_dispatch",
    )(pos_flat, h2)


def _expert_kernel(it_ref, ie_ref, lo_ref, hi_ref, first_ref, n_ref, x_ref, wg_ref, wu_ref, wd_ref, y_ref,
                   wg_s, wu_s, wd_s):
    i = pl.program_id(0)

    @pl.when(i < n_ref[0])
    def _compute():
        fresh = jnp.logical_or(i == 0, ie_ref[i] != ie_ref[jnp.maximum(i - 1, 0)])

        @pl.when(fresh)
        def _cast():
            wg_s[...] = wg_ref[0].astype(BF16)
            wu_s[...] = wu_ref[0].astype(BF16)
            wd_s[...] = wd_ref[0].astype(BF16)

        x = x_ref[...].astype(BF16)
        gate = jnp.dot(x, wg_s[...], preferred_element_type=F32)
        up = jnp.dot(x, wu_s[...], preferred_element_type=F32)
        hid = gate * _sigmoid(gate) * up
        y = jnp.dot(hid.astype(BF16), wd_s[...], preferred_element_type=F32)
        TR = y.shape[0]
        rows = it_ref[i] * TR + lax.broadcasted_iota(jnp.int32, (TR, 1), 0)
        mine = (rows >= lo_ref[i]) & (rows < hi_ref[i])

        @pl.when(first_ref[i] == 1)
        def _first():
            y_ref[...] = jnp.where(mine, y, 0.0)

        @pl.when(first_ref[i] == 0)
        def _later():
            y_ref[...] = jnp.where(mine, y, y_ref[...])


def _experts(items, xs, e_wg, e_wu, e_wd):
    TR = MOE_ROW_TILE
    n_work = items[0].shape[0]
    rows = lambda i, it, ie, lo, hi, fi, n: (it[i], 0)
    wsel = lambda i, it, ie, lo, hi, fi, n: (ie[i], 0, 0)
    return pl.pallas_call(
        _expert_kernel,
        grid_spec=pltpu.PrefetchScalarGridSpec(
            num_scalar_prefetch=6,
            grid=(n_work,),
            in_specs=[pl.BlockSpec((TR, D_MODEL), rows),
                      pl.BlockSpec((1, D_MODEL, EXP_FF), wsel),
                      pl.BlockSpec((1, D_MODEL, EXP_FF), wsel),
                      pl.BlockSpec((1, EXP_FF, D_MODEL), wsel)],
            out_specs=pl.BlockSpec((TR, D_MODEL), rows),
            scratch_shapes=[pltpu.VMEM((D_MODEL, EXP_FF), BF16), pltpu.VMEM((D_MODEL, EXP_FF), BF16),
                            pltpu.VMEM((EXP_FF, D_MODEL), BF16)],
        ),
        out_shape=jax.ShapeDtypeStruct(xs.shape, F32),
        compiler_params=_cparams(("arbitrary",)),
        name="moe_experts",
    )(*items, xs, e_wg, e_wu, e_wd)


def _moe_schedule(route, counts, n_rows):
    TR = MOE_ROW_TILE
    assert n_rows % TR == 0
    n_work = n_rows // TR + N_EXPERTS - 1
    rt = route[:, :SUBLANES].T
    cnt = counts[0, ROUTE_LANE0:ROUTE_LANE0 + N_EXPERTS].astype(jnp.int32)
    g_end = jnp.cumsum(cnt)
    g_start = g_end - cnt
    onehot = rt[0:2, :, None] == jnp.arange(N_EXPERTS, dtype=F32)
    start_of = jnp.sum(jnp.where(onehot, g_start.astype(F32), 0.0), axis=-1)
    pos_flat = (start_of + rt[4:6]).astype(jnp.int32).T.reshape(-1)
    first_tile = g_start // TR
    n_items_e = jnp.where(cnt > 0, (g_end - 1) // TR - first_tile + 1, 0)
    item_end = jnp.cumsum(n_items_e)
    n_items = item_end[-1:]
    idx = jnp.minimum(jnp.arange(n_work, dtype=jnp.int32), n_items[0] - 1)
    ie = jnp.minimum(jnp.sum((item_end[None, :] <= idx[:, None]).astype(jnp.int32), axis=1), N_EXPERTS - 1)
    it = first_tile[ie] + idx - (item_end - n_items_e)[ie]
    first = jnp.concatenate([jnp.ones((1,), jnp.int32), (it[1:] != it[:-1]).astype(jnp.int32)])
    items = (it.astype(jnp.int32), ie, g_start[ie], g_end[ie], first, n_items.astype(jnp.int32))
    return pos_flat, items


def _combine_kernel(pos_ref, h2_ref, route_ref, ys_ref, g_ref, b_ref, yp_ref, ysm_ref,
                    buf, sem, *, tm, n_prompt_tiles):
    i = pl.program_id(0)
    base = i * tm

    def start(r, carry):
        for k in range(TOP_K):
            _row_copy(ys_ref, pos_ref[TOP_K * (base + r) + k], buf.at[k], r, sem).start()
        return carry

    def wait(r, carry):
        for k in range(TOP_K):
            _row_copy(ys_ref, pos_ref[TOP_K * (base + r) + k], buf.at[k], r, sem).wait()
        return carry

    lax.fori_loop(0, tm, start, 0)
    lax.fori_loop(0, tm, wait, 0)
    route = route_ref[...]
    moe = route[:, 2:3] * buf[0] + route[:, 3:4] * buf[1]
    y = _layer_norm(ALPHA * h2_ref[...] + moe, g_ref[...], b_ref[...])

    @pl.when(i < n_prompt_tiles)
    def _prompt():
        yp_ref[...] = y

    @pl.when(i >= n_prompt_tiles)
    def _sample():
        ysm_ref[...] = y


def _combine(pos_flat, h2, route, ys, ln_g, ln_b, *, Np, Ns, tm):
    NT = h2.shape[0]
    tm = _tile(math.gcd(Np, Ns), tm)
    npt = Np // tm
    const = lambda i, pos: (0, 0)
    return pl.pallas_call(
        functools.partial(_combine_kernel, tm=tm, n_prompt_tiles=npt),
        grid_spec=pltpu.PrefetchScalarGridSpec(
            num_scalar_prefetch=1,
            grid=(NT // tm,),
            in_specs=[pl.BlockSpec((tm, D_MODEL), lambda i, pos: (i, 0)),
                      pl.BlockSpec((tm, LANES), lambda i, pos: (i, 0)),
                      pl.BlockSpec(memory_space=pl.ANY),
                      pl.BlockSpec((1, D_MODEL), const),
                      pl.BlockSpec((1, D_MODEL), const)],
            out_specs=[pl.BlockSpec((tm, D_MODEL), lambda i, pos: (jnp.minimum(i, npt - 1), 0)),
                       pl.BlockSpec((tm, D_MODEL), lambda i, pos: (jnp.maximum(i - npt, 0), 0))],
            scratch_shapes=[pltpu.VMEM((TOP_K, tm, D_MODEL), F32), pltpu.SemaphoreType.DMA(())],
        ),
        out_shape=[jax.ShapeDtypeStruct((Np, D_MODEL), F32),
                   jax.ShapeDtypeStruct((Ns, D_MODEL), F32)],
        compiler_params=_cparams(("arbitrary",)),
        name="moe_combine_ln3",
    )(pos_flat, h2, route, ys, ln_g, ln_b)


def kernel(x_prompt, x_sample, mem_prompt, cache_mem_k, cache_mem_v, state_mlstm_C, state_mlstm_n,
           state_mlstm_m, state_mlstm_conv, state_hgrn_S, w_in, b_in, conv_w, mlstm_gn, lb_logits, hgrn_gn,
           w_bm, w_bh, w_out, ln1_g, ln1_b, xa_wq, xa_wk, xa_wv, xa_wo, ln2_g, ln2_b,
           r1_w, r1_b, r2_w, r2_b, e_wg, e_wu, e_wd, ln3_g, ln3_b):
    Bp, Tp, _ = x_prompt.shape
    Bs, Ts, _ = x_sample.shape
    MEM = mem_prompt.shape[1]
    Np, Ns = Bp * Tp, Bs * Ts
    NT = Np + Ns
    Lp_m = math.gcd(Tp, 256)
    Lp_h = math.gcd(Tp, 64)
    Ls = Ts
    assert Ts % SUBLANES == 0 and Np % Ls == 0

    xp2 = x_prompt.reshape(Np, D_MODEL)
    xs2 = x_sample.reshape(Ns, D_MODEL)

    w = w_in[0]
    bi = b_in[0]
    w_main = jnp.concatenate([w[:, :GATE_LO], w[:, GATE_LO + 2 * M_H:]], axis=1).astype(BF16)
    b_main = jnp.concatenate([bi[:GATE_LO], bi[GATE_LO + 2 * M_H:]])[None]
    w_gate = jnp.pad(w[:, GATE_LO:GATE_LO + 2 * M_H], ((0, 0), (0, LANES - 2 * M_H))).astype(BF16)
    b_gate = jnp.pad(bi[GATE_LO:GATE_LO + 2 * M_H], (0, LANES - 2 * M_H))[None]
    lb = jnp.cumsum(jax.nn.softmax(lb_logits.astype(F32), axis=0), axis=0)[0][None]
    wr = jnp.zeros((D_MODEL, LANES), F32)
    wr = wr.at[:, :N_GROUPS].set(r1_w[0])
    wr = wr.at[:, ROUTE_LANE0:ROUTE_LANE0 + N_EXPERTS].set(
        jnp.transpose(r2_w[0], (1, 0, 2)).reshape(D_MODEL, N_EXPERTS)).astype(BF16)
    br = jnp.zeros((1, LANES), F32)
    br = br.at[0, :N_GROUPS].set(r1_b[0])
    br = br.at[0, ROUTE_LANE0:ROUTE_LANE0 + N_EXPERTS].set(r2_b[0].reshape(N_EXPERTS))
    zeros_d = jnp.zeros((1, D_MODEL), F32)

    memb = mem_prompt.reshape(Bp * MEM, D_MODEL).astype(BF16)
    kv5 = lambda a: a.reshape(1, Bp, MEM, XA_H, XA_D)
    mk = kv5(_matmul_bias(memb, xa_wk[0].astype(BF16), zeros_d, tm=1024, tn=1024, name="mem_k"))
    mv = kv5(_matmul_bias(memb, xa_wv[0].astype(BF16), zeros_d, tm=1024, tn=1024, name="mem_v"))

    u_main, u_gate = _in_proj(xp2, xs2, w_main, b_main, w_gate, b_gate, tm=1024, tn=1024)

    padc = lambda c: jnp.pad(c, ((0, 0), (SUBLANES - (CONV_K - 1), 0), (0, 0)))
    zC = jnp.zeros((Bp, M_H, M_DK, M_DV), F32)
    zn = jnp.zeros((Bp, M_H, M_DK), F32)
    zm = jnp.zeros((Bp, 1, M_H), F32)
    zconv = jnp.zeros((Bp, SUBLANES, 2 * M_W), F32)
    zS = jnp.zeros((Bp, HG_H, HG_DK, HG_DV), F32)
    cw = conv_w[0]
    mgn = mlstm_gn[0][None]
    hgn = hgrn_gn[0][None]
    hm_p, C_p, n_p, m_p, conv_p = _mlstm(u_main, u_gate, zconv, zC, zn, zm, cw, mgn,
                                         B=Bp, T=Tp, L=Lp_m, row0=0)
    hm_s, C_s, n_s, m_s, conv_s = _mlstm(u_main, u_gate, padc(state_mlstm_conv[0]), state_mlstm_C[0],
                                         state_mlstm_n[0], state_mlstm_m[0][:, None, :], cw, mgn,
                                         B=Bs, T=Ts, L=Ls, row0=Np)
    og_p, S_p = _hgrn(u_main, zS, lb, hgn, B=Bp, T=Tp, L=Lp_h, row0=0)
    og_s, S_s = _hgrn(u_main, state_hgrn_S[0], lb, hgn, B=Bs, T=Ts, L=Ls, row0=Np)

    h1, h1b = _merge(hm_p, hm_s, og_p, og_s, u_main, xp2, xs2, w_bm[0].astype(BF16), w_bh[0].astype(BF16),
                     w_out[0].astype(BF16), ln1_g, ln1_b, tm=256)

    q = _matmul_bias(h1b, xa_wq[0].astype(BF16), zeros_d, tm=1024, tn=1024, name="xa_q")
    att_p = _attention(q, mk, mv, B=Bp, T=Tp, tq=512, row0=0)
    att_s = _attention(q, cache_mem_k, cache_mem_v, B=Bs, T=Ts, tq=Ts, row0=Np)
    h2, route, counts = _oln(att_p, att_s, h1, xa_wo[0].astype(BF16), ln2_g, ln2_b, wr, br, tm=512)

    pos_flat, items = _moe_schedule(route, counts, TOP_K * NT)
    xs_sorted = _dispatch(pos_flat, h2, TOP_K * NT, tm=256)
    ys_sorted = _experts(items, xs_sorted, e_wg[0], e_wu[0], e_wd[0])
    y_p, y_s = _combine(pos_flat, h2, route, ys_sorted, ln3_g, ln3_b, Np=Np, Ns=Ns, tm=256)

    return (y_p.reshape(Bp, Tp, D_MODEL), y_s.reshape(Bs, Ts, D_MODEL), mk, mv,
            C_p[None], n_p[None], m_p.reshape(1, Bp, M_H), conv_p[None], S_p[None],
            C_s[None], n_s[None], m_s.reshape(1, Bs, M_H), conv_s[None], S_s[None])
```

```python
import functools
import math

import jax
import jax.numpy as jnp
from jax import lax
from jax.experimental import pallas as pl
from jax.experimental.pallas import tpu as pltpu

F32 = jnp.float32
BF16 = jnp.bfloat16

D_MODEL = 2048
M_W = 1024
M_H = 4
M_DK = 256
M_DV = 256
CONV_K = 4
HG_W = 1024
HG_H = 8
HG_DK = 128
HG_DV = 128
XA_H = 4
XA_D = 512
N_GROUPS = 4
EXP_PER_GROUP = 8
N_EXPERTS = 32
TOP_K = 2
EXP_FF = 512
DEPTH = 1
ALPHA = (2 * DEPTH) ** 0.25
LN_EPS = 1e-5

COL_QK, COL_V, COL_O = 0, 2048, 3072
COL_QH, COL_FH, COL_IH, COL_GH = 4096, 5120, 6144, 7168
COL_GM, COL_GHH = 8192, 10240
N_MAIN = 12288
GATE_LO = 4 * M_W
LANES = 128
SUBLANES = 8
ROUTE_LANE0 = N_GROUPS

VMEM_LIMIT = 56 << 20
MOE_ROW_TILE = 256
DMA_UNROLL = 8


def _cparams(sem, vmem=VMEM_LIMIT):
    return pltpu.CompilerParams(dimension_semantics=sem, vmem_limit_bytes=vmem)


def _tile(n, pref):
    t = math.gcd(n, pref)
    assert t % SUBLANES == 0, (n, pref)
    return t


def _bdot(a, b):
    return jnp.dot(a.astype(BF16), b.astype(BF16), preferred_element_type=F32)


def _bdot_nt(a, b):
    return lax.dot_general(a.astype(BF16), b.astype(BF16), (((1,), (1,)), ((), ())),
                           preferred_element_type=F32)


def _bdot_tn(a, b):
    return lax.dot_general(a.astype(BF16), b.astype(BF16), (((0,), (0,)), ((), ())),
                           preferred_element_type=F32)


def _sigmoid(x):
    return 1.0 / (1.0 + jnp.exp(-x))


def _cumsum_rows(x):
    n = x.shape[0]
    row = lax.broadcasted_iota(jnp.int32, x.shape, 0)
    d = 1
    while d < n:
        x = x + jnp.where(row >= d, pltpu.roll(x, d, axis=0), 0.0)
        d *= 2
    return x


def _col_to_row(col, eye):
    return jnp.sum(jnp.where(eye, col, 0.0), axis=0, keepdims=True)


def _row_to_col(row, eye):
    return jnp.sum(jnp.where(eye, row, 0.0), axis=1, keepdims=True)


def _layer_norm(x, g, b):
    mu = jnp.mean(x, axis=-1, keepdims=True)
    xc = x - mu
    var = jnp.mean(xc * xc, axis=-1, keepdims=True)
    return xc * lax.rsqrt(var + LN_EPS) * g + b


def _mm_kernel(x_ref, w_ref, b_ref, o_ref):
    acc = jnp.dot(x_ref[...], w_ref[...], preferred_element_type=F32)
    o_ref[...] = (acc + b_ref[...]).astype(o_ref.dtype)


def _matmul_bias(x, w, b, *, tm, tn, out_dtype=F32, name):
    M, K = x.shape
    N = w.shape[1]
    tm = _tile(M, tm)
    tn = _tile(N, tn)
    return pl.pallas_call(
        _mm_kernel,
        grid=(N // tn, M // tm),
        in_specs=[pl.BlockSpec((tm, K), lambda j, i: (i, 0)),
                  pl.BlockSpec((K, tn), lambda j, i: (0, j)),
                  pl.BlockSpec((1, tn), lambda j, i: (0, j))],
        out_specs=pl.BlockSpec((tm, tn), lambda j, i: (i, j)),
        out_shape=jax.ShapeDtypeStruct((M, N), out_dtype),
        compiler_params=_cparams(("parallel", "parallel")),
        name=name,
    )(x, w, b)


def _inproj_kernel(xp_ref, xs_ref, wa_ref, wb_ref, b_ref, wg_ref, bg_ref, u_ref, ug_ref, xb_scr, *,
                   n_prompt_tiles, n_a_tiles):
    i = pl.program_id(0)
    j = pl.program_id(1)

    @pl.when(j == 0)
    def _row_tile():
        xb = jnp.where(i < n_prompt_tiles, xp_ref[...], xs_ref[...]).astype(BF16)
        xb_scr[...] = xb
        ug_ref[...] = jnp.dot(xb, wg_ref[...], preferred_element_type=F32) + bg_ref[...]

    @pl.when(j < n_a_tiles)
    def _lead():
        u_ref[...] = jnp.dot(xb_scr[...], wa_ref[...], preferred_element_type=F32) + b_ref[...]

    @pl.when(j >= n_a_tiles)
    def _rest():
        u_ref[...] = jnp.dot(xb_scr[...], wb_ref[...], preferred_element_type=F32) + b_ref[...]


def _in_proj(x_p, x_s, w_all, w_rest, b_main, w_gate, b_gate, *, tm, tn):
    Np, Ns = x_p.shape[0], x_s.shape[0]
    NT = Np + Ns
    K = w_all.shape[0]
    N = GATE_LO + w_rest.shape[1]
    tm = _tile(math.gcd(Np, Ns), tm)
    tn = _tile(math.gcd(GATE_LO, w_rest.shape[1]), tn)
    npt = Np // tm
    na = GATE_LO // tn
    return pl.pallas_call(
        functools.partial(_inproj_kernel, n_prompt_tiles=npt, n_a_tiles=na),
        grid=(NT // tm, N // tn),
        in_specs=[pl.BlockSpec((tm, K), lambda i, j: (jnp.minimum(i, npt - 1), 0), pipeline_mode=pl.Buffered(1)),
                  pl.BlockSpec((tm, K), lambda i, j: (jnp.maximum(i - npt, 0), 0), pipeline_mode=pl.Buffered(1)),
                  pl.BlockSpec((K, tn), lambda i, j: (0, jnp.minimum(j, na - 1))),
                  pl.BlockSpec((K, tn), lambda i, j: (0, jnp.maximum(j - na, 0))),
                  pl.BlockSpec((1, tn), lambda i, j: (0, j)),
                  pl.BlockSpec((K, LANES), lambda i, j: (0, 0)),
                  pl.BlockSpec((1, LANES), lambda i, j: (0, 0))],
        out_specs=[pl.BlockSpec((tm, tn), lambda i, j: (i, j)),
                   pl.BlockSpec((tm, LANES), lambda i, j: (i, 0))],
        out_shape=[jax.ShapeDtypeStruct((NT, N), F32), jax.ShapeDtypeStruct((NT, LANES), F32)],
        scratch_shapes=[pltpu.VMEM((tm, K), BF16)],
        compiler_params=_cparams(("parallel", "arbitrary")),
        name="in_proj",
    )(x_p, x_s, w_all, w_rest, b_main, w_gate, b_gate)


def _mlstm_kernel(*refs, L, NC):
    (qk_ref, v_ref, o_ref, g_ref, conv0_ref, C0_ref, n0_ref, m0_ref, cw_ref, gn_ref,
     h_ref, Co_ref, no_ref, mo_ref, convo_ref, C_scr, n_scr, m_scr, tail_scr) = refs
    c = pl.program_id(1)
    if NC == 1:
        C_in, n_in, m_in, tail_in = C0_ref.at[0], n0_ref.at[0], m0_ref.at[0], conv0_ref.at[0]
        C_out, n_out, m_out = Co_ref.at[0], no_ref.at[0], mo_ref.at[0]
    else:
        C_in = C_out = C_scr
        n_in = n_out = n_scr
        m_in = m_out = m_scr
        tail_in = tail_scr

        @pl.when(c == 0)
        def _init():
            C_scr[...] = C0_ref[0]
            n_scr[...] = n0_ref[0]
            m_scr[...] = m0_ref[0]
            tail_scr[...] = conv0_ref[0]

    qk_pre = qk_ref[...]
    ext = jnp.concatenate([tail_in[...], qk_pre], axis=0)
    cw = cw_ref[...]
    acc = qk_pre * cw[CONV_K - 1:CONV_K, :]
    for j in range(1, CONV_K):
        acc = acc + pltpu.roll(ext, j, axis=0)[SUBLANES:, :] * cw[CONV_K - 1 - j:CONV_K - j, :]
    if NC > 1:
        tail_scr[...] = qk_pre[L - SUBLANES:, :]
    qk = acc * _sigmoid(acc)

    g = g_ref[...]
    lf_all = jnp.minimum(g, 0.0) - jnp.log(1.0 + jnp.exp(-jnp.abs(g)))
    F_all = _cumsum_rows(lf_all)
    ti = lax.broadcasted_iota(jnp.int32, (L, L), 0)
    si = lax.broadcasted_iota(jnp.int32, (L, L), 1)
    eye = ti == si
    causal = si <= ti
    gn = gn_ref[...]
    m_all = m_in[...]
    n_all = n_in[...]
    head_lane = lax.broadcasted_iota(jnp.int32, m_all.shape, 1)
    head_row = lax.broadcasted_iota(jnp.int32, n_all.shape, 0)
    m_new = m_all
    n_new = n_all

    heads = []
    for h in range(M_H):
        ks = slice(h * M_DK, (h + 1) * M_DK)
        q = qk[:, ks] * (M_DK ** -0.5)
        k = qk[:, M_W + h * M_DK:M_W + (h + 1) * M_DK]
        v = v_ref[:, ks]
        ig = g[:, h:h + 1]
        F = F_all[:, M_H + h:M_H + h + 1]
        m_prev = m_all[:, h:h + 1]
        r_row = _col_to_row(ig - F, eye)
        Dm = jnp.where(causal, F + r_row, -jnp.inf)
        init_w = F + m_prev
        m_t = jnp.maximum(init_w, jnp.max(Dm, axis=1, keepdims=True))
        P = jnp.exp(Dm - m_t)
        a0 = jnp.exp(init_w - m_t)
        FL = F[L - 1:L, :]
        mL = m_t[L - 1:L, :]
        wL = jnp.exp(FL - F + ig - mL)
        decay = jnp.exp(FL + m_prev - mL)
        heads.append(dict(ks=ks, q=q, k=k, v=v, m_t=m_t, P=P, a0=a0, mL=mL, decay=decay, kw=wL * k))

    for h, d in enumerate(heads):
        d["S"] = _bdot_nt(d["q"], d["k"])
        d["qC"] = _bdot(d["q"], C_in[h])
        d["kv"] = _bdot_tn(d["kw"], d["v"])

    for d in heads:
        d["Sc"] = d["S"] * d["P"]
        d["num"] = _bdot(d["Sc"], d["v"])

    for h, d in enumerate(heads):
        ks, q, a0, decay = d["ks"], d["q"], d["a0"], d["decay"]
        n_row = n_all[h:h + 1, :]
        num = d["num"] + a0 * d["qC"]
        den = jnp.sum(d["Sc"], axis=1, keepdims=True) + a0 * jnp.sum(q * n_row, axis=1, keepdims=True)
        hh = num * (1.0 / jnp.maximum(jnp.abs(den), jnp.exp(-d["m_t"])))
        C_out[h] = decay * C_in[h] + d["kv"]
        n_new = jnp.where(head_row == h, decay * n_row + jnp.sum(d["kw"], axis=0, keepdims=True), n_new)
        m_new = jnp.where(head_lane == h, d["mL"], m_new)

        hm = _sigmoid(o_ref[:, ks]) * hh
        hm = hm - jnp.mean(hm, axis=1, keepdims=True)
        hm = hm * lax.rsqrt(jnp.mean(hm * hm, axis=1, keepdims=True) + LN_EPS) * gn[:, ks]
        h_ref[:, ks] = hm

    n_out[...] = n_new
    m_out[...] = m_new
    if NC == 1:
        convo_ref[0] = qk_pre[L - (CONV_K - 1):, :]
    else:
        @pl.when(c == NC - 1)
        def _fin():
            Co_ref[0] = C_scr[...]
            no_ref[0] = n_new
            mo_ref[0] = m_new
            convo_ref[0] = qk_pre[L - (CONV_K - 1):, :]


def _mlstm(u_main, u_gate, conv0p, C0, n0, m0, conv_w, gn, *, B, T, L, row0):
    NC = T // L
    rb0 = row0 // L
    tok = lambda b, c: rb0 + b * NC + c
    in_specs = [
        pl.BlockSpec((L, 2 * M_W), lambda b, c: (tok(b, c), COL_QK // (2 * M_W))),
        pl.BlockSpec((L, M_W), lambda b, c: (tok(b, c), COL_V // M_W)),
        pl.BlockSpec((L, M_W), lambda b, c: (tok(b, c), COL_O // M_W)),
        pl.BlockSpec((L, LANES), lambda b, c: (tok(b, c), 0)),
        pl.BlockSpec((1, SUBLANES, 2 * M_W), lambda b, c: (b, 0, 0)),
        pl.BlockSpec((1, M_H, M_DK, M_DV), lambda b, c: (b, 0, 0, 0)),
        pl.BlockSpec((1, M_H, M_DK), lambda b, c: (b, 0, 0)),
        pl.BlockSpec((1, 1, M_H), lambda b, c: (b, 0, 0)),
        pl.BlockSpec((CONV_K, 2 * M_W), lambda b, c: (0, 0)),
        pl.BlockSpec((1, M_W), lambda b, c: (0, 0)),
    ]
    args = [u_main, u_main, u_main, u_gate, conv0p, C0, n0, m0, conv_w, gn]
    out_specs = [
        pl.BlockSpec((L, M_W), lambda b, c: (b * NC + c, 0)),
        pl.BlockSpec((1, M_H, M_DK, M_DV), lambda b, c: (b, 0, 0, 0)),
        pl.BlockSpec((1, M_H, M_DK), lambda b, c: (b, 0, 0)),
        pl.BlockSpec((1, 1, M_H), lambda b, c: (b, 0, 0)),
        pl.BlockSpec((1, CONV_K - 1, 2 * M_W), lambda b, c: (b, 0, 0)),
    ]
    out_shape = [
        jax.ShapeDtypeStruct((B * T, M_W), F32),
        jax.ShapeDtypeStruct((B, M_H, M_DK, M_DV), F32),
        jax.ShapeDtypeStruct((B, M_H, M_DK), F32),
        jax.ShapeDtypeStruct((B, 1, M_H), F32),
        jax.ShapeDtypeStruct((B, CONV_K - 1, 2 * M_W), F32),
    ]
    return pl.pallas_call(
        functools.partial(_mlstm_kernel, L=L, NC=NC),
        grid=(B, NC),
        in_specs=in_specs,
        out_specs=out_specs,
        out_shape=out_shape,
        scratch_shapes=[pltpu.VMEM((M_H, M_DK, M_DV), F32), pltpu.VMEM((M_H, M_DK), F32),
                        pltpu.VMEM((1, M_H), F32), pltpu.VMEM((SUBLANES, 2 * M_W), F32)],
        compiler_params=_cparams(("parallel", "arbitrary")),
        name=f"mlstm_L{L}",
    )(*args)


def _hgrn_kernel(*refs, L, NC):
    (q_ref, f_ref, i_ref, g_ref, S0_ref, lb_ref, gn_ref, o_ref, So_ref, S_scr) = refs
    c = pl.program_id(1)
    if NC == 1:
        S_in, S_out = S0_ref.at[0], So_ref.at[0]
    else:
        S_in = S_out = S_scr

        @pl.when(c == 0)
        def _init():
            S_scr[...] = S0_ref[0]

    lb = lb_ref[...]
    f = lb + (1.0 - lb) * _sigmoid(f_ref[...])
    kk = 1.0 - f
    b = _cumsum_rows(jnp.log(f))
    qh = q_ref[...]
    q = qh * _sigmoid(qh)
    v = i_ref[...]
    gh = g_ref[...]
    gsilu = gh * _sigmoid(gh)
    gn = gn_ref[...]
    bL = b[L - 1:L, :]
    mid = max(L // 2 - 1, 0)
    bm = b[mid:mid + 1, :]
    q_in = q * jnp.exp(b)
    q_t = q * jnp.exp(b - bm)
    k_t = kk * jnp.exp(bm - b)
    k_st = kk * jnp.exp(bL - b)
    ti = lax.broadcasted_iota(jnp.int32, (L, L), 0)
    si = lax.broadcasted_iota(jnp.int32, (L, L), 1)
    causal = si <= ti
    ci = lax.broadcasted_iota(jnp.int32, (HG_DK, HG_DK), 0)
    cj = lax.broadcasted_iota(jnp.int32, (HG_DK, HG_DK), 1)
    eye = ci == cj

    hsl = [slice(h * HG_DK, (h + 1) * HG_DK) for h in range(HG_H)]
    A = [jnp.where(causal, _bdot_nt(q_t[:, hs], k_t[:, hs]), 0.0) for hs in hsl]
    qS = [_bdot(q_in[:, hs], S_in[h]) for h, hs in enumerate(hsl)]
    kv = [_bdot_tn(k_st[:, hs], v[:, hs]) for hs in hsl]
    o_intra = [_bdot(A[h], v[:, hs]) for h, hs in enumerate(hsl)]
    for h, hs in enumerate(hsl):
        o = o_intra[h] + qS[h]
        dec = jnp.exp(_row_to_col(bL[:, hs], eye))
        S_out[h] = dec * S_in[h] + kv[h]
        o = o * lax.rsqrt(jnp.mean(o * o, axis=1, keepdims=True) + LN_EPS) * gn[:, hs]
        o_ref[:, hs] = o * gsilu[:, hs]

    if NC > 1:
        @pl.when(c == NC - 1)
        def _fin():
            So_ref[0] = S_scr[...]


def _hgrn(u_main, S0, lb, gn, *, B, T, L, row0):
    NC = T // L
    rb0 = row0 // L
    tok = lambda b, c: rb0 + b * NC + c
    col = lambda off: (lambda b, c: (tok(b, c), off // HG_W))
    in_specs = [
        pl.BlockSpec((L, HG_W), col(COL_QH)),
        pl.BlockSpec((L, HG_W), col(COL_FH)),
        pl.BlockSpec((L, HG_W), col(COL_IH)),
        pl.BlockSpec((L, HG_W), col(COL_GH)),
        pl.BlockSpec((1, HG_H, HG_DK, HG_DV), lambda b, c: (b, 0, 0, 0)),
        pl.BlockSpec((1, HG_W), lambda b, c: (0, 0)),
        pl.BlockSpec((1, HG_W), lambda b, c: (0, 0)),
    ]
    args = [u_main, u_main, u_main, u_main, S0, lb, gn]
    return pl.pallas_call(
        functools.partial(_hgrn_kernel, L=L, NC=NC),
        grid=(B, NC),
        in_specs=in_specs,
        out_specs=[pl.BlockSpec((L, HG_W), lambda b, c: (b * NC + c, 0)),
                   pl.BlockSpec((1, HG_H, HG_DK, HG_DV), lambda b, c: (b, 0, 0, 0))],
        out_shape=[jax.ShapeDtypeStruct((B * T, HG_W), F32),
                   jax.ShapeDtypeStruct((B, HG_H, HG_DK, HG_DV), F32)],
        scratch_shapes=[pltpu.VMEM((HG_H, HG_DK, HG_DV), F32)],
        compiler_params=_cparams(("parallel", "arbitrary")),
        name=f"hgrn_L{L}",
    )(*args)


def _merge_kernel(hmp_ref, hms_ref, ogp_ref, ogs_ref, gm_ref, gh_ref, xp_ref, xs_ref, wbm_ref, wbh_ref,
                  wo_ref, g_ref, b_ref, h1_ref, h1b_ref, *, n_prompt_tiles):
    is_prompt = pl.program_id(0) < n_prompt_tiles
    a = _bdot(jnp.where(is_prompt, hmp_ref[...], hms_ref[...]), wbm_ref[...])
    bb = _bdot(jnp.where(is_prompt, ogp_ref[...], ogs_ref[...]), wbh_ref[...])
    merged = _sigmoid(gm_ref[...]) * a + _sigmoid(gh_ref[...]) * bb
    mix = _bdot(merged, wo_ref[...])
    x = jnp.where(is_prompt, xp_ref[...], xs_ref[...])
    h1 = _layer_norm(ALPHA * x + mix, g_ref[...], b_ref[...])
    h1_ref[...] = h1
    h1b_ref[...] = h1.astype(BF16)


def _merge(hm_p, hm_s, og_p, og_s, u_main, x_p, x_s, w_bm, w_bh, w_out, ln_g, ln_b, *, tm):
    Np, Ns = x_p.shape[0], x_s.shape[0]
    NT = Np + Ns
    tm = _tile(math.gcd(Np, Ns), tm)
    npt = Np // tm
    const = lambda i: (0, 0)
    prompt_rows = lambda i: (jnp.minimum(i, npt - 1), 0)
    sample_rows = lambda i: (jnp.maximum(i - npt, 0), 0)
    return pl.pallas_call(
        functools.partial(_merge_kernel, n_prompt_tiles=npt),
        grid=(NT // tm,),
        in_specs=[
            pl.BlockSpec((tm, M_W), prompt_rows),
            pl.BlockSpec((tm, M_W), sample_rows),
            pl.BlockSpec((tm, HG_W), prompt_rows),
            pl.BlockSpec((tm, HG_W), sample_rows),
            pl.BlockSpec((tm, D_MODEL), lambda i: (i, COL_GM // D_MODEL)),
            pl.BlockSpec((tm, D_MODEL), lambda i: (i, COL_GHH // D_MODEL)),
            pl.BlockSpec((tm, D_MODEL), prompt_rows),
            pl.BlockSpec((tm, D_MODEL), sample_rows),
            pl.BlockSpec((M_W, D_MODEL), const, pipeline_mode=pl.Buffered(1)),
            pl.BlockSpec((HG_W, D_MODEL), const, pipeline_mode=pl.Buffered(1)),
            pl.BlockSpec((D_MODEL, D_MODEL), const, pipeline_mode=pl.Buffered(1)),
            pl.BlockSpec((1, D_MODEL), const),
            pl.BlockSpec((1, D_MODEL), const),
        ],
        out_specs=[pl.BlockSpec((tm, D_MODEL), lambda i: (i, 0)),
                   pl.BlockSpec((tm, D_MODEL), lambda i: (i, 0))],
        out_shape=[jax.ShapeDtypeStruct((NT, D_MODEL), F32),
                   jax.ShapeDtypeStruct((NT, D_MODEL), BF16)],
        compiler_params=_cparams(("parallel",)),
        name="merge_ln1",
    )(hm_p, hm_s, og_p, og_s, u_main, u_main, x_p, x_s, w_bm, w_bh, w_out, ln_g, ln_b)


def _attn_kernel(q_ref, *refs):
    k_refs, v_refs, o_ref = refs[:XA_H], refs[XA_H:2 * XA_H], refs[2 * XA_H]
    for h in range(XA_H):
        sl = slice(h * XA_D, (h + 1) * XA_D)
        s = _bdot_nt(q_ref[:, sl], k_refs[h][...]) * (XA_D ** -0.5)
        e = jnp.exp(s - jnp.max(s, axis=1, keepdims=True))
        p = e / jnp.sum(e, axis=1, keepdims=True)
        o_ref[:, sl] = _bdot(p, v_refs[h][...])


def _kv_head_copies(k_hbm, v_hbm, kbuf, vbuf, sem, b, slot):
    out = []
    for j, (src, dst) in enumerate(((k_hbm, kbuf), (v_hbm, vbuf))):
        for h in range(XA_H):
            out.append(pltpu.make_async_copy(src.at[0, b, :, h, :], dst.at[slot, h], sem.at[slot, j, h]))
    return out


def _attn_cache_kernel(q_ref, k_hbm, v_hbm, o_ref, kbuf, vbuf, sem):
    b = pl.program_id(0)
    slot = b % 2

    @pl.when(b == 0)
    def _first():
        for c in _kv_head_copies(k_hbm, v_hbm, kbuf, vbuf, sem, b, slot):
            c.start()

    @pl.when(b + 1 < pl.num_programs(0))
    def _prefetch():
        for c in _kv_head_copies(k_hbm, v_hbm, kbuf, vbuf, sem, b + 1, 1 - slot):
            c.start()

    for c in _kv_head_copies(k_hbm, v_hbm, kbuf, vbuf, sem, b, slot):
        c.wait()
    T = q_ref.shape[0]
    s = jnp.concatenate([_bdot_nt(q_ref[:, h * XA_D:(h + 1) * XA_D], kbuf[slot, h]) for h in range(XA_H)],
                        axis=0) * (XA_D ** -0.5)
    e = jnp.exp(s - jnp.max(s, axis=1, keepdims=True))
    p = e / jnp.sum(e, axis=1, keepdims=True)
    for h in range(XA_H):
        o_ref[:, h * XA_D:(h + 1) * XA_D] = _bdot(p[h * T:(h + 1) * T], vbuf[slot, h])


def _attention_cache(q, cache_k, cache_v, *, B, T, row0):
    M = cache_k.shape[2]
    rb0 = row0 // T
    return pl.pallas_call(
        _attn_cache_kernel,
        grid=(B,),
        in_specs=[pl.BlockSpec((T, D_MODEL), lambda b: (rb0 + b, 0)),
                  pl.BlockSpec(memory_space=pl.ANY), pl.BlockSpec(memory_space=pl.ANY)],
        out_specs=pl.BlockSpec((T, D_MODEL), lambda b: (b, 0)),
        out_shape=jax.ShapeDtypeStruct((B * T, D_MODEL), F32),
        scratch_shapes=[pltpu.VMEM((2, XA_H, M, XA_D), F32), pltpu.VMEM((2, XA_H, M, XA_D), F32),
                        pltpu.SemaphoreType.DMA((2, 2, XA_H))],
        compiler_params=_cparams(("arbitrary",)),
        name="xattn_cache",
    )(q, cache_k, cache_v)


def _attention(q, mem_k, mem_v, *, B, T, tq, row0):
    tq = _tile(T, tq)
    nq = T // tq
    rb0 = row0 // tq
    M = mem_k.shape[0] // B
    kv_specs = [pl.BlockSpec((M, XA_D), lambda b, t, h=h: (b, h)) for h in range(XA_H)]
    return pl.pallas_call(
        _attn_kernel,
        grid=(B, nq),
        in_specs=[pl.BlockSpec((tq, D_MODEL), lambda b, t: (rb0 + b * nq + t, 0))] + kv_specs + kv_specs,
        out_specs=pl.BlockSpec((tq, D_MODEL), lambda b, t: (b * nq + t, 0)),
        out_shape=jax.ShapeDtypeStruct((B * T, D_MODEL), F32),
        compiler_params=_cparams(("parallel", "parallel")),
        name=f"xattn_T{T}",
    )(q, *([mem_k] * XA_H), *([mem_v] * XA_H))


def _oln_kernel(op_ref, os_ref, h1_ref, wo_ref, g_ref, b_ref, wr_ref, br_ref,
                h2_ref, route_ref, cnt_ref, carry_scr, *, n_prompt_tiles):
    i = pl.program_id(0)

    @pl.when(i == 0)
    def _init():
        carry_scr[...] = jnp.zeros_like(carry_scr)

    att = jnp.where(i < n_prompt_tiles, op_ref[...], os_ref[...])
    h2 = _layer_norm(ALPHA * h1_ref[...] + _bdot(att, wo_ref[...]), g_ref[...], b_ref[...])
    h2_ref[...] = h2

    logits = _bdot(h2, wr_ref[...]) + br_ref[...]
    tm = logits.shape[0]
    lane = lax.broadcasted_iota(jnp.int32, logits.shape, 1)
    lane_f = lane.astype(F32)
    neg = -jnp.inf

    def first_argmax(vals):
        mx = jnp.max(vals, axis=1, keepdims=True)
        idx = jnp.min(jnp.where(vals == mx, lane_f, float(LANES)), axis=1, keepdims=True)
        return mx, idx.astype(jnp.int32)

    gl = jnp.where(lane < N_GROUPS, logits, neg)
    gmax, grp = first_argmax(gl)
    p_grp = 1.0 / jnp.sum(jnp.exp(gl - gmax), axis=1, keepdims=True)
    lo = ROUTE_LANE0 + grp * EXP_PER_GROUP
    el = jnp.where((lane >= lo) & (lane < lo + EXP_PER_GROUP), logits, neg)
    v1, i1 = first_argmax(el)
    v2, i2 = first_argmax(jnp.where(lane == i1, neg, el))
    e21 = jnp.exp(v2 - v1)
    w1 = p_grp / (1.0 + e21)
    w2 = p_grp * e21 / (1.0 + e21)

    pick1 = lane == i1
    pick2 = lane == i2
    onehot = jnp.where(pick1 | pick2, 1.0, 0.0)
    ti = lax.broadcasted_iota(jnp.int32, (tm, tm), 0)
    si = lax.broadcasted_iota(jnp.int32, (tm, tm), 1)
    tri = jnp.where(si <= ti, 1.0, 0.0)
    cnt = _bdot(tri, onehot) + carry_scr[...]
    r1 = jnp.sum(jnp.where(pick1, cnt, 0.0), axis=1, keepdims=True) - 1.0
    r2 = jnp.sum(jnp.where(pick2, cnt, 0.0), axis=1, keepdims=True) - 1.0
    last = cnt[tm - 1:tm, :]
    carry_scr[...] = last
    cnt_ref[...] = jnp.broadcast_to(last, cnt_ref.shape)

    e1 = (i1 - ROUTE_LANE0).astype(F32)
    e2 = (i2 - ROUTE_LANE0).astype(F32)
    packed = jnp.zeros_like(logits)
    for idx, val in enumerate((e1, e2, w1, w2, r1, r2)):
        packed = jnp.where(lane == idx, val, packed)
    route_ref[...] = packed


def _oln(o_p, o_s, h1, wo, ln_g, ln_b, wr, br, *, tm):
    Np, Ns = o_p.shape[0], o_s.shape[0]
    NT = Np + Ns
    tm = _tile(math.gcd(Np, Ns), tm)
    npt = Np // tm
    const = lambda i: (0, 0)
    return pl.pallas_call(
        functools.partial(_oln_kernel, n_prompt_tiles=npt),
        grid=(NT // tm,),
        in_specs=[pl.BlockSpec((tm, D_MODEL), lambda i: (jnp.minimum(i, npt - 1), 0)),
                  pl.BlockSpec((tm, D_MODEL), lambda i: (jnp.maximum(i - npt, 0), 0)),
                  pl.BlockSpec((tm, D_MODEL), lambda i: (i, 0)),
                  pl.BlockSpec((D_MODEL, D_MODEL), const, pipeline_mode=pl.Buffered(1)),
                  pl.BlockSpec((1, D_MODEL), const),
                  pl.BlockSpec((1, D_MODEL), const),
                  pl.BlockSpec((D_MODEL, LANES), const),
                  pl.BlockSpec((1, LANES), const)],
        out_specs=[pl.BlockSpec((tm, D_MODEL), lambda i: (i, 0)),
                   pl.BlockSpec((tm, LANES), lambda i: (i, 0)),
                   pl.BlockSpec((SUBLANES, LANES), const)],
        out_shape=[jax.ShapeDtypeStruct((NT, D_MODEL), F32),
                   jax.ShapeDtypeStruct((NT, LANES), F32),
                   jax.ShapeDtypeStruct((SUBLANES, LANES), F32)],
        scratch_shapes=[pltpu.VMEM((1, LANES), F32)],
        compiler_params=_cparams(("arbitrary",)),
        name="oproj_ln2_router",
    )(o_p, o_s, h1, wo, ln_g, ln_b, wr, br)


def _row_copy(src_ref, src_row, dst_ref, dst_row, sem):
    return pltpu.make_async_copy(src_ref.at[pl.ds(src_row, 1), :], dst_ref.at[pl.ds(dst_row, 1), :], sem)


def _dispatch_kernel(pos_ref, h_ref, xs_ref, sem, *, tm):
    base = pl.program_id(0) * tm

    def start(r, carry):
        for k in range(TOP_K):
            _row_copy(h_ref, r, xs_ref, pos_ref[TOP_K * (base + r) + k], sem).start()
        return carry

    def wait(r, carry):
        for k in range(TOP_K):
            _row_copy(h_ref, r, xs_ref, pos_ref[TOP_K * (base + r) + k], sem).wait()
        return carry

    lax.fori_loop(0, tm, start, 0, unroll=DMA_UNROLL)
    lax.fori_loop(0, tm, wait, 0, unroll=DMA_UNROLL)


def _dispatch(pos_flat, h2, cap, *, tm):
    NT = h2.shape[0]
    tm = _tile(NT, tm)
    return pl.pallas_call(
        functools.partial(_dispatch_kernel, tm=tm),
        grid_spec=pltpu.PrefetchScalarGridSpec(
            num_scalar_prefetch=1,
            grid=(NT // tm,),
            in_specs=[pl.BlockSpec((tm, D_MODEL), lambda i, pos: (i, 0))],
            out_specs=pl.BlockSpec(memory_space=pl.ANY),
            scratch_shapes=[pltpu.SemaphoreType.DMA(())],
        ),
        out_shape=jax.ShapeDtypeStruct((cap, D_MODEL), F32),
        compiler_params=_cparams(("arbitrary",)),
        name="moe_dispatch",
    )(pos_flat, h2)


def _expert_kernel(it_ref, ie_ref, lo_ref, hi_ref, first_ref, n_ref, x_ref, wg_ref, wu_ref, wd_ref, y_ref,
                   wg_s, wu_s, wd_s):
    i = pl.program_id(0)

    @pl.when(i < n_ref[0])
    def _compute():
        fresh = jnp.logical_or(i == 0, ie_ref[i] != ie_ref[jnp.maximum(i - 1, 0)])

        @pl.when(fresh)
        def _cast():
            wg_s[...] = wg_ref[0].astype(BF16)
            wu_s[...] = wu_ref[0].astype(BF16)
            wd_s[...] = wd_ref[0].astype(BF16)

        x = x_ref[...].astype(BF16)
        gate = jnp.dot(x, wg_s[...], preferred_element_type=F32)
        up = jnp.dot(x, wu_s[...], preferred_element_type=F32)
        hid = gate * _sigmoid(gate) * up
        y = jnp.dot(hid.astype(BF16), wd_s[...], preferred_element_type=F32)
        TR = y.shape[0]
        rows = it_ref[i] * TR + lax.broadcasted_iota(jnp.int32, (TR, 1), 0)
        mine = (rows >= lo_ref[i]) & (rows < hi_ref[i])

        @pl.when(first_ref[i] == 1)
        def _first():
            y_ref[...] = jnp.where(mine, y, 0.0)

        @pl.when(first_ref[i] == 0)
        def _later():
            y_ref[...] = jnp.where(mine, y, y_ref[...])


def _experts(items, xs, e_wg, e_wu, e_wd):
    TR = MOE_ROW_TILE
    n_work = items[0].shape[0]
    rows = lambda i, it, ie, lo, hi, fi, n: (it[i], 0)
    wsel = lambda i, it, ie, lo, hi, fi, n: (ie[i], 0, 0)
    return pl.pallas_call(
        _expert_kernel,
        grid_spec=pltpu.PrefetchScalarGridSpec(
            num_scalar_prefetch=6,
            grid=(n_work,),
            in_specs=[pl.BlockSpec((TR, D_MODEL), rows),
                      pl.BlockSpec((1, D_MODEL, EXP_FF), wsel),
                      pl.BlockSpec((1, D_MODEL, EXP_FF), wsel),
                      pl.BlockSpec((1, EXP_FF, D_MODEL), wsel)],
            out_specs=pl.BlockSpec((TR, D_MODEL), rows),
            scratch_shapes=[pltpu.VMEM((D_MODEL, EXP_FF), BF16), pltpu.VMEM((D_MODEL, EXP_FF), BF16),
                            pltpu.VMEM((EXP_FF, D_MODEL), BF16)],
        ),
        out_shape=jax.ShapeDtypeStruct(xs.shape, F32),
        compiler_params=_cparams(("arbitrary",)),
        name="moe_experts",
    )(*items, xs, e_wg, e_wu, e_wd)


def _moe_schedule(route, counts, n_rows):
    TR = MOE_ROW_TILE
    assert n_rows % TR == 0
    n_work = n_rows // TR + N_EXPERTS - 1
    rt = route[:, :SUBLANES].T
    cnt = counts[0, ROUTE_LANE0:ROUTE_LANE0 + N_EXPERTS].astype(jnp.int32)
    g_end = jnp.cumsum(cnt)
    g_start = g_end - cnt
    onehot = rt[0:2, :, None] == jnp.arange(N_EXPERTS, dtype=F32)
    start_of = jnp.sum(jnp.where(onehot, g_start.astype(F32), 0.0), axis=-1)
    pos_flat = (start_of + rt[4:6]).astype(jnp.int32).T.reshape(-1)
    first_tile = g_start // TR
    n_items_e = jnp.where(cnt > 0, (g_end - 1) // TR - first_tile + 1, 0)
    item_end = jnp.cumsum(n_items_e)
    n_items = item_end[-1:]
    idx = jnp.minimum(jnp.arange(n_work, dtype=jnp.int32), n_items[0] - 1)
    ie = jnp.minimum(jnp.sum((item_end[None, :] <= idx[:, None]).astype(jnp.int32), axis=1), N_EXPERTS - 1)
    it = first_tile[ie] + idx - (item_end - n_items_e)[ie]
    first = jnp.concatenate([jnp.ones((1,), jnp.int32), (it[1:] != it[:-1]).astype(jnp.int32)])
    items = (it.astype(jnp.int32), ie, g_start[ie], g_end[ie], first, n_items.astype(jnp.int32))
    return pos_flat, items


def _combine_kernel(pos_ref, h2_ref, route_ref, ys_ref, g_ref, b_ref, yp_ref, ysm_ref,
                    buf, sem, *, tm, n_prompt_tiles):
    i = pl.program_id(0)
    base = i * tm

    def start(r, carry):
        for k in range(TOP_K):
            _row_copy(ys_ref, pos_ref[TOP_K * (base + r) + k], buf.at[k], r, sem).start()
        return carry

    def wait(r, carry):
        for k in range(TOP_K):
            _row_copy(ys_ref, pos_ref[TOP_K * (base + r) + k], buf.at[k], r, sem).wait()
        return carry

    lax.fori_loop(0, tm, start, 0, unroll=DMA_UNROLL)
    lax.fori_loop(0, tm, wait, 0, unroll=DMA_UNROLL)
    route = route_ref[...]
    moe = route[:, 2:3] * buf[0] + route[:, 3:4] * buf[1]
    y = _layer_norm(ALPHA * h2_ref[...] + moe, g_ref[...], b_ref[...])

    @pl.when(i < n_prompt_tiles)
    def _prompt():
        yp_ref[...] = y

    @pl.when(i >= n_prompt_tiles)
    def _sample():
        ysm_ref[...] = y


def _combine(pos_flat, h2, route, ys, ln_g, ln_b, *, Np, Ns, tm):
    NT = h2.shape[0]
    tm = _tile(math.gcd(Np, Ns), tm)
    npt = Np // tm
    const = lambda i, pos: (0, 0)
    return pl.pallas_call(
        functools.partial(_combine_kernel, tm=tm, n_prompt_tiles=npt),
        grid_spec=pltpu.PrefetchScalarGridSpec(
            num_scalar_prefetch=1,
            grid=(NT // tm,),
            in_specs=[pl.BlockSpec((tm, D_MODEL), lambda i, pos: (i, 0)),
                      pl.BlockSpec((tm, LANES), lambda i, pos: (i, 0)),
                      pl.BlockSpec(memory_space=pl.ANY),
                      pl.BlockSpec((1, D_MODEL), const),
                      pl.BlockSpec((1, D_MODEL), const)],
            out_specs=[pl.BlockSpec((tm, D_MODEL), lambda i, pos: (jnp.minimum(i, npt - 1), 0)),
                       pl.BlockSpec((tm, D_MODEL), lambda i, pos: (jnp.maximum(i - npt, 0), 0))],
            scratch_shapes=[pltpu.VMEM((TOP_K, tm, D_MODEL), F32), pltpu.SemaphoreType.DMA(())],
        ),
        out_shape=[jax.ShapeDtypeStruct((Np, D_MODEL), F32),
                   jax.ShapeDtypeStruct((Ns, D_MODEL), F32)],
        compiler_params=_cparams(("arbitrary",)),
        name="moe_combine_ln3",
    )(pos_flat, h2, route, ys, ln_g, ln_b)


def kernel(x_prompt, x_sample, mem_prompt, cache_mem_k, cache_mem_v, state_mlstm_C, state_mlstm_n,
           state_mlstm_m, state_mlstm_conv, state_hgrn_S, w_in, b_in, conv_w, mlstm_gn, lb_logits, hgrn_gn,
           w_bm, w_bh, w_out, ln1_g, ln1_b, xa_wq, xa_wk, xa_wv, xa_wo, ln2_g, ln2_b,
           r1_w, r1_b, r2_w, r2_b, e_wg, e_wu, e_wd, ln3_g, ln3_b):
    Bp, Tp, _ = x_prompt.shape
    Bs, Ts, _ = x_sample.shape
    MEM = mem_prompt.shape[1]
    Np, Ns = Bp * Tp, Bs * Ts
    NT = Np + Ns
    Lp_m = math.gcd(Tp, 256)
    Lp_h = math.gcd(Tp, 64)
    Ls = Ts
    assert Ts % SUBLANES == 0 and Np % Ls == 0

    xp2 = x_prompt.reshape(Np, D_MODEL)
    xs2 = x_sample.reshape(Ns, D_MODEL)

    w = w_in[0]
    bi = b_in[0]
    w16 = w.astype(BF16)
    w_rest = w16[:, GATE_LO + 2 * M_H:]
    b_main = jnp.concatenate([bi[:GATE_LO], bi[GATE_LO + 2 * M_H:]])[None]
    w_gate = jnp.pad(w[:, GATE_LO:GATE_LO + 2 * M_H], ((0, 0), (0, LANES - 2 * M_H))).astype(BF16)
    b_gate = jnp.pad(bi[GATE_LO:GATE_LO + 2 * M_H], (0, LANES - 2 * M_H))[None]
    lb = jnp.cumsum(jax.nn.softmax(lb_logits.astype(F32), axis=0), axis=0)[0][None]
    wr = jnp.zeros((D_MODEL, LANES), F32)
    wr = wr.at[:, :N_GROUPS].set(r1_w[0])
    wr = wr.at[:, ROUTE_LANE0:ROUTE_LANE0 + N_EXPERTS].set(
        jnp.transpose(r2_w[0], (1, 0, 2)).reshape(D_MODEL, N_EXPERTS)).astype(BF16)
    br = jnp.zeros((1, LANES), F32)
    br = br.at[0, :N_GROUPS].set(r1_b[0])
    br = br.at[0, ROUTE_LANE0:ROUTE_LANE0 + N_EXPERTS].set(r2_b[0].reshape(N_EXPERTS))
    zeros_d = jnp.zeros((1, D_MODEL), F32)

    memb = mem_prompt.reshape(Bp * MEM, D_MODEL).astype(BF16)
    mk = _matmul_bias(memb, xa_wk[0].astype(BF16), zeros_d, tm=1024, tn=1024, name="mem_k")
    mv = _matmul_bias(memb, xa_wv[0].astype(BF16), zeros_d, tm=1024, tn=1024, name="mem_v")

    u_main, u_gate = _in_proj(xp2, xs2, w16, w_rest, b_main, w_gate, b_gate, tm=1024, tn=1024)

    padc = lambda c: jnp.pad(c, ((0, 0), (SUBLANES - (CONV_K - 1), 0), (0, 0)))
    zC = jnp.zeros((Bp, M_H, M_DK, M_DV), F32)
    zn = jnp.zeros((Bp, M_H, M_DK), F32)
    zm = jnp.zeros((Bp, 1, M_H), F32)
    zconv = jnp.zeros((Bp, SUBLANES, 2 * M_W), F32)
    zS = jnp.zeros((Bp, HG_H, HG_DK, HG_DV), F32)
    cw = conv_w[0]
    mgn = mlstm_gn[0][None]
    hgn = hgrn_gn[0][None]
    hm_p, C_p, n_p, m_p, conv_p = _mlstm(u_main, u_gate, zconv, zC, zn, zm, cw, mgn,
                                         B=Bp, T=Tp, L=Lp_m, row0=0)
    hm_s, C_s, n_s, m_s, conv_s = _mlstm(u_main, u_gate, padc(state_mlstm_conv[0]), state_mlstm_C[0],
                                         state_mlstm_n[0], state_mlstm_m[0][:, None, :], cw, mgn,
                                         B=Bs, T=Ts, L=Ls, row0=Np)
    og_p, S_p = _hgrn(u_main, zS, lb, hgn, B=Bp, T=Tp, L=Lp_h, row0=0)
    og_s, S_s = _hgrn(u_main, state_hgrn_S[0], lb, hgn, B=Bs, T=Ts, L=Ls, row0=Np)

    h1, h1b = _merge(hm_p, hm_s, og_p, og_s, u_main, xp2, xs2, w_bm[0].astype(BF16), w_bh[0].astype(BF16),
                     w_out[0].astype(BF16), ln1_g, ln1_b, tm=256)

    q = _matmul_bias(h1b, xa_wq[0].astype(BF16), zeros_d, tm=1024, tn=1024, name="xa_q")
    att_p = _attention(q, mk, mv, B=Bp, T=Tp, tq=512, row0=0)
    att_s = _attention_cache(q, cache_mem_k, cache_mem_v, B=Bs, T=Ts, row0=Np)
    h2, route, counts = _oln(att_p, att_s, h1, xa_wo[0].astype(BF16), ln2_g, ln2_b, wr, br, tm=512)

    pos_flat, items = _moe_schedule(route, counts, TOP_K * NT)
    xs_sorted = _dispatch(pos_flat, h2, TOP_K * NT, tm=256)
    ys_sorted = _experts(items, xs_sorted, e_wg[0], e_wu[0], e_wd[0])
    y_p, y_s = _combine(pos_flat, h2, route, ys_sorted, ln3_g, ln3_b, Np=Np, Ns=Ns, tm=256)

    kv5 = lambda a: a.reshape(1, Bp, MEM, XA_H, XA_D)
    return (y_p.reshape(Bp, Tp, D_MODEL), y_s.reshape(Bs, Ts, D_MODEL), kv5(mk), kv5(mv),
            C_p[None], n_p[None], m_p.reshape(1, Bp, M_H), conv_p[None], S_p[None],
            C_s[None], n_s[None], m_s.reshape(1, Bs, M_H), conv_s[None], S_s[None])
```

```python
import functools
import math

import jax
import jax.numpy as jnp
from jax import lax
from jax.experimental import pallas as pl
from jax.experimental.pallas import tpu as pltpu

F32 = jnp.float32
BF16 = jnp.bfloat16

D_MODEL = 2048
M_W = 1024
M_H = 4
M_DK = 256
M_DV = 256
CONV_K = 4
HG_W = 1024
HG_H = 8
HG_DK = 128
HG_DV = 128
XA_H = 4
XA_D = 512
N_GROUPS = 4
EXP_PER_GROUP = 8
N_EXPERTS = 32
TOP_K = 2
EXP_FF = 512
DEPTH = 1
ALPHA = (2 * DEPTH) ** 0.25
LN_EPS = 1e-5

COL_QK, COL_V, COL_O = 0, 2048, 3072
COL_QH, COL_FH, COL_IH, COL_GH = 4096, 5120, 6144, 7168
COL_GM, COL_GHH = 8192, 10240
N_MAIN = 12288
GATE_LO = 4 * M_W
LANES = 128
SUBLANES = 8
ROUTE_LANE0 = N_GROUPS

VMEM_LIMIT = 56 << 20
MOE_ROW_TILE = 256
SAMPLE_SEQS_PER_STEP = 4
DMA_UNROLL = 8


def _cparams(sem, vmem=VMEM_LIMIT):
    return pltpu.CompilerParams(dimension_semantics=sem, vmem_limit_bytes=vmem)


def _tile(n, pref):
    t = math.gcd(n, pref)
    assert t % SUBLANES == 0, (n, pref)
    return t


def _bdot(a, b):
    return jnp.dot(a.astype(BF16), b.astype(BF16), preferred_element_type=F32)


def _bdot_nt(a, b):
    return lax.dot_general(a.astype(BF16), b.astype(BF16), (((1,), (1,)), ((), ())),
                           preferred_element_type=F32)


def _bdot_tn(a, b):
    return lax.dot_general(a.astype(BF16), b.astype(BF16), (((0,), (0,)), ((), ())),
                           preferred_element_type=F32)


def _sigmoid(x):
    return 1.0 / (1.0 + jnp.exp(-x))


def _cumsum_rows(x, seg=None):
    n = seg or x.shape[0]
    assert n & (n - 1) == 0
    row = lax.broadcasted_iota(jnp.int32, x.shape, 0) & (n - 1)
    d = 1
    while d < n:
        x = x + jnp.where(row >= d, pltpu.roll(x, d, axis=0), 0.0)
        d *= 2
    return x


def _col_to_row(col, eye):
    return jnp.sum(jnp.where(eye, col, 0.0), axis=0, keepdims=True)


def _row_to_col(row, eye):
    return jnp.sum(jnp.where(eye, row, 0.0), axis=1, keepdims=True)


def _layer_norm(x, g, b):
    mu = jnp.mean(x, axis=-1, keepdims=True)
    xc = x - mu
    var = jnp.mean(xc * xc, axis=-1, keepdims=True)
    return xc * lax.rsqrt(var + LN_EPS) * g + b


def _mm_kernel(x_ref, w_ref, b_ref, o_ref, wb_scr):
    @pl.when(pl.program_id(1) == 0)
    def _new_weight_tile():
        wb_scr[...] = w_ref[...].astype(BF16)

    acc = jnp.dot(x_ref[...], wb_scr[...], preferred_element_type=F32)
    o_ref[...] = (acc + b_ref[...]).astype(o_ref.dtype)


def _matmul_bias(x, w, b, *, tm, tn, out_dtype=F32, name):
    M, K = x.shape
    N = w.shape[1]
    tm = _tile(M, tm)
    tn = _tile(N, tn)
    return pl.pallas_call(
        _mm_kernel,
        grid=(N // tn, M // tm),
        in_specs=[pl.BlockSpec((tm, K), lambda j, i: (i, 0)),
                  pl.BlockSpec((K, tn), lambda j, i: (0, j)),
                  pl.BlockSpec((1, tn), lambda j, i: (0, j))],
        out_specs=pl.BlockSpec((tm, tn), lambda j, i: (i, j)),
        out_shape=jax.ShapeDtypeStruct((M, N), out_dtype),
        scratch_shapes=[pltpu.VMEM((K, tn), BF16)],
        compiler_params=_cparams(("parallel", "arbitrary")),
        name=name,
    )(x, w, b)


def _inproj_kernel(xp_ref, xs_ref, wa_ref, wb_ref, b_ref, wg_ref, bg_ref, u_ref, ug_ref, xb_scr, *,
                   n_prompt_tiles, n_a_tiles):
    i = pl.program_id(0)
    j = pl.program_id(1)

    @pl.when(j == 0)
    def _row_tile():
        xb = jnp.where(i < n_prompt_tiles, xp_ref[...], xs_ref[...]).astype(BF16)
        xb_scr[...] = xb
        ug_ref[...] = jnp.dot(xb, wg_ref[...], preferred_element_type=F32) + bg_ref[...]

    @pl.when(j < n_a_tiles)
    def _lead():
        u_ref[...] = jnp.dot(xb_scr[...], wa_ref[...], preferred_element_type=F32) + b_ref[...]

    @pl.when(j >= n_a_tiles)
    def _rest():
        u_ref[...] = jnp.dot(xb_scr[...], wb_ref[...], preferred_element_type=F32) + b_ref[...]


def _in_proj(x_p, x_s, w_all, w_rest, b_main, w_gate, b_gate, *, tm, tn):
    Np, Ns = x_p.shape[0], x_s.shape[0]
    NT = Np + Ns
    K = w_all.shape[0]
    N = GATE_LO + w_rest.shape[1]
    tm = _tile(math.gcd(Np, Ns), tm)
    tn = _tile(math.gcd(GATE_LO, w_rest.shape[1]), tn)
    npt = Np // tm
    na = GATE_LO // tn
    return pl.pallas_call(
        functools.partial(_inproj_kernel, n_prompt_tiles=npt, n_a_tiles=na),
        grid=(NT // tm, N // tn),
        in_specs=[pl.BlockSpec((tm, K), lambda i, j: (jnp.minimum(i, npt - 1), 0), pipeline_mode=pl.Buffered(1)),
                  pl.BlockSpec((tm, K), lambda i, j: (jnp.maximum(i - npt, 0), 0), pipeline_mode=pl.Buffered(1)),
                  pl.BlockSpec((K, tn), lambda i, j: (0, jnp.minimum(j, na - 1))),
                  pl.BlockSpec((K, tn), lambda i, j: (0, jnp.maximum(j - na, 0))),
                  pl.BlockSpec((1, tn), lambda i, j: (0, j)),
                  pl.BlockSpec((K, LANES), lambda i, j: (0, 0)),
                  pl.BlockSpec((1, LANES), lambda i, j: (0, 0))],
        out_specs=[pl.BlockSpec((tm, tn), lambda i, j: (i, j)),
                   pl.BlockSpec((tm, LANES), lambda i, j: (i, 0))],
        out_shape=[jax.ShapeDtypeStruct((NT, N), F32), jax.ShapeDtypeStruct((NT, LANES), F32)],
        scratch_shapes=[pltpu.VMEM((tm, K), BF16)],
        compiler_params=_cparams(("parallel", "arbitrary")),
        name="in_proj",
    )(x_p, x_s, w_all, w_rest, b_main, w_gate, b_gate)


def _mlstm_kernel(*refs, L, NC, NB):
    (qk_ref, v_ref, o_ref, g_ref, conv0_ref, C0_ref, n0_ref, m0_ref, cw_ref, gn_ref,
     h_ref, Co_ref, no_ref, mo_ref, convo_ref, C_scr, n_scr, m_scr, tail_scr) = refs
    c = pl.program_id(1)
    if NC > 1:
        @pl.when(c == 0)
        def _init():
            C_scr[...] = C0_ref[0]
            n_scr[...] = n0_ref[0]
            m_scr[...] = m0_ref[0]
            tail_scr[...] = conv0_ref[0]

    cw = cw_ref[...]
    gn = gn_ref[...]
    ti = lax.broadcasted_iota(jnp.int32, (L, L), 0)
    si = lax.broadcasted_iota(jnp.int32, (L, L), 1)
    eye = ti == si
    causal = si <= ti
    head_lane = lax.broadcasted_iota(jnp.int32, (1, M_H), 1)
    head_row = lax.broadcasted_iota(jnp.int32, (M_H, M_DK), 0)

    seqs = []
    for nb in range(NB):
        rows = slice(nb * L, (nb + 1) * L)
        if NC == 1:
            sq = dict(C_in=C0_ref.at[nb], C_out=Co_ref.at[nb], n_out=no_ref.at[nb], m_out=mo_ref.at[nb])
            n_all, m_all, tail = n0_ref[nb], m0_ref[nb], conv0_ref[nb]
        else:
            sq = dict(C_in=C_scr, C_out=C_scr, n_out=n_scr, m_out=m_scr)
            n_all, m_all, tail = n_scr[...], m_scr[...], tail_scr[...]
        qk_pre = qk_ref[rows, :]
        ext = jnp.concatenate([tail, qk_pre], axis=0)
        acc = qk_pre * cw[CONV_K - 1:CONV_K, :]
        for j in range(1, CONV_K):
            acc = acc + pltpu.roll(ext, j, axis=0)[SUBLANES:, :] * cw[CONV_K - 1 - j:CONV_K - j, :]
        if NC > 1:
            tail_scr[...] = qk_pre[L - SUBLANES:, :]
        qk = acc * _sigmoid(acc)
        g = g_ref[rows, :]
        lf_all = jnp.minimum(g, 0.0) - jnp.log(1.0 + jnp.exp(-jnp.abs(g)))
        F_all = _cumsum_rows(lf_all)
        heads = []
        for h in range(M_H):
            ks = slice(h * M_DK, (h + 1) * M_DK)
            q = qk[:, ks] * (M_DK ** -0.5)
            k = qk[:, M_W + h * M_DK:M_W + (h + 1) * M_DK]
            v = v_ref[rows, ks]
            ig = g[:, h:h + 1]
            F = F_all[:, M_H + h:M_H + h + 1]
            m_prev = m_all[:, h:h + 1]
            r_row = _col_to_row(ig - F, eye)
            Dm = jnp.where(causal, F + r_row, -jnp.inf)
            init_w = F + m_prev
            m_t = jnp.maximum(init_w, jnp.max(Dm, axis=1, keepdims=True))
            P = jnp.exp(Dm - m_t)
            a0 = jnp.exp(init_w - m_t)
            FL = F[L - 1:L, :]
            mL = m_t[L - 1:L, :]
            wL = jnp.exp(FL - F + ig - mL)
            decay = jnp.exp(FL + m_prev - mL)
            heads.append(dict(ks=ks, q=q, k=k, v=v, m_t=m_t, P=P, a0=a0, mL=mL, decay=decay, kw=wL * k))
        sq.update(rows=rows, heads=heads, n_all=n_all, m_all=m_all, qk_pre=qk_pre)
        seqs.append(sq)

    for sq in seqs:
        for h, d in enumerate(sq["heads"]):
            d["S"] = _bdot_nt(d["q"], d["k"])
            d["qC"] = _bdot(d["q"], sq["C_in"][h])
            d["kv"] = _bdot_tn(d["kw"], d["v"])

    for sq in seqs:
        for d in sq["heads"]:
            d["Sc"] = d["S"] * d["P"]
            d["num"] = _bdot(d["Sc"], d["v"])

    for nb, sq in enumerate(seqs):
        rows, n_new, m_new = sq["rows"], sq["n_all"], sq["m_all"]
        for h, d in enumerate(sq["heads"]):
            ks, q, a0, decay = d["ks"], d["q"], d["a0"], d["decay"]
            n_row = sq["n_all"][h:h + 1, :]
            num = d["num"] + a0 * d["qC"]
            den = jnp.sum(d["Sc"], axis=1, keepdims=True) + a0 * jnp.sum(q * n_row, axis=1, keepdims=True)
            hh = num * (1.0 / jnp.maximum(jnp.abs(den), jnp.exp(-d["m_t"])))
            sq["C_out"][h] = decay * sq["C_in"][h] + d["kv"]
            n_new = jnp.where(head_row == h, decay * n_row + jnp.sum(d["kw"], axis=0, keepdims=True), n_new)
            m_new = jnp.where(head_lane == h, d["mL"], m_new)

            hm = _sigmoid(o_ref[rows, ks]) * hh
            hm = hm - jnp.mean(hm, axis=1, keepdims=True)
            hm = hm * lax.rsqrt(jnp.mean(hm * hm, axis=1, keepdims=True) + LN_EPS) * gn[:, ks]
            h_ref[rows, ks] = hm
        sq["n_out"][...] = n_new
        sq["m_out"][...] = m_new
        conv_tail = sq["qk_pre"][L - (CONV_K - 1):, :]
        if NC == 1:
            convo_ref[nb] = conv_tail
        else:
            @pl.when(c == NC - 1)
            def _fin():
                Co_ref[0] = C_scr[...]
                no_ref[0] = n_new
                mo_ref[0] = m_new
                convo_ref[0] = conv_tail


def _mlstm(u_main, u_gate, conv0p, C0, n0, m0, conv_w, gn, *, B, T, L, row0, NB=1):
    NC = T // L
    assert NB == 1 or NC == 1
    assert B % NB == 0 and row0 % (NB * L) == 0
    R = NB * L
    rb0 = row0 // R
    tok = lambda b, c: rb0 + b * NC + c
    in_specs = [
        pl.BlockSpec((R, 2 * M_W), lambda b, c: (tok(b, c), COL_QK // (2 * M_W))),
        pl.BlockSpec((R, M_W), lambda b, c: (tok(b, c), COL_V // M_W)),
        pl.BlockSpec((R, M_W), lambda b, c: (tok(b, c), COL_O // M_W)),
        pl.BlockSpec((R, LANES), lambda b, c: (tok(b, c), 0)),
        pl.BlockSpec((NB, SUBLANES, 2 * M_W), lambda b, c: (b, 0, 0)),
        pl.BlockSpec((NB, M_H, M_DK, M_DV), lambda b, c: (b, 0, 0, 0)),
        pl.BlockSpec((NB, M_H, M_DK), lambda b, c: (b, 0, 0)),
        pl.BlockSpec((NB, 1, M_H), lambda b, c: (b, 0, 0)),
        pl.BlockSpec((CONV_K, 2 * M_W), lambda b, c: (0, 0)),
        pl.BlockSpec((1, M_W), lambda b, c: (0, 0)),
    ]
    args = [u_main, u_main, u_main, u_gate, conv0p, C0, n0, m0, conv_w, gn]
    out_specs = [
        pl.BlockSpec((R, M_W), lambda b, c: (b * NC + c, 0)),
        pl.BlockSpec((NB, M_H, M_DK, M_DV), lambda b, c: (b, 0, 0, 0)),
        pl.BlockSpec((NB, M_H, M_DK), lambda b, c: (b, 0, 0)),
        pl.BlockSpec((NB, 1, M_H), lambda b, c: (b, 0, 0)),
        pl.BlockSpec((NB, CONV_K - 1, 2 * M_W), lambda b, c: (b, 0, 0)),
    ]
    out_shape = [
        jax.ShapeDtypeStruct((B * T, M_W), F32),
        jax.ShapeDtypeStruct((B, M_H, M_DK, M_DV), F32),
        jax.ShapeDtypeStruct((B, M_H, M_DK), F32),
        jax.ShapeDtypeStruct((B, 1, M_H), F32),
        jax.ShapeDtypeStruct((B, CONV_K - 1, 2 * M_W), F32),
    ]
    return pl.pallas_call(
        functools.partial(_mlstm_kernel, L=L, NC=NC, NB=NB),
        grid=(B // NB, NC),
        in_specs=in_specs,
        out_specs=out_specs,
        out_shape=out_shape,
        scratch_shapes=[pltpu.VMEM((M_H, M_DK, M_DV), F32), pltpu.VMEM((M_H, M_DK), F32),
                        pltpu.VMEM((1, M_H), F32), pltpu.VMEM((SUBLANES, 2 * M_W), F32)],
        compiler_params=_cparams(("parallel", "arbitrary")),
        name=f"mlstm_L{L}",
    )(*args)


def _hgrn_kernel(*refs, L, NC, NB):
    (q_ref, f_ref, i_ref, g_ref, S0_ref, lb_ref, gn_ref, o_ref, So_ref, S_scr) = refs
    c = pl.program_id(1)
    if NC > 1:
        @pl.when(c == 0)
        def _init():
            S_scr[...] = S0_ref[0]

    lb = lb_ref[...]
    f = lb + (1.0 - lb) * _sigmoid(f_ref[...])
    kk = 1.0 - f
    b = _cumsum_rows(jnp.log(f), seg=L)
    qh = q_ref[...]
    q = qh * _sigmoid(qh)
    v = i_ref[...]
    gh = g_ref[...]
    gsilu = gh * _sigmoid(gh)
    gn = gn_ref[...]
    ti = lax.broadcasted_iota(jnp.int32, (L, L), 0)
    si = lax.broadcasted_iota(jnp.int32, (L, L), 1)
    causal = si <= ti
    ci = lax.broadcasted_iota(jnp.int32, (HG_DK, HG_DK), 0)
    cj = lax.broadcasted_iota(jnp.int32, (HG_DK, HG_DK), 1)
    eye = ci == cj
    mid = max(L // 2 - 1, 0)
    hsl = [slice(h * HG_DK, (h + 1) * HG_DK) for h in range(HG_H)]

    seqs = []
    for nb in range(NB):
        rows = slice(nb * L, (nb + 1) * L)
        bs, qs, ks = b[rows], q[rows], kk[rows]
        bL = bs[L - 1:L, :]
        bm = bs[mid:mid + 1, :]
        S_in, S_out = (S0_ref.at[nb], So_ref.at[nb]) if NC == 1 else (S_scr, S_scr)
        seqs.append(dict(rows=rows, bL=bL, v=v[rows], q_in=qs * jnp.exp(bs), q_t=qs * jnp.exp(bs - bm),
                         k_t=ks * jnp.exp(bm - bs), k_st=ks * jnp.exp(bL - bs), S_in=S_in, S_out=S_out))

    for sq in seqs:
        sq["A"] = [jnp.where(causal, _bdot_nt(sq["q_t"][:, hs], sq["k_t"][:, hs]), 0.0) for hs in hsl]
        sq["qS"] = [_bdot(sq["q_in"][:, hs], sq["S_in"][h]) for h, hs in enumerate(hsl)]
        sq["kv"] = [_bdot_tn(sq["k_st"][:, hs], sq["v"][:, hs]) for hs in hsl]
    for sq in seqs:
        sq["o_intra"] = [_bdot(sq["A"][h], sq["v"][:, hs]) for h, hs in enumerate(hsl)]
    for sq in seqs:
        rows = sq["rows"]
        for h, hs in enumerate(hsl):
            o = sq["o_intra"][h] + sq["qS"][h]
            dec = jnp.exp(_row_to_col(sq["bL"][:, hs], eye))
            sq["S_out"][h] = dec * sq["S_in"][h] + sq["kv"][h]
            o = o * lax.rsqrt(jnp.mean(o * o, axis=1, keepdims=True) + LN_EPS) * gn[:, hs]
            o_ref[rows, hs] = o * gsilu[rows, hs]

    if NC > 1:
        @pl.when(c == NC - 1)
        def _fin():
            So_ref[0] = S_scr[...]


def _hgrn(u_main, S0, lb, gn, *, B, T, L, row0, NB=1):
    NC = T // L
    assert NB == 1 or NC == 1
    assert B % NB == 0 and row0 % (NB * L) == 0
    R = NB * L
    rb0 = row0 // R
    tok = lambda b, c: rb0 + b * NC + c
    col = lambda off: (lambda b, c: (tok(b, c), off // HG_W))
    in_specs = [
        pl.BlockSpec((R, HG_W), col(COL_QH)),
        pl.BlockSpec((R, HG_W), col(COL_FH)),
        pl.BlockSpec((R, HG_W), col(COL_IH)),
        pl.BlockSpec((R, HG_W), col(COL_GH)),
        pl.BlockSpec((NB, HG_H, HG_DK, HG_DV), lambda b, c: (b, 0, 0, 0)),
        pl.BlockSpec((1, HG_W), lambda b, c: (0, 0)),
        pl.BlockSpec((1, HG_W), lambda b, c: (0, 0)),
    ]
    args = [u_main, u_main, u_main, u_main, S0, lb, gn]
    return pl.pallas_call(
        functools.partial(_hgrn_kernel, L=L, NC=NC, NB=NB),
        grid=(B // NB, NC),
        in_specs=in_specs,
        out_specs=[pl.BlockSpec((R, HG_W), lambda b, c: (b * NC + c, 0)),
                   pl.BlockSpec((NB, HG_H, HG_DK, HG_DV), lambda b, c: (b, 0, 0, 0))],
        out_shape=[jax.ShapeDtypeStruct((B * T, HG_W), F32),
                   jax.ShapeDtypeStruct((B, HG_H, HG_DK, HG_DV), F32)],
        scratch_shapes=[pltpu.VMEM((HG_H, HG_DK, HG_DV), F32)],
        compiler_params=_cparams(("parallel", "arbitrary")),
        name=f"hgrn_L{L}",
    )(*args)


def _merge_kernel(hmp_ref, hms_ref, ogp_ref, ogs_ref, gm_ref, gh_ref, xp_ref, xs_ref, wbm_ref, wbh_ref,
                  wo_ref, g_ref, b_ref, h1_ref, h1b_ref, *, n_prompt_tiles):
    is_prompt = pl.program_id(0) < n_prompt_tiles
    a = _bdot(jnp.where(is_prompt, hmp_ref[...], hms_ref[...]), wbm_ref[...])
    bb = _bdot(jnp.where(is_prompt, ogp_ref[...], ogs_ref[...]), wbh_ref[...])
    merged = _sigmoid(gm_ref[...]) * a + _sigmoid(gh_ref[...]) * bb
    mix = _bdot(merged, wo_ref[...])
    x = jnp.where(is_prompt, xp_ref[...], xs_ref[...])
    h1 = _layer_norm(ALPHA * x + mix, g_ref[...], b_ref[...])
    h1_ref[...] = h1
    h1b_ref[...] = h1.astype(BF16)


def _merge(hm_p, hm_s, og_p, og_s, u_main, x_p, x_s, w_bm, w_bh, w_out, ln_g, ln_b, *, tm):
    Np, Ns = x_p.shape[0], x_s.shape[0]
    NT = Np + Ns
    tm = _tile(math.gcd(Np, Ns), tm)
    npt = Np // tm
    const = lambda i: (0, 0)
    prompt_rows = lambda i: (jnp.minimum(i, npt - 1), 0)
    sample_rows = lambda i: (jnp.maximum(i - npt, 0), 0)
    return pl.pallas_call(
        functools.partial(_merge_kernel, n_prompt_tiles=npt),
        grid=(NT // tm,),
        in_specs=[
            pl.BlockSpec((tm, M_W), prompt_rows),
            pl.BlockSpec((tm, M_W), sample_rows),
            pl.BlockSpec((tm, HG_W), prompt_rows),
            pl.BlockSpec((tm, HG_W), sample_rows),
            pl.BlockSpec((tm, D_MODEL), lambda i: (i, COL_GM // D_MODEL)),
            pl.BlockSpec((tm, D_MODEL), lambda i: (i, COL_GHH // D_MODEL)),
            pl.BlockSpec((tm, D_MODEL), prompt_rows),
            pl.BlockSpec((tm, D_MODEL), sample_rows),
            pl.BlockSpec((M_W, D_MODEL), const, pipeline_mode=pl.Buffered(1)),
            pl.BlockSpec((HG_W, D_MODEL), const, pipeline_mode=pl.Buffered(1)),
            pl.BlockSpec((D_MODEL, D_MODEL), const, pipeline_mode=pl.Buffered(1)),
            pl.BlockSpec((1, D_MODEL), const),
            pl.BlockSpec((1, D_MODEL), const),
        ],
        out_specs=[pl.BlockSpec((tm, D_MODEL), lambda i: (i, 0)),
                   pl.BlockSpec((tm, D_MODEL), lambda i: (i, 0))],
        out_shape=[jax.ShapeDtypeStruct((NT, D_MODEL), F32),
                   jax.ShapeDtypeStruct((NT, D_MODEL), BF16)],
        compiler_params=_cparams(("parallel",)),
        name="merge_ln1",
    )(hm_p, hm_s, og_p, og_s, u_main, u_main, x_p, x_s, w_bm, w_bh, w_out, ln_g, ln_b)


def _attn_kernel(q_ref, *refs):
    k_refs, v_refs, o_ref = refs[:XA_H], refs[XA_H:2 * XA_H], refs[2 * XA_H]
    for h in range(XA_H):
        sl = slice(h * XA_D, (h + 1) * XA_D)
        s = _bdot_nt(q_ref[:, sl], k_refs[h][...]) * (XA_D ** -0.5)
        e = jnp.exp(s - jnp.max(s, axis=1, keepdims=True))
        p = e / jnp.sum(e, axis=1, keepdims=True)
        o_ref[:, sl] = _bdot(p, v_refs[h][...])


def _kv_head_copies(k_hbm, v_hbm, kbuf, vbuf, sem, b, slot):
    out = []
    for j, (src, dst) in enumerate(((k_hbm, kbuf), (v_hbm, vbuf))):
        for h in range(XA_H):
            out.append(pltpu.make_async_copy(src.at[0, b, :, h, :], dst.at[slot, h], sem.at[slot, j, h]))
    return out


def _attn_cache_kernel(q_ref, k_hbm, v_hbm, o_ref, kbuf, vbuf, sem):
    b = pl.program_id(0)
    slot = b % 2

    @pl.when(b == 0)
    def _first():
        for c in _kv_head_copies(k_hbm, v_hbm, kbuf, vbuf, sem, b, slot):
            c.start()

    @pl.when(b + 1 < pl.num_programs(0))
    def _prefetch():
        for c in _kv_head_copies(k_hbm, v_hbm, kbuf, vbuf, sem, b + 1, 1 - slot):
            c.start()

    for c in _kv_head_copies(k_hbm, v_hbm, kbuf, vbuf, sem, b, slot):
        c.wait()
    T = q_ref.shape[0]
    s = jnp.concatenate([_bdot_nt(q_ref[:, h * XA_D:(h + 1) * XA_D], kbuf[slot, h]) for h in range(XA_H)],
                        axis=0) * (XA_D ** -0.5)
    e = jnp.exp(s - jnp.max(s, axis=1, keepdims=True))
    p = e / jnp.sum(e, axis=1, keepdims=True)
    for h in range(XA_H):
        o_ref[:, h * XA_D:(h + 1) * XA_D] = _bdot(p[h * T:(h + 1) * T], vbuf[slot, h])


def _attention_cache(q, cache_k, cache_v, *, B, T, row0):
    M = cache_k.shape[2]
    rb0 = row0 // T
    return pl.pallas_call(
        _attn_cache_kernel,
        grid=(B,),
        in_specs=[pl.BlockSpec((T, D_MODEL), lambda b: (rb0 + b, 0)),
                  pl.BlockSpec(memory_space=pl.ANY), pl.BlockSpec(memory_space=pl.ANY)],
        out_specs=pl.BlockSpec((T, D_MODEL), lambda b: (b, 0)),
        out_shape=jax.ShapeDtypeStruct((B * T, D_MODEL), F32),
        scratch_shapes=[pltpu.VMEM((2, XA_H, M, XA_D), F32), pltpu.VMEM((2, XA_H, M, XA_D), F32),
                        pltpu.SemaphoreType.DMA((2, 2, XA_H))],
        compiler_params=_cparams(("arbitrary",)),
        name="xattn_cache",
    )(q, cache_k, cache_v)


def _attention(q, mem_k, mem_v, *, B, T, tq, row0):
    tq = _tile(T, tq)
    nq = T // tq
    rb0 = row0 // tq
    M = mem_k.shape[0] // B
    kv_specs = [pl.BlockSpec((M, XA_D), lambda b, t, h=h: (b, h)) for h in range(XA_H)]
    return pl.pallas_call(
        _attn_kernel,
        grid=(B, nq),
        in_specs=[pl.BlockSpec((tq, D_MODEL), lambda b, t: (rb0 + b * nq + t, 0))] + kv_specs + kv_specs,
        out_specs=pl.BlockSpec((tq, D_MODEL), lambda b, t: (b * nq + t, 0)),
        out_shape=jax.ShapeDtypeStruct((B * T, D_MODEL), F32),
        compiler_params=_cparams(("parallel", "parallel")),
        name=f"xattn_T{T}",
    )(q, *([mem_k] * XA_H), *([mem_v] * XA_H))


def _oln_kernel(op_ref, os_ref, h1_ref, wo_ref, g_ref, b_ref, wr_ref, br_ref,
                h2_ref, route_ref, cnt_ref, carry_scr, *, n_prompt_tiles):
    i = pl.program_id(0)

    @pl.when(i == 0)
    def _init():
        carry_scr[...] = jnp.zeros_like(carry_scr)

    att = jnp.where(i < n_prompt_tiles, op_ref[...], os_ref[...])
    h2 = _layer_norm(ALPHA * h1_ref[...] + _bdot(att, wo_ref[...]), g_ref[...], b_ref[...])
    h2_ref[...] = h2

    logits = _bdot(h2, wr_ref[...]) + br_ref[...]
    tm = logits.shape[0]
    lane = lax.broadcasted_iota(jnp.int32, logits.shape, 1)
    lane_f = lane.astype(F32)
    neg = -jnp.inf

    def first_argmax(vals):
        mx = jnp.max(vals, axis=1, keepdims=True)
        idx = jnp.min(jnp.where(vals == mx, lane_f, float(LANES)), axis=1, keepdims=True)
        return mx, idx.astype(jnp.int32)

    gl = jnp.where(lane < N_GROUPS, logits, neg)
    gmax, grp = first_argmax(gl)
    p_grp = 1.0 / jnp.sum(jnp.exp(gl - gmax), axis=1, keepdims=True)
    lo = ROUTE_LANE0 + grp * EXP_PER_GROUP
    el = jnp.where((lane >= lo) & (lane < lo + EXP_PER_GROUP), logits, neg)
    v1, i1 = first_argmax(el)
    v2, i2 = first_argmax(jnp.where(lane == i1, neg, el))
    e21 = jnp.exp(v2 - v1)
    w1 = p_grp / (1.0 + e21)
    w2 = p_grp * e21 / (1.0 + e21)

    pick1 = lane == i1
    pick2 = lane == i2
    onehot = jnp.where(pick1 | pick2, 1.0, 0.0)
    ti = lax.broadcasted_iota(jnp.int32, (tm, tm), 0)
    si = lax.broadcasted_iota(jnp.int32, (tm, tm), 1)
    tri = jnp.where(si <= ti, 1.0, 0.0)
    cnt = _bdot(tri, onehot) + carry_scr[...]
    r1 = jnp.sum(jnp.where(pick1, cnt, 0.0), axis=1, keepdims=True) - 1.0
    r2 = jnp.sum(jnp.where(pick2, cnt, 0.0), axis=1, keepdims=True) - 1.0
    last = cnt[tm - 1:tm, :]
    carry_scr[...] = last
    cnt_ref[...] = jnp.broadcast_to(last, cnt_ref.shape)

    e1 = (i1 - ROUTE_LANE0).astype(F32)
    e2 = (i2 - ROUTE_LANE0).astype(F32)
    packed = jnp.zeros_like(logits)
    for idx, val in enumerate((e1, e2, w1, w2, r1, r2)):
        packed = jnp.where(lane == idx, val, packed)
    route_ref[...] = packed


def _oln(o_p, o_s, h1, wo, ln_g, ln_b, wr, br, *, tm):
    Np, Ns = o_p.shape[0], o_s.shape[0]
    NT = Np + Ns
    tm = _tile(math.gcd(Np, Ns), tm)
    npt = Np // tm
    const = lambda i: (0, 0)
    return pl.pallas_call(
        functools.partial(_oln_kernel, n_prompt_tiles=npt),
        grid=(NT // tm,),
        in_specs=[pl.BlockSpec((tm, D_MODEL), lambda i: (jnp.minimum(i, npt - 1), 0)),
                  pl.BlockSpec((tm, D_MODEL), lambda i: (jnp.maximum(i - npt, 0), 0)),
                  pl.BlockSpec((tm, D_MODEL), lambda i: (i, 0)),
                  pl.BlockSpec((D_MODEL, D_MODEL), const, pipeline_mode=pl.Buffered(1)),
                  pl.BlockSpec((1, D_MODEL), const),
                  pl.BlockSpec((1, D_MODEL), const),
                  pl.BlockSpec((D_MODEL, LANES), const),
                  pl.BlockSpec((1, LANES), const)],
        out_specs=[pl.BlockSpec((tm, D_MODEL), lambda i: (i, 0)),
                   pl.BlockSpec((tm, LANES), lambda i: (i, 0)),
                   pl.BlockSpec((SUBLANES, LANES), const)],
        out_shape=[jax.ShapeDtypeStruct((NT, D_MODEL), F32),
                   jax.ShapeDtypeStruct((NT, LANES), F32),
                   jax.ShapeDtypeStruct((SUBLANES, LANES), F32)],
        scratch_shapes=[pltpu.VMEM((1, LANES), F32)],
        compiler_params=_cparams(("arbitrary",)),
        name="oproj_ln2_router",
    )(o_p, o_s, h1, wo, ln_g, ln_b, wr, br)


def _row_copy(src_ref, src_row, dst_ref, dst_row, sem):
    return pltpu.make_async_copy(src_ref.at[pl.ds(src_row, 1), :], dst_ref.at[pl.ds(dst_row, 1), :], sem)


def _dispatch_kernel(pos_ref, h_ref, xs_ref, sem, *, tm):
    base = pl.program_id(0) * tm

    def start(r, carry):
        for k in range(TOP_K):
            _row_copy(h_ref, r, xs_ref, pos_ref[TOP_K * (base + r) + k], sem).start()
        return carry

    def wait(r, carry):
        for k in range(TOP_K):
            _row_copy(h_ref, r, xs_ref, pos_ref[TOP_K * (base + r) + k], sem).wait()
        return carry

    lax.fori_loop(0, tm, start, 0, unroll=DMA_UNROLL)
    lax.fori_loop(0, tm, wait, 0, unroll=DMA_UNROLL)


def _dispatch(pos_flat, h2, cap, *, tm):
    NT = h2.shape[0]
    tm = _tile(NT, tm)
    return pl.pallas_call(
        functools.partial(_dispatch_kernel, tm=tm),
        grid_spec=pltpu.PrefetchScalarGridSpec(
            num_scalar_prefetch=1,
            grid=(NT // tm,),
            in_specs=[pl.BlockSpec((tm, D_MODEL), lambda i, pos: (i, 0))],
            out_specs=pl.BlockSpec(memory_space=pl.ANY),
            scratch_shapes=[pltpu.SemaphoreType.DMA(())],
        ),
        out_shape=jax.ShapeDtypeStruct((cap, D_MODEL), F32),
        compiler_params=_cparams(("arbitrary",)),
        name="moe_dispatch",
    )(pos_flat, h2)


def _expert_kernel(it_ref, ie_ref, lo_ref, hi_ref, first_ref, n_ref, x_ref, wg_ref, wu_ref, wd_ref, y_ref,
                   wg_s, wu_s, wd_s):
    i = pl.program_id(0)

    @pl.when(i < n_ref[0])
    def _compute():
        fresh = jnp.logical_or(i == 0, ie_ref[i] != ie_ref[jnp.maximum(i - 1, 0)])

        @pl.when(fresh)
        def _cast():
            wg_s[...] = wg_ref[0].astype(BF16)
            wu_s[...] = wu_ref[0].astype(BF16)
            wd_s[...] = wd_ref[0].astype(BF16)

        x = x_ref[...].astype(BF16)
        gate = jnp.dot(x, wg_s[...], preferred_element_type=F32)
        up = jnp.dot(x, wu_s[...], preferred_element_type=F32)
        hid = gate * _sigmoid(gate) * up
        y = jnp.dot(hid.astype(BF16), wd_s[...], preferred_element_type=F32)
        TR = y.shape[0]
        rows = it_ref[i] * TR + lax.broadcasted_iota(jnp.int32, (TR, 1), 0)
        mine = (rows >= lo_ref[i]) & (rows < hi_ref[i])

        @pl.when(first_ref[i] == 1)
        def _first():
            y_ref[...] = jnp.where(mine, y, 0.0)

        @pl.when(first_ref[i] == 0)
        def _later():
            y_ref[...] = jnp.where(mine, y, y_ref[...])


def _experts(items, xs, e_wg, e_wu, e_wd):
    TR = MOE_ROW_TILE
    n_work = items[0].shape[0]
    rows = lambda i, it, ie, lo, hi, fi, n: (it[i], 0)
    wsel = lambda i, it, ie, lo, hi, fi, n: (ie[i], 0, 0)
    return pl.pallas_call(
        _expert_kernel,
        grid_spec=pltpu.PrefetchScalarGridSpec(
            num_scalar_prefetch=6,
            grid=(n_work,),
            in_specs=[pl.BlockSpec((TR, D_MODEL), rows),
                      pl.BlockSpec((1, D_MODEL, EXP_FF), wsel),
                      pl.BlockSpec((1, D_MODEL, EXP_FF), wsel),
                      pl.BlockSpec((1, EXP_FF, D_MODEL), wsel)],
            out_specs=pl.BlockSpec((TR, D_MODEL), rows),
            scratch_shapes=[pltpu.VMEM((D_MODEL, EXP_FF), BF16), pltpu.VMEM((D_MODEL, EXP_FF), BF16),
                            pltpu.VMEM((EXP_FF, D_MODEL), BF16)],
        ),
        out_shape=jax.ShapeDtypeStruct(xs.shape, F32),
        compiler_params=_cparams(("arbitrary",)),
        name="moe_experts",
    )(*items, xs, e_wg, e_wu, e_wd)


def _moe_schedule(route, counts, n_rows):
    TR = MOE_ROW_TILE
    assert n_rows % TR == 0
    n_work = n_rows // TR + N_EXPERTS - 1
    rt = route[:, :SUBLANES].T
    cnt = counts[0, ROUTE_LANE0:ROUTE_LANE0 + N_EXPERTS].astype(jnp.int32)
    g_end = jnp.cumsum(cnt)
    g_start = g_end - cnt
    onehot = rt[0:2, :, None] == jnp.arange(N_EXPERTS, dtype=F32)
    start_of = jnp.sum(jnp.where(onehot, g_start.astype(F32), 0.0), axis=-1)
    pos_flat = (start_of + rt[4:6]).astype(jnp.int32).T.reshape(-1)
    first_tile = g_start // TR
    n_items_e = jnp.where(cnt > 0, (g_end - 1) // TR - first_tile + 1, 0)
    item_end = jnp.cumsum(n_items_e)
    n_items = item_end[-1:]
    idx = jnp.minimum(jnp.arange(n_work, dtype=jnp.int32), n_items[0] - 1)
    ie = jnp.minimum(jnp.sum((item_end[None, :] <= idx[:, None]).astype(jnp.int32), axis=1), N_EXPERTS - 1)
    it = first_tile[ie] + idx - (item_end - n_items_e)[ie]
    first = jnp.concatenate([jnp.ones((1,), jnp.int32), (it[1:] != it[:-1]).astype(jnp.int32)])
    items = (it.astype(jnp.int32), ie, g_start[ie], g_end[ie], first, n_items.astype(jnp.int32))
    return pos_flat, items


def _combine_kernel(pos_ref, h2_ref, route_ref, ys_ref, g_ref, b_ref, yp_ref, ysm_ref,
                    buf, sem, *, tm, n_prompt_tiles):
    i = pl.program_id(0)
    base = i * tm

    def start(r, carry):
        for k in range(TOP_K):
            _row_copy(ys_ref, pos_ref[TOP_K * (base + r) + k], buf.at[k], r, sem).start()
        return carry

    def wait(r, carry):
        for k in range(TOP_K):
            _row_copy(ys_ref, pos_ref[TOP_K * (base + r) + k], buf.at[k], r, sem).wait()
        return carry

    lax.fori_loop(0, tm, start, 0, unroll=DMA_UNROLL)
    lax.fori_loop(0, tm, wait, 0, unroll=DMA_UNROLL)
    route = route_ref[...]
    moe = route[:, 2:3] * buf[0] + route[:, 3:4] * buf[1]
    y = _layer_norm(ALPHA * h2_ref[...] + moe, g_ref[...], b_ref[...])

    @pl.when(i < n_prompt_tiles)
    def _prompt():
        yp_ref[...] = y

    @pl.when(i >= n_prompt_tiles)
    def _sample():
        ysm_ref[...] = y


def _combine(pos_flat, h2, route, ys, ln_g, ln_b, *, Np, Ns, tm):
    NT = h2.shape[0]
    tm = _tile(math.gcd(Np, Ns), tm)
    npt = Np // tm
    const = lambda i, pos: (0, 0)
    return pl.pallas_call(
        functools.partial(_combine_kernel, tm=tm, n_prompt_tiles=npt),
        grid_spec=pltpu.PrefetchScalarGridSpec(
            num_scalar_prefetch=1,
            grid=(NT // tm,),
            in_specs=[pl.BlockSpec((tm, D_MODEL), lambda i, pos: (i, 0)),
                      pl.BlockSpec((tm, LANES), lambda i, pos: (i, 0)),
                      pl.BlockSpec(memory_space=pl.ANY),
                      pl.BlockSpec((1, D_MODEL), const),
                      pl.BlockSpec((1, D_MODEL), const)],
            out_specs=[pl.BlockSpec((tm, D_MODEL), lambda i, pos: (jnp.minimum(i, npt - 1), 0)),
                       pl.BlockSpec((tm, D_MODEL), lambda i, pos: (jnp.maximum(i - npt, 0), 0))],
            scratch_shapes=[pltpu.VMEM((TOP_K, tm, D_MODEL), F32), pltpu.SemaphoreType.DMA(())],
        ),
        out_shape=[jax.ShapeDtypeStruct((Np, D_MODEL), F32),
                   jax.ShapeDtypeStruct((Ns, D_MODEL), F32)],
        compiler_params=_cparams(("arbitrary",)),
        name="moe_combine_ln3",
    )(pos_flat, h2, route, ys, ln_g, ln_b)


def kernel(x_prompt, x_sample, mem_prompt, cache_mem_k, cache_mem_v, state_mlstm_C, state_mlstm_n,
           state_mlstm_m, state_mlstm_conv, state_hgrn_S, w_in, b_in, conv_w, mlstm_gn, lb_logits, hgrn_gn,
           w_bm, w_bh, w_out, ln1_g, ln1_b, xa_wq, xa_wk, xa_wv, xa_wo, ln2_g, ln2_b,
           r1_w, r1_b, r2_w, r2_b, e_wg, e_wu, e_wd, ln3_g, ln3_b):
    Bp, Tp, _ = x_prompt.shape
    Bs, Ts, _ = x_sample.shape
    MEM = mem_prompt.shape[1]
    Np, Ns = Bp * Tp, Bs * Ts
    NT = Np + Ns
    Lp_m = math.gcd(Tp, 256)
    Lp_h = math.gcd(Tp, 64)
    Ls = Ts
    assert Ts % SUBLANES == 0 and Np % Ls == 0

    xp2 = x_prompt.reshape(Np, D_MODEL)
    xs2 = x_sample.reshape(Ns, D_MODEL)

    w = w_in[0]
    bi = b_in[0]
    w16 = w.astype(BF16)
    w_rest = w16[:, GATE_LO + 2 * M_H:]
    b_main = jnp.concatenate([bi[:GATE_LO], bi[GATE_LO + 2 * M_H:]])[None]
    w_gate = jnp.pad(w[:, GATE_LO:GATE_LO + 2 * M_H], ((0, 0), (0, LANES - 2 * M_H))).astype(BF16)
    b_gate = jnp.pad(bi[GATE_LO:GATE_LO + 2 * M_H], (0, LANES - 2 * M_H))[None]
    lb = jnp.cumsum(jax.nn.softmax(lb_logits.astype(F32), axis=0), axis=0)[0][None]
    wr = jnp.zeros((D_MODEL, LANES), F32)
    wr = wr.at[:, :N_GROUPS].set(r1_w[0])
    wr = wr.at[:, ROUTE_LANE0:ROUTE_LANE0 + N_EXPERTS].set(
        jnp.transpose(r2_w[0], (1, 0, 2)).reshape(D_MODEL, N_EXPERTS)).astype(BF16)
    br = jnp.zeros((1, LANES), F32)
    br = br.at[0, :N_GROUPS].set(r1_b[0])
    br = br.at[0, ROUTE_LANE0:ROUTE_LANE0 + N_EXPERTS].set(r2_b[0].reshape(N_EXPERTS))
    zeros_d = jnp.zeros((1, D_MODEL), F32)

    memb = mem_prompt.reshape(Bp * MEM, D_MODEL).astype(BF16)
    mk = _matmul_bias(memb, xa_wk[0], zeros_d, tm=1024, tn=1024, name="mem_k")
    mv = _matmul_bias(memb, xa_wv[0], zeros_d, tm=1024, tn=1024, name="mem_v")

    u_main, u_gate = _in_proj(xp2, xs2, w16, w_rest, b_main, w_gate, b_gate, tm=1024, tn=1024)

    padc = lambda c: jnp.pad(c, ((0, 0), (SUBLANES - (CONV_K - 1), 0), (0, 0)))
    zC = jnp.zeros((Bp, M_H, M_DK, M_DV), F32)
    zn = jnp.zeros((Bp, M_H, M_DK), F32)
    zm = jnp.zeros((Bp, 1, M_H), F32)
    zconv = jnp.zeros((Bp, SUBLANES, 2 * M_W), F32)
    zS = jnp.zeros((Bp, HG_H, HG_DK, HG_DV), F32)
    cw = conv_w[0]
    mgn = mlstm_gn[0][None]
    hgn = hgrn_gn[0][None]
    hm_p, C_p, n_p, m_p, conv_p = _mlstm(u_main, u_gate, zconv, zC, zn, zm, cw, mgn,
                                         B=Bp, T=Tp, L=Lp_m, row0=0)
    hm_s, C_s, n_s, m_s, conv_s = _mlstm(u_main, u_gate, padc(state_mlstm_conv[0]), state_mlstm_C[0],
                                         state_mlstm_n[0], state_mlstm_m[0][:, None, :], cw, mgn,
                                         B=Bs, T=Ts, L=Ls, row0=Np, NB=SAMPLE_SEQS_PER_STEP)
    og_p, S_p = _hgrn(u_main, zS, lb, hgn, B=Bp, T=Tp, L=Lp_h, row0=0)
    og_s, S_s = _hgrn(u_main, state_hgrn_S[0], lb, hgn, B=Bs, T=Ts, L=Ls, row0=Np, NB=SAMPLE_SEQS_PER_STEP)

    h1, h1b = _merge(hm_p, hm_s, og_p, og_s, u_main, xp2, xs2, w_bm[0].astype(BF16), w_bh[0].astype(BF16),
                     w_out[0].astype(BF16), ln1_g, ln1_b, tm=256)

    q = _matmul_bias(h1b, xa_wq[0], zeros_d, tm=1024, tn=1024, name="xa_q")
    att_p = _attention(q, mk, mv, B=Bp, T=Tp, tq=512, row0=0)
    att_s = _attention_cache(q, cache_mem_k, cache_mem_v, B=Bs, T=Ts, row0=Np)
    h2, route, counts = _oln(att_p, att_s, h1, xa_wo[0].astype(BF16), ln2_g, ln2_b, wr, br, tm=512)

    pos_flat, items = _moe_schedule(route, counts, TOP_K * NT)
    xs_sorted = _dispatch(pos_flat, h2, TOP_K * NT, tm=256)
    ys_sorted = _experts(items, xs_sorted, e_wg[0], e_wu[0], e_wd[0])
    y_p, y_s = _combine(pos_flat, h2, route, ys_sorted, ln3_g, ln3_b, Np=Np, Ns=Ns, tm=256)

    kv5 = lambda a: a.reshape(1, Bp, MEM, XA_H, XA_D)
    return (y_p.reshape(Bp, Tp, D_MODEL), y_s.reshape(Bs, Ts, D_MODEL), kv5(mk), kv5(mv),
            C_p[None], n_p[None], m_p.reshape(1, Bp, M_H), conv_p[None], S_p[None],
            C_s[None], n_s[None], m_s.reshape(1, Bs, M_H), conv_s[None], S_s[None])
```

```python
import functools
import math

import jax
import jax.numpy as jnp
from jax import lax
from jax.experimental import pallas as pl
from jax.experimental.pallas import tpu as pltpu

F32 = jnp.float32
BF16 = jnp.bfloat16

D_MODEL = 2048
M_W = 1024
M_H = 4
M_DK = 256
M_DV = 256
CONV_K = 4
HG_W = 1024
HG_H = 8
HG_DK = 128
HG_DV = 128
XA_H = 4
XA_D = 512
N_GROUPS = 4
EXP_PER_GROUP = 8
N_EXPERTS = 32
TOP_K = 2
EXP_FF = 512
DEPTH = 1
ALPHA = (2 * DEPTH) ** 0.25
LN_EPS = 1e-5

COL_QK, COL_V, COL_O = 0, 2048, 3072
COL_QH, COL_FH, COL_IH, COL_GH = 4096, 5120, 6144, 7168
COL_GM, COL_GHH = 8192, 10240
N_MAIN = 12288
GATE_LO = 4 * M_W
LANES = 128
SUBLANES = 8
ROUTE_LANE0 = N_GROUPS

VMEM_LIMIT = 56 << 20
MOE_ROW_TILE = 256
SAMPLE_SEQS_PER_STEP = 4
DMA_UNROLL = 8


def _cparams(sem, vmem=VMEM_LIMIT):
    return pltpu.CompilerParams(dimension_semantics=sem, vmem_limit_bytes=vmem)


def _tile(n, pref):
    t = math.gcd(n, pref)
    assert t % SUBLANES == 0, (n, pref)
    return t


def _bdot(a, b):
    return jnp.dot(a.astype(BF16), b.astype(BF16), preferred_element_type=F32)


def _bdot_nt(a, b):
    return lax.dot_general(a.astype(BF16), b.astype(BF16), (((1,), (1,)), ((), ())),
                           preferred_element_type=F32)


def _bdot_tn(a, b):
    return lax.dot_general(a.astype(BF16), b.astype(BF16), (((0,), (0,)), ((), ())),
                           preferred_element_type=F32)


def _sigmoid(x):
    return 1.0 / (1.0 + jnp.exp(-x))


def _cumsum_rows(x, seg=None):
    n = seg or x.shape[0]
    assert n & (n - 1) == 0
    row = lax.broadcasted_iota(jnp.int32, x.shape, 0) & (n - 1)
    d = 1
    while d < n:
        x = x + jnp.where(row >= d, pltpu.roll(x, d, axis=0), 0.0)
        d *= 2
    return x


def _col_to_row(col, eye):
    return jnp.sum(jnp.where(eye, col, 0.0), axis=0, keepdims=True)


def _row_to_col(row, eye):
    return jnp.sum(jnp.where(eye, row, 0.0), axis=1, keepdims=True)


def _layer_norm(x, g, b):
    mu = jnp.mean(x, axis=-1, keepdims=True)
    xc = x - mu
    var = jnp.mean(xc * xc, axis=-1, keepdims=True)
    return xc * lax.rsqrt(var + LN_EPS) * g + b


def _mm_kernel(x_ref, w_ref, b_ref, o_ref, wb_scr):
    @pl.when(pl.program_id(1) == 0)
    def _new_weight_tile():
        wb_scr[...] = w_ref[...].astype(BF16)

    acc = jnp.dot(x_ref[...], wb_scr[...], preferred_element_type=F32)
    o_ref[...] = (acc + b_ref[...]).astype(o_ref.dtype)


def _matmul_bias(x, w, b, *, tm, tn, out_dtype=F32, name):
    M, K = x.shape
    N = w.shape[1]
    tm = _tile(M, tm)
    tn = _tile(N, tn)
    return pl.pallas_call(
        _mm_kernel,
        grid=(N // tn, M // tm),
        in_specs=[pl.BlockSpec((tm, K), lambda j, i: (i, 0)),
                  pl.BlockSpec((K, tn), lambda j, i: (0, j)),
                  pl.BlockSpec((1, tn), lambda j, i: (0, j))],
        out_specs=pl.BlockSpec((tm, tn), lambda j, i: (i, j)),
        out_shape=jax.ShapeDtypeStruct((M, N), out_dtype),
        scratch_shapes=[pltpu.VMEM((K, tn), BF16)],
        compiler_params=_cparams(("parallel", "arbitrary")),
        name=name,
    )(x, w, b)


def _inproj_kernel(xp_ref, xs_ref, wa_ref, wb_ref, b_ref, wg_ref, bg_ref, u_ref, ug_ref, xb_scr, *,
                   n_prompt_tiles, n_a_tiles):
    i = pl.program_id(0)
    j = pl.program_id(1)

    @pl.when(j == 0)
    def _row_tile():
        xb = jnp.where(i < n_prompt_tiles, xp_ref[...], xs_ref[...]).astype(BF16)
        xb_scr[...] = xb
        ug_ref[...] = jnp.dot(xb, wg_ref[...], preferred_element_type=F32) + bg_ref[...]

    @pl.when(j < n_a_tiles)
    def _lead():
        u_ref[...] = jnp.dot(xb_scr[...], wa_ref[...], preferred_element_type=F32) + b_ref[...]

    @pl.when(j >= n_a_tiles)
    def _rest():
        u_ref[...] = jnp.dot(xb_scr[...], wb_ref[...], preferred_element_type=F32) + b_ref[...]


def _in_proj(x_p, x_s, w_all, w_rest, b_main, w_gate, b_gate, *, tm, tn):
    Np, Ns = x_p.shape[0], x_s.shape[0]
    NT = Np + Ns
    K = w_all.shape[0]
    N = GATE_LO + w_rest.shape[1]
    tm = _tile(math.gcd(Np, Ns), tm)
    tn = _tile(math.gcd(GATE_LO, w_rest.shape[1]), tn)
    npt = Np // tm
    na = GATE_LO // tn
    return pl.pallas_call(
        functools.partial(_inproj_kernel, n_prompt_tiles=npt, n_a_tiles=na),
        grid=(NT // tm, N // tn),
        in_specs=[pl.BlockSpec((tm, K), lambda i, j: (jnp.minimum(i, npt - 1), 0), pipeline_mode=pl.Buffered(1)),
                  pl.BlockSpec((tm, K), lambda i, j: (jnp.maximum(i - npt, 0), 0), pipeline_mode=pl.Buffered(1)),
                  pl.BlockSpec((K, tn), lambda i, j: (0, jnp.minimum(j, na - 1))),
                  pl.BlockSpec((K, tn), lambda i, j: (0, jnp.maximum(j - na, 0))),
                  pl.BlockSpec((1, tn), lambda i, j: (0, j)),
                  pl.BlockSpec((K, LANES), lambda i, j: (0, 0)),
                  pl.BlockSpec((1, LANES), lambda i, j: (0, 0))],
        out_specs=[pl.BlockSpec((tm, tn), lambda i, j: (i, j)),
                   pl.BlockSpec((tm, LANES), lambda i, j: (i, 0))],
        out_shape=[jax.ShapeDtypeStruct((NT, N), F32), jax.ShapeDtypeStruct((NT, LANES), F32)],
        scratch_shapes=[pltpu.VMEM((tm, K), BF16)],
        compiler_params=_cparams(("parallel", "arbitrary")),
        name="in_proj",
    )(x_p, x_s, w_all, w_rest, b_main, w_gate, b_gate)


def _mlstm_kernel(*refs, L, NC, NB):
    (qk_ref, v_ref, o_ref, g_ref, conv0_ref, C0_ref, n0_ref, m0_ref, cw_ref, gn_ref,
     h_ref, Co_ref, no_ref, mo_ref, convo_ref, C_scr, n_scr, m_scr, tail_scr) = refs
    c = pl.program_id(1)
    if NC > 1:
        @pl.when(c == 0)
        def _init():
            C_scr[...] = C0_ref[0]
            n_scr[...] = n0_ref[0]
            m_scr[...] = m0_ref[0]
            tail_scr[...] = conv0_ref[0]

    cw = cw_ref[...]
    gn = gn_ref[...]
    ti = lax.broadcasted_iota(jnp.int32, (L, L), 0)
    si = lax.broadcasted_iota(jnp.int32, (L, L), 1)
    eye = ti == si
    causal = si <= ti
    head_lane = lax.broadcasted_iota(jnp.int32, (1, M_H), 1)
    head_row = lax.broadcasted_iota(jnp.int32, (M_H, M_DK), 0)

    seqs = []
    for nb in range(NB):
        rows = slice(nb * L, (nb + 1) * L)
        if NC == 1:
            sq = dict(C_in=C0_ref.at[nb], C_out=Co_ref.at[nb], n_out=no_ref.at[nb], m_out=mo_ref.at[nb])
            n_all, m_all, tail = n0_ref[nb], m0_ref[nb], conv0_ref[nb]
        else:
            sq = dict(C_in=C_scr, C_out=C_scr, n_out=n_scr, m_out=m_scr)
            n_all, m_all, tail = n_scr[...], m_scr[...], tail_scr[...]
        qk_pre = qk_ref[rows, :]
        ext = jnp.concatenate([tail, qk_pre], axis=0)
        acc = qk_pre * cw[CONV_K - 1:CONV_K, :]
        for j in range(1, CONV_K):
            acc = acc + pltpu.roll(ext, j, axis=0)[SUBLANES:, :] * cw[CONV_K - 1 - j:CONV_K - j, :]
        if NC > 1:
            tail_scr[...] = qk_pre[L - SUBLANES:, :]
        qk = acc * _sigmoid(acc)
        g = g_ref[rows, :]
        lf_all = jnp.minimum(g, 0.0) - jnp.log(1.0 + jnp.exp(-jnp.abs(g)))
        F_all = _cumsum_rows(lf_all)
        heads = []
        for h in range(M_H):
            ks = slice(h * M_DK, (h + 1) * M_DK)
            q = qk[:, ks] * (M_DK ** -0.5)
            k = qk[:, M_W + h * M_DK:M_W + (h + 1) * M_DK]
            v = v_ref[rows, ks]
            ig = g[:, h:h + 1]
            F = F_all[:, M_H + h:M_H + h + 1]
            m_prev = m_all[:, h:h + 1]
            r_row = _col_to_row(ig - F, eye)
            Dm = jnp.where(causal, F + r_row, -jnp.inf)
            init_w = F + m_prev
            m_t = jnp.maximum(init_w, jnp.max(Dm, axis=1, keepdims=True))
            P = jnp.exp(Dm - m_t)
            a0 = jnp.exp(init_w - m_t)
            FL = F[L - 1:L, :]
            mL = m_t[L - 1:L, :]
            wL = jnp.exp(FL - F + ig - mL)
            decay = jnp.exp(FL + m_prev - mL)
            heads.append(dict(ks=ks, q=q, k=k, v=v, m_t=m_t, P=P, a0=a0, mL=mL, decay=decay, kw=wL * k))
        sq.update(rows=rows, heads=heads, n_all=n_all, m_all=m_all, qk_pre=qk_pre)
        seqs.append(sq)

    for sq in seqs:
        for h, d in enumerate(sq["heads"]):
            d["S"] = _bdot_nt(d["q"], d["k"])
            d["qC"] = _bdot(d["q"], sq["C_in"][h])
            d["kv"] = _bdot_tn(d["kw"], d["v"])

    for sq in seqs:
        for d in sq["heads"]:
            d["Sc"] = d["S"] * d["P"]
            d["num"] = _bdot(d["Sc"], d["v"])

    for nb, sq in enumerate(seqs):
        rows, n_new, m_new = sq["rows"], sq["n_all"], sq["m_all"]
        for h, d in enumerate(sq["heads"]):
            ks, q, a0, decay = d["ks"], d["q"], d["a0"], d["decay"]
            n_row = sq["n_all"][h:h + 1, :]
            num = d["num"] + a0 * d["qC"]
            den = jnp.sum(d["Sc"], axis=1, keepdims=True) + a0 * jnp.sum(q * n_row, axis=1, keepdims=True)
            hh = num * (1.0 / jnp.maximum(jnp.abs(den), jnp.exp(-d["m_t"])))
            sq["C_out"][h] = decay * sq["C_in"][h] + d["kv"]
            n_new = jnp.where(head_row == h, decay * n_row + jnp.sum(d["kw"], axis=0, keepdims=True), n_new)
            m_new = jnp.where(head_lane == h, d["mL"], m_new)

            hm = _sigmoid(o_ref[rows, ks]) * hh
            hm = hm - jnp.mean(hm, axis=1, keepdims=True)
            hm = hm * lax.rsqrt(jnp.mean(hm * hm, axis=1, keepdims=True) + LN_EPS) * gn[:, ks]
            h_ref[rows, ks] = hm
        sq["n_out"][...] = n_new
        sq["m_out"][...] = m_new
        conv_tail = sq["qk_pre"][L - (CONV_K - 1):, :]
        if NC == 1:
            convo_ref[nb] = conv_tail
        else:
            @pl.when(c == NC - 1)
            def _fin():
                Co_ref[0] = C_scr[...]
                no_ref[0] = n_new
                mo_ref[0] = m_new
                convo_ref[0] = conv_tail


def _mlstm(u_main, u_gate, conv0p, C0, n0, m0, conv_w, gn, *, B, T, L, row0, NB=1):
    NC = T // L
    assert NB == 1 or NC == 1
    assert B % NB == 0 and row0 % (NB * L) == 0
    R = NB * L
    rb0 = row0 // R
    tok = lambda b, c: rb0 + b * NC + c
    in_specs = [
        pl.BlockSpec((R, 2 * M_W), lambda b, c: (tok(b, c), COL_QK // (2 * M_W))),
        pl.BlockSpec((R, M_W), lambda b, c: (tok(b, c), COL_V // M_W)),
        pl.BlockSpec((R, M_W), lambda b, c: (tok(b, c), COL_O // M_W)),
        pl.BlockSpec((R, LANES), lambda b, c: (tok(b, c), 0)),
        pl.BlockSpec((NB, SUBLANES, 2 * M_W), lambda b, c: (b, 0, 0)),
        pl.BlockSpec((NB, M_H, M_DK, M_DV), lambda b, c: (b, 0, 0, 0)),
        pl.BlockSpec((NB, M_H, M_DK), lambda b, c: (b, 0, 0)),
        pl.BlockSpec((NB, 1, M_H), lambda b, c: (b, 0, 0)),
        pl.BlockSpec((CONV_K, 2 * M_W), lambda b, c: (0, 0)),
        pl.BlockSpec((1, M_W), lambda b, c: (0, 0)),
    ]
    args = [u_main, u_main, u_main, u_gate, conv0p, C0, n0, m0, conv_w, gn]
    out_specs = [
        pl.BlockSpec((R, M_W), lambda b, c: (b * NC + c, 0)),
        pl.BlockSpec((NB, M_H, M_DK, M_DV), lambda b, c: (b, 0, 0, 0)),
        pl.BlockSpec((NB, M_H, M_DK), lambda b, c: (b, 0, 0)),
        pl.BlockSpec((NB, 1, M_H), lambda b, c: (b, 0, 0)),
        pl.BlockSpec((NB, CONV_K - 1, 2 * M_W), lambda b, c: (b, 0, 0)),
    ]
    out_shape = [
        jax.ShapeDtypeStruct((B * T, M_W), F32),
        jax.ShapeDtypeStruct((B, M_H, M_DK, M_DV), F32),
        jax.ShapeDtypeStruct((B, M_H, M_DK), F32),
        jax.ShapeDtypeStruct((B, 1, M_H), F32),
        jax.ShapeDtypeStruct((B, CONV_K - 1, 2 * M_W), F32),
    ]
    return pl.pallas_call(
        functools.partial(_mlstm_kernel, L=L, NC=NC, NB=NB),
        grid=(B // NB, NC),
        in_specs=in_specs,
        out_specs=out_specs,
        out_shape=out_shape,
        scratch_shapes=[pltpu.VMEM((M_H, M_DK, M_DV), F32), pltpu.VMEM((M_H, M_DK), F32),
                        pltpu.VMEM((1, M_H), F32), pltpu.VMEM((SUBLANES, 2 * M_W), F32)],
        compiler_params=_cparams(("parallel", "arbitrary")),
        name=f"mlstm_L{L}",
    )(*args)


def _hgrn_kernel(*refs, L, NC, NB):
    (q_ref, f_ref, i_ref, g_ref, S0_ref, lb_ref, gn_ref, o_ref, So_ref, S_scr) = refs
    c = pl.program_id(1)
    if NC > 1:
        @pl.when(c == 0)
        def _init():
            S_scr[...] = S0_ref[0]

    lb = lb_ref[...]
    f = lb + (1.0 - lb) * _sigmoid(f_ref[...])
    kk = 1.0 - f
    b = _cumsum_rows(jnp.log(f), seg=L)
    qh = q_ref[...]
    q = qh * _sigmoid(qh)
    v = i_ref[...]
    gh = g_ref[...]
    gsilu = gh * _sigmoid(gh)
    gn = gn_ref[...]
    ti = lax.broadcasted_iota(jnp.int32, (L, L), 0)
    si = lax.broadcasted_iota(jnp.int32, (L, L), 1)
    causal = si <= ti
    ci = lax.broadcasted_iota(jnp.int32, (HG_DK, HG_DK), 0)
    cj = lax.broadcasted_iota(jnp.int32, (HG_DK, HG_DK), 1)
    eye = ci == cj
    mid = max(L // 2 - 1, 0)
    hsl = [slice(h * HG_DK, (h + 1) * HG_DK) for h in range(HG_H)]

    seqs = []
    for nb in range(NB):
        rows = slice(nb * L, (nb + 1) * L)
        bs, qs, ks = b[rows], q[rows], kk[rows]
        bL = bs[L - 1:L, :]
        bm = bs[mid:mid + 1, :]
        S_in, S_out = (S0_ref.at[nb], So_ref.at[nb]) if NC == 1 else (S_scr, S_scr)
        seqs.append(dict(rows=rows, bL=bL, v=v[rows], q_in=qs * jnp.exp(bs), q_t=qs * jnp.exp(bs - bm),
                         k_t=ks * jnp.exp(bm - bs), k_st=ks * jnp.exp(bL - bs), S_in=S_in, S_out=S_out))

    for sq in seqs:
        sq["A"] = [jnp.where(causal, _bdot_nt(sq["q_t"][:, hs], sq["k_t"][:, hs]), 0.0) for hs in hsl]
        sq["qS"] = [_bdot(sq["q_in"][:, hs], sq["S_in"][h]) for h, hs in enumerate(hsl)]
        sq["kv"] = [_bdot_tn(sq["k_st"][:, hs], sq["v"][:, hs]) for hs in hsl]
    for sq in seqs:
        sq["o_intra"] = [_bdot(sq["A"][h], sq["v"][:, hs]) for h, hs in enumerate(hsl)]
    for sq in seqs:
        rows = sq["rows"]
        for h, hs in enumerate(hsl):
            o = sq["o_intra"][h] + sq["qS"][h]
            dec = jnp.exp(_row_to_col(sq["bL"][:, hs], eye))
            sq["S_out"][h] = dec * sq["S_in"][h] + sq["kv"][h]
            o = o * lax.rsqrt(jnp.mean(o * o, axis=1, keepdims=True) + LN_EPS) * gn[:, hs]
            o_ref[rows, hs] = o * gsilu[rows, hs]

    if NC > 1:
        @pl.when(c == NC - 1)
        def _fin():
            So_ref[0] = S_scr[...]


def _hgrn(u_main, S0, lb, gn, *, B, T, L, row0, NB=1):
    NC = T // L
    assert NB == 1 or NC == 1
    assert B % NB == 0 and row0 % (NB * L) == 0
    R = NB * L
    rb0 = row0 // R
    tok = lambda b, c: rb0 + b * NC + c
    col = lambda off: (lambda b, c: (tok(b, c), off // HG_W))
    in_specs = [
        pl.BlockSpec((R, HG_W), col(COL_QH)),
        pl.BlockSpec((R, HG_W), col(COL_FH)),
        pl.BlockSpec((R, HG_W), col(COL_IH)),
        pl.BlockSpec((R, HG_W), col(COL_GH)),
        pl.BlockSpec((NB, HG_H, HG_DK, HG_DV), lambda b, c: (b, 0, 0, 0)),
        pl.BlockSpec((1, HG_W), lambda b, c: (0, 0)),
        pl.BlockSpec((1, HG_W), lambda b, c: (0, 0)),
    ]
    args = [u_main, u_main, u_main, u_main, S0, lb, gn]
    return pl.pallas_call(
        functools.partial(_hgrn_kernel, L=L, NC=NC, NB=NB),
        grid=(B // NB, NC),
        in_specs=in_specs,
        out_specs=[pl.BlockSpec((R, HG_W), lambda b, c: (b * NC + c, 0)),
                   pl.BlockSpec((NB, HG_H, HG_DK, HG_DV), lambda b, c: (b, 0, 0, 0))],
        out_shape=[jax.ShapeDtypeStruct((B * T, HG_W), F32),
                   jax.ShapeDtypeStruct((B, HG_H, HG_DK, HG_DV), F32)],
        scratch_shapes=[pltpu.VMEM((HG_H, HG_DK, HG_DV), F32)],
        compiler_params=_cparams(("parallel", "arbitrary")),
        name=f"hgrn_L{L}",
    )(*args)


def _merge_kernel(hmp_ref, hms_ref, ogp_ref, ogs_ref, gm_ref, gh_ref, xp_ref, xs_ref, wbm_ref, wbh_ref,
                  wo_ref, g_ref, b_ref, h1_ref, h1b_ref, *, n_prompt_tiles):
    is_prompt = pl.program_id(0) < n_prompt_tiles
    a = _bdot(jnp.where(is_prompt, hmp_ref[...], hms_ref[...]), wbm_ref[...])
    bb = _bdot(jnp.where(is_prompt, ogp_ref[...], ogs_ref[...]), wbh_ref[...])
    merged = _sigmoid(gm_ref[...]) * a + _sigmoid(gh_ref[...]) * bb
    mix = _bdot(merged, wo_ref[...])
    x = jnp.where(is_prompt, xp_ref[...], xs_ref[...])
    h1 = _layer_norm(ALPHA * x + mix, g_ref[...], b_ref[...])
    h1_ref[...] = h1
    h1b_ref[...] = h1.astype(BF16)


def _merge(hm_p, hm_s, og_p, og_s, u_main, x_p, x_s, w_bm, w_bh, w_out, ln_g, ln_b, *, tm):
    Np, Ns = x_p.shape[0], x_s.shape[0]
    NT = Np + Ns
    tm = _tile(math.gcd(Np, Ns), tm)
    npt = Np // tm
    const = lambda i: (0, 0)
    prompt_rows = lambda i: (jnp.minimum(i, npt - 1), 0)
    sample_rows = lambda i: (jnp.maximum(i - npt, 0), 0)
    return pl.pallas_call(
        functools.partial(_merge_kernel, n_prompt_tiles=npt),
        grid=(NT // tm,),
        in_specs=[
            pl.BlockSpec((tm, M_W), prompt_rows),
            pl.BlockSpec((tm, M_W), sample_rows),
            pl.BlockSpec((tm, HG_W), prompt_rows),
            pl.BlockSpec((tm, HG_W), sample_rows),
            pl.BlockSpec((tm, D_MODEL), lambda i: (i, COL_GM // D_MODEL)),
            pl.BlockSpec((tm, D_MODEL), lambda i: (i, COL_GHH // D_MODEL)),
            pl.BlockSpec((tm, D_MODEL), prompt_rows),
            pl.BlockSpec((tm, D_MODEL), sample_rows),
            pl.BlockSpec((M_W, D_MODEL), const, pipeline_mode=pl.Buffered(1)),
            pl.BlockSpec((HG_W, D_MODEL), const, pipeline_mode=pl.Buffered(1)),
            pl.BlockSpec((D_MODEL, D_MODEL), const, pipeline_mode=pl.Buffered(1)),
            pl.BlockSpec((1, D_MODEL), const),
            pl.BlockSpec((1, D_MODEL), const),
        ],
        out_specs=[pl.BlockSpec((tm, D_MODEL), lambda i: (i, 0)),
                   pl.BlockSpec((tm, D_MODEL), lambda i: (i, 0))],
        out_shape=[jax.ShapeDtypeStruct((NT, D_MODEL), F32),
                   jax.ShapeDtypeStruct((NT, D_MODEL), BF16)],
        compiler_params=_cparams(("parallel",)),
        name="merge_ln1",
    )(hm_p, hm_s, og_p, og_s, u_main, u_main, x_p, x_s, w_bm, w_bh, w_out, ln_g, ln_b)


def _attn_kernel(q_ref, *refs):
    k_refs, v_refs, o_ref = refs[:XA_H], refs[XA_H:2 * XA_H], refs[2 * XA_H]
    for h in range(XA_H):
        sl = slice(h * XA_D, (h + 1) * XA_D)
        s = _bdot_nt(q_ref[:, sl], k_refs[h][...]) * (XA_D ** -0.5)
        e = jnp.exp(s - jnp.max(s, axis=1, keepdims=True))
        p = e / jnp.sum(e, axis=1, keepdims=True)
        o_ref[:, sl] = _bdot(p, v_refs[h][...])


def _kv_head_copies(k_hbm, v_hbm, kbuf, vbuf, sem, b, slot):
    out = []
    for j, (src, dst) in enumerate(((k_hbm, kbuf), (v_hbm, vbuf))):
        for h in range(XA_H):
            out.append(pltpu.make_async_copy(src.at[0, b, :, h, :], dst.at[slot, h], sem.at[slot, j, h]))
    return out


def _attn_cache_kernel(q_ref, k_hbm, v_hbm, o_ref, kbuf, vbuf, sem):
    b = pl.program_id(0)
    slot = b % 2

    @pl.when(b == 0)
    def _first():
        for c in _kv_head_copies(k_hbm, v_hbm, kbuf, vbuf, sem, b, slot):
            c.start()

    @pl.when(b + 1 < pl.num_programs(0))
    def _prefetch():
        for c in _kv_head_copies(k_hbm, v_hbm, kbuf, vbuf, sem, b + 1, 1 - slot):
            c.start()

    for c in _kv_head_copies(k_hbm, v_hbm, kbuf, vbuf, sem, b, slot):
        c.wait()
    T = q_ref.shape[0]
    s = jnp.concatenate([_bdot_nt(q_ref[:, h * XA_D:(h + 1) * XA_D], kbuf[slot, h]) for h in range(XA_H)],
                        axis=0) * (XA_D ** -0.5)
    e = jnp.exp(s - jnp.max(s, axis=1, keepdims=True))
    p = e / jnp.sum(e, axis=1, keepdims=True)
    for h in range(XA_H):
        o_ref[:, h * XA_D:(h + 1) * XA_D] = _bdot(p[h * T:(h + 1) * T], vbuf[slot, h])


def _attention_cache(q, cache_k, cache_v, *, B, T, row0):
    M = cache_k.shape[2]
    rb0 = row0 // T
    return pl.pallas_call(
        _attn_cache_kernel,
        grid=(B,),
        in_specs=[pl.BlockSpec((T, D_MODEL), lambda b: (rb0 + b, 0)),
                  pl.BlockSpec(memory_space=pl.ANY), pl.BlockSpec(memory_space=pl.ANY)],
        out_specs=pl.BlockSpec((T, D_MODEL), lambda b: (b, 0)),
        out_shape=jax.ShapeDtypeStruct((B * T, D_MODEL), F32),
        scratch_shapes=[pltpu.VMEM((2, XA_H, M, XA_D), F32), pltpu.VMEM((2, XA_H, M, XA_D), F32),
                        pltpu.SemaphoreType.DMA((2, 2, XA_H))],
        compiler_params=_cparams(("arbitrary",)),
        name="xattn_cache",
    )(q, cache_k, cache_v)


def _attention(q, mem_k, mem_v, *, B, T, tq, row0):
    tq = _tile(T, tq)
    nq = T // tq
    rb0 = row0 // tq
    M = mem_k.shape[0] // B
    kv_specs = [pl.BlockSpec((M, XA_D), lambda b, t, h=h: (b, h)) for h in range(XA_H)]
    return pl.pallas_call(
        _attn_kernel,
        grid=(B, nq),
        in_specs=[pl.BlockSpec((tq, D_MODEL), lambda b, t: (rb0 + b * nq + t, 0))] + kv_specs + kv_specs,
        out_specs=pl.BlockSpec((tq, D_MODEL), lambda b, t: (b * nq + t, 0)),
        out_shape=jax.ShapeDtypeStruct((B * T, D_MODEL), F32),
        compiler_params=_cparams(("parallel", "parallel")),
        name=f"xattn_T{T}",
    )(q, *([mem_k] * XA_H), *([mem_v] * XA_H))


def _oln_kernel(op_ref, os_ref, h1_ref, wo_ref, g_ref, b_ref, wr_ref, br_ref,
                h2_ref, route_ref, cnt_ref, carry_scr, *, n_prompt_tiles):
    i = pl.program_id(0)

    @pl.when(i == 0)
    def _init():
        carry_scr[...] = jnp.zeros_like(carry_scr)

    att = jnp.where(i < n_prompt_tiles, op_ref[...], os_ref[...])
    h2 = _layer_norm(ALPHA * h1_ref[...] + _bdot(att, wo_ref[...]), g_ref[...], b_ref[...])
    h2_ref[...] = h2

    logits = _bdot(h2, wr_ref[...]) + br_ref[...]
    tm = logits.shape[0]
    lane = lax.broadcasted_iota(jnp.int32, logits.shape, 1)
    lane_f = lane.astype(F32)
    neg = -jnp.inf

    def first_argmax(vals):
        mx = jnp.max(vals, axis=1, keepdims=True)
        idx = jnp.min(jnp.where(vals == mx, lane_f, float(LANES)), axis=1, keepdims=True)
        return mx, idx.astype(jnp.int32)

    gl = jnp.where(lane < N_GROUPS, logits, neg)
    gmax, grp = first_argmax(gl)
    p_grp = 1.0 / jnp.sum(jnp.exp(gl - gmax), axis=1, keepdims=True)
    lo = ROUTE_LANE0 + grp * EXP_PER_GROUP
    el = jnp.where((lane >= lo) & (lane < lo + EXP_PER_GROUP), logits, neg)
    v1, i1 = first_argmax(el)
    v2, i2 = first_argmax(jnp.where(lane == i1, neg, el))
    e21 = jnp.exp(v2 - v1)
    w1 = p_grp / (1.0 + e21)
    w2 = p_grp * e21 / (1.0 + e21)

    pick1 = lane == i1
    pick2 = lane == i2
    onehot = jnp.where(pick1 | pick2, 1.0, 0.0)
    ti = lax.broadcasted_iota(jnp.int32, (tm, tm), 0)
    si = lax.broadcasted_iota(jnp.int32, (tm, tm), 1)
    tri = jnp.where(si <= ti, 1.0, 0.0)
    cnt = _bdot(tri, onehot) + carry_scr[...]
    r1 = jnp.sum(jnp.where(pick1, cnt, 0.0), axis=1, keepdims=True) - 1.0
    r2 = jnp.sum(jnp.where(pick2, cnt, 0.0), axis=1, keepdims=True) - 1.0
    last = cnt[tm - 1:tm, :]
    carry_scr[...] = last
    cnt_ref[...] = jnp.broadcast_to(last, cnt_ref.shape)

    e1 = (i1 - ROUTE_LANE0).astype(F32)
    e2 = (i2 - ROUTE_LANE0).astype(F32)
    packed = jnp.zeros_like(logits)
    for idx, val in enumerate((e1, e2, w1, w2, r1, r2)):
        packed = jnp.where(lane == idx, val, packed)
    route_ref[...] = packed


def _oln(o_p, o_s, h1, wo, ln_g, ln_b, wr, br, *, tm):
    Np, Ns = o_p.shape[0], o_s.shape[0]
    NT = Np + Ns
    tm = _tile(math.gcd(Np, Ns), tm)
    npt = Np // tm
    const = lambda i: (0, 0)
    return pl.pallas_call(
        functools.partial(_oln_kernel, n_prompt_tiles=npt),
        grid=(NT // tm,),
        in_specs=[pl.BlockSpec((tm, D_MODEL), lambda i: (jnp.minimum(i, npt - 1), 0)),
                  pl.BlockSpec((tm, D_MODEL), lambda i: (jnp.maximum(i - npt, 0), 0)),
                  pl.BlockSpec((tm, D_MODEL), lambda i: (i, 0)),
                  pl.BlockSpec((D_MODEL, D_MODEL), const, pipeline_mode=pl.Buffered(1)),
                  pl.BlockSpec((1, D_MODEL), const),
                  pl.BlockSpec((1, D_MODEL), const),
                  pl.BlockSpec((D_MODEL, LANES), const),
                  pl.BlockSpec((1, LANES), const)],
        out_specs=[pl.BlockSpec((tm, D_MODEL), lambda i: (i, 0)),
                   pl.BlockSpec((tm, LANES), lambda i: (i, 0)),
                   pl.BlockSpec((SUBLANES, LANES), const)],
        out_shape=[jax.ShapeDtypeStruct((NT, D_MODEL), F32),
                   jax.ShapeDtypeStruct((NT, LANES), F32),
                   jax.ShapeDtypeStruct((SUBLANES, LANES), F32)],
        scratch_shapes=[pltpu.VMEM((1, LANES), F32)],
        compiler_params=_cparams(("arbitrary",)),
        name="oproj_ln2_router",
    )(o_p, o_s, h1, wo, ln_g, ln_b, wr, br)


def _row_copy(src_ref, src_row, dst_ref, dst_row, sem):
    return pltpu.make_async_copy(src_ref.at[pl.ds(src_row, 1), :], dst_ref.at[pl.ds(dst_row, 1), :], sem)


def _expert_weight_copies(e, slot, w_hbm, w_f32, sem):
    return [pltpu.make_async_copy(src.at[e], dst.at[slot], sem.at[slot, j])
            for j, (src, dst) in enumerate(zip(w_hbm, w_f32))]


def _expert_kernel(it_ref, ie_ref, lo_ref, hi_ref, first_ref, n_ref, ord_ref, nxt_ref, tok_ref,
                   h2_hbm, wg_hbm, wu_hbm, wd_hbm, y_ref,
                   xbuf, xsem, wg_f, wu_f, wd_f, wsem, wg_s, wu_s, wd_s):
    i = pl.program_id(0)
    n = n_ref[0]
    TR = xbuf.shape[1]
    w_hbm = (wg_hbm, wu_hbm, wd_hbm)
    w_f32 = (wg_f, wu_f, wd_f)

    def gather(base, slot, r):
        return _row_copy(h2_hbm, tok_ref[base + r], xbuf.at[slot], r, xsem.at[slot])

    def wait_rows(slot):
        def body(r, carry):
            gather(0, slot, r).wait()
            return carry
        lax.fori_loop(0, TR, body, 0, unroll=DMA_UNROLL)

    @pl.when(i == 0)
    def _prologue():
        for c in _expert_weight_copies(ie_ref[0], 0, w_hbm, w_f32, wsem):
            c.start()
        base0 = it_ref[0] * TR

        def body(r, carry):
            gather(base0, 0, r).start()
            return carry
        lax.fori_loop(0, TR, body, 0, unroll=DMA_UNROLL)

    @pl.when(i < n)
    def _compute():
        slot = i % 2
        wslot = ord_ref[i] % 2
        fresh = jnp.logical_or(i == 0, ie_ref[i] != ie_ref[jnp.maximum(i - 1, 0)])

        @pl.when(fresh)
        def _new_expert():
            for c in _expert_weight_copies(ie_ref[i], wslot, w_hbm, w_f32, wsem):
                c.wait()
            wg_s[...] = wg_f[wslot].astype(BF16)
            wu_s[...] = wu_f[wslot].astype(BF16)
            wd_s[...] = wd_f[wslot].astype(BF16)

            @pl.when(nxt_ref[i] >= 0)
            def _prefetch():
                for c in _expert_weight_copies(nxt_ref[i], 1 - wslot, w_hbm, w_f32, wsem):
                    c.start()

        wait_rows(slot)
        x = xbuf[slot].astype(BF16)
        base_next = it_ref[jnp.minimum(i + 1, n - 1)] * TR
        for r in range(TR):
            gather(base_next, 1 - slot, r).start()
        gate = jnp.dot(x, wg_s[...], preferred_element_type=F32)
        up = jnp.dot(x, wu_s[...], preferred_element_type=F32)
        hid = gate * _sigmoid(gate) * up
        y = jnp.dot(hid.astype(BF16), wd_s[...], preferred_element_type=F32)
        rows = it_ref[i] * TR + lax.broadcasted_iota(jnp.int32, (TR, 1), 0)
        mine = (rows >= lo_ref[i]) & (rows < hi_ref[i])

        @pl.when(first_ref[i] == 1)
        def _first():
            y_ref[...] = jnp.where(mine, y, 0.0)

        @pl.when(first_ref[i] == 0)
        def _later():
            y_ref[...] = jnp.where(mine, y, y_ref[...])

        @pl.when(i == n - 1)
        def _drain():
            wait_rows(1 - slot)


def _experts(items, row_token, h2, e_wg, e_wu, e_wd):
    TR = MOE_ROW_TILE
    n_work = items[0].shape[0]
    n_rows = row_token.shape[0]
    any_spec = pl.BlockSpec(memory_space=pl.ANY)
    return pl.pallas_call(
        _expert_kernel,
        grid_spec=pltpu.PrefetchScalarGridSpec(
            num_scalar_prefetch=9,
            grid=(n_work,),
            in_specs=[any_spec, any_spec, any_spec, any_spec],
            out_specs=pl.BlockSpec((TR, D_MODEL), lambda i, it, *_: (it[i], 0)),
            scratch_shapes=[pltpu.VMEM((2, TR, D_MODEL), F32), pltpu.SemaphoreType.DMA((2,)),
                            pltpu.VMEM((2, D_MODEL, EXP_FF), F32), pltpu.VMEM((2, D_MODEL, EXP_FF), F32),
                            pltpu.VMEM((2, EXP_FF, D_MODEL), F32), pltpu.SemaphoreType.DMA((2, 3)),
                            pltpu.VMEM((D_MODEL, EXP_FF), BF16), pltpu.VMEM((D_MODEL, EXP_FF), BF16),
                            pltpu.VMEM((EXP_FF, D_MODEL), BF16)],
        ),
        out_shape=jax.ShapeDtypeStruct((n_rows, D_MODEL), F32),
        compiler_params=_cparams(("arbitrary",)),
        name="moe_experts",
    )(*items, row_token, h2, e_wg, e_wu, e_wd)


def _moe_schedule(route, counts, n_rows):
    TR = MOE_ROW_TILE
    assert n_rows % TR == 0
    n_work = n_rows // TR + N_EXPERTS - 1
    rt = route[:, :SUBLANES].T
    cnt = counts[0, ROUTE_LANE0:ROUTE_LANE0 + N_EXPERTS].astype(jnp.int32)
    g_end = jnp.cumsum(cnt)
    g_start = g_end - cnt
    onehot = rt[0:2, :, None] == jnp.arange(N_EXPERTS, dtype=F32)
    start_of = jnp.sum(jnp.where(onehot, g_start.astype(F32), 0.0), axis=-1)
    pos_flat = (start_of + rt[4:6]).astype(jnp.int32).T.reshape(-1)
    first_tile = g_start // TR
    n_items_e = jnp.where(cnt > 0, (g_end - 1) // TR - first_tile + 1, 0)
    item_end = jnp.cumsum(n_items_e)
    n_items = item_end[-1:]
    idx = jnp.minimum(jnp.arange(n_work, dtype=jnp.int32), n_items[0] - 1)
    ie = jnp.minimum(jnp.sum((item_end[None, :] <= idx[:, None]).astype(jnp.int32), axis=1), N_EXPERTS - 1)
    it = first_tile[ie] + idx - (item_end - n_items_e)[ie]
    first = jnp.concatenate([jnp.ones((1,), jnp.int32), (it[1:] != it[:-1]).astype(jnp.int32)])
    has = n_items_e > 0
    e_ids = jnp.arange(N_EXPERTS, dtype=jnp.int32)
    ordinal = jnp.cumsum(has.astype(jnp.int32)) - 1
    later = (e_ids[None, :] > e_ids[:, None]) & has[None, :]
    nxt = jnp.min(jnp.where(later, e_ids[None, :], N_EXPERTS), axis=1)
    nxt = jnp.where(nxt == N_EXPERTS, -1, nxt)
    items = (it.astype(jnp.int32), ie, g_start[ie], g_end[ie], first, n_items.astype(jnp.int32),
             ordinal[ie], nxt[ie])
    row_token = jnp.zeros((n_rows,), jnp.int32).at[pos_flat].set(
        jnp.arange(n_rows, dtype=jnp.int32) // TOP_K)
    return pos_flat, row_token, items


def _combine_kernel(pos_ref, h2_ref, route_ref, ys_ref, g_ref, b_ref, yp_ref, ysm_ref,
                    buf, sem, *, tm, n_prompt_tiles):
    i = pl.program_id(0)
    last = pl.num_programs(0) - 1
    slot = i % 2

    def fetch(tile, slot, r, k):
        return _row_copy(ys_ref, pos_ref[TOP_K * (tile * tm + r) + k], buf.at[slot, k], r, sem.at[slot])

    def wait_rows(slot):
        def body(r, carry):
            for k in range(TOP_K):
                fetch(0, slot, r, k).wait()
            return carry
        lax.fori_loop(0, tm, body, 0, unroll=DMA_UNROLL)

    @pl.when(i == 0)
    def _prologue():
        def body(r, carry):
            for k in range(TOP_K):
                fetch(0, 0, r, k).start()
            return carry
        lax.fori_loop(0, tm, body, 0, unroll=DMA_UNROLL)

    wait_rows(slot)
    nxt = jnp.minimum(i + 1, last)
    for r in range(tm):
        for k in range(TOP_K):
            fetch(nxt, 1 - slot, r, k).start()
    route = route_ref[...]
    moe = route[:, 2:3] * buf[slot, 0] + route[:, 3:4] * buf[slot, 1]
    y = _layer_norm(ALPHA * h2_ref[...] + moe, g_ref[...], b_ref[...])

    @pl.when(i == last)
    def _drain():
        wait_rows(1 - slot)

    @pl.when(i < n_prompt_tiles)
    def _prompt():
        yp_ref[...] = y

    @pl.when(i >= n_prompt_tiles)
    def _sample():
        ysm_ref[...] = y


def _combine(pos_flat, h2, route, ys, ln_g, ln_b, *, Np, Ns, tm):
    NT = h2.shape[0]
    tm = _tile(math.gcd(Np, Ns), tm)
    npt = Np // tm
    const = lambda i, pos: (0, 0)
    return pl.pallas_call(
        functools.partial(_combine_kernel, tm=tm, n_prompt_tiles=npt),
        grid_spec=pltpu.PrefetchScalarGridSpec(
            num_scalar_prefetch=1,
            grid=(NT // tm,),
            in_specs=[pl.BlockSpec((tm, D_MODEL), lambda i, pos: (i, 0)),
                      pl.BlockSpec((tm, LANES), lambda i, pos: (i, 0)),
                      pl.BlockSpec(memory_space=pl.ANY),
                      pl.BlockSpec((1, D_MODEL), const),
                      pl.BlockSpec((1, D_MODEL), const)],
            out_specs=[pl.BlockSpec((tm, D_MODEL), lambda i, pos: (jnp.minimum(i, npt - 1), 0)),
                       pl.BlockSpec((tm, D_MODEL), lambda i, pos: (jnp.maximum(i - npt, 0), 0))],
            scratch_shapes=[pltpu.VMEM((2, TOP_K, tm, D_MODEL), F32), pltpu.SemaphoreType.DMA((2,))],
        ),
        out_shape=[jax.ShapeDtypeStruct((Np, D_MODEL), F32),
                   jax.ShapeDtypeStruct((Ns, D_MODEL), F32)],
        compiler_params=_cparams(("arbitrary",)),
        name="moe_combine_ln3",
    )(pos_flat, h2, route, ys, ln_g, ln_b)


def kernel(x_prompt, x_sample, mem_prompt, cache_mem_k, cache_mem_v, state_mlstm_C, state_mlstm_n,
           state_mlstm_m, state_mlstm_conv, state_hgrn_S, w_in, b_in, conv_w, mlstm_gn, lb_logits, hgrn_gn,
           w_bm, w_bh, w_out, ln1_g, ln1_b, xa_wq, xa_wk, xa_wv, xa_wo, ln2_g, ln2_b,
           r1_w, r1_b, r2_w, r2_b, e_wg, e_wu, e_wd, ln3_g, ln3_b):
    Bp, Tp, _ = x_prompt.shape
    Bs, Ts, _ = x_sample.shape
    MEM = mem_prompt.shape[1]
    Np, Ns = Bp * Tp, Bs * Ts
    NT = Np + Ns
    Lp_m = math.gcd(Tp, 256)
    Lp_h = math.gcd(Tp, 64)
    Ls = Ts
    assert Ts % SUBLANES == 0 and Np % Ls == 0

    xp2 = x_prompt.reshape(Np, D_MODEL)
    xs2 = x_sample.reshape(Ns, D_MODEL)

    w = w_in[0]
    bi = b_in[0]
    w16 = w.astype(BF16)
    w_rest = w16[:, GATE_LO + 2 * M_H:]
    b_main = jnp.concatenate([bi[:GATE_LO], bi[GATE_LO + 2 * M_H:]])[None]
    w_gate = jnp.pad(w[:, GATE_LO:GATE_LO + 2 * M_H], ((0, 0), (0, LANES - 2 * M_H))).astype(BF16)
    b_gate = jnp.pad(bi[GATE_LO:GATE_LO + 2 * M_H], (0, LANES - 2 * M_H))[None]
    lb = jnp.cumsum(jax.nn.softmax(lb_logits.astype(F32), axis=0), axis=0)[0][None]
    wr = jnp.zeros((D_MODEL, LANES), F32)
    wr = wr.at[:, :N_GROUPS].set(r1_w[0])
    wr = wr.at[:, ROUTE_LANE0:ROUTE_LANE0 + N_EXPERTS].set(
        jnp.transpose(r2_w[0], (1, 0, 2)).reshape(D_MODEL, N_EXPERTS)).astype(BF16)
    br = jnp.zeros((1, LANES), F32)
    br = br.at[0, :N_GROUPS].set(r1_b[0])
    br = br.at[0, ROUTE_LANE0:ROUTE_LANE0 + N_EXPERTS].set(r2_b[0].reshape(N_EXPERTS))
    zeros_d = jnp.zeros((1, D_MODEL), F32)

    memb = mem_prompt.reshape(Bp * MEM, D_MODEL).astype(BF16)
    mk = _matmul_bias(memb, xa_wk[0], zeros_d, tm=1024, tn=1024, name="mem_k")
    mv = _matmul_bias(memb, xa_wv[0], zeros_d, tm=1024, tn=1024, name="mem_v")

    u_main, u_gate = _in_proj(xp2, xs2, w16, w_rest, b_main, w_gate, b_gate, tm=1024, tn=1024)

    padc = lambda c: jnp.pad(c, ((0, 0), (SUBLANES - (CONV_K - 1), 0), (0, 0)))
    zC = jnp.zeros((Bp, M_H, M_DK, M_DV), F32)
    zn = jnp.zeros((Bp, M_H, M_DK), F32)
    zm = jnp.zeros((Bp, 1, M_H), F32)
    zconv = jnp.zeros((Bp, SUBLANES, 2 * M_W), F32)
    zS = jnp.zeros((Bp, HG_H, HG_DK, HG_DV), F32)
    cw = conv_w[0]
    mgn = mlstm_gn[0][None]
    hgn = hgrn_gn[0][None]
    hm_p, C_p, n_p, m_p, conv_p = _mlstm(u_main, u_gate, zconv, zC, zn, zm, cw, mgn,
                                         B=Bp, T=Tp, L=Lp_m, row0=0)
    hm_s, C_s, n_s, m_s, conv_s = _mlstm(u_main, u_gate, padc(state_mlstm_conv[0]), state_mlstm_C[0],
                                         state_mlstm_n[0], state_mlstm_m[0][:, None, :], cw, mgn,
                                         B=Bs, T=Ts, L=Ls, row0=Np, NB=SAMPLE_SEQS_PER_STEP)
    og_p, S_p = _hgrn(u_main, zS, lb, hgn, B=Bp, T=Tp, L=Lp_h, row0=0)
    og_s, S_s = _hgrn(u_main, state_hgrn_S[0], lb, hgn, B=Bs, T=Ts, L=Ls, row0=Np, NB=SAMPLE_SEQS_PER_STEP)

    h1, h1b = _merge(hm_p, hm_s, og_p, og_s, u_main, xp2, xs2, w_bm[0].astype(BF16), w_bh[0].astype(BF16),
                     w_out[0].astype(BF16), ln1_g, ln1_b, tm=256)

    q = _matmul_bias(h1b, xa_wq[0], zeros_d, tm=1024, tn=1024, name="xa_q")
    att_p = _attention(q, mk, mv, B=Bp, T=Tp, tq=512, row0=0)
    att_s = _attention_cache(q, cache_mem_k, cache_mem_v, B=Bs, T=Ts, row0=Np)
    h2, route, counts = _oln(att_p, att_s, h1, xa_wo[0].astype(BF16), ln2_g, ln2_b, wr, br, tm=512)

    pos_flat, row_token, items = _moe_schedule(route, counts, TOP_K * NT)
    ys_sorted = _experts(items, row_token, h2, e_wg[0], e_wu[0], e_wd[0])
    y_p, y_s = _combine(pos_flat, h2, route, ys_sorted, ln3_g, ln3_b, Np=Np, Ns=Ns, tm=256)

    kv5 = lambda a: a.reshape(1, Bp, MEM, XA_H, XA_D)
    return (y_p.reshape(Bp, Tp, D_MODEL), y_s.reshape(Bs, Ts, D_MODEL), kv5(mk), kv5(mv),
            C_p[None], n_p[None], m_p.reshape(1, Bp, M_H), conv_p[None], S_p[None],
            C_s[None], n_s[None], m_s.reshape(1, Bs, M_H), conv_s[None], S_s[None])
```

```python
import functools
import math

import jax
import jax.numpy as jnp
from jax import lax
from jax.experimental import pallas as pl
from jax.experimental.pallas import tpu as pltpu

F32 = jnp.float32
BF16 = jnp.bfloat16

D_MODEL = 2048
M_W = 1024
M_H = 4
M_DK = 256
M_DV = 256
CONV_K = 4
HG_W = 1024
HG_H = 8
HG_DK = 128
HG_DV = 128
XA_H = 4
XA_D = 512
N_GROUPS = 4
EXP_PER_GROUP = 8
N_EXPERTS = 32
TOP_K = 2
EXP_FF = 512
DEPTH = 1
ALPHA = (2 * DEPTH) ** 0.25
LN_EPS = 1e-5

COL_QK, COL_V, COL_O = 0, 2048, 3072
COL_QH, COL_FH, COL_IH, COL_GH = 4096, 5120, 6144, 7168
COL_GM, COL_GHH = 8192, 10240
N_MAIN = 12288
GATE_LO = 4 * M_W
LANES = 128
SUBLANES = 8
ROUTE_LANE0 = N_GROUPS

VMEM_LIMIT = 56 << 20
MOE_ROW_TILE = 256
SAMPLE_SEQS_PER_STEP = 4
N_XBUF = 3
DMA_UNROLL = 8
INVERT_GROUP = 16


def _cparams(sem, vmem=VMEM_LIMIT):
    return pltpu.CompilerParams(dimension_semantics=sem, vmem_limit_bytes=vmem)


def _tile(n, pref):
    t = math.gcd(n, pref)
    assert t % SUBLANES == 0, (n, pref)
    return t


def _bdot(a, b):
    return jnp.dot(a.astype(BF16), b.astype(BF16), preferred_element_type=F32)


def _bdot_nt(a, b):
    return lax.dot_general(a.astype(BF16), b.astype(BF16), (((1,), (1,)), ((), ())),
                           preferred_element_type=F32)


def _bdot_tn(a, b):
    return lax.dot_general(a.astype(BF16), b.astype(BF16), (((0,), (0,)), ((), ())),
                           preferred_element_type=F32)


def _sigmoid(x):
    return 1.0 / (1.0 + jnp.exp(-x))


def _cumsum_rows(x, seg=None):
    n = seg or x.shape[0]
    assert n & (n - 1) == 0
    row = lax.broadcasted_iota(jnp.int32, x.shape, 0) & (n - 1)
    d = 1
    while d < n:
        x = x + jnp.where(row >= d, pltpu.roll(x, d, axis=0), 0.0)
        d *= 2
    return x


def _col_to_row(col, eye):
    return jnp.sum(jnp.where(eye, col, 0.0), axis=0, keepdims=True)


def _row_to_col(row, eye):
    return jnp.sum(jnp.where(eye, row, 0.0), axis=1, keepdims=True)


def _layer_norm(x, g, b):
    mu = jnp.mean(x, axis=-1, keepdims=True)
    xc = x - mu
    var = jnp.mean(xc * xc, axis=-1, keepdims=True)
    return xc * lax.rsqrt(var + LN_EPS) * g + b


def _mm_kernel(x_ref, w_ref, b_ref, o_ref, wb_scr):
    @pl.when(pl.program_id(1) == 0)
    def _new_weight_tile():
        wb_scr[...] = w_ref[...].astype(BF16)

    acc = jnp.dot(x_ref[...], wb_scr[...], preferred_element_type=F32)
    o_ref[...] = (acc + b_ref[...]).astype(o_ref.dtype)


def _matmul_bias(x, w, b, *, tm, tn, out_dtype=F32, name):
    M, K = x.shape
    N = w.shape[1]
    tm = _tile(M, tm)
    tn = _tile(N, tn)
    return pl.pallas_call(
        _mm_kernel,
        grid=(N // tn, M // tm),
        in_specs=[pl.BlockSpec((tm, K), lambda j, i: (i, 0)),
                  pl.BlockSpec((K, tn), lambda j, i: (0, j)),
                  pl.BlockSpec((1, tn), lambda j, i: (0, j))],
        out_specs=pl.BlockSpec((tm, tn), lambda j, i: (i, j)),
        out_shape=jax.ShapeDtypeStruct((M, N), out_dtype),
        scratch_shapes=[pltpu.VMEM((K, tn), BF16)],
        compiler_params=_cparams(("parallel", "arbitrary")),
        name=name,
    )(x, w, b)


def _inproj_kernel(xp_ref, xs_ref, wa_ref, wb_ref, b_ref, wg_ref, bg_ref, u_ref, ug_ref, xb_scr, *,
                   n_prompt_tiles, n_a_tiles):
    i = pl.program_id(0)
    j = pl.program_id(1)

    @pl.when(j == 0)
    def _row_tile():
        xb = jnp.where(i < n_prompt_tiles, xp_ref[...], xs_ref[...]).astype(BF16)
        xb_scr[...] = xb
        ug_ref[...] = jnp.dot(xb, wg_ref[...], preferred_element_type=F32) + bg_ref[...]

    @pl.when(j < n_a_tiles)
    def _lead():
        u_ref[...] = jnp.dot(xb_scr[...], wa_ref[...], preferred_element_type=F32) + b_ref[...]

    @pl.when(j >= n_a_tiles)
    def _rest():
        u_ref[...] = jnp.dot(xb_scr[...], wb_ref[...], preferred_element_type=F32) + b_ref[...]


def _in_proj(x_p, x_s, w_all, w_rest, b_main, w_gate, b_gate, *, tm, tn):
    Np, Ns = x_p.shape[0], x_s.shape[0]
    NT = Np + Ns
    K = w_all.shape[0]
    N = GATE_LO + w_rest.shape[1]
    tm = _tile(math.gcd(Np, Ns), tm)
    tn = _tile(math.gcd(GATE_LO, w_rest.shape[1]), tn)
    npt = Np // tm
    na = GATE_LO // tn
    return pl.pallas_call(
        functools.partial(_inproj_kernel, n_prompt_tiles=npt, n_a_tiles=na),
        grid=(NT // tm, N // tn),
        in_specs=[pl.BlockSpec((tm, K), lambda i, j: (jnp.minimum(i, npt - 1), 0), pipeline_mode=pl.Buffered(1)),
                  pl.BlockSpec((tm, K), lambda i, j: (jnp.maximum(i - npt, 0), 0), pipeline_mode=pl.Buffered(1)),
                  pl.BlockSpec((K, tn), lambda i, j: (0, jnp.minimum(j, na - 1))),
                  pl.BlockSpec((K, tn), lambda i, j: (0, jnp.maximum(j - na, 0))),
                  pl.BlockSpec((1, tn), lambda i, j: (0, j)),
                  pl.BlockSpec((K, LANES), lambda i, j: (0, 0)),
                  pl.BlockSpec((1, LANES), lambda i, j: (0, 0))],
        out_specs=[pl.BlockSpec((tm, tn), lambda i, j: (i, j)),
                   pl.BlockSpec((tm, LANES), lambda i, j: (i, 0))],
        out_shape=[jax.ShapeDtypeStruct((NT, N), F32), jax.ShapeDtypeStruct((NT, LANES), F32)],
        scratch_shapes=[pltpu.VMEM((tm, K), BF16)],
        compiler_params=_cparams(("parallel", "arbitrary")),
        name="in_proj",
    )(x_p, x_s, w_all, w_rest, b_main, w_gate, b_gate)


def _mlstm_kernel(*refs, L, NC, NB):
    (qk_ref, v_ref, o_ref, g_ref, conv0_ref, C0_ref, n0_ref, m0_ref, cw_ref, gn_ref,
     h_ref, Co_ref, no_ref, mo_ref, convo_ref, C_scr, n_scr, m_scr, tail_scr) = refs
    c = pl.program_id(1)
    if NC > 1:
        @pl.when(c == 0)
        def _init():
            C_scr[...] = C0_ref[0]
            n_scr[...] = n0_ref[0]
            m_scr[...] = m0_ref[0]
            tail_scr[...] = conv0_ref[0]

    cw = cw_ref[...]
    gn = gn_ref[...]
    ti = lax.broadcasted_iota(jnp.int32, (L, L), 0)
    si = lax.broadcasted_iota(jnp.int32, (L, L), 1)
    eye = ti == si
    causal = si <= ti
    head_lane = lax.broadcasted_iota(jnp.int32, (1, M_H), 1)
    head_row = lax.broadcasted_iota(jnp.int32, (M_H, M_DK), 0)

    seqs = []
    for nb in range(NB):
        rows = slice(nb * L, (nb + 1) * L)
        if NC == 1:
            sq = dict(C_in=C0_ref.at[nb], C_out=Co_ref.at[nb], n_out=no_ref.at[nb], m_out=mo_ref.at[nb])
            n_all, m_all, tail = n0_ref[nb], m0_ref[nb], conv0_ref[nb]
        else:
            sq = dict(C_in=C_scr, C_out=C_scr, n_out=n_scr, m_out=m_scr)
            n_all, m_all, tail = n_scr[...], m_scr[...], tail_scr[...]
        qk_pre = qk_ref[rows, :]
        ext = jnp.concatenate([tail, qk_pre], axis=0)
        acc = qk_pre * cw[CONV_K - 1:CONV_K, :]
        for j in range(1, CONV_K):
            acc = acc + pltpu.roll(ext, j, axis=0)[SUBLANES:, :] * cw[CONV_K - 1 - j:CONV_K - j, :]
        if NC > 1:
            tail_scr[...] = qk_pre[L - SUBLANES:, :]
        qk = acc * _sigmoid(acc)
        g = g_ref[rows, :]
        lf_all = jnp.minimum(g, 0.0) - jnp.log(1.0 + jnp.exp(-jnp.abs(g)))
        F_all = _cumsum_rows(lf_all)
        heads = []
        for h in range(M_H):
            ks = slice(h * M_DK, (h + 1) * M_DK)
            q = qk[:, ks] * (M_DK ** -0.5)
            k = qk[:, M_W + h * M_DK:M_W + (h + 1) * M_DK]
            v = v_ref[rows, ks]
            ig = g[:, h:h + 1]
            F = F_all[:, M_H + h:M_H + h + 1]
            m_prev = m_all[:, h:h + 1]
            r_row = _col_to_row(ig - F, eye)
            Dm = jnp.where(causal, F + r_row, -jnp.inf)
            init_w = F + m_prev
            m_t = jnp.maximum(init_w, jnp.max(Dm, axis=1, keepdims=True))
            P = jnp.exp(Dm - m_t)
            a0 = jnp.exp(init_w - m_t)
            FL = F[L - 1:L, :]
            mL = m_t[L - 1:L, :]
            wL = jnp.exp(FL - F + ig - mL)
            decay = jnp.exp(FL + m_prev - mL)
            heads.append(dict(ks=ks, q=q, k=k, v=v, m_t=m_t, P=P, a0=a0, mL=mL, decay=decay, kw=wL * k))
        sq.update(rows=rows, heads=heads, n_all=n_all, m_all=m_all, qk_pre=qk_pre)
        seqs.append(sq)

    for sq in seqs:
        for h, d in enumerate(sq["heads"]):
            d["S"] = _bdot_nt(d["q"], d["k"])
            d["qC"] = _bdot(d["q"], sq["C_in"][h])
            d["kv"] = _bdot_tn(d["kw"], d["v"])

    for sq in seqs:
        for d in sq["heads"]:
            d["Sc"] = d["S"] * d["P"]
            d["num"] = _bdot(d["Sc"], d["v"])

    for nb, sq in enumerate(seqs):
        rows, n_new, m_new = sq["rows"], sq["n_all"], sq["m_all"]
        for h, d in enumerate(sq["heads"]):
            ks, q, a0, decay = d["ks"], d["q"], d["a0"], d["decay"]
            n_row = sq["n_all"][h:h + 1, :]
            num = d["num"] + a0 * d["qC"]
            den = jnp.sum(d["Sc"], axis=1, keepdims=True) + a0 * jnp.sum(q * n_row, axis=1, keepdims=True)
            hh = num * (1.0 / jnp.maximum(jnp.abs(den), jnp.exp(-d["m_t"])))
            sq["C_out"][h] = decay * sq["C_in"][h] + d["kv"]
            n_new = jnp.where(head_row == h, decay * n_row + jnp.sum(d["kw"], axis=0, keepdims=True), n_new)
            m_new = jnp.where(head_lane == h, d["mL"], m_new)

            hm = _sigmoid(o_ref[rows, ks]) * hh
            hm = hm - jnp.mean(hm, axis=1, keepdims=True)
            hm = hm * lax.rsqrt(jnp.mean(hm * hm, axis=1, keepdims=True) + LN_EPS) * gn[:, ks]
            h_ref[rows, ks] = hm
        sq["n_out"][...] = n_new
        sq["m_out"][...] = m_new
        conv_tail = sq["qk_pre"][L - (CONV_K - 1):, :]
        if NC == 1:
            convo_ref[nb] = conv_tail
        else:
            @pl.when(c == NC - 1)
            def _fin():
                Co_ref[0] = C_scr[...]
                no_ref[0] = n_new
                mo_ref[0] = m_new
                convo_ref[0] = conv_tail


def _mlstm(u_main, u_gate, conv0p, C0, n0, m0, conv_w, gn, *, B, T, L, row0, NB=1):
    NC = T // L
    assert NB == 1 or NC == 1
    assert B % NB == 0 and row0 % (NB * L) == 0
    R = NB * L
    rb0 = row0 // R
    tok = lambda b, c: rb0 + b * NC + c
    in_specs = [
        pl.BlockSpec((R, 2 * M_W), lambda b, c: (tok(b, c), COL_QK // (2 * M_W))),
        pl.BlockSpec((R, M_W), lambda b, c: (tok(b, c), COL_V // M_W)),
        pl.BlockSpec((R, M_W), lambda b, c: (tok(b, c), COL_O // M_W)),
        pl.BlockSpec((R, LANES), lambda b, c: (tok(b, c), 0)),
        pl.BlockSpec((NB, SUBLANES, 2 * M_W), lambda b, c: (b, 0, 0)),
        pl.BlockSpec((NB, M_H, M_DK, M_DV), lambda b, c: (b, 0, 0, 0)),
        pl.BlockSpec((NB, M_H, M_DK), lambda b, c: (b, 0, 0)),
        pl.BlockSpec((NB, 1, M_H), lambda b, c: (b, 0, 0)),
        pl.BlockSpec((CONV_K, 2 * M_W), lambda b, c: (0, 0)),
        pl.BlockSpec((1, M_W), lambda b, c: (0, 0)),
    ]
    args = [u_main, u_main, u_main, u_gate, conv0p, C0, n0, m0, conv_w, gn]
    out_specs = [
        pl.BlockSpec((R, M_W), lambda b, c: (b * NC + c, 0)),
        pl.BlockSpec((NB, M_H, M_DK, M_DV), lambda b, c: (b, 0, 0, 0)),
        pl.BlockSpec((NB, M_H, M_DK), lambda b, c: (b, 0, 0)),
        pl.BlockSpec((NB, 1, M_H), lambda b, c: (b, 0, 0)),
        pl.BlockSpec((NB, CONV_K - 1, 2 * M_W), lambda b, c: (b, 0, 0)),
    ]
    out_shape = [
        jax.ShapeDtypeStruct((B * T, M_W), F32),
        jax.ShapeDtypeStruct((B, M_H, M_DK, M_DV), F32),
        jax.ShapeDtypeStruct((B, M_H, M_DK), F32),
        jax.ShapeDtypeStruct((B, 1, M_H), F32),
        jax.ShapeDtypeStruct((B, CONV_K - 1, 2 * M_W), F32),
    ]
    return pl.pallas_call(
        functools.partial(_mlstm_kernel, L=L, NC=NC, NB=NB),
        grid=(B // NB, NC),
        in_specs=in_specs,
        out_specs=out_specs,
        out_shape=out_shape,
        scratch_shapes=[pltpu.VMEM((M_H, M_DK, M_DV), F32), pltpu.VMEM((M_H, M_DK), F32),
                        pltpu.VMEM((1, M_H), F32), pltpu.VMEM((SUBLANES, 2 * M_W), F32)],
        compiler_params=_cparams(("parallel", "arbitrary")),
        name=f"mlstm_L{L}",
    )(*args)


def _hgrn_kernel(*refs, L, NC, NB):
    (q_ref, f_ref, i_ref, g_ref, S0_ref, lb_ref, gn_ref, o_ref, So_ref, S_scr) = refs
    c = pl.program_id(1)
    if NC > 1:
        @pl.when(c == 0)
        def _init():
            S_scr[...] = S0_ref[0]

    lb = lb_ref[...]
    f = lb + (1.0 - lb) * _sigmoid(f_ref[...])
    kk = 1.0 - f
    b = _cumsum_rows(jnp.log(f), seg=L)
    qh = q_ref[...]
    q = qh * _sigmoid(qh)
    v = i_ref[...]
    gh = g_ref[...]
    gsilu = gh * _sigmoid(gh)
    gn = gn_ref[...]
    ti = lax.broadcasted_iota(jnp.int32, (L, L), 0)
    si = lax.broadcasted_iota(jnp.int32, (L, L), 1)
    causal = si <= ti
    ci = lax.broadcasted_iota(jnp.int32, (HG_DK, HG_DK), 0)
    cj = lax.broadcasted_iota(jnp.int32, (HG_DK, HG_DK), 1)
    eye = ci == cj
    mid = max(L // 2 - 1, 0)
    hsl = [slice(h * HG_DK, (h + 1) * HG_DK) for h in range(HG_H)]

    seqs = []
    for nb in range(NB):
        rows = slice(nb * L, (nb + 1) * L)
        bs, qs, ks = b[rows], q[rows], kk[rows]
        bL = bs[L - 1:L, :]
        bm = bs[mid:mid + 1, :]
        S_in, S_out = (S0_ref.at[nb], So_ref.at[nb]) if NC == 1 else (S_scr, S_scr)
        seqs.append(dict(rows=rows, bL=bL, v=v[rows], q_in=qs * jnp.exp(bs), q_t=qs * jnp.exp(bs - bm),
                         k_t=ks * jnp.exp(bm - bs), k_st=ks * jnp.exp(bL - bs), S_in=S_in, S_out=S_out))

    for sq in seqs:
        sq["A"] = [jnp.where(causal, _bdot_nt(sq["q_t"][:, hs], sq["k_t"][:, hs]), 0.0) for hs in hsl]
        sq["qS"] = [_bdot(sq["q_in"][:, hs], sq["S_in"][h]) for h, hs in enumerate(hsl)]
        sq["kv"] = [_bdot_tn(sq["k_st"][:, hs], sq["v"][:, hs]) for hs in hsl]
    for sq in seqs:
        sq["o_intra"] = [_bdot(sq["A"][h], sq["v"][:, hs]) for h, hs in enumerate(hsl)]
    for sq in seqs:
        rows = sq["rows"]
        for h, hs in enumerate(hsl):
            o = sq["o_intra"][h] + sq["qS"][h]
            dec = jnp.exp(_row_to_col(sq["bL"][:, hs], eye))
            sq["S_out"][h] = dec * sq["S_in"][h] + sq["kv"][h]
            o = o * lax.rsqrt(jnp.mean(o * o, axis=1, keepdims=True) + LN_EPS) * gn[:, hs]
            o_ref[rows, hs] = o * gsilu[rows, hs]

    if NC > 1:
        @pl.when(c == NC - 1)
        def _fin():
            So_ref[0] = S_scr[...]


def _hgrn(u_main, S0, lb, gn, *, B, T, L, row0, NB=1):
    NC = T // L
    assert NB == 1 or NC == 1
    assert B % NB == 0 and row0 % (NB * L) == 0
    R = NB * L
    rb0 = row0 // R
    tok = lambda b, c: rb0 + b * NC + c
    col = lambda off: (lambda b, c: (tok(b, c), off // HG_W))
    in_specs = [
        pl.BlockSpec((R, HG_W), col(COL_QH)),
        pl.BlockSpec((R, HG_W), col(COL_FH)),
        pl.BlockSpec((R, HG_W), col(COL_IH)),
        pl.BlockSpec((R, HG_W), col(COL_GH)),
        pl.BlockSpec((NB, HG_H, HG_DK, HG_DV), lambda b, c: (b, 0, 0, 0)),
        pl.BlockSpec((1, HG_W), lambda b, c: (0, 0)),
        pl.BlockSpec((1, HG_W), lambda b, c: (0, 0)),
    ]
    args = [u_main, u_main, u_main, u_main, S0, lb, gn]
    return pl.pallas_call(
        functools.partial(_hgrn_kernel, L=L, NC=NC, NB=NB),
        grid=(B // NB, NC),
        in_specs=in_specs,
        out_specs=[pl.BlockSpec((R, HG_W), lambda b, c: (b * NC + c, 0)),
                   pl.BlockSpec((NB, HG_H, HG_DK, HG_DV), lambda b, c: (b, 0, 0, 0))],
        out_shape=[jax.ShapeDtypeStruct((B * T, HG_W), F32),
                   jax.ShapeDtypeStruct((B, HG_H, HG_DK, HG_DV), F32)],
        scratch_shapes=[pltpu.VMEM((HG_H, HG_DK, HG_DV), F32)],
        compiler_params=_cparams(("parallel", "arbitrary")),
        name=f"hgrn_L{L}",
    )(*args)


def _merge_kernel(hmp_ref, hms_ref, ogp_ref, ogs_ref, gm_ref, gh_ref, xp_ref, xs_ref, wbm_ref, wbh_ref,
                  wo_ref, g_ref, b_ref, h1_ref, h1b_ref, *, n_prompt_tiles):
    is_prompt = pl.program_id(0) < n_prompt_tiles
    a = _bdot(jnp.where(is_prompt, hmp_ref[...], hms_ref[...]), wbm_ref[...])
    bb = _bdot(jnp.where(is_prompt, ogp_ref[...], ogs_ref[...]), wbh_ref[...])
    merged = _sigmoid(gm_ref[...]) * a + _sigmoid(gh_ref[...]) * bb
    mix = _bdot(merged, wo_ref[...])
    x = jnp.where(is_prompt, xp_ref[...], xs_ref[...])
    h1 = _layer_norm(ALPHA * x + mix, g_ref[...], b_ref[...])
    h1_ref[...] = h1
    h1b_ref[...] = h1.astype(BF16)


def _merge(hm_p, hm_s, og_p, og_s, u_main, x_p, x_s, w_bm, w_bh, w_out, ln_g, ln_b, *, tm):
    Np, Ns = x_p.shape[0], x_s.shape[0]
    NT = Np + Ns
    tm = _tile(math.gcd(Np, Ns), tm)
    npt = Np // tm
    const = lambda i: (0, 0)
    prompt_rows = lambda i: (jnp.minimum(i, npt - 1), 0)
    sample_rows = lambda i: (jnp.maximum(i - npt, 0), 0)
    return pl.pallas_call(
        functools.partial(_merge_kernel, n_prompt_tiles=npt),
        grid=(NT // tm,),
        in_specs=[
            pl.BlockSpec((tm, M_W), prompt_rows),
            pl.BlockSpec((tm, M_W), sample_rows),
            pl.BlockSpec((tm, HG_W), prompt_rows),
            pl.BlockSpec((tm, HG_W), sample_rows),
            pl.BlockSpec((tm, D_MODEL), lambda i: (i, COL_GM // D_MODEL)),
            pl.BlockSpec((tm, D_MODEL), lambda i: (i, COL_GHH // D_MODEL)),
            pl.BlockSpec((tm, D_MODEL), prompt_rows),
            pl.BlockSpec((tm, D_MODEL), sample_rows),
            pl.BlockSpec((M_W, D_MODEL), const, pipeline_mode=pl.Buffered(1)),
            pl.BlockSpec((HG_W, D_MODEL), const, pipeline_mode=pl.Buffered(1)),
            pl.BlockSpec((D_MODEL, D_MODEL), const, pipeline_mode=pl.Buffered(1)),
            pl.BlockSpec((1, D_MODEL), const),
            pl.BlockSpec((1, D_MODEL), const),
        ],
        out_specs=[pl.BlockSpec((tm, D_MODEL), lambda i: (i, 0)),
                   pl.BlockSpec((tm, D_MODEL), lambda i: (i, 0))],
        out_shape=[jax.ShapeDtypeStruct((NT, D_MODEL), F32),
                   jax.ShapeDtypeStruct((NT, D_MODEL), BF16)],
        compiler_params=_cparams(("parallel",)),
        name="merge_ln1",
    )(hm_p, hm_s, og_p, og_s, u_main, u_main, x_p, x_s, w_bm, w_bh, w_out, ln_g, ln_b)


def _attn_kernel(q_ref, *refs):
    k_refs, v_refs, o_ref = refs[:XA_H], refs[XA_H:2 * XA_H], refs[2 * XA_H]
    for h in range(XA_H):
        sl = slice(h * XA_D, (h + 1) * XA_D)
        s = _bdot_nt(q_ref[:, sl], k_refs[h][...]) * (XA_D ** -0.5)
        e = jnp.exp(s - jnp.max(s, axis=1, keepdims=True))
        p = e / jnp.sum(e, axis=1, keepdims=True)
        o_ref[:, sl] = _bdot(p, v_refs[h][...])


def _kv_head_copies(k_hbm, v_hbm, kbuf, vbuf, sem, b, slot):
    out = []
    for j, (src, dst) in enumerate(((k_hbm, kbuf), (v_hbm, vbuf))):
        for h in range(XA_H):
            out.append(pltpu.make_async_copy(src.at[0, b, :, h, :], dst.at[slot, h], sem.at[slot, j, h]))
    return out


def _attn_cache_kernel(q_ref, k_hbm, v_hbm, o_ref, kbuf, vbuf, sem):
    b = pl.program_id(0)
    slot = b % 2

    @pl.when(b == 0)
    def _first():
        for c in _kv_head_copies(k_hbm, v_hbm, kbuf, vbuf, sem, b, slot):
            c.start()

    @pl.when(b + 1 < pl.num_programs(0))
    def _prefetch():
        for c in _kv_head_copies(k_hbm, v_hbm, kbuf, vbuf, sem, b + 1, 1 - slot):
            c.start()

    for c in _kv_head_copies(k_hbm, v_hbm, kbuf, vbuf, sem, b, slot):
        c.wait()
    T = q_ref.shape[0]
    s = jnp.concatenate([_bdot_nt(q_ref[:, h * XA_D:(h + 1) * XA_D], kbuf[slot, h]) for h in range(XA_H)],
                        axis=0) * (XA_D ** -0.5)
    e = jnp.exp(s - jnp.max(s, axis=1, keepdims=True))
    p = e / jnp.sum(e, axis=1, keepdims=True)
    for h in range(XA_H):
        o_ref[:, h * XA_D:(h + 1) * XA_D] = _bdot(p[h * T:(h + 1) * T], vbuf[slot, h])


def _attention_cache(q, cache_k, cache_v, *, B, T, row0):
    M = cache_k.shape[2]
    rb0 = row0 // T
    return pl.pallas_call(
        _attn_cache_kernel,
        grid=(B,),
        in_specs=[pl.BlockSpec((T, D_MODEL), lambda b: (rb0 + b, 0)),
                  pl.BlockSpec(memory_space=pl.ANY), pl.BlockSpec(memory_space=pl.ANY)],
        out_specs=pl.BlockSpec((T, D_MODEL), lambda b: (b, 0)),
        out_shape=jax.ShapeDtypeStruct((B * T, D_MODEL), F32),
        scratch_shapes=[pltpu.VMEM((2, XA_H, M, XA_D), F32), pltpu.VMEM((2, XA_H, M, XA_D), F32),
                        pltpu.SemaphoreType.DMA((2, 2, XA_H))],
        compiler_params=_cparams(("arbitrary",)),
        name="xattn_cache",
    )(q, cache_k, cache_v)


def _attention(q, mem_k, mem_v, *, B, T, tq, row0):
    tq = _tile(T, tq)
    nq = T // tq
    rb0 = row0 // tq
    M = mem_k.shape[0] // B
    kv_specs = [pl.BlockSpec((M, XA_D), lambda b, t, h=h: (b, h)) for h in range(XA_H)]
    return pl.pallas_call(
        _attn_kernel,
        grid=(B, nq),
        in_specs=[pl.BlockSpec((tq, D_MODEL), lambda b, t: (rb0 + b * nq + t, 0))] + kv_specs + kv_specs,
        out_specs=pl.BlockSpec((tq, D_MODEL), lambda b, t: (b * nq + t, 0)),
        out_shape=jax.ShapeDtypeStruct((B * T, D_MODEL), F32),
        compiler_params=_cparams(("parallel", "parallel")),
        name=f"xattn_T{T}",
    )(q, *([mem_k] * XA_H), *([mem_v] * XA_H))


def _oln_kernel(op_ref, os_ref, h1_ref, wo_ref, g_ref, b_ref, wr_ref, br_ref,
                h2_ref, route_ref, cnt_ref, carry_scr, *, n_prompt_tiles, n_sub):
    i = pl.program_id(0)

    @pl.when(i == 0)
    def _init():
        carry_scr[...] = jnp.zeros_like(carry_scr)

    ts = h1_ref.shape[0] // n_sub
    lane = lax.broadcasted_iota(jnp.int32, (ts, LANES), 1)
    lane_f = lane.astype(F32)
    neg = -jnp.inf
    ti = lax.broadcasted_iota(jnp.int32, (ts, ts), 0)
    si = lax.broadcasted_iota(jnp.int32, (ts, ts), 1)
    tri = jnp.where(si <= ti, 1.0, 0.0).astype(BF16)
    is_prompt = i < n_prompt_tiles

    def first_argmax(vals):
        mx = jnp.max(vals, axis=1, keepdims=True)
        idx = jnp.min(jnp.where(vals == mx, lane_f, float(LANES)), axis=1, keepdims=True)
        return mx, idx.astype(jnp.int32)

    carry = carry_scr[...]
    for sb in range(n_sub):
        rows = slice(sb * ts, (sb + 1) * ts)
        att = jnp.where(is_prompt, op_ref[rows, :], os_ref[rows, :])
        h2 = _layer_norm(ALPHA * h1_ref[rows, :] + _bdot(att, wo_ref[...]), g_ref[...], b_ref[...])
        h2_ref[rows, :] = h2

        logits = _bdot(h2, wr_ref[...]) + br_ref[...]
        gl = jnp.where(lane < N_GROUPS, logits, neg)
        gmax, grp = first_argmax(gl)
        p_grp = 1.0 / jnp.sum(jnp.exp(gl - gmax), axis=1, keepdims=True)
        lo = ROUTE_LANE0 + grp * EXP_PER_GROUP
        el = jnp.where((lane >= lo) & (lane < lo + EXP_PER_GROUP), logits, neg)
        v1, i1 = first_argmax(el)
        v2, i2 = first_argmax(jnp.where(lane == i1, neg, el))
        e21 = jnp.exp(v2 - v1)
        w1 = p_grp / (1.0 + e21)
        w2 = p_grp * e21 / (1.0 + e21)

        pick1 = lane == i1
        pick2 = lane == i2
        onehot = jnp.where(pick1 | pick2, 1.0, 0.0)
        cnt = _bdot(tri, onehot) + carry
        r1 = jnp.sum(jnp.where(pick1, cnt, 0.0), axis=1, keepdims=True) - 1.0
        r2 = jnp.sum(jnp.where(pick2, cnt, 0.0), axis=1, keepdims=True) - 1.0
        carry = cnt[ts - 1:ts, :]

        e1 = (i1 - ROUTE_LANE0).astype(F32)
        e2 = (i2 - ROUTE_LANE0).astype(F32)
        packed = jnp.zeros_like(logits)
        for idx, val in enumerate((e1, e2, w1, w2, r1, r2)):
            packed = jnp.where(lane == idx, val, packed)
        route_ref[rows, :] = packed

    carry_scr[...] = carry
    cnt_ref[...] = jnp.broadcast_to(carry, cnt_ref.shape)


def _oln(o_p, o_s, h1, wo, ln_g, ln_b, wr, br, *, tm):
    Np, Ns = o_p.shape[0], o_s.shape[0]
    NT = Np + Ns
    tm = _tile(math.gcd(Np, Ns), tm)
    npt = Np // tm
    const = lambda i: (0, 0)
    return pl.pallas_call(
        functools.partial(_oln_kernel, n_prompt_tiles=npt, n_sub=2 if tm % (2 * SUBLANES) == 0 else 1),
        grid=(NT // tm,),
        in_specs=[pl.BlockSpec((tm, D_MODEL), lambda i: (jnp.minimum(i, npt - 1), 0)),
                  pl.BlockSpec((tm, D_MODEL), lambda i: (jnp.maximum(i - npt, 0), 0)),
                  pl.BlockSpec((tm, D_MODEL), lambda i: (i, 0)),
                  pl.BlockSpec((D_MODEL, D_MODEL), const, pipeline_mode=pl.Buffered(1)),
                  pl.BlockSpec((1, D_MODEL), const),
                  pl.BlockSpec((1, D_MODEL), const),
                  pl.BlockSpec((D_MODEL, LANES), const),
                  pl.BlockSpec((1, LANES), const)],
        out_specs=[pl.BlockSpec((tm, D_MODEL), lambda i: (i, 0)),
                   pl.BlockSpec((tm, LANES), lambda i: (i, 0)),
                   pl.BlockSpec((SUBLANES, LANES), const)],
        out_shape=[jax.ShapeDtypeStruct((NT, D_MODEL), F32),
                   jax.ShapeDtypeStruct((NT, LANES), F32),
                   jax.ShapeDtypeStruct((SUBLANES, LANES), F32)],
        scratch_shapes=[pltpu.VMEM((1, LANES), F32)],
        compiler_params=_cparams(("arbitrary",)),
        name="oproj_ln2_router",
    )(o_p, o_s, h1, wo, ln_g, ln_b, wr, br)


def _row_copy(src_ref, src_row, dst_ref, dst_row, sem):
    return pltpu.make_async_copy(src_ref.at[pl.ds(src_row, 1), :], dst_ref.at[pl.ds(dst_row, 1), :], sem)


def _expert_weight_copies(e, slot, w_hbm, w_f32, sem):
    return [pltpu.make_async_copy(src.at[e], dst.at[slot], sem.at[slot, j])
            for j, (src, dst) in enumerate(zip(w_hbm, w_f32))]


def _expert_kernel(it_ref, ie_ref, lo_ref, hi_ref, first_ref, n_ref, ord_ref, nxt_ref, pos_ref,
                   h2_hbm, wg_hbm, wu_hbm, wd_hbm, y_ref,
                   xbuf, xsem, wg_f, wu_f, wd_f, wsem, wg_s, wu_s, wd_s, tok_ref):
    i = pl.program_id(0)
    n = n_ref[0]
    TR = xbuf.shape[1]
    w_hbm = (wg_hbm, wu_hbm, wd_hbm)
    w_f32 = (wg_f, wu_f, wd_f)

    def gather(base, slot, r):
        return _row_copy(h2_hbm, tok_ref[base + r], xbuf.at[slot], r, xsem.at[slot])

    def wait_rows(slot):
        def body(r, carry):
            gather(0, slot, r).wait()
            return carry
        lax.fori_loop(0, TR, body, 0, unroll=DMA_UNROLL)

    @pl.when(i == 0)
    def _prologue():
        for c in _expert_weight_copies(ie_ref[0], 0, w_hbm, w_f32, wsem):
            c.start()

        def invert(c, carry):
            p0 = pl.multiple_of(c * INVERT_GROUP, INVERT_GROUP)
            t0 = c * (INVERT_GROUP // TOP_K)
            rows = [pos_ref[p0 + u] for u in range(INVERT_GROUP)]
            for u in range(INVERT_GROUP):
                tok_ref[rows[u]] = t0 + u // TOP_K
            return carry
        lax.fori_loop(0, pos_ref.shape[0] // INVERT_GROUP, invert, 0)

        for ahead in range(N_XBUF - 1):
            base = it_ref[jnp.minimum(ahead, n - 1)] * TR

            def body(r, carry, base=base, ahead=ahead):
                gather(base, ahead, r).start()
                return carry
            lax.fori_loop(0, TR, body, 0, unroll=DMA_UNROLL)

    @pl.when(i < n)
    def _compute():
        slot = i % N_XBUF
        fill = (i + N_XBUF - 1) % N_XBUF
        wslot = ord_ref[i] % 2
        fresh = jnp.logical_or(i == 0, ie_ref[i] != ie_ref[jnp.maximum(i - 1, 0)])

        @pl.when(fresh)
        def _new_expert():
            for c in _expert_weight_copies(ie_ref[i], wslot, w_hbm, w_f32, wsem):
                c.wait()
            wg_s[...] = wg_f[wslot].astype(BF16)
            wu_s[...] = wu_f[wslot].astype(BF16)
            wd_s[...] = wd_f[wslot].astype(BF16)

            @pl.when(nxt_ref[i] >= 0)
            def _prefetch():
                for c in _expert_weight_copies(nxt_ref[i], 1 - wslot, w_hbm, w_f32, wsem):
                    c.start()

        wait_rows(slot)
        x = xbuf[slot].astype(BF16)
        base_next = it_ref[jnp.minimum(i + N_XBUF - 1, n - 1)] * TR
        for r in range(TR):
            gather(base_next, fill, r).start()
        gate = jnp.dot(x, wg_s[...], preferred_element_type=F32)
        up = jnp.dot(x, wu_s[...], preferred_element_type=F32)
        hid = gate * _sigmoid(gate) * up
        y = jnp.dot(hid.astype(BF16), wd_s[...], preferred_element_type=F32)
        rows = it_ref[i] * TR + lax.broadcasted_iota(jnp.int32, (TR, 1), 0)
        mine = (rows >= lo_ref[i]) & (rows < hi_ref[i])

        @pl.when(first_ref[i] == 1)
        def _first():
            y_ref[...] = jnp.where(mine, y, 0.0)

        @pl.when(first_ref[i] == 0)
        def _later():
            y_ref[...] = jnp.where(mine, y, y_ref[...])

        @pl.when(i == n - 1)
        def _drain():
            for ahead in range(1, N_XBUF):
                wait_rows((i + ahead) % N_XBUF)


def _experts(items, pos_flat, h2, e_wg, e_wu, e_wd):
    TR = MOE_ROW_TILE
    n_work = items[0].shape[0]
    n_rows = pos_flat.shape[0]
    assert n_rows % INVERT_GROUP == 0 and INVERT_GROUP % TOP_K == 0
    any_spec = pl.BlockSpec(memory_space=pl.ANY)
    return pl.pallas_call(
        _expert_kernel,
        grid_spec=pltpu.PrefetchScalarGridSpec(
            num_scalar_prefetch=9,
            grid=(n_work,),
            in_specs=[any_spec, any_spec, any_spec, any_spec],
            out_specs=pl.BlockSpec((TR, D_MODEL), lambda i, it, *_: (it[i], 0)),
            scratch_shapes=[pltpu.VMEM((N_XBUF, TR, D_MODEL), F32), pltpu.SemaphoreType.DMA((N_XBUF,)),
                            pltpu.VMEM((2, D_MODEL, EXP_FF), F32), pltpu.VMEM((2, D_MODEL, EXP_FF), F32),
                            pltpu.VMEM((2, EXP_FF, D_MODEL), F32), pltpu.SemaphoreType.DMA((2, 3)),
                            pltpu.VMEM((D_MODEL, EXP_FF), BF16), pltpu.VMEM((D_MODEL, EXP_FF), BF16),
                            pltpu.VMEM((EXP_FF, D_MODEL), BF16), pltpu.SMEM((n_rows,), jnp.int32)],
        ),
        out_shape=jax.ShapeDtypeStruct((n_rows, D_MODEL), F32),
        compiler_params=_cparams(("arbitrary",)),
        name="moe_experts",
    )(*items, pos_flat, h2, e_wg, e_wu, e_wd)


def _moe_schedule(route, counts, n_rows):
    TR = MOE_ROW_TILE
    assert n_rows % TR == 0
    n_work = n_rows // TR + N_EXPERTS - 1
    rt = route[:, :SUBLANES].T
    cnt = counts[0, ROUTE_LANE0:ROUTE_LANE0 + N_EXPERTS].astype(jnp.int32)
    g_end = jnp.cumsum(cnt)
    g_start = g_end - cnt
    onehot = rt[0:2, :, None] == jnp.arange(N_EXPERTS, dtype=F32)
    start_of = jnp.sum(jnp.where(onehot, g_start.astype(F32), 0.0), axis=-1)
    pos_flat = (start_of + rt[4:6]).astype(jnp.int32).T.reshape(-1)
    first_tile = g_start // TR
    n_items_e = jnp.where(cnt > 0, (g_end - 1) // TR - first_tile + 1, 0)
    item_end = jnp.cumsum(n_items_e)
    n_items = item_end[-1:]
    idx = jnp.minimum(jnp.arange(n_work, dtype=jnp.int32), n_items[0] - 1)
    ie = jnp.minimum(jnp.sum((item_end[None, :] <= idx[:, None]).astype(jnp.int32), axis=1), N_EXPERTS - 1)
    it = first_tile[ie] + idx - (item_end - n_items_e)[ie]
    first = jnp.concatenate([jnp.ones((1,), jnp.int32), (it[1:] != it[:-1]).astype(jnp.int32)])
    has = n_items_e > 0
    e_ids = jnp.arange(N_EXPERTS, dtype=jnp.int32)
    ordinal = jnp.cumsum(has.astype(jnp.int32)) - 1
    later = (e_ids[None, :] > e_ids[:, None]) & has[None, :]
    nxt = jnp.min(jnp.where(later, e_ids[None, :], N_EXPERTS), axis=1)
    nxt = jnp.where(nxt == N_EXPERTS, -1, nxt)
    items = (it.astype(jnp.int32), ie, g_start[ie], g_end[ie], first, n_items.astype(jnp.int32),
             ordinal[ie], nxt[ie])
    return pos_flat, items


def _combine_kernel(pos_ref, h2_ref, route_ref, ys_ref, g_ref, b_ref, yp_ref, ysm_ref,
                    buf, sem, *, tm, n_prompt_tiles):
    i = pl.program_id(0)
    last = pl.num_programs(0) - 1
    slot = i % 2

    def fetch(tile, slot, r, k):
        return _row_copy(ys_ref, pos_ref[TOP_K * (tile * tm + r) + k], buf.at[slot, k], r, sem.at[slot])

    def wait_rows(slot):
        def body(r, carry):
            for k in range(TOP_K):
                fetch(0, slot, r, k).wait()
            return carry
        lax.fori_loop(0, tm, body, 0, unroll=DMA_UNROLL)

    @pl.when(i == 0)
    def _prologue():
        def body(r, carry):
            for k in range(TOP_K):
                fetch(0, 0, r, k).start()
            return carry
        lax.fori_loop(0, tm, body, 0, unroll=DMA_UNROLL)

    wait_rows(slot)
    nxt = jnp.minimum(i + 1, last)
    for r in range(tm):
        for k in range(TOP_K):
            fetch(nxt, 1 - slot, r, k).start()
    route = route_ref[...]
    moe = route[:, 2:3] * buf[slot, 0] + route[:, 3:4] * buf[slot, 1]
    y = _layer_norm(ALPHA * h2_ref[...] + moe, g_ref[...], b_ref[...])

    @pl.when(i == last)
    def _drain():
        wait_rows(1 - slot)

    @pl.when(i < n_prompt_tiles)
    def _prompt():
        yp_ref[...] = y

    @pl.when(i >= n_prompt_tiles)
    def _sample():
        ysm_ref[...] = y


def _combine(pos_flat, h2, route, ys, ln_g, ln_b, *, Np, Ns, tm):
    NT = h2.shape[0]
    tm = _tile(math.gcd(Np, Ns), tm)
    npt = Np // tm
    const = lambda i, pos: (0, 0)
    return pl.pallas_call(
        functools.partial(_combine_kernel, tm=tm, n_prompt_tiles=npt),
        grid_spec=pltpu.PrefetchScalarGridSpec(
            num_scalar_prefetch=1,
            grid=(NT // tm,),
            in_specs=[pl.BlockSpec((tm, D_MODEL), lambda i, pos: (i, 0)),
                      pl.BlockSpec((tm, LANES), lambda i, pos: (i, 0)),
                      pl.BlockSpec(memory_space=pl.ANY),
                      pl.BlockSpec((1, D_MODEL), const),
                      pl.BlockSpec((1, D_MODEL), const)],
            out_specs=[pl.BlockSpec((tm, D_MODEL), lambda i, pos: (jnp.minimum(i, npt - 1), 0)),
                       pl.BlockSpec((tm, D_MODEL), lambda i, pos: (jnp.maximum(i - npt, 0), 0))],
            scratch_shapes=[pltpu.VMEM((2, TOP_K, tm, D_MODEL), F32), pltpu.SemaphoreType.DMA((2,))],
        ),
        out_shape=[jax.ShapeDtypeStruct((Np, D_MODEL), F32),
                   jax.ShapeDtypeStruct((Ns, D_MODEL), F32)],
        compiler_params=_cparams(("arbitrary",)),
        name="moe_combine_ln3",
    )(pos_flat, h2, route, ys, ln_g, ln_b)


def kernel(x_prompt, x_sample, mem_prompt, cache_mem_k, cache_mem_v, state_mlstm_C, state_mlstm_n,
           state_mlstm_m, state_mlstm_conv, state_hgrn_S, w_in, b_in, conv_w, mlstm_gn, lb_logits, hgrn_gn,
           w_bm, w_bh, w_out, ln1_g, ln1_b, xa_wq, xa_wk, xa_wv, xa_wo, ln2_g, ln2_b,
           r1_w, r1_b, r2_w, r2_b, e_wg, e_wu, e_wd, ln3_g, ln3_b):
    Bp, Tp, _ = x_prompt.shape
    Bs, Ts, _ = x_sample.shape
    MEM = mem_prompt.shape[1]
    Np, Ns = Bp * Tp, Bs * Ts
    NT = Np + Ns
    Lp_m = math.gcd(Tp, 256)
    Lp_h = math.gcd(Tp, 64)
    Ls = Ts
    assert Ts % SUBLANES == 0 and Np % Ls == 0

    xp2 = x_prompt.reshape(Np, D_MODEL)
    xs2 = x_sample.reshape(Ns, D_MODEL)

    w = w_in[0]
    bi = b_in[0]
    w16 = w.astype(BF16)
    w_rest = w16[:, GATE_LO + 2 * M_H:]
    b_main = jnp.concatenate([bi[:GATE_LO], bi[GATE_LO + 2 * M_H:]])[None]
    w_gate = jnp.pad(w[:, GATE_LO:GATE_LO + 2 * M_H], ((0, 0), (0, LANES - 2 * M_H))).astype(BF16)
    b_gate = jnp.pad(bi[GATE_LO:GATE_LO + 2 * M_H], (0, LANES - 2 * M_H))[None]
    lb = jnp.cumsum(jax.nn.softmax(lb_logits.astype(F32), axis=0), axis=0)[0][None]
    wr = jnp.zeros((D_MODEL, LANES), F32)
    wr = wr.at[:, :N_GROUPS].set(r1_w[0])
    wr = wr.at[:, ROUTE_LANE0:ROUTE_LANE0 + N_EXPERTS].set(
        jnp.transpose(r2_w[0], (1, 0, 2)).reshape(D_MODEL, N_EXPERTS)).astype(BF16)
    br = jnp.zeros((1, LANES), F32)
    br = br.at[0, :N_GROUPS].set(r1_b[0])
    br = br.at[0, ROUTE_LANE0:ROUTE_LANE0 + N_EXPERTS].set(r2_b[0].reshape(N_EXPERTS))
    zeros_d = jnp.zeros((1, D_MODEL), F32)

    memb = mem_prompt.reshape(Bp * MEM, D_MODEL).astype(BF16)
    mk = _matmul_bias(memb, xa_wk[0], zeros_d, tm=1024, tn=1024, name="mem_k")
    mv = _matmul_bias(memb, xa_wv[0], zeros_d, tm=1024, tn=1024, name="mem_v")

    u_main, u_gate = _in_proj(xp2, xs2, w16, w_rest, b_main, w_gate, b_gate, tm=1024, tn=1024)

    padc = lambda c: jnp.pad(c, ((0, 0), (SUBLANES - (CONV_K - 1), 0), (0, 0)))
    zC = jnp.zeros((Bp, M_H, M_DK, M_DV), F32)
    zn = jnp.zeros((Bp, M_H, M_DK), F32)
    zm = jnp.zeros((Bp, 1, M_H), F32)
    zconv = jnp.zeros((Bp, SUBLANES, 2 * M_W), F32)
    zS = jnp.zeros((Bp, HG_H, HG_DK, HG_DV), F32)
    cw = conv_w[0]
    mgn = mlstm_gn[0][None]
    hgn = hgrn_gn[0][None]
    hm_p, C_p, n_p, m_p, conv_p = _mlstm(u_main, u_gate, zconv, zC, zn, zm, cw, mgn,
                                         B=Bp, T=Tp, L=Lp_m, row0=0)
    hm_s, C_s, n_s, m_s, conv_s = _mlstm(u_main, u_gate, padc(state_mlstm_conv[0]), state_mlstm_C[0],
                                         state_mlstm_n[0], state_mlstm_m[0][:, None, :], cw, mgn,
                                         B=Bs, T=Ts, L=Ls, row0=Np, NB=SAMPLE_SEQS_PER_STEP)
    og_p, S_p = _hgrn(u_main, zS, lb, hgn, B=Bp, T=Tp, L=Lp_h, row0=0)
    og_s, S_s = _hgrn(u_main, state_hgrn_S[0], lb, hgn, B=Bs, T=Ts, L=Ls, row0=Np, NB=SAMPLE_SEQS_PER_STEP)

    h1, h1b = _merge(hm_p, hm_s, og_p, og_s, u_main, xp2, xs2, w_bm[0].astype(BF16), w_bh[0].astype(BF16),
                     w_out[0].astype(BF16), ln1_g, ln1_b, tm=256)

    q = _matmul_bias(h1b, xa_wq[0], zeros_d, tm=1024, tn=1024, name="xa_q")
    att_p = _attention(q, mk, mv, B=Bp, T=Tp, tq=512, row0=0)
    att_s = _attention_cache(q, cache_mem_k, cache_mem_v, B=Bs, T=Ts, row0=Np)
    h2, route, counts = _oln(att_p, att_s, h1, xa_wo[0].astype(BF16), ln2_g, ln2_b, wr, br, tm=512)

    pos_flat, items = _moe_schedule(route, counts, TOP_K * NT)
    ys_sorted = _experts(items, pos_flat, h2, e_wg[0], e_wu[0], e_wd[0])
    y_p, y_s = _combine(pos_flat, h2, route, ys_sorted, ln3_g, ln3_b, Np=Np, Ns=Ns, tm=256)

    kv5 = lambda a: a.reshape(1, Bp, MEM, XA_H, XA_D)
    return (y_p.reshape(Bp, Tp, D_MODEL), y_s.reshape(Bs, Ts, D_MODEL), kv5(mk), kv5(mv),
            C_p[None], n_p[None], m_p.reshape(1, Bp, M_H), conv_p[None], S_p[None],
            C_s[None], n_s[None], m_s.reshape(1, Bs, M_H), conv_s[None], S_s[None])
```

```python
import functools
import math

import jax
import jax.numpy as jnp
from jax import lax
from jax.experimental import pallas as pl
from jax.experimental.pallas import tpu as pltpu

F32 = jnp.float32
BF16 = jnp.bfloat16

D_MODEL = 2048
M_W = 1024
M_H = 4
M_DK = 256
M_DV = 256
CONV_K = 4
HG_W = 1024
HG_H = 8
HG_DK = 128
HG_DV = 128
XA_H = 4
XA_D = 512
N_GROUPS = 4
EXP_PER_GROUP = 8
N_EXPERTS = 32
TOP_K = 2
EXP_FF = 512
DEPTH = 1
ALPHA = (2 * DEPTH) ** 0.25
LN_EPS = 1e-5

COL_QK, COL_V, COL_O = 0, 2048, 3072
COL_QH, COL_FH, COL_IH, COL_GH = 4096, 5120, 6144, 7168
COL_GM, COL_GHH = 8192, 10240
N_MAIN = 12288
GATE_LO = 4 * M_W
LANES = 128
SUBLANES = 8
ROUTE_LANE0 = N_GROUPS

VMEM_LIMIT = 56 << 20
MOE_ROW_TILE = 256
SAMPLE_SEQS_PER_STEP = 4
N_XBUF = 3
DMA_UNROLL = 8
INVERT_GROUP = 16


def _cparams(sem, vmem=VMEM_LIMIT):
    return pltpu.CompilerParams(dimension_semantics=sem, vmem_limit_bytes=vmem)


def _tile(n, pref):
    t = math.gcd(n, pref)
    assert t % SUBLANES == 0, (n, pref)
    return t


def _bdot(a, b):
    return jnp.dot(a.astype(BF16), b.astype(BF16), preferred_element_type=F32)


def _bdot_nt(a, b):
    return lax.dot_general(a.astype(BF16), b.astype(BF16), (((1,), (1,)), ((), ())),
                           preferred_element_type=F32)


def _bdot_tn(a, b):
    return lax.dot_general(a.astype(BF16), b.astype(BF16), (((0,), (0,)), ((), ())),
                           preferred_element_type=F32)


def _sigmoid(x):
    return 1.0 / (1.0 + jnp.exp(-x))


def _cumsum_rows(x, seg=None):
    n = seg or x.shape[0]
    assert n & (n - 1) == 0
    row = lax.broadcasted_iota(jnp.int32, x.shape, 0) & (n - 1)
    d = 1
    while d < n:
        x = x + jnp.where(row >= d, pltpu.roll(x, d, axis=0), 0.0)
        d *= 2
    return x


def _col_to_row(col, eye):
    return jnp.sum(jnp.where(eye, col, 0.0), axis=0, keepdims=True)


def _row_to_col(row, eye):
    return jnp.sum(jnp.where(eye, row, 0.0), axis=1, keepdims=True)


def _layer_norm(x, g, b):
    mu = jnp.mean(x, axis=-1, keepdims=True)
    xc = x - mu
    var = jnp.mean(xc * xc, axis=-1, keepdims=True)
    return xc * lax.rsqrt(var + LN_EPS) * g + b


def _mm_kernel(x_ref, w_ref, b_ref, o_ref, wb_scr):
    @pl.when(pl.program_id(1) == 0)
    def _new_weight_tile():
        wb_scr[...] = w_ref[...].astype(BF16)

    acc = jnp.dot(x_ref[...], wb_scr[...], preferred_element_type=F32)
    o_ref[...] = (acc + b_ref[...]).astype(o_ref.dtype)


def _matmul_bias(x, w, b, *, tm, tn, out_dtype=F32, name):
    M, K = x.shape
    N = w.shape[1]
    tm = _tile(M, tm)
    tn = _tile(N, tn)
    return pl.pallas_call(
        _mm_kernel,
        grid=(N // tn, M // tm),
        in_specs=[pl.BlockSpec((tm, K), lambda j, i: (i, 0)),
                  pl.BlockSpec((K, tn), lambda j, i: (0, j)),
                  pl.BlockSpec((1, tn), lambda j, i: (0, j))],
        out_specs=pl.BlockSpec((tm, tn), lambda j, i: (i, j)),
        out_shape=jax.ShapeDtypeStruct((M, N), out_dtype),
        scratch_shapes=[pltpu.VMEM((K, tn), BF16)],
        compiler_params=_cparams(("parallel", "arbitrary")),
        name=name,
    )(x, w, b)


def _inproj_kernel(xp_ref, xs_ref, wt_ref, b_ref, u_ref, wb_scr, *, n_prompt_tiles):
    i = pl.program_id(1)

    @pl.when(i == 0)
    def _new_weight_tile():
        wb_scr[...] = wt_ref[...].astype(BF16)

    xb = jnp.where(i < n_prompt_tiles, xp_ref[...], xs_ref[...])
    u_ref[...] = _bdot_nt(xb, wb_scr[...]) + b_ref[...]


def _in_proj(xb_p, xb_s, w_t, b_main, *, tm, tn):
    Np, Ns = xb_p.shape[0], xb_s.shape[0]
    NT = Np + Ns
    K = w_t.shape[1]
    N = w_t.shape[0] - 2 * M_H
    tm = _tile(math.gcd(Np, Ns), tm)
    tn = _tile(math.gcd(GATE_LO, N), tn)
    assert (2 * M_H) % SUBLANES == 0
    npt = Np // tm
    na = GATE_LO // tn
    w_rows = lambda j, i: (pl.multiple_of(j * tn + jnp.where(j < na, 0, 2 * M_H), SUBLANES), 0)
    return pl.pallas_call(
        functools.partial(_inproj_kernel, n_prompt_tiles=npt),
        grid=(N // tn, NT // tm),
        in_specs=[pl.BlockSpec((tm, K), lambda j, i: (jnp.minimum(i, npt - 1), 0)),
                  pl.BlockSpec((tm, K), lambda j, i: (jnp.maximum(i - npt, 0), 0)),
                  pl.BlockSpec((pl.Element(tn), pl.Element(K)), w_rows),
                  pl.BlockSpec((1, tn), lambda j, i: (0, j))],
        out_specs=pl.BlockSpec((tm, tn), lambda j, i: (i, j)),
        out_shape=jax.ShapeDtypeStruct((NT, N), F32),
        scratch_shapes=[pltpu.VMEM((tn, K), BF16)],
        compiler_params=_cparams(("parallel", "arbitrary")),
        name="in_proj",
    )(xb_p, xb_s, w_t, b_main)


def _mlstm_kernel(*refs, L, NC, NB):
    (qk_ref, v_ref, o_ref, xb_ref, wg_ref, bg_ref, conv0_ref, C0_ref, n0_ref, m0_ref, cw_ref, gn_ref,
     h_ref, Co_ref, no_ref, mo_ref, convo_ref, C_scr, n_scr, m_scr, tail_scr) = refs
    c = pl.program_id(1)
    if NC > 1:
        @pl.when(c == 0)
        def _init():
            C_scr[...] = C0_ref[0]
            n_scr[...] = n0_ref[0]
            m_scr[...] = m0_ref[0]
            tail_scr[...] = conv0_ref[0]

    cw = cw_ref[...]
    gn = gn_ref[...]
    g_all = jnp.dot(xb_ref[...], wg_ref[...], preferred_element_type=F32) + bg_ref[...]
    ti = lax.broadcasted_iota(jnp.int32, (L, L), 0)
    si = lax.broadcasted_iota(jnp.int32, (L, L), 1)
    eye = ti == si
    causal = si <= ti
    head_lane = lax.broadcasted_iota(jnp.int32, (1, M_H), 1)
    head_row = lax.broadcasted_iota(jnp.int32, (M_H, M_DK), 0)

    seqs = []
    for nb in range(NB):
        rows = slice(nb * L, (nb + 1) * L)
        if NC == 1:
            sq = dict(C_in=C0_ref.at[nb], C_out=Co_ref.at[nb], n_out=no_ref.at[nb], m_out=mo_ref.at[nb])
            n_all, m_all, tail = n0_ref[nb], m0_ref[nb], conv0_ref[nb]
        else:
            sq = dict(C_in=C_scr, C_out=C_scr, n_out=n_scr, m_out=m_scr)
            n_all, m_all, tail = n_scr[...], m_scr[...], tail_scr[...]
        qk_pre = qk_ref[rows, :]
        ext = jnp.concatenate([tail, qk_pre], axis=0)
        acc = qk_pre * cw[CONV_K - 1:CONV_K, :]
        for j in range(1, CONV_K):
            acc = acc + pltpu.roll(ext, j, axis=0)[SUBLANES:, :] * cw[CONV_K - 1 - j:CONV_K - j, :]
        if NC > 1:
            tail_scr[...] = qk_pre[L - SUBLANES:, :]
        qk = acc * _sigmoid(acc)
        g = g_all[rows, :]
        lf_all = jnp.minimum(g, 0.0) - jnp.log(1.0 + jnp.exp(-jnp.abs(g)))
        F_all = _cumsum_rows(lf_all)
        heads = []
        for h in range(M_H):
            ks = slice(h * M_DK, (h + 1) * M_DK)
            q = qk[:, ks] * (M_DK ** -0.5)
            k = qk[:, M_W + h * M_DK:M_W + (h + 1) * M_DK]
            v = v_ref[rows, ks]
            ig = g[:, h:h + 1]
            F = F_all[:, M_H + h:M_H + h + 1]
            m_prev = m_all[:, h:h + 1]
            r_row = _col_to_row(ig - F, eye)
            Dm = jnp.where(causal, F + r_row, -jnp.inf)
            init_w = F + m_prev
            m_t = jnp.maximum(init_w, jnp.max(Dm, axis=1, keepdims=True))
            P = jnp.exp(Dm - m_t)
            a0 = jnp.exp(init_w - m_t)
            FL = F[L - 1:L, :]
            mL = m_t[L - 1:L, :]
            wL = jnp.exp(FL - F + ig - mL)
            decay = jnp.exp(FL + m_prev - mL)
            heads.append(dict(ks=ks, q=q, k=k, v=v, m_t=m_t, P=P, a0=a0, mL=mL, decay=decay, kw=wL * k))
        sq.update(rows=rows, heads=heads, n_all=n_all, m_all=m_all, qk_pre=qk_pre)
        seqs.append(sq)

    for sq in seqs:
        for h, d in enumerate(sq["heads"]):
            d["S"] = _bdot_nt(d["q"], d["k"])
            d["qC"] = _bdot(d["q"], sq["C_in"][h])
            d["kv"] = _bdot_tn(d["kw"], d["v"])

    for sq in seqs:
        for d in sq["heads"]:
            d["Sc"] = d["S"] * d["P"]
            d["num"] = _bdot(d["Sc"], d["v"])

    for nb, sq in enumerate(seqs):
        rows, n_new, m_new = sq["rows"], sq["n_all"], sq["m_all"]
        for h, d in enumerate(sq["heads"]):
            ks, q, a0, decay = d["ks"], d["q"], d["a0"], d["decay"]
            n_row = sq["n_all"][h:h + 1, :]
            num = d["num"] + a0 * d["qC"]
            den = jnp.sum(d["Sc"], axis=1, keepdims=True) + a0 * jnp.sum(q * n_row, axis=1, keepdims=True)
            hh = num * (1.0 / jnp.maximum(jnp.abs(den), jnp.exp(-d["m_t"])))
            sq["C_out"][h] = decay * sq["C_in"][h] + d["kv"]
            n_new = jnp.where(head_row == h, decay * n_row + jnp.sum(d["kw"], axis=0, keepdims=True), n_new)
            m_new = jnp.where(head_lane == h, d["mL"], m_new)

            hm = _sigmoid(o_ref[rows, ks]) * hh
            hm = hm - jnp.mean(hm, axis=1, keepdims=True)
            hm = hm * lax.rsqrt(jnp.mean(hm * hm, axis=1, keepdims=True) + LN_EPS) * gn[:, ks]
            h_ref[rows, ks] = hm
        sq["n_out"][...] = n_new
        sq["m_out"][...] = m_new
        conv_tail = sq["qk_pre"][L - (CONV_K - 1):, :]
        if NC == 1:
            convo_ref[nb] = conv_tail
        else:
            @pl.when(c == NC - 1)
            def _fin():
                Co_ref[0] = C_scr[...]
                no_ref[0] = n_new
                mo_ref[0] = m_new
                convo_ref[0] = conv_tail


def _mlstm(u_main, xb, w_gate, b_gate, conv0p, C0, n0, m0, conv_w, gn, *, B, T, L, row0, NB=1):
    NC = T // L
    assert NB == 1 or NC == 1
    assert B % NB == 0 and row0 % (NB * L) == 0
    R = NB * L
    rb0 = row0 // R
    tok = lambda b, c: rb0 + b * NC + c
    in_specs = [
        pl.BlockSpec((R, 2 * M_W), lambda b, c: (tok(b, c), COL_QK // (2 * M_W))),
        pl.BlockSpec((R, M_W), lambda b, c: (tok(b, c), COL_V // M_W)),
        pl.BlockSpec((R, M_W), lambda b, c: (tok(b, c), COL_O // M_W)),
        pl.BlockSpec((R, D_MODEL), lambda b, c: (b * NC + c, 0)),
        pl.BlockSpec((D_MODEL, LANES), lambda b, c: (0, 0)),
        pl.BlockSpec((1, LANES), lambda b, c: (0, 0)),
        pl.BlockSpec((NB, SUBLANES, 2 * M_W), lambda b, c: (b, 0, 0)),
        pl.BlockSpec((NB, M_H, M_DK, M_DV), lambda b, c: (b, 0, 0, 0)),
        pl.BlockSpec((NB, M_H, M_DK), lambda b, c: (b, 0, 0)),
        pl.BlockSpec((NB, 1, M_H), lambda b, c: (b, 0, 0)),
        pl.BlockSpec((CONV_K, 2 * M_W), lambda b, c: (0, 0)),
        pl.BlockSpec((1, M_W), lambda b, c: (0, 0)),
    ]
    args = [u_main, u_main, u_main, xb, w_gate, b_gate, conv0p, C0, n0, m0, conv_w, gn]
    out_specs = [
        pl.BlockSpec((R, M_W), lambda b, c: (b * NC + c, 0)),
        pl.BlockSpec((NB, M_H, M_DK, M_DV), lambda b, c: (b, 0, 0, 0)),
        pl.BlockSpec((NB, M_H, M_DK), lambda b, c: (b, 0, 0)),
        pl.BlockSpec((NB, 1, M_H), lambda b, c: (b, 0, 0)),
        pl.BlockSpec((NB, CONV_K - 1, 2 * M_W), lambda b, c: (b, 0, 0)),
    ]
    out_shape = [
        jax.ShapeDtypeStruct((B * T, M_W), F32),
        jax.ShapeDtypeStruct((B, M_H, M_DK, M_DV), F32),
        jax.ShapeDtypeStruct((B, M_H, M_DK), F32),
        jax.ShapeDtypeStruct((B, 1, M_H), F32),
        jax.ShapeDtypeStruct((B, CONV_K - 1, 2 * M_W), F32),
    ]
    return pl.pallas_call(
        functools.partial(_mlstm_kernel, L=L, NC=NC, NB=NB),
        grid=(B // NB, NC),
        in_specs=in_specs,
        out_specs=out_specs,
        out_shape=out_shape,
        scratch_shapes=[pltpu.VMEM((M_H, M_DK, M_DV), F32), pltpu.VMEM((M_H, M_DK), F32),
                        pltpu.VMEM((1, M_H), F32), pltpu.VMEM((SUBLANES, 2 * M_W), F32)],
        compiler_params=_cparams(("parallel", "arbitrary")),
        name=f"mlstm_L{L}",
    )(*args)


def _hgrn_kernel(*refs, L, NC, NB):
    (q_ref, f_ref, i_ref, g_ref, S0_ref, lb_ref, gn_ref, o_ref, So_ref, S_scr) = refs
    c = pl.program_id(1)
    if NC > 1:
        @pl.when(c == 0)
        def _init():
            S_scr[...] = S0_ref[0]

    lb = lb_ref[...]
    f = lb + (1.0 - lb) * _sigmoid(f_ref[...])
    kk = 1.0 - f
    b = _cumsum_rows(jnp.log(f), seg=L)
    qh = q_ref[...]
    q = qh * _sigmoid(qh)
    v = i_ref[...]
    gh = g_ref[...]
    gsilu = gh * _sigmoid(gh)
    gn = gn_ref[...]
    ti = lax.broadcasted_iota(jnp.int32, (L, L), 0)
    si = lax.broadcasted_iota(jnp.int32, (L, L), 1)
    causal = si <= ti
    ci = lax.broadcasted_iota(jnp.int32, (HG_DK, HG_DK), 0)
    cj = lax.broadcasted_iota(jnp.int32, (HG_DK, HG_DK), 1)
    eye = ci == cj
    mid = max(L // 2 - 1, 0)
    hsl = [slice(h * HG_DK, (h + 1) * HG_DK) for h in range(HG_H)]

    seqs = []
    for nb in range(NB):
        rows = slice(nb * L, (nb + 1) * L)
        bs, qs, ks = b[rows], q[rows], kk[rows]
        bL = bs[L - 1:L, :]
        bm = bs[mid:mid + 1, :]
        S_in, S_out = (S0_ref.at[nb], So_ref.at[nb]) if NC == 1 else (S_scr, S_scr)
        seqs.append(dict(rows=rows, bL=bL, v=v[rows], q_in=qs * jnp.exp(bs), q_t=qs * jnp.exp(bs - bm),
                         k_t=ks * jnp.exp(bm - bs), k_st=ks * jnp.exp(bL - bs), S_in=S_in, S_out=S_out))

    for sq in seqs:
        sq["A"] = [jnp.where(causal, _bdot_nt(sq["q_t"][:, hs], sq["k_t"][:, hs]), 0.0) for hs in hsl]
        sq["qS"] = [_bdot(sq["q_in"][:, hs], sq["S_in"][h]) for h, hs in enumerate(hsl)]
        sq["kv"] = [_bdot_tn(sq["k_st"][:, hs], sq["v"][:, hs]) for hs in hsl]
    for sq in seqs:
        sq["o_intra"] = [_bdot(sq["A"][h], sq["v"][:, hs]) for h, hs in enumerate(hsl)]
    for sq in seqs:
        rows = sq["rows"]
        for h, hs in enumerate(hsl):
            o = sq["o_intra"][h] + sq["qS"][h]
            dec = jnp.exp(_row_to_col(sq["bL"][:, hs], eye))
            sq["S_out"][h] = dec * sq["S_in"][h] + sq["kv"][h]
            o = o * lax.rsqrt(jnp.mean(o * o, axis=1, keepdims=True) + LN_EPS) * gn[:, hs]
            o_ref[rows, hs] = o * gsilu[rows, hs]

    if NC > 1:
        @pl.when(c == NC - 1)
        def _fin():
            So_ref[0] = S_scr[...]


def _hgrn(u_main, S0, lb, gn, *, B, T, L, row0, NB=1):
    NC = T // L
    assert NB == 1 or NC == 1
    assert B % NB == 0 and row0 % (NB * L) == 0
    R = NB * L
    rb0 = row0 // R
    tok = lambda b, c: rb0 + b * NC + c
    col = lambda off: (lambda b, c: (tok(b, c), off // HG_W))
    in_specs = [
        pl.BlockSpec((R, HG_W), col(COL_QH)),
        pl.BlockSpec((R, HG_W), col(COL_FH)),
        pl.BlockSpec((R, HG_W), col(COL_IH)),
        pl.BlockSpec((R, HG_W), col(COL_GH)),
        pl.BlockSpec((NB, HG_H, HG_DK, HG_DV), lambda b, c: (b, 0, 0, 0)),
        pl.BlockSpec((1, HG_W), lambda b, c: (0, 0)),
        pl.BlockSpec((1, HG_W), lambda b, c: (0, 0)),
    ]
    args = [u_main, u_main, u_main, u_main, S0, lb, gn]
    return pl.pallas_call(
        functools.partial(_hgrn_kernel, L=L, NC=NC, NB=NB),
        grid=(B // NB, NC),
        in_specs=in_specs,
        out_specs=[pl.BlockSpec((R, HG_W), lambda b, c: (b * NC + c, 0)),
                   pl.BlockSpec((NB, HG_H, HG_DK, HG_DV), lambda b, c: (b, 0, 0, 0))],
        out_shape=[jax.ShapeDtypeStruct((B * T, HG_W), F32),
                   jax.ShapeDtypeStruct((B, HG_H, HG_DK, HG_DV), F32)],
        scratch_shapes=[pltpu.VMEM((HG_H, HG_DK, HG_DV), F32)],
        compiler_params=_cparams(("parallel", "arbitrary")),
        name=f"hgrn_L{L}",
    )(*args)


def _merge_kernel(hmp_ref, hms_ref, ogp_ref, ogs_ref, gm_ref, gh_ref, xp_ref, xs_ref, wbm_ref, wbh_ref,
                  wo_ref, g_ref, b_ref, h1_ref, h1b_ref, *, n_prompt_tiles):
    is_prompt = pl.program_id(0) < n_prompt_tiles
    a = _bdot(jnp.where(is_prompt, hmp_ref[...], hms_ref[...]), wbm_ref[...])
    bb = _bdot(jnp.where(is_prompt, ogp_ref[...], ogs_ref[...]), wbh_ref[...])
    merged = _sigmoid(gm_ref[...]) * a + _sigmoid(gh_ref[...]) * bb
    mix = _bdot(merged, wo_ref[...])
    x = jnp.where(is_prompt, xp_ref[...], xs_ref[...])
    h1 = _layer_norm(ALPHA * x + mix, g_ref[...], b_ref[...])
    h1_ref[...] = h1
    h1b_ref[...] = h1.astype(BF16)


def _merge(hm_p, hm_s, og_p, og_s, u_main, x_p, x_s, w_bm, w_bh, w_out, ln_g, ln_b, *, tm):
    Np, Ns = x_p.shape[0], x_s.shape[0]
    NT = Np + Ns
    tm = _tile(math.gcd(Np, Ns), tm)
    npt = Np // tm
    const = lambda i: (0, 0)
    prompt_rows = lambda i: (jnp.minimum(i, npt - 1), 0)
    sample_rows = lambda i: (jnp.maximum(i - npt, 0), 0)
    return pl.pallas_call(
        functools.partial(_merge_kernel, n_prompt_tiles=npt),
        grid=(NT // tm,),
        in_specs=[
            pl.BlockSpec((tm, M_W), prompt_rows),
            pl.BlockSpec((tm, M_W), sample_rows),
            pl.BlockSpec((tm, HG_W), prompt_rows),
            pl.BlockSpec((tm, HG_W), sample_rows),
            pl.BlockSpec((tm, D_MODEL), lambda i: (i, COL_GM // D_MODEL)),
            pl.BlockSpec((tm, D_MODEL), lambda i: (i, COL_GHH // D_MODEL)),
            pl.BlockSpec((tm, D_MODEL), prompt_rows),
            pl.BlockSpec((tm, D_MODEL), sample_rows),
            pl.BlockSpec((M_W, D_MODEL), const, pipeline_mode=pl.Buffered(1)),
            pl.BlockSpec((HG_W, D_MODEL), const, pipeline_mode=pl.Buffered(1)),
            pl.BlockSpec((D_MODEL, D_MODEL), const, pipeline_mode=pl.Buffered(1)),
            pl.BlockSpec((1, D_MODEL), const),
            pl.BlockSpec((1, D_MODEL), const),
        ],
        out_specs=[pl.BlockSpec((tm, D_MODEL), lambda i: (i, 0)),
                   pl.BlockSpec((tm, D_MODEL), lambda i: (i, 0))],
        out_shape=[jax.ShapeDtypeStruct((NT, D_MODEL), F32),
                   jax.ShapeDtypeStruct((NT, D_MODEL), BF16)],
        compiler_params=_cparams(("parallel",)),
        name="merge_ln1",
    )(hm_p, hm_s, og_p, og_s, u_main, u_main, x_p, x_s, w_bm, w_bh, w_out, ln_g, ln_b)


def _attn_kernel(q_ref, *refs):
    k_refs, v_refs, o_ref = refs[:XA_H], refs[XA_H:2 * XA_H], refs[2 * XA_H]
    for h in range(XA_H):
        sl = slice(h * XA_D, (h + 1) * XA_D)
        s = _bdot_nt(q_ref[:, sl], k_refs[h][...]) * (XA_D ** -0.5)
        e = jnp.exp(s - jnp.max(s, axis=1, keepdims=True))
        p = e / jnp.sum(e, axis=1, keepdims=True)
        o_ref[:, sl] = _bdot(p, v_refs[h][...])


def _attn_cache_kernel(q_ref, k_ref, v_ref, o_ref):
    T = q_ref.shape[0]
    M = k_ref.shape[2]
    k_all = k_ref[0, 0].reshape(M * XA_H, XA_D)
    v_all = v_ref[0, 0].reshape(M * XA_H, XA_D)
    q_all = jnp.concatenate([q_ref[:, h * XA_D:(h + 1) * XA_D] for h in range(XA_H)], axis=0)
    s = _bdot_nt(q_all, k_all) * (XA_D ** -0.5)
    q_head = lax.broadcasted_iota(jnp.int32, s.shape, 0) // T
    k_head = lax.broadcasted_iota(jnp.int32, s.shape, 1) % XA_H
    s = jnp.where(q_head == k_head, s, -jnp.inf)
    e = jnp.exp(s - jnp.max(s, axis=1, keepdims=True))
    p = e / jnp.sum(e, axis=1, keepdims=True)
    o = _bdot(p, v_all)
    for h in range(XA_H):
        o_ref[:, h * XA_D:(h + 1) * XA_D] = o[h * T:(h + 1) * T]


def _attention_cache(q, cache_k, cache_v, *, B, T, row0):
    M = cache_k.shape[2]
    rb0 = row0 // T
    kv_spec = pl.BlockSpec((1, 1, M, XA_H, XA_D), lambda b: (0, b, 0, 0, 0))
    return pl.pallas_call(
        _attn_cache_kernel,
        grid=(B,),
        in_specs=[pl.BlockSpec((T, D_MODEL), lambda b: (rb0 + b, 0)), kv_spec, kv_spec],
        out_specs=pl.BlockSpec((T, D_MODEL), lambda b: (b, 0)),
        out_shape=jax.ShapeDtypeStruct((B * T, D_MODEL), F32),
        compiler_params=_cparams(("parallel",)),
        name="xattn_cache",
    )(q, cache_k, cache_v)


def _attention(q, mem_k, mem_v, *, B, T, tq, row0):
    tq = _tile(T, tq)
    nq = T // tq
    rb0 = row0 // tq
    M = mem_k.shape[0] // B
    kv_specs = [pl.BlockSpec((M, XA_D), lambda b, t, h=h: (b, h)) for h in range(XA_H)]
    return pl.pallas_call(
        _attn_kernel,
        grid=(B, nq),
        in_specs=[pl.BlockSpec((tq, D_MODEL), lambda b, t: (rb0 + b * nq + t, 0))] + kv_specs + kv_specs,
        out_specs=pl.BlockSpec((tq, D_MODEL), lambda b, t: (b * nq + t, 0)),
        out_shape=jax.ShapeDtypeStruct((B * T, D_MODEL), F32),
        compiler_params=_cparams(("parallel", "parallel")),
        name=f"xattn_T{T}",
    )(q, *([mem_k] * XA_H), *([mem_v] * XA_H))


def _oln_kernel(op_ref, os_ref, h1_ref, wo_ref, g_ref, b_ref, wr_ref, br_ref,
                h2_ref, route_ref, cnt_ref, carry_scr, *, n_prompt_tiles, n_sub):
    i = pl.program_id(0)

    @pl.when(i == 0)
    def _init():
        carry_scr[...] = jnp.zeros_like(carry_scr)

    ts = h1_ref.shape[0] // n_sub
    lane = lax.broadcasted_iota(jnp.int32, (ts, LANES), 1)
    lane_f = lane.astype(F32)
    neg = -jnp.inf
    ti = lax.broadcasted_iota(jnp.int32, (ts, ts), 0)
    si = lax.broadcasted_iota(jnp.int32, (ts, ts), 1)
    tri = jnp.where(si <= ti, 1.0, 0.0).astype(BF16)
    is_prompt = i < n_prompt_tiles

    def first_argmax(vals):
        mx = jnp.max(vals, axis=1, keepdims=True)
        idx = jnp.min(jnp.where(vals == mx, lane_f, float(LANES)), axis=1, keepdims=True)
        return mx, idx.astype(jnp.int32)

    carry = carry_scr[...]
    for sb in range(n_sub):
        rows = slice(sb * ts, (sb + 1) * ts)
        att = jnp.where(is_prompt, op_ref[rows, :], os_ref[rows, :])
        h2 = _layer_norm(ALPHA * h1_ref[rows, :] + _bdot(att, wo_ref[...]), g_ref[...], b_ref[...])
        h2_ref[rows, :] = h2

        logits = _bdot(h2, wr_ref[...]) + br_ref[...]
        gl = jnp.where(lane < N_GROUPS, logits, neg)
        gmax, grp = first_argmax(gl)
        p_grp = 1.0 / jnp.sum(jnp.exp(gl - gmax), axis=1, keepdims=True)
        lo = ROUTE_LANE0 + grp * EXP_PER_GROUP
        el = jnp.where((lane >= lo) & (lane < lo + EXP_PER_GROUP), logits, neg)
        v1, i1 = first_argmax(el)
        v2, i2 = first_argmax(jnp.where(lane == i1, neg, el))
        e21 = jnp.exp(v2 - v1)
        w1 = p_grp / (1.0 + e21)
        w2 = p_grp * e21 / (1.0 + e21)

        pick1 = lane == i1
        pick2 = lane == i2
        onehot = jnp.where(pick1 | pick2, 1.0, 0.0)
        cnt = _bdot(tri, onehot) + carry
        r1 = jnp.sum(jnp.where(pick1, cnt, 0.0), axis=1, keepdims=True) - 1.0
        r2 = jnp.sum(jnp.where(pick2, cnt, 0.0), axis=1, keepdims=True) - 1.0
        carry = cnt[ts - 1:ts, :]

        e1 = (i1 - ROUTE_LANE0).astype(F32)
        e2 = (i2 - ROUTE_LANE0).astype(F32)
        packed = jnp.zeros_like(logits)
        for idx, val in enumerate((e1, e2, w1, w2, r1, r2)):
            packed = jnp.where(lane == idx, val, packed)
        route_ref[rows, :] = packed

    carry_scr[...] = carry
    cnt_ref[...] = jnp.broadcast_to(carry, cnt_ref.shape)


def _oln(o_p, o_s, h1, wo, ln_g, ln_b, wr, br, *, tm):
    Np, Ns = o_p.shape[0], o_s.shape[0]
    NT = Np + Ns
    tm = _tile(math.gcd(Np, Ns), tm)
    npt = Np // tm
    const = lambda i: (0, 0)
    return pl.pallas_call(
        functools.partial(_oln_kernel, n_prompt_tiles=npt, n_sub=2 if tm % (2 * SUBLANES) == 0 else 1),
        grid=(NT // tm,),
        in_specs=[pl.BlockSpec((tm, D_MODEL), lambda i: (jnp.minimum(i, npt - 1), 0)),
                  pl.BlockSpec((tm, D_MODEL), lambda i: (jnp.maximum(i - npt, 0), 0)),
                  pl.BlockSpec((tm, D_MODEL), lambda i: (i, 0)),
                  pl.BlockSpec((D_MODEL, D_MODEL), const, pipeline_mode=pl.Buffered(1)),
                  pl.BlockSpec((1, D_MODEL), const),
                  pl.BlockSpec((1, D_MODEL), const),
                  pl.BlockSpec((D_MODEL, LANES), const),
                  pl.BlockSpec((1, LANES), const)],
        out_specs=[pl.BlockSpec((tm, D_MODEL), lambda i: (i, 0)),
                   pl.BlockSpec((tm, LANES), lambda i: (i, 0)),
                   pl.BlockSpec((SUBLANES, LANES), const)],
        out_shape=[jax.ShapeDtypeStruct((NT, D_MODEL), F32),
                   jax.ShapeDtypeStruct((NT, LANES), F32),
                   jax.ShapeDtypeStruct((SUBLANES, LANES), F32)],
        scratch_shapes=[pltpu.VMEM((1, LANES), F32)],
        compiler_params=_cparams(("arbitrary",)),
        name="oproj_ln2_router",
    )(o_p, o_s, h1, wo, ln_g, ln_b, wr, br)


def _row_copy(src_ref, src_row, dst_ref, dst_row, sem):
    return pltpu.make_async_copy(src_ref.at[pl.ds(src_row, 1), :], dst_ref.at[pl.ds(dst_row, 1), :], sem)


def _expert_weight_copies(e, slot, w_hbm, w_f32, sem):
    return [pltpu.make_async_copy(src.at[e], dst.at[slot], sem.at[slot, j])
            for j, (src, dst) in enumerate(zip(w_hbm, w_f32))]


def _expert_kernel(it_ref, ie_ref, lo_ref, hi_ref, first_ref, n_ref, ord_ref, nxt_ref, pos_ref,
                   h2_hbm, wg_hbm, wu_hbm, wd_hbm, y_ref,
                   xbuf, xsem, wg_f, wu_f, wd_f, wsem, wg_s, wu_s, wd_s, tok_ref):
    i = pl.program_id(0)
    n = n_ref[0]
    TR = xbuf.shape[1]
    w_hbm = (wg_hbm, wu_hbm, wd_hbm)
    w_f32 = (wg_f, wu_f, wd_f)

    def gather(base, slot, r):
        return _row_copy(h2_hbm, tok_ref[base + r], xbuf.at[slot], r, xsem.at[slot])

    def wait_rows(slot):
        def body(r, carry):
            gather(0, slot, r).wait()
            return carry
        lax.fori_loop(0, TR, body, 0, unroll=DMA_UNROLL)

    @pl.when(i == 0)
    def _prologue():
        for c in _expert_weight_copies(ie_ref[0], 0, w_hbm, w_f32, wsem):
            c.start()

        def invert(c, carry):
            p0 = pl.multiple_of(c * INVERT_GROUP, INVERT_GROUP)
            t0 = c * (INVERT_GROUP // TOP_K)
            rows = [pos_ref[p0 + u] for u in range(INVERT_GROUP)]
            for u in range(INVERT_GROUP):
                tok_ref[rows[u]] = t0 + u // TOP_K
            return carry
        lax.fori_loop(0, pos_ref.shape[0] // INVERT_GROUP, invert, 0)

        for ahead in range(N_XBUF - 1):
            base = it_ref[jnp.minimum(ahead, n - 1)] * TR

            def body(r, carry, base=base, ahead=ahead):
                gather(base, ahead, r).start()
                return carry
            lax.fori_loop(0, TR, body, 0, unroll=DMA_UNROLL)

    @pl.when(i < n)
    def _compute():
        slot = i % N_XBUF
        fill = (i + N_XBUF - 1) % N_XBUF
        wslot = ord_ref[i] % 2
        fresh = jnp.logical_or(i == 0, ie_ref[i] != ie_ref[jnp.maximum(i - 1, 0)])

        @pl.when(fresh)
        def _new_expert():
            for c in _expert_weight_copies(ie_ref[i], wslot, w_hbm, w_f32, wsem):
                c.wait()
            wg_s[...] = wg_f[wslot].astype(BF16)
            wu_s[...] = wu_f[wslot].astype(BF16)
            wd_s[...] = wd_f[wslot].astype(BF16)

            @pl.when(nxt_ref[i] >= 0)
            def _prefetch():
                for c in _expert_weight_copies(nxt_ref[i], 1 - wslot, w_hbm, w_f32, wsem):
                    c.start()

        wait_rows(slot)
        x = xbuf[slot].astype(BF16)
        base_next = it_ref[jnp.minimum(i + N_XBUF - 1, n - 1)] * TR
        for r in range(TR):
            gather(base_next, fill, r).start()
        gate = jnp.dot(x, wg_s[...], preferred_element_type=F32)
        up = jnp.dot(x, wu_s[...], preferred_element_type=F32)
        hid = gate * _sigmoid(gate) * up
        y = jnp.dot(hid.astype(BF16), wd_s[...], preferred_element_type=F32)
        rows = it_ref[i] * TR + lax.broadcasted_iota(jnp.int32, (TR, 1), 0)
        mine = (rows >= lo_ref[i]) & (rows < hi_ref[i])

        @pl.when(first_ref[i] == 1)
        def _first():
            y_ref[...] = jnp.where(mine, y, 0.0)

        @pl.when(first_ref[i] == 0)
        def _later():
            y_ref[...] = jnp.where(mine, y, y_ref[...])

        @pl.when(i == n - 1)
        def _drain():
            for ahead in range(1, N_XBUF):
                wait_rows((i + ahead) % N_XBUF)


def _experts(items, pos_flat, h2, e_wg, e_wu, e_wd):
    TR = MOE_ROW_TILE
    n_work = items[0].shape[0]
    n_rows = pos_flat.shape[0]
    assert n_rows % INVERT_GROUP == 0 and INVERT_GROUP % TOP_K == 0
    any_spec = pl.BlockSpec(memory_space=pl.ANY)
    return pl.pallas_call(
        _expert_kernel,
        grid_spec=pltpu.PrefetchScalarGridSpec(
            num_scalar_prefetch=9,
            grid=(n_work,),
            in_specs=[any_spec, any_spec, any_spec, any_spec],
            out_specs=pl.BlockSpec((TR, D_MODEL), lambda i, it, *_: (it[i], 0)),
            scratch_shapes=[pltpu.VMEM((N_XBUF, TR, D_MODEL), F32), pltpu.SemaphoreType.DMA((N_XBUF,)),
                            pltpu.VMEM((2, D_MODEL, EXP_FF), F32), pltpu.VMEM((2, D_MODEL, EXP_FF), F32),
                            pltpu.VMEM((2, EXP_FF, D_MODEL), F32), pltpu.SemaphoreType.DMA((2, 3)),
                            pltpu.VMEM((D_MODEL, EXP_FF), BF16), pltpu.VMEM((D_MODEL, EXP_FF), BF16),
                            pltpu.VMEM((EXP_FF, D_MODEL), BF16), pltpu.SMEM((n_rows,), jnp.int32)],
        ),
        out_shape=jax.ShapeDtypeStruct((n_rows, D_MODEL), F32),
        compiler_params=_cparams(("arbitrary",)),
        name="moe_experts",
    )(*items, pos_flat, h2, e_wg, e_wu, e_wd)


def _moe_schedule(route, counts, n_rows):
    TR = MOE_ROW_TILE
    assert n_rows % TR == 0
    n_work = n_rows // TR + N_EXPERTS - 1
    rt = route[:, :SUBLANES].T
    cnt = counts[0, ROUTE_LANE0:ROUTE_LANE0 + N_EXPERTS].astype(jnp.int32)
    g_end = jnp.cumsum(cnt)
    g_start = g_end - cnt
    onehot = rt[0:2, :, None] == jnp.arange(N_EXPERTS, dtype=F32)
    start_of = jnp.sum(jnp.where(onehot, g_start.astype(F32), 0.0), axis=-1)
    pos_flat = (start_of + rt[4:6]).astype(jnp.int32).T.reshape(-1)
    first_tile = g_start // TR
    n_items_e = jnp.where(cnt > 0, (g_end - 1) // TR - first_tile + 1, 0)
    item_end = jnp.cumsum(n_items_e)
    n_items = item_end[-1:]
    idx = jnp.minimum(jnp.arange(n_work, dtype=jnp.int32), n_items[0] - 1)
    ie = jnp.minimum(jnp.sum((item_end[None, :] <= idx[:, None]).astype(jnp.int32), axis=1), N_EXPERTS - 1)
    it = first_tile[ie] + idx - (item_end - n_items_e)[ie]
    first = jnp.concatenate([jnp.ones((1,), jnp.int32), (it[1:] != it[:-1]).astype(jnp.int32)])
    has = n_items_e > 0
    e_ids = jnp.arange(N_EXPERTS, dtype=jnp.int32)
    ordinal = jnp.cumsum(has.astype(jnp.int32)) - 1
    later = (e_ids[None, :] > e_ids[:, None]) & has[None, :]
    nxt = jnp.min(jnp.where(later, e_ids[None, :], N_EXPERTS), axis=1)
    nxt = jnp.where(nxt == N_EXPERTS, -1, nxt)
    items = (it.astype(jnp.int32), ie, g_start[ie], g_end[ie], first, n_items.astype(jnp.int32),
             ordinal[ie], nxt[ie])
    return pos_flat, items


def _combine_kernel(pos_ref, h2_ref, route_ref, ys_ref, g_ref, b_ref, yp_ref, ysm_ref,
                    buf, sem, *, tm, n_prompt_tiles):
    i = pl.program_id(0)
    last = pl.num_programs(0) - 1
    slot = i % 2

    def fetch(tile, slot, r, k):
        return _row_copy(ys_ref, pos_ref[TOP_K * (tile * tm + r) + k], buf.at[slot, k], r, sem.at[slot])

    def wait_rows(slot):
        def body(r, carry):
            for k in range(TOP_K):
                fetch(0, slot, r, k).wait()
            return carry
        lax.fori_loop(0, tm, body, 0, unroll=DMA_UNROLL)

    @pl.when(i == 0)
    def _prologue():
        def body(r, carry):
            for k in range(TOP_K):
                fetch(0, 0, r, k).start()
            return carry
        lax.fori_loop(0, tm, body, 0, unroll=DMA_UNROLL)

    wait_rows(slot)
    nxt = jnp.minimum(i + 1, last)
    for r in range(tm):
        for k in range(TOP_K):
            fetch(nxt, 1 - slot, r, k).start()
    route = route_ref[...]
    moe = route[:, 2:3] * buf[slot, 0] + route[:, 3:4] * buf[slot, 1]
    y = _layer_norm(ALPHA * h2_ref[...] + moe, g_ref[...], b_ref[...])

    @pl.when(i == last)
    def _drain():
        wait_rows(1 - slot)

    @pl.when(i < n_prompt_tiles)
    def _prompt():
        yp_ref[...] = y

    @pl.when(i >= n_prompt_tiles)
    def _sample():
        ysm_ref[...] = y


def _combine(pos_flat, h2, route, ys, ln_g, ln_b, *, Np, Ns, tm):
    NT = h2.shape[0]
    tm = _tile(math.gcd(Np, Ns), tm)
    npt = Np // tm
    const = lambda i, pos: (0, 0)
    return pl.pallas_call(
        functools.partial(_combine_kernel, tm=tm, n_prompt_tiles=npt),
        grid_spec=pltpu.PrefetchScalarGridSpec(
            num_scalar_prefetch=1,
            grid=(NT // tm,),
            in_specs=[pl.BlockSpec((tm, D_MODEL), lambda i, pos: (i, 0)),
                      pl.BlockSpec((tm, LANES), lambda i, pos: (i, 0)),
                      pl.BlockSpec(memory_space=pl.ANY),
                      pl.BlockSpec((1, D_MODEL), const),
                      pl.BlockSpec((1, D_MODEL), const)],
            out_specs=[pl.BlockSpec((tm, D_MODEL), lambda i, pos: (jnp.minimum(i, npt - 1), 0)),
                       pl.BlockSpec((tm, D_MODEL), lambda i, pos: (jnp.maximum(i - npt, 0), 0))],
            scratch_shapes=[pltpu.VMEM((2, TOP_K, tm, D_MODEL), F32), pltpu.SemaphoreType.DMA((2,))],
        ),
        out_shape=[jax.ShapeDtypeStruct((Np, D_MODEL), F32),
                   jax.ShapeDtypeStruct((Ns, D_MODEL), F32)],
        compiler_params=_cparams(("arbitrary",)),
        name="moe_combine_ln3",
    )(pos_flat, h2, route, ys, ln_g, ln_b)


def kernel(x_prompt, x_sample, mem_prompt, cache_mem_k, cache_mem_v, state_mlstm_C, state_mlstm_n,
           state_mlstm_m, state_mlstm_conv, state_hgrn_S, w_in, b_in, conv_w, mlstm_gn, lb_logits, hgrn_gn,
           w_bm, w_bh, w_out, ln1_g, ln1_b, xa_wq, xa_wk, xa_wv, xa_wo, ln2_g, ln2_b,
           r1_w, r1_b, r2_w, r2_b, e_wg, e_wu, e_wd, ln3_g, ln3_b):
    Bp, Tp, _ = x_prompt.shape
    Bs, Ts, _ = x_sample.shape
    MEM = mem_prompt.shape[1]
    Np, Ns = Bp * Tp, Bs * Ts
    NT = Np + Ns
    Lp_m = math.gcd(Tp, 256)
    Lp_h = math.gcd(Tp, 64)
    Ls = Ts
    assert Ts % SUBLANES == 0 and Np % Ls == 0

    xp2 = x_prompt.reshape(Np, D_MODEL)
    xs2 = x_sample.reshape(Ns, D_MODEL)

    w = w_in[0]
    bi = b_in[0]
    w_t = jnp.swapaxes(w, 0, 1)
    xb_p = xp2.astype(BF16)
    xb_s = xs2.astype(BF16)
    b_main = jnp.concatenate([bi[:GATE_LO], bi[GATE_LO + 2 * M_H:]])[None]
    w_gate = jnp.pad(w[:, GATE_LO:GATE_LO + 2 * M_H], ((0, 0), (0, LANES - 2 * M_H))).astype(BF16)
    b_gate = jnp.pad(bi[GATE_LO:GATE_LO + 2 * M_H], (0, LANES - 2 * M_H))[None]
    lb = jnp.cumsum(jax.nn.softmax(lb_logits.astype(F32), axis=0), axis=0)[0][None]
    wr = jnp.zeros((D_MODEL, LANES), F32)
    wr = wr.at[:, :N_GROUPS].set(r1_w[0])
    wr = wr.at[:, ROUTE_LANE0:ROUTE_LANE0 + N_EXPERTS].set(
        jnp.transpose(r2_w[0], (1, 0, 2)).reshape(D_MODEL, N_EXPERTS)).astype(BF16)
    br = jnp.zeros((1, LANES), F32)
    br = br.at[0, :N_GROUPS].set(r1_b[0])
    br = br.at[0, ROUTE_LANE0:ROUTE_LANE0 + N_EXPERTS].set(r2_b[0].reshape(N_EXPERTS))
    zeros_d = jnp.zeros((1, D_MODEL), F32)

    memb = mem_prompt.reshape(Bp * MEM, D_MODEL).astype(BF16)
    mk = _matmul_bias(memb, xa_wk[0], zeros_d, tm=1024, tn=1024, name="mem_k")
    mv = _matmul_bias(memb, xa_wv[0], zeros_d, tm=1024, tn=1024, name="mem_v")

    u_main = _in_proj(xb_p, xb_s, w_t, b_main, tm=1024, tn=1024)

    padc = lambda c: jnp.pad(c, ((0, 0), (SUBLANES - (CONV_K - 1), 0), (0, 0)))
    zC = jnp.zeros((Bp, M_H, M_DK, M_DV), F32)
    zn = jnp.zeros((Bp, M_H, M_DK), F32)
    zm = jnp.zeros((Bp, 1, M_H), F32)
    zconv = jnp.zeros((Bp, SUBLANES, 2 * M_W), F32)
    zS = jnp.zeros((Bp, HG_H, HG_DK, HG_DV), F32)
    cw = conv_w[0]
    mgn = mlstm_gn[0][None]
    hgn = hgrn_gn[0][None]
    hm_p, C_p, n_p, m_p, conv_p = _mlstm(u_main, xb_p, w_gate, b_gate, zconv, zC, zn, zm, cw, mgn,
                                         B=Bp, T=Tp, L=Lp_m, row0=0)
    hm_s, C_s, n_s, m_s, conv_s = _mlstm(u_main, xb_s, w_gate, b_gate, padc(state_mlstm_conv[0]), state_mlstm_C[0],
                                         state_mlstm_n[0], state_mlstm_m[0][:, None, :], cw, mgn,
                                         B=Bs, T=Ts, L=Ls, row0=Np, NB=SAMPLE_SEQS_PER_STEP)
    og_p, S_p = _hgrn(u_main, zS, lb, hgn, B=Bp, T=Tp, L=Lp_h, row0=0)
    og_s, S_s = _hgrn(u_main, state_hgrn_S[0], lb, hgn, B=Bs, T=Ts, L=Ls, row0=Np, NB=SAMPLE_SEQS_PER_STEP)

    h1, h1b = _merge(hm_p, hm_s, og_p, og_s, u_main, xp2, xs2, w_bm[0].astype(BF16), w_bh[0].astype(BF16),
                     w_out[0].astype(BF16), ln1_g, ln1_b, tm=256)

    q = _matmul_bias(h1b, xa_wq[0], zeros_d, tm=1024, tn=1024, name="xa_q")
    att_p = _attention(q, mk, mv, B=Bp, T=Tp, tq=512, row0=0)
    att_s = _attention_cache(q, cache_mem_k, cache_mem_v, B=Bs, T=Ts, row0=Np)
    h2, route, counts = _oln(att_p, att_s, h1, xa_wo[0].astype(BF16), ln2_g, ln2_b, wr, br, tm=512)

    pos_flat, items = _moe_schedule(route, counts, TOP_K * NT)
    ys_sorted = _experts(items, pos_flat, h2, e_wg[0], e_wu[0], e_wd[0])
    y_p, y_s = _combine(pos_flat, h2, route, ys_sorted, ln3_g, ln3_b, Np=Np, Ns=Ns, tm=256)

    kv5 = lambda a: a.reshape(1, Bp, MEM, XA_H, XA_D)
    return (y_p.reshape(Bp, Tp, D_MODEL), y_s.reshape(Bs, Ts, D_MODEL), kv5(mk), kv5(mv),
            C_p[None], n_p[None], m_p.reshape(1, Bp, M_H), conv_p[None], S_p[None],
            C_s[None], n_s[None], m_s.reshape(1, Bs, M_H), conv_s[None], S_s[None])
```

```python
import functools
import math

import jax
import jax.numpy as jnp
from jax import lax
from jax.experimental import pallas as pl
from jax.experimental.pallas import tpu as pltpu

F32 = jnp.float32
BF16 = jnp.bfloat16

D_MODEL = 2048
M_W = 1024
M_H = 4
M_DK = 256
M_DV = 256
CONV_K = 4
HG_W = 1024
HG_H = 8
HG_DK = 128
HG_DV = 128
XA_H = 4
XA_D = 512
N_GROUPS = 4
EXP_PER_GROUP = 8
N_EXPERTS = 32
TOP_K = 2
EXP_FF = 512
DEPTH = 1
ALPHA = (2 * DEPTH) ** 0.25
LN_EPS = 1e-5

COL_QK, COL_V, COL_O = 0, 2048, 3072
COL_QH, COL_FH, COL_IH, COL_GH = 4096, 5120, 6144, 7168
COL_GM, COL_GHH = 8192, 10240
N_MAIN = 12288
GATE_LO = 4 * M_W
LANES = 128
SUBLANES = 8
ROUTE_LANE0 = N_GROUPS

VMEM_LIMIT = 56 << 20
MOE_ROW_TILE = 256
SAMPLE_SEQS_PER_STEP = 4
N_XBUF = 3
DMA_UNROLL = 8
INVERT_GROUP = 16


def _cparams(sem, vmem=VMEM_LIMIT):
    return pltpu.CompilerParams(dimension_semantics=sem, vmem_limit_bytes=vmem)


def _tile(n, pref):
    t = math.gcd(n, pref)
    assert t % SUBLANES == 0, (n, pref)
    return t


def _bdot(a, b):
    return jnp.dot(a.astype(BF16), b.astype(BF16), preferred_element_type=F32)


def _bdot_nt(a, b):
    return lax.dot_general(a.astype(BF16), b.astype(BF16), (((1,), (1,)), ((), ())),
                           preferred_element_type=F32)


def _bdot_tn(a, b):
    return lax.dot_general(a.astype(BF16), b.astype(BF16), (((0,), (0,)), ((), ())),
                           preferred_element_type=F32)


def _sigmoid(x):
    return 1.0 / (1.0 + jnp.exp(-x))


def _cumsum_rows(x, seg=None):
    n = seg or x.shape[0]
    assert n & (n - 1) == 0
    row = lax.broadcasted_iota(jnp.int32, x.shape, 0) & (n - 1)
    d = 1
    while d < n:
        x = x + jnp.where(row >= d, pltpu.roll(x, d, axis=0), 0.0)
        d *= 2
    return x


def _col_to_row(col, eye):
    return jnp.sum(jnp.where(eye, col, 0.0), axis=0, keepdims=True)


def _row_to_col(row, eye):
    return jnp.sum(jnp.where(eye, row, 0.0), axis=1, keepdims=True)


def _layer_norm(x, g, b):
    mu = jnp.mean(x, axis=-1, keepdims=True)
    xc = x - mu
    var = jnp.mean(xc * xc, axis=-1, keepdims=True)
    return xc * lax.rsqrt(var + LN_EPS) * g + b


def _mm_kernel(x_ref, w_ref, b_ref, o_ref, wb_scr):
    @pl.when(pl.program_id(1) == 0)
    def _new_weight_tile():
        wb_scr[...] = w_ref[...].astype(BF16)

    acc = jnp.dot(x_ref[...], wb_scr[...], preferred_element_type=F32)
    o_ref[...] = (acc + b_ref[...]).astype(o_ref.dtype)


def _matmul_bias(x, w, b, *, tm, tn, out_dtype=F32, name):
    M, K = x.shape
    N = w.shape[1]
    tm = _tile(M, tm)
    tn = _tile(N, tn)
    return pl.pallas_call(
        _mm_kernel,
        grid=(N // tn, M // tm),
        in_specs=[pl.BlockSpec((tm, K), lambda j, i: (i, 0)),
                  pl.BlockSpec((K, tn), lambda j, i: (0, j)),
                  pl.BlockSpec((1, tn), lambda j, i: (0, j))],
        out_specs=pl.BlockSpec((tm, tn), lambda j, i: (i, j)),
        out_shape=jax.ShapeDtypeStruct((M, N), out_dtype),
        scratch_shapes=[pltpu.VMEM((K, tn), BF16)],
        compiler_params=_cparams(("parallel", "arbitrary")),
        name=name,
    )(x, w, b)


def _inproj_kernel(xp_ref, xs_ref, wt_ref, b_ref, u_ref, wb_scr, *, n_prompt_tiles):
    i = pl.program_id(1)

    @pl.when(i == 0)
    def _new_weight_tile():
        wb_scr[...] = wt_ref[...].astype(BF16)

    xb = jnp.where(i < n_prompt_tiles, xp_ref[...], xs_ref[...])
    u_ref[...] = _bdot_nt(xb, wb_scr[...]) + b_ref[...]


def _in_proj(xb_p, xb_s, w_t, b_main, *, tm, tn):
    Np, Ns = xb_p.shape[0], xb_s.shape[0]
    NT = Np + Ns
    K = w_t.shape[1]
    N = w_t.shape[0] - 2 * M_H
    tm = _tile(math.gcd(Np, Ns), tm)
    tn = _tile(math.gcd(GATE_LO, N), tn)
    assert (2 * M_H) % SUBLANES == 0
    npt = Np // tm
    na = GATE_LO // tn
    w_rows = lambda j, i: (pl.multiple_of(j * tn + jnp.where(j < na, 0, 2 * M_H), SUBLANES), 0)
    return pl.pallas_call(
        functools.partial(_inproj_kernel, n_prompt_tiles=npt),
        grid=(N // tn, NT // tm),
        in_specs=[pl.BlockSpec((tm, K), lambda j, i: (jnp.minimum(i, npt - 1), 0)),
                  pl.BlockSpec((tm, K), lambda j, i: (jnp.maximum(i - npt, 0), 0)),
                  pl.BlockSpec((pl.Element(tn), pl.Element(K)), w_rows),
                  pl.BlockSpec((1, tn), lambda j, i: (0, j))],
        out_specs=pl.BlockSpec((tm, tn), lambda j, i: (i, j)),
        out_shape=jax.ShapeDtypeStruct((NT, N), F32),
        scratch_shapes=[pltpu.VMEM((tn, K), BF16)],
        compiler_params=_cparams(("parallel", "arbitrary")),
        name="in_proj",
    )(xb_p, xb_s, w_t, b_main)


def _mlstm_kernel(*refs, L, NC, NB):
    (qk_ref, v_ref, o_ref, xb_ref, wg_ref, bg_ref, conv0_ref, C0_ref, n0_ref, m0_ref, cw_ref, gn_ref,
     h_ref, Co_ref, no_ref, mo_ref, convo_ref, C_scr, n_scr, m_scr, tail_scr) = refs
    c = pl.program_id(1)
    if NC > 1:
        @pl.when(c == 0)
        def _init():
            C_scr[...] = C0_ref[0]
            n_scr[...] = n0_ref[0]
            m_scr[...] = m0_ref[0]
            tail_scr[...] = conv0_ref[0]

    cw = cw_ref[...]
    gn = gn_ref[...]
    g_all = jnp.dot(xb_ref[...], wg_ref[...], preferred_element_type=F32) + bg_ref[...]
    ti = lax.broadcasted_iota(jnp.int32, (L, L), 0)
    si = lax.broadcasted_iota(jnp.int32, (L, L), 1)
    eye = ti == si
    causal = si <= ti
    head_lane = lax.broadcasted_iota(jnp.int32, (1, M_H), 1)
    head_row = lax.broadcasted_iota(jnp.int32, (M_H, M_DK), 0)

    seqs = []
    for nb in range(NB):
        rows = slice(nb * L, (nb + 1) * L)
        if NC == 1:
            sq = dict(C_in=C0_ref.at[nb], C_out=Co_ref.at[nb], n_out=no_ref.at[nb], m_out=mo_ref.at[nb])
            n_all, m_all, tail = n0_ref[nb], m0_ref[nb], conv0_ref[nb]
        else:
            sq = dict(C_in=C_scr, C_out=C_scr, n_out=n_scr, m_out=m_scr)
            n_all, m_all, tail = n_scr[...], m_scr[...], tail_scr[...]
        qk_pre = qk_ref[rows, :]
        ext = jnp.concatenate([tail, qk_pre], axis=0)
        acc = qk_pre * cw[CONV_K - 1:CONV_K, :]
        for j in range(1, CONV_K):
            acc = acc + pltpu.roll(ext, j, axis=0)[SUBLANES:, :] * cw[CONV_K - 1 - j:CONV_K - j, :]
        if NC > 1:
            tail_scr[...] = qk_pre[L - SUBLANES:, :]
        qk = acc * _sigmoid(acc)
        g = g_all[rows, :]
        lf_all = jnp.minimum(g, 0.0) - jnp.log(1.0 + jnp.exp(-jnp.abs(g)))
        F_all = _cumsum_rows(lf_all)
        heads = []
        for h in range(M_H):
            ks = slice(h * M_DK, (h + 1) * M_DK)
            q = qk[:, ks] * (M_DK ** -0.5)
            k = qk[:, M_W + h * M_DK:M_W + (h + 1) * M_DK]
            v = v_ref[rows, ks]
            ig = g[:, h:h + 1]
            F = F_all[:, M_H + h:M_H + h + 1]
            m_prev = m_all[:, h:h + 1]
            r_row = _col_to_row(ig - F, eye)
            Dm = jnp.where(causal, F + r_row, -jnp.inf)
            init_w = F + m_prev
            m_t = jnp.maximum(init_w, jnp.max(Dm, axis=1, keepdims=True))
            P = jnp.exp(Dm - m_t)
            a0 = jnp.exp(init_w - m_t)
            FL = F[L - 1:L, :]
            mL = m_t[L - 1:L, :]
            wL = jnp.exp(FL - F + ig - mL)
            decay = jnp.exp(FL + m_prev - mL)
            heads.append(dict(ks=ks, q=q, k=k, v=v, m_t=m_t, P=P, a0=a0, mL=mL, decay=decay, kw=wL * k))
        sq.update(rows=rows, heads=heads, n_all=n_all, m_all=m_all, qk_pre=qk_pre)
        seqs.append(sq)

    for sq in seqs:
        for h, d in enumerate(sq["heads"]):
            d["S"] = _bdot_nt(d["q"], d["k"])
            d["qC"] = _bdot(d["q"], sq["C_in"][h])
            d["kv"] = _bdot_tn(d["kw"], d["v"])

    for sq in seqs:
        for d in sq["heads"]:
            d["Sc"] = d["S"] * d["P"]
            d["num"] = _bdot(d["Sc"], d["v"])

    for nb, sq in enumerate(seqs):
        rows, n_new, m_new = sq["rows"], sq["n_all"], sq["m_all"]
        for h, d in enumerate(sq["heads"]):
            ks, q, a0, decay = d["ks"], d["q"], d["a0"], d["decay"]
            n_row = sq["n_all"][h:h + 1, :]
            num = d["num"] + a0 * d["qC"]
            den = jnp.sum(d["Sc"], axis=1, keepdims=True) + a0 * jnp.sum(q * n_row, axis=1, keepdims=True)
            hh = num * (1.0 / jnp.maximum(jnp.abs(den), jnp.exp(-d["m_t"])))
            sq["C_out"][h] = decay * sq["C_in"][h] + d["kv"]
            n_new = jnp.where(head_row == h, decay * n_row + jnp.sum(d["kw"], axis=0, keepdims=True), n_new)
            m_new = jnp.where(head_lane == h, d["mL"], m_new)

            hm = _sigmoid(o_ref[rows, ks]) * hh
            hm = hm - jnp.mean(hm, axis=1, keepdims=True)
            hm = hm * lax.rsqrt(jnp.mean(hm * hm, axis=1, keepdims=True) + LN_EPS) * gn[:, ks]
            h_ref[rows, ks] = hm
        sq["n_out"][...] = n_new
        sq["m_out"][...] = m_new
        conv_tail = sq["qk_pre"][L - (CONV_K - 1):, :]
        if NC == 1:
            convo_ref[nb] = conv_tail
        else:
            @pl.when(c == NC - 1)
            def _fin():
                Co_ref[0] = C_scr[...]
                no_ref[0] = n_new
                mo_ref[0] = m_new
                convo_ref[0] = conv_tail


def _mlstm(u_main, xb, w_gate, b_gate, conv0p, C0, n0, m0, conv_w, gn, *, B, T, L, row0, NB=1):
    NC = T // L
    assert NB == 1 or NC == 1
    assert B % NB == 0 and row0 % (NB * L) == 0
    R = NB * L
    rb0 = row0 // R
    tok = lambda b, c: rb0 + b * NC + c
    in_specs = [
        pl.BlockSpec((R, 2 * M_W), lambda b, c: (tok(b, c), COL_QK // (2 * M_W))),
        pl.BlockSpec((R, M_W), lambda b, c: (tok(b, c), COL_V // M_W)),
        pl.BlockSpec((R, M_W), lambda b, c: (tok(b, c), COL_O // M_W)),
        pl.BlockSpec((R, D_MODEL), lambda b, c: (b * NC + c, 0)),
        pl.BlockSpec((D_MODEL, LANES), lambda b, c: (0, 0)),
        pl.BlockSpec((1, LANES), lambda b, c: (0, 0)),
        pl.BlockSpec((NB, SUBLANES, 2 * M_W), lambda b, c: (b, 0, 0)),
        pl.BlockSpec((NB, M_H, M_DK, M_DV), lambda b, c: (b, 0, 0, 0)),
        pl.BlockSpec((NB, M_H, M_DK), lambda b, c: (b, 0, 0)),
        pl.BlockSpec((NB, 1, M_H), lambda b, c: (b, 0, 0)),
        pl.BlockSpec((CONV_K, 2 * M_W), lambda b, c: (0, 0)),
        pl.BlockSpec((1, M_W), lambda b, c: (0, 0)),
    ]
    args = [u_main, u_main, u_main, xb, w_gate, b_gate, conv0p, C0, n0, m0, conv_w, gn]
    out_specs = [
        pl.BlockSpec((R, M_W), lambda b, c: (b * NC + c, 0)),
        pl.BlockSpec((NB, M_H, M_DK, M_DV), lambda b, c: (b, 0, 0, 0)),
        pl.BlockSpec((NB, M_H, M_DK), lambda b, c: (b, 0, 0)),
        pl.BlockSpec((NB, 1, M_H), lambda b, c: (b, 0, 0)),
        pl.BlockSpec((NB, CONV_K - 1, 2 * M_W), lambda b, c: (b, 0, 0)),
    ]
    out_shape = [
        jax.ShapeDtypeStruct((B * T, M_W), F32),
        jax.ShapeDtypeStruct((B, M_H, M_DK, M_DV), F32),
        jax.ShapeDtypeStruct((B, M_H, M_DK), F32),
        jax.ShapeDtypeStruct((B, 1, M_H), F32),
        jax.ShapeDtypeStruct((B, CONV_K - 1, 2 * M_W), F32),
    ]
    return pl.pallas_call(
        functools.partial(_mlstm_kernel, L=L, NC=NC, NB=NB),
        grid=(B // NB, NC),
        in_specs=in_specs,
        out_specs=out_specs,
        out_shape=out_shape,
        scratch_shapes=[pltpu.VMEM((M_H, M_DK, M_DV), F32), pltpu.VMEM((M_H, M_DK), F32),
                        pltpu.VMEM((1, M_H), F32), pltpu.VMEM((SUBLANES, 2 * M_W), F32)],
        compiler_params=_cparams(("parallel", "arbitrary")),
        name=f"mlstm_L{L}",
    )(*args)


def _hgrn_kernel(*refs, L, NC, NB):
    (q_ref, f_ref, i_ref, g_ref, S0_ref, lb_ref, gn_ref, o_ref, So_ref, S_scr) = refs
    c = pl.program_id(1)
    if NC > 1:
        @pl.when(c == 0)
        def _init():
            S_scr[...] = S0_ref[0]

    lb = lb_ref[...]
    f = lb + (1.0 - lb) * _sigmoid(f_ref[...])
    kk = 1.0 - f
    b = _cumsum_rows(jnp.log(f), seg=L)
    qh = q_ref[...]
    q = qh * _sigmoid(qh)
    v = i_ref[...]
    gh = g_ref[...]
    gsilu = gh * _sigmoid(gh)
    gn = gn_ref[...]
    ti = lax.broadcasted_iota(jnp.int32, (L, L), 0)
    si = lax.broadcasted_iota(jnp.int32, (L, L), 1)
    causal = si <= ti
    ci = lax.broadcasted_iota(jnp.int32, (HG_DK, HG_DK), 0)
    cj = lax.broadcasted_iota(jnp.int32, (HG_DK, HG_DK), 1)
    eye = ci == cj
    mid = max(L // 2 - 1, 0)
    hsl = [slice(h * HG_DK, (h + 1) * HG_DK) for h in range(HG_H)]

    seqs = []
    for nb in range(NB):
        rows = slice(nb * L, (nb + 1) * L)
        bs, qs, ks = b[rows], q[rows], kk[rows]
        bL = bs[L - 1:L, :]
        bm = bs[mid:mid + 1, :]
        S_in, S_out = (S0_ref.at[nb], So_ref.at[nb]) if NC == 1 else (S_scr, S_scr)
        seqs.append(dict(rows=rows, bL=bL, v=v[rows], q_in=qs * jnp.exp(bs), q_t=qs * jnp.exp(bs - bm),
                         k_t=ks * jnp.exp(bm - bs), k_st=ks * jnp.exp(bL - bs), S_in=S_in, S_out=S_out))

    for sq in seqs:
        sq["A"] = [jnp.where(causal, _bdot_nt(sq["q_t"][:, hs], sq["k_t"][:, hs]), 0.0) for hs in hsl]
        sq["qS"] = [_bdot(sq["q_in"][:, hs], sq["S_in"][h]) for h, hs in enumerate(hsl)]
        sq["kv"] = [_bdot_tn(sq["k_st"][:, hs], sq["v"][:, hs]) for hs in hsl]
    for sq in seqs:
        sq["o_intra"] = [_bdot(sq["A"][h], sq["v"][:, hs]) for h, hs in enumerate(hsl)]
    for sq in seqs:
        rows = sq["rows"]
        for h, hs in enumerate(hsl):
            o = sq["o_intra"][h] + sq["qS"][h]
            dec = jnp.exp(_row_to_col(sq["bL"][:, hs], eye))
            sq["S_out"][h] = dec * sq["S_in"][h] + sq["kv"][h]
            o = o * lax.rsqrt(jnp.mean(o * o, axis=1, keepdims=True) + LN_EPS) * gn[:, hs]
            o_ref[rows, hs] = o * gsilu[rows, hs]

    if NC > 1:
        @pl.when(c == NC - 1)
        def _fin():
            So_ref[0] = S_scr[...]


def _hgrn(u_main, S0, lb, gn, *, B, T, L, row0, NB=1):
    NC = T // L
    assert NB == 1 or NC == 1
    assert B % NB == 0 and row0 % (NB * L) == 0
    R = NB * L
    rb0 = row0 // R
    tok = lambda b, c: rb0 + b * NC + c
    col = lambda off: (lambda b, c: (tok(b, c), off // HG_W))
    in_specs = [
        pl.BlockSpec((R, HG_W), col(COL_QH)),
        pl.BlockSpec((R, HG_W), col(COL_FH)),
        pl.BlockSpec((R, HG_W), col(COL_IH)),
        pl.BlockSpec((R, HG_W), col(COL_GH)),
        pl.BlockSpec((NB, HG_H, HG_DK, HG_DV), lambda b, c: (b, 0, 0, 0)),
        pl.BlockSpec((1, HG_W), lambda b, c: (0, 0)),
        pl.BlockSpec((1, HG_W), lambda b, c: (0, 0)),
    ]
    args = [u_main, u_main, u_main, u_main, S0, lb, gn]
    return pl.pallas_call(
        functools.partial(_hgrn_kernel, L=L, NC=NC, NB=NB),
        grid=(B // NB, NC),
        in_specs=in_specs,
        out_specs=[pl.BlockSpec((R, HG_W), lambda b, c: (b * NC + c, 0)),
                   pl.BlockSpec((NB, HG_H, HG_DK, HG_DV), lambda b, c: (b, 0, 0, 0))],
        out_shape=[jax.ShapeDtypeStruct((B * T, HG_W), F32),
                   jax.ShapeDtypeStruct((B, HG_H, HG_DK, HG_DV), F32)],
        scratch_shapes=[pltpu.VMEM((HG_H, HG_DK, HG_DV), F32)],
        compiler_params=_cparams(("parallel", "arbitrary")),
        name=f"hgrn_L{L}",
    )(*args)


def _merge_kernel(hmp_ref, hms_ref, ogp_ref, ogs_ref, gm_ref, gh_ref, xp_ref, xs_ref, wbm_ref, wbh_ref,
                  wo_ref, g_ref, b_ref, h1_ref, h1b_ref, *, n_prompt_tiles):
    is_prompt = pl.program_id(0) < n_prompt_tiles
    a = _bdot(jnp.where(is_prompt, hmp_ref[...], hms_ref[...]), wbm_ref[...])
    bb = _bdot(jnp.where(is_prompt, ogp_ref[...], ogs_ref[...]), wbh_ref[...])
    merged = _sigmoid(gm_ref[...]) * a + _sigmoid(gh_ref[...]) * bb
    mix = _bdot(merged, wo_ref[...])
    x = jnp.where(is_prompt, xp_ref[...], xs_ref[...])
    h1 = _layer_norm(ALPHA * x + mix, g_ref[...], b_ref[...])
    h1_ref[...] = h1
    h1b_ref[...] = h1.astype(BF16)


def _merge(hm_p, hm_s, og_p, og_s, u_main, x_p, x_s, w_bm, w_bh, w_out, ln_g, ln_b, *, tm):
    Np, Ns = x_p.shape[0], x_s.shape[0]
    NT = Np + Ns
    tm = _tile(math.gcd(Np, Ns), tm)
    npt = Np // tm
    const = lambda i: (0, 0)
    prompt_rows = lambda i: (jnp.minimum(i, npt - 1), 0)
    sample_rows = lambda i: (jnp.maximum(i - npt, 0), 0)
    return pl.pallas_call(
        functools.partial(_merge_kernel, n_prompt_tiles=npt),
        grid=(NT // tm,),
        in_specs=[
            pl.BlockSpec((tm, M_W), prompt_rows),
            pl.BlockSpec((tm, M_W), sample_rows),
            pl.BlockSpec((tm, HG_W), prompt_rows),
            pl.BlockSpec((tm, HG_W), sample_rows),
            pl.BlockSpec((tm, D_MODEL), lambda i: (i, COL_GM // D_MODEL)),
            pl.BlockSpec((tm, D_MODEL), lambda i: (i, COL_GHH // D_MODEL)),
            pl.BlockSpec((tm, D_MODEL), prompt_rows),
            pl.BlockSpec((tm, D_MODEL), sample_rows),
            pl.BlockSpec((M_W, D_MODEL), const, pipeline_mode=pl.Buffered(1)),
            pl.BlockSpec((HG_W, D_MODEL), const, pipeline_mode=pl.Buffered(1)),
            pl.BlockSpec((D_MODEL, D_MODEL), const, pipeline_mode=pl.Buffered(1)),
            pl.BlockSpec((1, D_MODEL), const),
            pl.BlockSpec((1, D_MODEL), const),
        ],
        out_specs=[pl.BlockSpec((tm, D_MODEL), lambda i: (i, 0)),
                   pl.BlockSpec((tm, D_MODEL), lambda i: (i, 0))],
        out_shape=[jax.ShapeDtypeStruct((NT, D_MODEL), F32),
                   jax.ShapeDtypeStruct((NT, D_MODEL), BF16)],
        compiler_params=_cparams(("parallel",)),
        name="merge_ln1",
    )(hm_p, hm_s, og_p, og_s, u_main, u_main, x_p, x_s, w_bm, w_bh, w_out, ln_g, ln_b)


def _attn_kernel(q_ref, *refs):
    k_refs, v_refs, o_ref = refs[:XA_H], refs[XA_H:2 * XA_H], refs[2 * XA_H]
    for h in range(XA_H):
        sl = slice(h * XA_D, (h + 1) * XA_D)
        s = _bdot_nt(q_ref[:, sl], k_refs[h][...]) * (XA_D ** -0.5)
        e = jnp.exp(s - jnp.max(s, axis=1, keepdims=True))
        p = e / jnp.sum(e, axis=1, keepdims=True)
        o_ref[:, sl] = _bdot(p, v_refs[h][...])


def _attn_cache_kernel(q_ref, k_ref, v_ref, o_ref):
    T = q_ref.shape[0]
    M = k_ref.shape[2]
    k_all = k_ref[0, 0].reshape(M * XA_H, XA_D)
    v_all = v_ref[0, 0].reshape(M * XA_H, XA_D)
    q_all = jnp.concatenate([q_ref[:, h * XA_D:(h + 1) * XA_D] for h in range(XA_H)], axis=0)
    s = _bdot_nt(q_all, k_all) * (XA_D ** -0.5)
    q_head = lax.broadcasted_iota(jnp.int32, s.shape, 0) // T
    k_head = lax.broadcasted_iota(jnp.int32, s.shape, 1) % XA_H
    s = jnp.where(q_head == k_head, s, -jnp.inf)
    e = jnp.exp(s - jnp.max(s, axis=1, keepdims=True))
    p = e / jnp.sum(e, axis=1, keepdims=True)
    o = _bdot(p, v_all)
    for h in range(XA_H):
        o_ref[:, h * XA_D:(h + 1) * XA_D] = o[h * T:(h + 1) * T]


def _attention_cache(q, cache_k, cache_v, *, B, T, row0):
    M = cache_k.shape[2]
    rb0 = row0 // T
    kv_spec = pl.BlockSpec((1, 1, M, XA_H, XA_D), lambda b: (0, b, 0, 0, 0))
    return pl.pallas_call(
        _attn_cache_kernel,
        grid=(B,),
        in_specs=[pl.BlockSpec((T, D_MODEL), lambda b: (rb0 + b, 0)), kv_spec, kv_spec],
        out_specs=pl.BlockSpec((T, D_MODEL), lambda b: (b, 0)),
        out_shape=jax.ShapeDtypeStruct((B * T, D_MODEL), F32),
        compiler_params=_cparams(("parallel",)),
        name="xattn_cache",
    )(q, cache_k, cache_v)


def _attention(q, mem_k, mem_v, *, B, T, tq, row0):
    tq = _tile(T, tq)
    nq = T // tq
    rb0 = row0 // tq
    M = mem_k.shape[0] // B
    kv_specs = [pl.BlockSpec((M, XA_D), lambda b, t, h=h: (b, h)) for h in range(XA_H)]
    return pl.pallas_call(
        _attn_kernel,
        grid=(B, nq),
        in_specs=[pl.BlockSpec((tq, D_MODEL), lambda b, t: (rb0 + b * nq + t, 0))] + kv_specs + kv_specs,
        out_specs=pl.BlockSpec((tq, D_MODEL), lambda b, t: (b * nq + t, 0)),
        out_shape=jax.ShapeDtypeStruct((B * T, D_MODEL), F32),
        compiler_params=_cparams(("parallel", "parallel")),
        name=f"xattn_T{T}",
    )(q, *([mem_k] * XA_H), *([mem_v] * XA_H))


def _oln_kernel(op_ref, os_ref, h1_ref, wo_ref, g_ref, b_ref, wr_ref, br_ref,
                h2_ref, route_ref, cnt_ref, carry_scr, *, n_prompt_tiles, n_sub):
    i = pl.program_id(0)

    @pl.when(i == 0)
    def _init():
        carry_scr[...] = jnp.zeros_like(carry_scr)

    ts = h1_ref.shape[0] // n_sub
    lane = lax.broadcasted_iota(jnp.int32, (ts, LANES), 1)
    lane_f = lane.astype(F32)
    neg = -jnp.inf
    ti = lax.broadcasted_iota(jnp.int32, (ts, ts), 0)
    si = lax.broadcasted_iota(jnp.int32, (ts, ts), 1)
    tri = jnp.where(si <= ti, 1.0, 0.0).astype(BF16)
    is_prompt = i < n_prompt_tiles

    def first_argmax(vals):
        mx = jnp.max(vals, axis=1, keepdims=True)
        idx = jnp.min(jnp.where(vals == mx, lane_f, float(LANES)), axis=1, keepdims=True)
        return mx, idx.astype(jnp.int32)

    carry = carry_scr[...]
    for sb in range(n_sub):
        rows = slice(sb * ts, (sb + 1) * ts)
        att = jnp.where(is_prompt, op_ref[rows, :], os_ref[rows, :])
        h2 = _layer_norm(ALPHA * h1_ref[rows, :] + _bdot(att, wo_ref[...]), g_ref[...], b_ref[...])
        h2_ref[rows, :] = h2

        logits = _bdot(h2, wr_ref[...]) + br_ref[...]
        gl = jnp.where(lane < N_GROUPS, logits, neg)
        gmax, grp = first_argmax(gl)
        p_grp = 1.0 / jnp.sum(jnp.exp(gl - gmax), axis=1, keepdims=True)
        lo = ROUTE_LANE0 + grp * EXP_PER_GROUP
        el = jnp.where((lane >= lo) & (lane < lo + EXP_PER_GROUP), logits, neg)
        v1, i1 = first_argmax(el)
        v2, i2 = first_argmax(jnp.where(lane == i1, neg, el))
        e21 = jnp.exp(v2 - v1)
        w1 = p_grp / (1.0 + e21)
        w2 = p_grp * e21 / (1.0 + e21)

        pick1 = lane == i1
        pick2 = lane == i2
        onehot = jnp.where(pick1 | pick2, 1.0, 0.0)
        cnt = _bdot(tri, onehot) + carry
        r1 = jnp.sum(jnp.where(pick1, cnt, 0.0), axis=1, keepdims=True) - 1.0
        r2 = jnp.sum(jnp.where(pick2, cnt, 0.0), axis=1, keepdims=True) - 1.0
        carry = cnt[ts - 1:ts, :]

        e1 = (i1 - ROUTE_LANE0).astype(F32)
        e2 = (i2 - ROUTE_LANE0).astype(F32)
        packed = jnp.zeros_like(logits)
        for idx, val in enumerate((e1, e2, w1, w2, r1, r2)):
            packed = jnp.where(lane == idx, val, packed)
        route_ref[rows, :] = packed

    carry_scr[...] = carry
    cnt_ref[...] = jnp.broadcast_to(carry, cnt_ref.shape)


def _oln(o_p, o_s, h1, wo, ln_g, ln_b, wr, br, *, tm):
    Np, Ns = o_p.shape[0], o_s.shape[0]
    NT = Np + Ns
    tm = _tile(math.gcd(Np, Ns), tm)
    npt = Np // tm
    const = lambda i: (0, 0)
    return pl.pallas_call(
        functools.partial(_oln_kernel, n_prompt_tiles=npt, n_sub=2 if tm % (2 * SUBLANES) == 0 else 1),
        grid=(NT // tm,),
        in_specs=[pl.BlockSpec((tm, D_MODEL), lambda i: (jnp.minimum(i, npt - 1), 0)),
                  pl.BlockSpec((tm, D_MODEL), lambda i: (jnp.maximum(i - npt, 0), 0)),
                  pl.BlockSpec((tm, D_MODEL), lambda i: (i, 0)),
                  pl.BlockSpec((D_MODEL, D_MODEL), const, pipeline_mode=pl.Buffered(1)),
                  pl.BlockSpec((1, D_MODEL), const),
                  pl.BlockSpec((1, D_MODEL), const),
                  pl.BlockSpec((D_MODEL, LANES), const),
                  pl.BlockSpec((1, LANES), const)],
        out_specs=[pl.BlockSpec((tm, D_MODEL), lambda i: (i, 0)),
                   pl.BlockSpec((tm, LANES), lambda i: (i, 0)),
                   pl.BlockSpec((SUBLANES, LANES), const)],
        out_shape=[jax.ShapeDtypeStruct((NT, D_MODEL), F32),
                   jax.ShapeDtypeStruct((NT, LANES), F32),
                   jax.ShapeDtypeStruct((SUBLANES, LANES), F32)],
        scratch_shapes=[pltpu.VMEM((1, LANES), F32)],
        compiler_params=_cparams(("arbitrary",)),
        name="oproj_ln2_router",
    )(o_p, o_s, h1, wo, ln_g, ln_b, wr, br)


def _row_copy(src_ref, src_row, dst_ref, dst_row, sem):
    return pltpu.make_async_copy(src_ref.at[pl.ds(src_row, 1), :], dst_ref.at[pl.ds(dst_row, 1), :], sem)


def _expert_weight_copies(e, slot, w_hbm, w_f32, sem):
    return [pltpu.make_async_copy(src.at[e], dst.at[slot], sem.at[slot, j])
            for j, (src, dst) in enumerate(zip(w_hbm, w_f32))]


def _expert_kernel(it_ref, ie_ref, lo_ref, hi_ref, first_ref, n_ref, ord_ref, nxt_ref, pos_ref,
                   h2_hbm, wg_hbm, wu_hbm, wd_hbm, y_ref,
                   xbuf, xsem, wg_f, wu_f, wd_f, wsem, wg_s, wu_s, wd_s, tok_ref):
    i = pl.program_id(0)
    n = n_ref[0]
    TR = xbuf.shape[1]
    w_hbm = (wg_hbm, wu_hbm, wd_hbm)
    w_f32 = (wg_f, wu_f, wd_f)

    def gather(base, slot, r):
        return _row_copy(h2_hbm, tok_ref[base + r], xbuf.at[slot], r, xsem.at[slot])

    def wait_rows(slot):
        def body(r, carry):
            gather(0, slot, r).wait()
            return carry
        lax.fori_loop(0, TR, body, 0, unroll=DMA_UNROLL)

    @pl.when(i == 0)
    def _prologue():
        for c in _expert_weight_copies(ie_ref[0], 0, w_hbm, w_f32, wsem):
            c.start()

        def invert(c, carry):
            p0 = pl.multiple_of(c * INVERT_GROUP, INVERT_GROUP)
            t0 = c * (INVERT_GROUP // TOP_K)
            rows = [pos_ref[p0 + u] for u in range(INVERT_GROUP)]
            for u in range(INVERT_GROUP):
                tok_ref[rows[u]] = t0 + u // TOP_K
            return carry
        lax.fori_loop(0, pos_ref.shape[0] // INVERT_GROUP, invert, 0)

        for ahead in range(N_XBUF - 1):
            base = it_ref[jnp.minimum(ahead, n - 1)] * TR

            def body(r, carry, base=base, ahead=ahead):
                gather(base, ahead, r).start()
                return carry
            lax.fori_loop(0, TR, body, 0, unroll=DMA_UNROLL)

    @pl.when(i < n)
    def _compute():
        slot = i % N_XBUF
        fill = (i + N_XBUF - 1) % N_XBUF
        wslot = ord_ref[i] % 2
        fresh = jnp.logical_or(i == 0, ie_ref[i] != ie_ref[jnp.maximum(i - 1, 0)])

        @pl.when(fresh)
        def _new_expert():
            for c in _expert_weight_copies(ie_ref[i], wslot, w_hbm, w_f32, wsem):
                c.wait()
            wg_s[...] = wg_f[wslot].astype(BF16)
            wu_s[...] = wu_f[wslot].astype(BF16)
            wd_s[...] = wd_f[wslot].astype(BF16)

            @pl.when(nxt_ref[i] >= 0)
            def _prefetch():
                for c in _expert_weight_copies(nxt_ref[i], 1 - wslot, w_hbm, w_f32, wsem):
                    c.start()

        wait_rows(slot)
        x = xbuf[slot].astype(BF16)
        base_next = it_ref[jnp.minimum(i + N_XBUF - 1, n - 1)] * TR
        for r in range(TR):
            gather(base_next, fill, r).start()
        gate = jnp.dot(x, wg_s[...], preferred_element_type=F32)
        up = jnp.dot(x, wu_s[...], preferred_element_type=F32)
        hid = gate * _sigmoid(gate) * up
        y = jnp.dot(hid.astype(BF16), wd_s[...], preferred_element_type=F32)
        rows = it_ref[i] * TR + lax.broadcasted_iota(jnp.int32, (TR, 1), 0)
        mine = (rows >= lo_ref[i]) & (rows < hi_ref[i])

        @pl.when(first_ref[i] == 1)
        def _first():
            y_ref[...] = jnp.where(mine, y, 0.0)

        @pl.when(first_ref[i] == 0)
        def _later():
            y_ref[...] = jnp.where(mine, y, y_ref[...])

        @pl.when(i == n - 1)
        def _drain():
            for ahead in range(1, N_XBUF):
                wait_rows((i + ahead) % N_XBUF)


def _experts(items, pos_flat, h2, e_wg, e_wu, e_wd):
    TR = MOE_ROW_TILE
    n_work = items[0].shape[0]
    n_rows = pos_flat.shape[0]
    assert n_rows % INVERT_GROUP == 0 and INVERT_GROUP % TOP_K == 0
    any_spec = pl.BlockSpec(memory_space=pl.ANY)
    return pl.pallas_call(
        _expert_kernel,
        grid_spec=pltpu.PrefetchScalarGridSpec(
            num_scalar_prefetch=9,
            grid=(n_work,),
            in_specs=[any_spec, any_spec, any_spec, any_spec],
            out_specs=pl.BlockSpec((TR, D_MODEL), lambda i, it, *_: (it[i], 0)),
            scratch_shapes=[pltpu.VMEM((N_XBUF, TR, D_MODEL), F32), pltpu.SemaphoreType.DMA((N_XBUF,)),
                            pltpu.VMEM((2, D_MODEL, EXP_FF), F32), pltpu.VMEM((2, D_MODEL, EXP_FF), F32),
                            pltpu.VMEM((2, EXP_FF, D_MODEL), F32), pltpu.SemaphoreType.DMA((2, 3)),
                            pltpu.VMEM((D_MODEL, EXP_FF), BF16), pltpu.VMEM((D_MODEL, EXP_FF), BF16),
                            pltpu.VMEM((EXP_FF, D_MODEL), BF16), pltpu.SMEM((n_rows,), jnp.int32)],
        ),
        out_shape=jax.ShapeDtypeStruct((n_rows, D_MODEL), F32),
        compiler_params=_cparams(("arbitrary",)),
        name="moe_experts",
    )(*items, pos_flat, h2, e_wg, e_wu, e_wd)


def _moe_schedule(route, counts, n_rows):
    TR = MOE_ROW_TILE
    assert n_rows % TR == 0
    n_work = n_rows // TR + N_EXPERTS - 1
    rt = route[:, :SUBLANES].T
    cnt = counts[0, ROUTE_LANE0:ROUTE_LANE0 + N_EXPERTS].astype(jnp.int32)
    g_end = jnp.cumsum(cnt)
    g_start = g_end - cnt
    onehot = rt[0:2, :, None] == jnp.arange(N_EXPERTS, dtype=F32)
    start_of = jnp.sum(jnp.where(onehot, g_start.astype(F32), 0.0), axis=-1)
    pos_flat = (start_of + rt[4:6]).astype(jnp.int32).T.reshape(-1)
    first_tile = g_start // TR
    n_items_e = jnp.where(cnt > 0, (g_end - 1) // TR - first_tile + 1, 0)
    item_end = jnp.cumsum(n_items_e)
    n_items = item_end[-1:]
    idx = jnp.minimum(jnp.arange(n_work, dtype=jnp.int32), n_items[0] - 1)
    ie = jnp.minimum(jnp.sum((item_end[None, :] <= idx[:, None]).astype(jnp.int32), axis=1), N_EXPERTS - 1)
    it = first_tile[ie] + idx - (item_end - n_items_e)[ie]
    first = jnp.concatenate([jnp.ones((1,), jnp.int32), (it[1:] != it[:-1]).astype(jnp.int32)])
    has = n_items_e > 0
    e_ids = jnp.arange(N_EXPERTS, dtype=jnp.int32)
    ordinal = jnp.cumsum(has.astype(jnp.int32)) - 1
    later = (e_ids[None, :] > e_ids[:, None]) & has[None, :]
    nxt = jnp.min(jnp.where(later, e_ids[None, :], N_EXPERTS), axis=1)
    nxt = jnp.where(nxt == N_EXPERTS, -1, nxt)
    items = (it.astype(jnp.int32), ie, g_start[ie], g_end[ie], first, n_items.astype(jnp.int32),
             ordinal[ie], nxt[ie])
    return pos_flat, items


def _combine_kernel(pos_ref, h2_ref, route_ref, ys_ref, g_ref, b_ref, yp_ref, ysm_ref,
                    buf, sem, *, tm, n_prompt_tiles):
    i = pl.program_id(0)
    last = pl.num_programs(0) - 1
    slot = i % 2

    def fetch(tile, slot, r, k):
        return _row_copy(ys_ref, pos_ref[TOP_K * (tile * tm + r) + k], buf.at[slot, k], r, sem.at[slot])

    def wait_rows(slot):
        def body(r, carry):
            for k in range(TOP_K):
                fetch(0, slot, r, k).wait()
            return carry
        lax.fori_loop(0, tm, body, 0, unroll=DMA_UNROLL)

    @pl.when(i == 0)
    def _prologue():
        def body(r, carry):
            for k in range(TOP_K):
                fetch(0, 0, r, k).start()
            return carry
        lax.fori_loop(0, tm, body, 0, unroll=DMA_UNROLL)

    wait_rows(slot)
    nxt = jnp.minimum(i + 1, last)
    for r in range(tm):
        for k in range(TOP_K):
            fetch(nxt, 1 - slot, r, k).start()
    route = route_ref[...]
    moe = route[:, 2:3] * buf[slot, 0] + route[:, 3:4] * buf[slot, 1]
    y = _layer_norm(ALPHA * h2_ref[...] + moe, g_ref[...], b_ref[...])

    @pl.when(i == last)
    def _drain():
        wait_rows(1 - slot)

    @pl.when(i < n_prompt_tiles)
    def _prompt():
        yp_ref[...] = y

    @pl.when(i >= n_prompt_tiles)
    def _sample():
        ysm_ref[...] = y


def _combine(pos_flat, h2, route, ys, ln_g, ln_b, *, Np, Ns, tm):
    NT = h2.shape[0]
    tm = _tile(math.gcd(Np, Ns), tm)
    npt = Np // tm
    const = lambda i, pos: (0, 0)
    return pl.pallas_call(
        functools.partial(_combine_kernel, tm=tm, n_prompt_tiles=npt),
        grid_spec=pltpu.PrefetchScalarGridSpec(
            num_scalar_prefetch=1,
            grid=(NT // tm,),
            in_specs=[pl.BlockSpec((tm, D_MODEL), lambda i, pos: (i, 0)),
                      pl.BlockSpec((tm, LANES), lambda i, pos: (i, 0)),
                      pl.BlockSpec(memory_space=pl.ANY),
                      pl.BlockSpec((1, D_MODEL), const),
                      pl.BlockSpec((1, D_MODEL), const)],
            out_specs=[pl.BlockSpec((tm, D_MODEL), lambda i, pos: (jnp.minimum(i, npt - 1), 0)),
                       pl.BlockSpec((tm, D_MODEL), lambda i, pos: (jnp.maximum(i - npt, 0), 0))],
            scratch_shapes=[pltpu.VMEM((2, TOP_K, tm, D_MODEL), F32), pltpu.SemaphoreType.DMA((2,))],
        ),
        out_shape=[jax.ShapeDtypeStruct((Np, D_MODEL), F32),
                   jax.ShapeDtypeStruct((Ns, D_MODEL), F32)],
        compiler_params=_cparams(("arbitrary",)),
        name="moe_combine_ln3",
    )(pos_flat, h2, route, ys, ln_g, ln_b)


def kernel(x_prompt, x_sample, mem_prompt, cache_mem_k, cache_mem_v, state_mlstm_C, state_mlstm_n,
           state_mlstm_m, state_mlstm_conv, state_hgrn_S, w_in, b_in, conv_w, mlstm_gn, lb_logits, hgrn_gn,
           w_bm, w_bh, w_out, ln1_g, ln1_b, xa_wq, xa_wk, xa_wv, xa_wo, ln2_g, ln2_b,
           r1_w, r1_b, r2_w, r2_b, e_wg, e_wu, e_wd, ln3_g, ln3_b):
    Bp, Tp, _ = x_prompt.shape
    Bs, Ts, _ = x_sample.shape
    MEM = mem_prompt.shape[1]
    Np, Ns = Bp * Tp, Bs * Ts
    NT = Np + Ns
    Lp_m = math.gcd(Tp, 256)
    Lp_h = math.gcd(Tp, 64)
    Ls = Ts
    assert Ts % SUBLANES == 0 and Np % Ls == 0

    xp2 = x_prompt.reshape(Np, D_MODEL)
    xs2 = x_sample.reshape(Ns, D_MODEL)

    w = w_in[0]
    bi = b_in[0]
    w_t = jnp.swapaxes(w, 0, 1)
    xb_p = xp2.astype(BF16)
    xb_s = xs2.astype(BF16)
    b_main = jnp.concatenate([bi[:GATE_LO], bi[GATE_LO + 2 * M_H:]])[None]
    w_gate_t = lax.slice(w_t, (GATE_LO, 0), (GATE_LO + 2 * M_H, D_MODEL))
    w_gate = jnp.pad(w_gate_t.T.astype(BF16), ((0, 0), (0, LANES - 2 * M_H)))
    b_gate = jnp.pad(bi[GATE_LO:GATE_LO + 2 * M_H], (0, LANES - 2 * M_H))[None]
    lb = jnp.cumsum(jax.nn.softmax(lb_logits.astype(F32), axis=0), axis=0)[0][None]
    wr = jnp.zeros((D_MODEL, LANES), F32)
    wr = wr.at[:, :N_GROUPS].set(r1_w[0])
    wr = wr.at[:, ROUTE_LANE0:ROUTE_LANE0 + N_EXPERTS].set(
        jnp.transpose(r2_w[0], (1, 0, 2)).reshape(D_MODEL, N_EXPERTS)).astype(BF16)
    br = jnp.zeros((1, LANES), F32)
    br = br.at[0, :N_GROUPS].set(r1_b[0])
    br = br.at[0, ROUTE_LANE0:ROUTE_LANE0 + N_EXPERTS].set(r2_b[0].reshape(N_EXPERTS))
    zeros_d = jnp.zeros((1, D_MODEL), F32)

    memb = mem_prompt.reshape(Bp * MEM, D_MODEL).astype(BF16)
    mk = _matmul_bias(memb, xa_wk[0], zeros_d, tm=1024, tn=1024, name="mem_k")
    mv = _matmul_bias(memb, xa_wv[0], zeros_d, tm=1024, tn=1024, name="mem_v")

    u_main = _in_proj(xb_p, xb_s, w_t, b_main, tm=1024, tn=1024)

    padc = lambda c: jnp.pad(c, ((0, 0), (SUBLANES - (CONV_K - 1), 0), (0, 0)))
    zC = jnp.zeros((Bp, M_H, M_DK, M_DV), F32)
    zn = jnp.zeros((Bp, M_H, M_DK), F32)
    zm = jnp.zeros((Bp, 1, M_H), F32)
    zconv = jnp.zeros((Bp, SUBLANES, 2 * M_W), F32)
    zS = jnp.zeros((Bp, HG_H, HG_DK, HG_DV), F32)
    cw = conv_w[0]
    mgn = mlstm_gn[0][None]
    hgn = hgrn_gn[0][None]
    hm_p, C_p, n_p, m_p, conv_p = _mlstm(u_main, xb_p, w_gate, b_gate, zconv, zC, zn, zm, cw, mgn,
                                         B=Bp, T=Tp, L=Lp_m, row0=0)
    hm_s, C_s, n_s, m_s, conv_s = _mlstm(u_main, xb_s, w_gate, b_gate, padc(state_mlstm_conv[0]), state_mlstm_C[0],
                                         state_mlstm_n[0], state_mlstm_m[0][:, None, :], cw, mgn,
                                         B=Bs, T=Ts, L=Ls, row0=Np, NB=SAMPLE_SEQS_PER_STEP)
    og_p, S_p = _hgrn(u_main, zS, lb, hgn, B=Bp, T=Tp, L=Lp_h, row0=0)
    og_s, S_s = _hgrn(u_main, state_hgrn_S[0], lb, hgn, B=Bs, T=Ts, L=Ls, row0=Np, NB=SAMPLE_SEQS_PER_STEP)

    h1, h1b = _merge(hm_p, hm_s, og_p, og_s, u_main, xp2, xs2, w_bm[0].astype(BF16), w_bh[0].astype(BF16),
                     w_out[0].astype(BF16), ln1_g, ln1_b, tm=256)

    q = _matmul_bias(h1b, xa_wq[0], zeros_d, tm=1024, tn=1024, name="xa_q")
    att_p = _attention(q, mk, mv, B=Bp, T=Tp, tq=512, row0=0)
    att_s = _attention_cache(q, cache_mem_k, cache_mem_v, B=Bs, T=Ts, row0=Np)
    h2, route, counts = _oln(att_p, att_s, h1, xa_wo[0].astype(BF16), ln2_g, ln2_b, wr, br, tm=512)

    pos_flat, items = _moe_schedule(route, counts, TOP_K * NT)
    ys_sorted = _experts(items, pos_flat, h2, e_wg[0], e_wu[0], e_wd[0])
    y_p, y_s = _combine(pos_flat, h2, route, ys_sorted, ln3_g, ln3_b, Np=Np, Ns=Ns, tm=256)

    kv5 = lambda a: a.reshape(1, Bp, MEM, XA_H, XA_D)
    return (y_p.reshape(Bp, Tp, D_MODEL), y_s.reshape(Bs, Ts, D_MODEL), kv5(mk), kv5(mv),
            C_p[None], n_p[None], m_p.reshape(1, Bp, M_H), conv_p[None], S_p[None],
            C_s[None], n_s[None], m_s.reshape(1, Bs, M_H), conv_s[None], S_s[None])
```

```python
import functools
import math

import jax
import jax.numpy as jnp
from jax import lax
from jax.experimental import pallas as pl
from jax.experimental.pallas import tpu as pltpu

F32 = jnp.float32
BF16 = jnp.bfloat16

D_MODEL = 2048
M_W = 1024
M_H = 4
M_DK = 256
M_DV = 256
CONV_K = 4
HG_W = 1024
HG_H = 8
HG_DK = 128
HG_DV = 128
XA_H = 4
XA_D = 512
N_GROUPS = 4
EXP_PER_GROUP = 8
N_EXPERTS = 32
TOP_K = 2
EXP_FF = 512
DEPTH = 1
ALPHA = (2 * DEPTH) ** 0.25
LN_EPS = 1e-5

COL_QK, COL_V, COL_O = 0, 2048, 3072
COL_QH, COL_FH, COL_IH, COL_GH = 4096, 5120, 6144, 7168
COL_GM, COL_GHH = 8192, 10240
N_MAIN = 12288
GATE_LO = 4 * M_W
LANES = 128
SUBLANES = 8
ROUTE_LANE0 = N_GROUPS

VMEM_LIMIT = 56 << 20
MOE_ROW_TILE = 256
SAMPLE_SEQS_PER_STEP = 4
N_XBUF = 3
DMA_UNROLL = 8
INVERT_GROUP = 16


def _cparams(sem, vmem=VMEM_LIMIT):
    return pltpu.CompilerParams(dimension_semantics=sem, vmem_limit_bytes=vmem)


def _tile(n, pref):
    t = math.gcd(n, pref)
    assert t % SUBLANES == 0, (n, pref)
    return t


def _bdot(a, b):
    return jnp.dot(a.astype(BF16), b.astype(BF16), preferred_element_type=F32)


def _bdot_nt(a, b):
    return lax.dot_general(a.astype(BF16), b.astype(BF16), (((1,), (1,)), ((), ())),
                           preferred_element_type=F32)


def _bdot_tn(a, b):
    return lax.dot_general(a.astype(BF16), b.astype(BF16), (((0,), (0,)), ((), ())),
                           preferred_element_type=F32)


def _sigmoid(x):
    return 1.0 / (1.0 + jnp.exp(-x))


def _cumsum_rows(x, seg=None):
    n = seg or x.shape[0]
    assert n & (n - 1) == 0
    row = lax.broadcasted_iota(jnp.int32, x.shape, 0) & (n - 1)
    d = 1
    while d < n:
        x = x + jnp.where(row >= d, pltpu.roll(x, d, axis=0), 0.0)
        d *= 2
    return x


def _col_to_row(col, eye):
    return jnp.sum(jnp.where(eye, col, 0.0), axis=0, keepdims=True)


def _row_to_col(row, eye):
    return jnp.sum(jnp.where(eye, row, 0.0), axis=1, keepdims=True)


def _layer_norm(x, g, b):
    mu = jnp.mean(x, axis=-1, keepdims=True)
    xc = x - mu
    var = jnp.mean(xc * xc, axis=-1, keepdims=True)
    return xc * lax.rsqrt(var + LN_EPS) * g + b


def _mm_kernel(x_ref, w_ref, b_ref, o_ref, wb_scr):
    @pl.when(pl.program_id(1) == 0)
    def _new_weight_tile():
        wb_scr[...] = w_ref[...].astype(BF16)

    acc = jnp.dot(x_ref[...], wb_scr[...], preferred_element_type=F32)
    o_ref[...] = (acc + b_ref[...]).astype(o_ref.dtype)


def _matmul_bias(x, w, b, *, tm, tn, out_dtype=F32, name):
    M, K = x.shape
    N = w.shape[1]
    tm = _tile(M, tm)
    tn = _tile(N, tn)
    return pl.pallas_call(
        _mm_kernel,
        grid=(N // tn, M // tm),
        in_specs=[pl.BlockSpec((tm, K), lambda j, i: (i, 0)),
                  pl.BlockSpec((K, tn), lambda j, i: (0, j)),
                  pl.BlockSpec((1, tn), lambda j, i: (0, j))],
        out_specs=pl.BlockSpec((tm, tn), lambda j, i: (i, j)),
        out_shape=jax.ShapeDtypeStruct((M, N), out_dtype),
        scratch_shapes=[pltpu.VMEM((K, tn), BF16)],
        compiler_params=_cparams(("parallel", "arbitrary")),
        name=name,
    )(x, w, b)


def _inproj_kernel(xp_ref, xs_ref, wt_ref, b_ref, u_ref, wb_scr, *, n_prompt_tiles):
    i = pl.program_id(1)

    @pl.when(i == 0)
    def _new_weight_tile():
        wb_scr[...] = wt_ref[...].astype(BF16)

    xb = jnp.where(i < n_prompt_tiles, xp_ref[...], xs_ref[...])
    u_ref[...] = _bdot_nt(xb, wb_scr[...]) + b_ref[...]


def _in_proj(xb_p, xb_s, w_t, b_main, *, tm, tn):
    Np, Ns = xb_p.shape[0], xb_s.shape[0]
    NT = Np + Ns
    K = w_t.shape[1]
    N = w_t.shape[0] - 2 * M_H
    tm = _tile(math.gcd(Np, Ns), tm)
    tn = _tile(math.gcd(GATE_LO, N), tn)
    assert (2 * M_H) % SUBLANES == 0
    npt = Np // tm
    na = GATE_LO // tn
    w_rows = lambda j, i: (pl.multiple_of(j * tn + jnp.where(j < na, 0, 2 * M_H), SUBLANES), 0)
    return pl.pallas_call(
        functools.partial(_inproj_kernel, n_prompt_tiles=npt),
        grid=(N // tn, NT // tm),
        in_specs=[pl.BlockSpec((tm, K), lambda j, i: (jnp.minimum(i, npt - 1), 0)),
                  pl.BlockSpec((tm, K), lambda j, i: (jnp.maximum(i - npt, 0), 0)),
                  pl.BlockSpec((pl.Element(tn), pl.Element(K)), w_rows),
                  pl.BlockSpec((1, tn), lambda j, i: (0, j))],
        out_specs=pl.BlockSpec((tm, tn), lambda j, i: (i, j)),
        out_shape=jax.ShapeDtypeStruct((NT, N), F32),
        scratch_shapes=[pltpu.VMEM((tn, K), BF16)],
        compiler_params=_cparams(("parallel", "arbitrary")),
        name="in_proj",
    )(xb_p, xb_s, w_t, b_main)


def _mlstm_kernel(*refs, L, NC, NB):
    (qk_ref, v_ref, o_ref, xb_ref, wg_ref, bg_ref, conv0_ref, C0_ref, n0_ref, m0_ref, cw_ref, gn_ref,
     h_ref, Co_ref, no_ref, mo_ref, convo_ref, C_scr, n_scr, m_scr, tail_scr) = refs
    c = pl.program_id(1)
    if NC > 1:
        @pl.when(c == 0)
        def _init():
            C_scr[...] = C0_ref[0]
            n_scr[...] = n0_ref[0]
            m_scr[...] = m0_ref[0]
            tail_scr[...] = conv0_ref[0]

    cw = cw_ref[...]
    gn = gn_ref[...]
    g_all = _bdot_nt(xb_ref[...], wg_ref[...]) + bg_ref[...]
    ti = lax.broadcasted_iota(jnp.int32, (L, L), 0)
    si = lax.broadcasted_iota(jnp.int32, (L, L), 1)
    eye = ti == si
    causal = si <= ti
    head_lane = lax.broadcasted_iota(jnp.int32, (1, M_H), 1)
    head_row = lax.broadcasted_iota(jnp.int32, (M_H, M_DK), 0)

    seqs = []
    for nb in range(NB):
        rows = slice(nb * L, (nb + 1) * L)
        if NC == 1:
            sq = dict(C_in=C0_ref.at[nb], C_out=Co_ref.at[nb], n_out=no_ref.at[nb], m_out=mo_ref.at[nb])
            n_all, m_all, tail = n0_ref[nb], m0_ref[nb], conv0_ref[nb]
        else:
            sq = dict(C_in=C_scr, C_out=C_scr, n_out=n_scr, m_out=m_scr)
            n_all, m_all, tail = n_scr[...], m_scr[...], tail_scr[...]
        qk_pre = qk_ref[rows, :]
        ext = jnp.concatenate([tail, qk_pre], axis=0)
        acc = qk_pre * cw[CONV_K - 1:CONV_K, :]
        for j in range(1, CONV_K):
            acc = acc + pltpu.roll(ext, j, axis=0)[SUBLANES:, :] * cw[CONV_K - 1 - j:CONV_K - j, :]
        if NC > 1:
            tail_scr[...] = qk_pre[L - SUBLANES:, :]
        qk = acc * _sigmoid(acc)
        g = g_all[rows, :]
        lf_all = jnp.minimum(g, 0.0) - jnp.log(1.0 + jnp.exp(-jnp.abs(g)))
        F_all = _cumsum_rows(lf_all)
        heads = []
        for h in range(M_H):
            ks = slice(h * M_DK, (h + 1) * M_DK)
            q = qk[:, ks] * (M_DK ** -0.5)
            k = qk[:, M_W + h * M_DK:M_W + (h + 1) * M_DK]
            v = v_ref[rows, ks]
            ig = g[:, h:h + 1]
            F = F_all[:, M_H + h:M_H + h + 1]
            m_prev = m_all[:, h:h + 1]
            r_row = _col_to_row(ig - F, eye)
            Dm = jnp.where(causal, F + r_row, -jnp.inf)
            init_w = F + m_prev
            m_t = jnp.maximum(init_w, jnp.max(Dm, axis=1, keepdims=True))
            P = jnp.exp(Dm - m_t)
            a0 = jnp.exp(init_w - m_t)
            FL = F[L - 1:L, :]
            mL = m_t[L - 1:L, :]
            wL = jnp.exp(FL - F + ig - mL)
            decay = jnp.exp(FL + m_prev - mL)
            heads.append(dict(ks=ks, q=q, k=k, v=v, m_t=m_t, P=P, a0=a0, mL=mL, decay=decay, kw=wL * k))
        sq.update(rows=rows, heads=heads, n_all=n_all, m_all=m_all, qk_pre=qk_pre)
        seqs.append(sq)

    for sq in seqs:
        for h, d in enumerate(sq["heads"]):
            d["S"] = _bdot_nt(d["q"], d["k"])
            d["qC"] = _bdot(d["q"], sq["C_in"][h])
            d["kv"] = _bdot_tn(d["kw"], d["v"])

    for sq in seqs:
        for d in sq["heads"]:
            d["Sc"] = d["S"] * d["P"]
            d["num"] = _bdot(d["Sc"], d["v"])

    for nb, sq in enumerate(seqs):
        rows, n_new, m_new = sq["rows"], sq["n_all"], sq["m_all"]
        for h, d in enumerate(sq["heads"]):
            ks, q, a0, decay = d["ks"], d["q"], d["a0"], d["decay"]
            n_row = sq["n_all"][h:h + 1, :]
            num = d["num"] + a0 * d["qC"]
            den = jnp.sum(d["Sc"], axis=1, keepdims=True) + a0 * jnp.sum(q * n_row, axis=1, keepdims=True)
            hh = num * (1.0 / jnp.maximum(jnp.abs(den), jnp.exp(-d["m_t"])))
            sq["C_out"][h] = decay * sq["C_in"][h] + d["kv"]
            n_new = jnp.where(head_row == h, decay * n_row + jnp.sum(d["kw"], axis=0, keepdims=True), n_new)
            m_new = jnp.where(head_lane == h, d["mL"], m_new)

            hm = _sigmoid(o_ref[rows, ks]) * hh
            hm = hm - jnp.mean(hm, axis=1, keepdims=True)
            hm = hm * lax.rsqrt(jnp.mean(hm * hm, axis=1, keepdims=True) + LN_EPS) * gn[:, ks]
            h_ref[rows, ks] = hm
        sq["n_out"][...] = n_new
        sq["m_out"][...] = m_new
        conv_tail = sq["qk_pre"][L - (CONV_K - 1):, :]
        if NC == 1:
            convo_ref[nb] = conv_tail
        else:
            @pl.when(c == NC - 1)
            def _fin():
                Co_ref[0] = C_scr[...]
                no_ref[0] = n_new
                mo_ref[0] = m_new
                convo_ref[0] = conv_tail


def _mlstm(u_main, xb, w_gate, b_gate, conv0p, C0, n0, m0, conv_w, gn, *, B, T, L, row0, NB=1):
    NC = T // L
    assert NB == 1 or NC == 1
    assert B % NB == 0 and row0 % (NB * L) == 0
    R = NB * L
    rb0 = row0 // R
    tok = lambda b, c: rb0 + b * NC + c
    in_specs = [
        pl.BlockSpec((R, 2 * M_W), lambda b, c: (tok(b, c), COL_QK // (2 * M_W))),
        pl.BlockSpec((R, M_W), lambda b, c: (tok(b, c), COL_V // M_W)),
        pl.BlockSpec((R, M_W), lambda b, c: (tok(b, c), COL_O // M_W)),
        pl.BlockSpec((R, D_MODEL), lambda b, c: (b * NC + c, 0)),
        pl.BlockSpec((LANES, D_MODEL), lambda b, c: (0, 0)),
        pl.BlockSpec((1, LANES), lambda b, c: (0, 0)),
        pl.BlockSpec((NB, SUBLANES, 2 * M_W), lambda b, c: (b, 0, 0)),
        pl.BlockSpec((NB, M_H, M_DK, M_DV), lambda b, c: (b, 0, 0, 0)),
        pl.BlockSpec((NB, M_H, M_DK), lambda b, c: (b, 0, 0)),
        pl.BlockSpec((NB, 1, M_H), lambda b, c: (b, 0, 0)),
        pl.BlockSpec((CONV_K, 2 * M_W), lambda b, c: (0, 0)),
        pl.BlockSpec((1, M_W), lambda b, c: (0, 0)),
    ]
    args = [u_main, u_main, u_main, xb, w_gate, b_gate, conv0p, C0, n0, m0, conv_w, gn]
    out_specs = [
        pl.BlockSpec((R, M_W), lambda b, c: (b * NC + c, 0)),
        pl.BlockSpec((NB, M_H, M_DK, M_DV), lambda b, c: (b, 0, 0, 0)),
        pl.BlockSpec((NB, M_H, M_DK), lambda b, c: (b, 0, 0)),
        pl.BlockSpec((NB, 1, M_H), lambda b, c: (b, 0, 0)),
        pl.BlockSpec((NB, CONV_K - 1, 2 * M_W), lambda b, c: (b, 0, 0)),
    ]
    out_shape = [
        jax.ShapeDtypeStruct((B * T, M_W), F32),
        jax.ShapeDtypeStruct((B, M_H, M_DK, M_DV), F32),
        jax.ShapeDtypeStruct((B, M_H, M_DK), F32),
        jax.ShapeDtypeStruct((B, 1, M_H), F32),
        jax.ShapeDtypeStruct((B, CONV_K - 1, 2 * M_W), F32),
    ]
    return pl.pallas_call(
        functools.partial(_mlstm_kernel, L=L, NC=NC, NB=NB),
        grid=(B // NB, NC),
        in_specs=in_specs,
        out_specs=out_specs,
        out_shape=out_shape,
        scratch_shapes=[pltpu.VMEM((M_H, M_DK, M_DV), F32), pltpu.VMEM((M_H, M_DK), F32),
                        pltpu.VMEM((1, M_H), F32), pltpu.VMEM((SUBLANES, 2 * M_W), F32)],
        compiler_params=_cparams(("parallel", "arbitrary")),
        name=f"mlstm_L{L}",
    )(*args)


def _hgrn_kernel(*refs, L, NC, NB):
    (q_ref, f_ref, i_ref, g_ref, S0_ref, lb_ref, gn_ref, o_ref, So_ref, S_scr) = refs
    c = pl.program_id(1)
    if NC > 1:
        @pl.when(c == 0)
        def _init():
            S_scr[...] = S0_ref[0]

    lb = lb_ref[...]
    f = lb + (1.0 - lb) * _sigmoid(f_ref[...])
    kk = 1.0 - f
    b = _cumsum_rows(jnp.log(f), seg=L)
    qh = q_ref[...]
    q = qh * _sigmoid(qh)
    v = i_ref[...]
    gh = g_ref[...]
    gsilu = gh * _sigmoid(gh)
    gn = gn_ref[...]
    ti = lax.broadcasted_iota(jnp.int32, (L, L), 0)
    si = lax.broadcasted_iota(jnp.int32, (L, L), 1)
    causal = si <= ti
    ci = lax.broadcasted_iota(jnp.int32, (HG_DK, HG_DK), 0)
    cj = lax.broadcasted_iota(jnp.int32, (HG_DK, HG_DK), 1)
    eye = ci == cj
    mid = max(L // 2 - 1, 0)
    hsl = [slice(h * HG_DK, (h + 1) * HG_DK) for h in range(HG_H)]

    seqs = []
    for nb in range(NB):
        rows = slice(nb * L, (nb + 1) * L)
        bs, qs, ks = b[rows], q[rows], kk[rows]
        bL = bs[L - 1:L, :]
        bm = bs[mid:mid + 1, :]
        S_in, S_out = (S0_ref.at[nb], So_ref.at[nb]) if NC == 1 else (S_scr, S_scr)
        seqs.append(dict(rows=rows, bL=bL, v=v[rows], q_in=qs * jnp.exp(bs), q_t=qs * jnp.exp(bs - bm),
                         k_t=ks * jnp.exp(bm - bs), k_st=ks * jnp.exp(bL - bs), S_in=S_in, S_out=S_out))

    for sq in seqs:
        sq["A"] = [jnp.where(causal, _bdot_nt(sq["q_t"][:, hs], sq["k_t"][:, hs]), 0.0) for hs in hsl]
        sq["qS"] = [_bdot(sq["q_in"][:, hs], sq["S_in"][h]) for h, hs in enumerate(hsl)]
        sq["kv"] = [_bdot_tn(sq["k_st"][:, hs], sq["v"][:, hs]) for hs in hsl]
    for sq in seqs:
        sq["o_intra"] = [_bdot(sq["A"][h], sq["v"][:, hs]) for h, hs in enumerate(hsl)]
    for sq in seqs:
        rows = sq["rows"]
        for h, hs in enumerate(hsl):
            o = sq["o_intra"][h] + sq["qS"][h]
            dec = jnp.exp(_row_to_col(sq["bL"][:, hs], eye))
            sq["S_out"][h] = dec * sq["S_in"][h] + sq["kv"][h]
            o = o * lax.rsqrt(jnp.mean(o * o, axis=1, keepdims=True) + LN_EPS) * gn[:, hs]
            o_ref[rows, hs] = o * gsilu[rows, hs]

    if NC > 1:
        @pl.when(c == NC - 1)
        def _fin():
            So_ref[0] = S_scr[...]


def _hgrn(u_main, S0, lb, gn, *, B, T, L, row0, NB=1):
    NC = T // L
    assert NB == 1 or NC == 1
    assert B % NB == 0 and row0 % (NB * L) == 0
    R = NB * L
    rb0 = row0 // R
    tok = lambda b, c: rb0 + b * NC + c
    col = lambda off: (lambda b, c: (tok(b, c), off // HG_W))
    in_specs = [
        pl.BlockSpec((R, HG_W), col(COL_QH)),
        pl.BlockSpec((R, HG_W), col(COL_FH)),
        pl.BlockSpec((R, HG_W), col(COL_IH)),
        pl.BlockSpec((R, HG_W), col(COL_GH)),
        pl.BlockSpec((NB, HG_H, HG_DK, HG_DV), lambda b, c: (b, 0, 0, 0)),
        pl.BlockSpec((1, HG_W), lambda b, c: (0, 0)),
        pl.BlockSpec((1, HG_W), lambda b, c: (0, 0)),
    ]
    args = [u_main, u_main, u_main, u_main, S0, lb, gn]
    return pl.pallas_call(
        functools.partial(_hgrn_kernel, L=L, NC=NC, NB=NB),
        grid=(B // NB, NC),
        in_specs=in_specs,
        out_specs=[pl.BlockSpec((R, HG_W), lambda b, c: (b * NC + c, 0)),
                   pl.BlockSpec((NB, HG_H, HG_DK, HG_DV), lambda b, c: (b, 0, 0, 0))],
        out_shape=[jax.ShapeDtypeStruct((B * T, HG_W), F32),
                   jax.ShapeDtypeStruct((B, HG_H, HG_DK, HG_DV), F32)],
        scratch_shapes=[pltpu.VMEM((HG_H, HG_DK, HG_DV), F32)],
        compiler_params=_cparams(("parallel", "arbitrary")),
        name=f"hgrn_L{L}",
    )(*args)


def _merge_kernel(hmp_ref, hms_ref, ogp_ref, ogs_ref, gm_ref, gh_ref, xp_ref, xs_ref, wbm_ref, wbh_ref,
                  wo_ref, g_ref, b_ref, h1_ref, h1b_ref, *, n_prompt_tiles):
    is_prompt = pl.program_id(0) < n_prompt_tiles
    a = _bdot(jnp.where(is_prompt, hmp_ref[...], hms_ref[...]), wbm_ref[...])
    bb = _bdot(jnp.where(is_prompt, ogp_ref[...], ogs_ref[...]), wbh_ref[...])
    merged = _sigmoid(gm_ref[...]) * a + _sigmoid(gh_ref[...]) * bb
    mix = _bdot(merged, wo_ref[...])
    x = jnp.where(is_prompt, xp_ref[...], xs_ref[...])
    h1 = _layer_norm(ALPHA * x + mix, g_ref[...], b_ref[...])
    h1_ref[...] = h1
    h1b_ref[...] = h1.astype(BF16)


def _merge(hm_p, hm_s, og_p, og_s, u_main, x_p, x_s, w_bm, w_bh, w_out, ln_g, ln_b, *, tm):
    Np, Ns = x_p.shape[0], x_s.shape[0]
    NT = Np + Ns
    tm = _tile(math.gcd(Np, Ns), tm)
    npt = Np // tm
    const = lambda i: (0, 0)
    prompt_rows = lambda i: (jnp.minimum(i, npt - 1), 0)
    sample_rows = lambda i: (jnp.maximum(i - npt, 0), 0)
    return pl.pallas_call(
        functools.partial(_merge_kernel, n_prompt_tiles=npt),
        grid=(NT // tm,),
        in_specs=[
            pl.BlockSpec((tm, M_W), prompt_rows),
            pl.BlockSpec((tm, M_W), sample_rows),
            pl.BlockSpec((tm, HG_W), prompt_rows),
            pl.BlockSpec((tm, HG_W), sample_rows),
            pl.BlockSpec((tm, D_MODEL), lambda i: (i, COL_GM // D_MODEL)),
            pl.BlockSpec((tm, D_MODEL), lambda i: (i, COL_GHH // D_MODEL)),
            pl.BlockSpec((tm, D_MODEL), prompt_rows),
            pl.BlockSpec((tm, D_MODEL), sample_rows),
            pl.BlockSpec((M_W, D_MODEL), const, pipeline_mode=pl.Buffered(1)),
            pl.BlockSpec((HG_W, D_MODEL), const, pipeline_mode=pl.Buffered(1)),
            pl.BlockSpec((D_MODEL, D_MODEL), const, pipeline_mode=pl.Buffered(1)),
            pl.BlockSpec((1, D_MODEL), const),
            pl.BlockSpec((1, D_MODEL), const),
        ],
        out_specs=[pl.BlockSpec((tm, D_MODEL), lambda i: (i, 0)),
                   pl.BlockSpec((tm, D_MODEL), lambda i: (i, 0))],
        out_shape=[jax.ShapeDtypeStruct((NT, D_MODEL), F32),
                   jax.ShapeDtypeStruct((NT, D_MODEL), BF16)],
        compiler_params=_cparams(("parallel",)),
        name="merge_ln1",
    )(hm_p, hm_s, og_p, og_s, u_main, u_main, x_p, x_s, w_bm, w_bh, w_out, ln_g, ln_b)


def _attn_kernel(q_ref, *refs):
    k_refs, v_refs, o_ref = refs[:XA_H], refs[XA_H:2 * XA_H], refs[2 * XA_H]
    for h in range(XA_H):
        sl = slice(h * XA_D, (h + 1) * XA_D)
        s = _bdot_nt(q_ref[:, sl], k_refs[h][...]) * (XA_D ** -0.5)
        e = jnp.exp(s - jnp.max(s, axis=1, keepdims=True))
        p = e / jnp.sum(e, axis=1, keepdims=True)
        o_ref[:, sl] = _bdot(p, v_refs[h][...])


def _attn_cache_kernel(q_ref, k_ref, v_ref, o_ref):
    T = q_ref.shape[0]
    M = k_ref.shape[2]
    k_all = k_ref[0, 0].reshape(M * XA_H, XA_D)
    v_all = v_ref[0, 0].reshape(M * XA_H, XA_D)
    q_all = jnp.concatenate([q_ref[:, h * XA_D:(h + 1) * XA_D] for h in range(XA_H)], axis=0)
    s = _bdot_nt(q_all, k_all) * (XA_D ** -0.5)
    q_head = lax.broadcasted_iota(jnp.int32, s.shape, 0) // T
    k_head = lax.broadcasted_iota(jnp.int32, s.shape, 1) % XA_H
    s = jnp.where(q_head == k_head, s, -jnp.inf)
    e = jnp.exp(s - jnp.max(s, axis=1, keepdims=True))
    p = e / jnp.sum(e, axis=1, keepdims=True)
    o = _bdot(p, v_all)
    for h in range(XA_H):
        o_ref[:, h * XA_D:(h + 1) * XA_D] = o[h * T:(h + 1) * T]


def _attention_cache(q, cache_k, cache_v, *, B, T, row0):
    M = cache_k.shape[2]
    rb0 = row0 // T
    kv_spec = pl.BlockSpec((1, 1, M, XA_H, XA_D), lambda b: (0, b, 0, 0, 0))
    return pl.pallas_call(
        _attn_cache_kernel,
        grid=(B,),
        in_specs=[pl.BlockSpec((T, D_MODEL), lambda b: (rb0 + b, 0)), kv_spec, kv_spec],
        out_specs=pl.BlockSpec((T, D_MODEL), lambda b: (b, 0)),
        out_shape=jax.ShapeDtypeStruct((B * T, D_MODEL), F32),
        compiler_params=_cparams(("parallel",)),
        name="xattn_cache",
    )(q, cache_k, cache_v)


def _attention(q, mem_k, mem_v, *, B, T, tq, row0):
    tq = _tile(T, tq)
    nq = T // tq
    rb0 = row0 // tq
    M = mem_k.shape[0] // B
    kv_specs = [pl.BlockSpec((M, XA_D), lambda b, t, h=h: (b, h)) for h in range(XA_H)]
    return pl.pallas_call(
        _attn_kernel,
        grid=(B, nq),
        in_specs=[pl.BlockSpec((tq, D_MODEL), lambda b, t: (rb0 + b * nq + t, 0))] + kv_specs + kv_specs,
        out_specs=pl.BlockSpec((tq, D_MODEL), lambda b, t: (b * nq + t, 0)),
        out_shape=jax.ShapeDtypeStruct((B * T, D_MODEL), F32),
        compiler_params=_cparams(("parallel", "parallel")),
        name=f"xattn_T{T}",
    )(q, *([mem_k] * XA_H), *([mem_v] * XA_H))


def _oln_kernel(op_ref, os_ref, h1_ref, wo_ref, g_ref, b_ref, wr_ref, br_ref,
                h2_ref, route_ref, cnt_ref, carry_scr, *, n_prompt_tiles, n_sub):
    i = pl.program_id(0)

    @pl.when(i == 0)
    def _init():
        carry_scr[...] = jnp.zeros_like(carry_scr)

    ts = h1_ref.shape[0] // n_sub
    lane = lax.broadcasted_iota(jnp.int32, (ts, LANES), 1)
    lane_f = lane.astype(F32)
    neg = -jnp.inf
    ti = lax.broadcasted_iota(jnp.int32, (ts, ts), 0)
    si = lax.broadcasted_iota(jnp.int32, (ts, ts), 1)
    tri = jnp.where(si <= ti, 1.0, 0.0).astype(BF16)
    is_prompt = i < n_prompt_tiles

    def first_argmax(vals):
        mx = jnp.max(vals, axis=1, keepdims=True)
        idx = jnp.min(jnp.where(vals == mx, lane_f, float(LANES)), axis=1, keepdims=True)
        return mx, idx.astype(jnp.int32)

    carry = carry_scr[...]
    for sb in range(n_sub):
        rows = slice(sb * ts, (sb + 1) * ts)
        att = jnp.where(is_prompt, op_ref[rows, :], os_ref[rows, :])
        h2 = _layer_norm(ALPHA * h1_ref[rows, :] + _bdot(att, wo_ref[...]), g_ref[...], b_ref[...])
        h2_ref[rows, :] = h2

        logits = _bdot(h2, wr_ref[...]) + br_ref[...]
        gl = jnp.where(lane < N_GROUPS, logits, neg)
        gmax, grp = first_argmax(gl)
        p_grp = 1.0 / jnp.sum(jnp.exp(gl - gmax), axis=1, keepdims=True)
        lo = ROUTE_LANE0 + grp * EXP_PER_GROUP
        el = jnp.where((lane >= lo) & (lane < lo + EXP_PER_GROUP), logits, neg)
        v1, i1 = first_argmax(el)
        v2, i2 = first_argmax(jnp.where(lane == i1, neg, el))
        e21 = jnp.exp(v2 - v1)
        w1 = p_grp / (1.0 + e21)
        w2 = p_grp * e21 / (1.0 + e21)

        pick1 = lane == i1
        pick2 = lane == i2
        onehot = jnp.where(pick1 | pick2, 1.0, 0.0)
        cnt = _bdot(tri, onehot) + carry
        r1 = jnp.sum(jnp.where(pick1, cnt, 0.0), axis=1, keepdims=True) - 1.0
        r2 = jnp.sum(jnp.where(pick2, cnt, 0.0), axis=1, keepdims=True) - 1.0
        carry = cnt[ts - 1:ts, :]

        e1 = (i1 - ROUTE_LANE0).astype(F32)
        e2 = (i2 - ROUTE_LANE0).astype(F32)
        packed = jnp.zeros_like(logits)
        for idx, val in enumerate((e1, e2, w1, w2, r1, r2)):
            packed = jnp.where(lane == idx, val, packed)
        route_ref[rows, :] = packed

    carry_scr[...] = carry
    cnt_ref[...] = jnp.broadcast_to(carry, cnt_ref.shape)


def _oln(o_p, o_s, h1, wo, ln_g, ln_b, wr, br, *, tm):
    Np, Ns = o_p.shape[0], o_s.shape[0]
    NT = Np + Ns
    tm = _tile(math.gcd(Np, Ns), tm)
    npt = Np // tm
    const = lambda i: (0, 0)
    return pl.pallas_call(
        functools.partial(_oln_kernel, n_prompt_tiles=npt, n_sub=2 if tm % (2 * SUBLANES) == 0 else 1),
        grid=(NT // tm,),
        in_specs=[pl.BlockSpec((tm, D_MODEL), lambda i: (jnp.minimum(i, npt - 1), 0)),
                  pl.BlockSpec((tm, D_MODEL), lambda i: (jnp.maximum(i - npt, 0), 0)),
                  pl.BlockSpec((tm, D_MODEL), lambda i: (i, 0)),
                  pl.BlockSpec((D_MODEL, D_MODEL), const, pipeline_mode=pl.Buffered(1)),
                  pl.BlockSpec((1, D_MODEL), const),
                  pl.BlockSpec((1, D_MODEL), const),
                  pl.BlockSpec((D_MODEL, LANES), const),
                  pl.BlockSpec((1, LANES), const)],
        out_specs=[pl.BlockSpec((tm, D_MODEL), lambda i: (i, 0)),
                   pl.BlockSpec((tm, LANES), lambda i: (i, 0)),
                   pl.BlockSpec((SUBLANES, LANES), const)],
        out_shape=[jax.ShapeDtypeStruct((NT, D_MODEL), F32),
                   jax.ShapeDtypeStruct((NT, LANES), F32),
                   jax.ShapeDtypeStruct((SUBLANES, LANES), F32)],
        scratch_shapes=[pltpu.VMEM((1, LANES), F32)],
        compiler_params=_cparams(("arbitrary",)),
        name="oproj_ln2_router",
    )(o_p, o_s, h1, wo, ln_g, ln_b, wr, br)


def _row_copy(src_ref, src_row, dst_ref, dst_row, sem):
    return pltpu.make_async_copy(src_ref.at[pl.ds(src_row, 1), :], dst_ref.at[pl.ds(dst_row, 1), :], sem)


def _expert_weight_copies(e, slot, w_hbm, w_f32, sem):
    return [pltpu.make_async_copy(src.at[e], dst.at[slot], sem.at[slot, j])
            for j, (src, dst) in enumerate(zip(w_hbm, w_f32))]


def _expert_kernel(it_ref, ie_ref, lo_ref, hi_ref, first_ref, n_ref, ord_ref, nxt_ref, pos_ref,
                   h2_hbm, wg_hbm, wu_hbm, wd_hbm, y_ref,
                   xbuf, xsem, wg_f, wu_f, wd_f, wsem, wg_s, wu_s, wd_s, tok_ref):
    i = pl.program_id(0)
    n = n_ref[0]
    TR = xbuf.shape[1]
    w_hbm = (wg_hbm, wu_hbm, wd_hbm)
    w_f32 = (wg_f, wu_f, wd_f)

    def gather(base, slot, r):
        return _row_copy(h2_hbm, tok_ref[base + r], xbuf.at[slot], r, xsem.at[slot])

    def wait_rows(slot):
        def body(r, carry):
            gather(0, slot, r).wait()
            return carry
        lax.fori_loop(0, TR, body, 0, unroll=DMA_UNROLL)

    @pl.when(i == 0)
    def _prologue():
        for c in _expert_weight_copies(ie_ref[0], 0, w_hbm, w_f32, wsem):
            c.start()

        def invert(c, carry):
            p0 = pl.multiple_of(c * INVERT_GROUP, INVERT_GROUP)
            t0 = c * (INVERT_GROUP // TOP_K)
            rows = [pos_ref[p0 + u] for u in range(INVERT_GROUP)]
            for u in range(INVERT_GROUP):
                tok_ref[rows[u]] = t0 + u // TOP_K
            return carry
        lax.fori_loop(0, pos_ref.shape[0] // INVERT_GROUP, invert, 0)

        for ahead in range(N_XBUF - 1):
            base = it_ref[jnp.minimum(ahead, n - 1)] * TR

            def body(r, carry, base=base, ahead=ahead):
                gather(base, ahead, r).start()
                return carry
            lax.fori_loop(0, TR, body, 0, unroll=DMA_UNROLL)

    @pl.when(i < n)
    def _compute():
        slot = i % N_XBUF
        fill = (i + N_XBUF - 1) % N_XBUF
        wslot = ord_ref[i] % 2
        fresh = jnp.logical_or(i == 0, ie_ref[i] != ie_ref[jnp.maximum(i - 1, 0)])

        @pl.when(fresh)
        def _new_expert():
            for c in _expert_weight_copies(ie_ref[i], wslot, w_hbm, w_f32, wsem):
                c.wait()
            wg_s[...] = wg_f[wslot].astype(BF16)
            wu_s[...] = wu_f[wslot].astype(BF16)
            wd_s[...] = wd_f[wslot].astype(BF16)

            @pl.when(nxt_ref[i] >= 0)
            def _prefetch():
                for c in _expert_weight_copies(nxt_ref[i], 1 - wslot, w_hbm, w_f32, wsem):
                    c.start()

        wait_rows(slot)
        x = xbuf[slot].astype(BF16)
        base_next = it_ref[jnp.minimum(i + N_XBUF - 1, n - 1)] * TR
        for r in range(TR):
            gather(base_next, fill, r).start()
        gate = jnp.dot(x, wg_s[...], preferred_element_type=F32)
        up = jnp.dot(x, wu_s[...], preferred_element_type=F32)
        hid = gate * _sigmoid(gate) * up
        y = jnp.dot(hid.astype(BF16), wd_s[...], preferred_element_type=F32)
        rows = it_ref[i] * TR + lax.broadcasted_iota(jnp.int32, (TR, 1), 0)
        mine = (rows >= lo_ref[i]) & (rows < hi_ref[i])

        @pl.when(first_ref[i] == 1)
        def _first():
            y_ref[...] = jnp.where(mine, y, 0.0)

        @pl.when(first_ref[i] == 0)
        def _later():
            y_ref[...] = jnp.where(mine, y, y_ref[...])

        @pl.when(i == n - 1)
        def _drain():
            for ahead in range(1, N_XBUF):
                wait_rows((i + ahead) % N_XBUF)


def _experts(items, pos_flat, h2, e_wg, e_wu, e_wd):
    TR = MOE_ROW_TILE
    n_work = items[0].shape[0]
    n_rows = pos_flat.shape[0]
    assert n_rows % INVERT_GROUP == 0 and INVERT_GROUP % TOP_K == 0
    any_spec = pl.BlockSpec(memory_space=pl.ANY)
    return pl.pallas_call(
        _expert_kernel,
        grid_spec=pltpu.PrefetchScalarGridSpec(
            num_scalar_prefetch=9,
            grid=(n_work,),
            in_specs=[any_spec, any_spec, any_spec, any_spec],
            out_specs=pl.BlockSpec((TR, D_MODEL), lambda i, it, *_: (it[i], 0)),
            scratch_shapes=[pltpu.VMEM((N_XBUF, TR, D_MODEL), F32), pltpu.SemaphoreType.DMA((N_XBUF,)),
                            pltpu.VMEM((2, D_MODEL, EXP_FF), F32), pltpu.VMEM((2, D_MODEL, EXP_FF), F32),
                            pltpu.VMEM((2, EXP_FF, D_MODEL), F32), pltpu.SemaphoreType.DMA((2, 3)),
                            pltpu.VMEM((D_MODEL, EXP_FF), BF16), pltpu.VMEM((D_MODEL, EXP_FF), BF16),
                            pltpu.VMEM((EXP_FF, D_MODEL), BF16), pltpu.SMEM((n_rows,), jnp.int32)],
        ),
        out_shape=jax.ShapeDtypeStruct((n_rows, D_MODEL), F32),
        compiler_params=_cparams(("arbitrary",)),
        name="moe_experts",
    )(*items, pos_flat, h2, e_wg, e_wu, e_wd)


def _moe_schedule(route, counts, n_rows):
    TR = MOE_ROW_TILE
    assert n_rows % TR == 0
    n_work = n_rows // TR + N_EXPERTS - 1
    rt = route[:, :SUBLANES].T
    cnt = counts[0, ROUTE_LANE0:ROUTE_LANE0 + N_EXPERTS].astype(jnp.int32)
    g_end = jnp.cumsum(cnt)
    g_start = g_end - cnt
    onehot = rt[0:2, :, None] == jnp.arange(N_EXPERTS, dtype=F32)
    start_of = jnp.sum(jnp.where(onehot, g_start.astype(F32), 0.0), axis=-1)
    pos_flat = (start_of + rt[4:6]).astype(jnp.int32).T.reshape(-1)
    first_tile = g_start // TR
    n_items_e = jnp.where(cnt > 0, (g_end - 1) // TR - first_tile + 1, 0)
    item_end = jnp.cumsum(n_items_e)
    n_items = item_end[-1:]
    idx = jnp.minimum(jnp.arange(n_work, dtype=jnp.int32), n_items[0] - 1)
    ie = jnp.minimum(jnp.sum((item_end[None, :] <= idx[:, None]).astype(jnp.int32), axis=1), N_EXPERTS - 1)
    it = first_tile[ie] + idx - (item_end - n_items_e)[ie]
    first = jnp.concatenate([jnp.ones((1,), jnp.int32), (it[1:] != it[:-1]).astype(jnp.int32)])
    has = n_items_e > 0
    e_ids = jnp.arange(N_EXPERTS, dtype=jnp.int32)
    ordinal = jnp.cumsum(has.astype(jnp.int32)) - 1
    later = (e_ids[None, :] > e_ids[:, None]) & has[None, :]
    nxt = jnp.min(jnp.where(later, e_ids[None, :], N_EXPERTS), axis=1)
    nxt = jnp.where(nxt == N_EXPERTS, -1, nxt)
    items = (it.astype(jnp.int32), ie, g_start[ie], g_end[ie], first, n_items.astype(jnp.int32),
             ordinal[ie], nxt[ie])
    return pos_flat, items


def _combine_kernel(pos_ref, h2_ref, route_ref, ys_ref, g_ref, b_ref, yp_ref, ysm_ref,
                    buf, sem, *, tm, n_prompt_tiles):
    i = pl.program_id(0)
    last = pl.num_programs(0) - 1
    slot = i % 2

    def fetch(tile, slot, r, k):
        return _row_copy(ys_ref, pos_ref[TOP_K * (tile * tm + r) + k], buf.at[slot, k], r, sem.at[slot])

    def wait_rows(slot):
        def body(r, carry):
            for k in range(TOP_K):
                fetch(0, slot, r, k).wait()
            return carry
        lax.fori_loop(0, tm, body, 0, unroll=DMA_UNROLL)

    @pl.when(i == 0)
    def _prologue():
        def body(r, carry):
            for k in range(TOP_K):
                fetch(0, 0, r, k).start()
            return carry
        lax.fori_loop(0, tm, body, 0, unroll=DMA_UNROLL)

    wait_rows(slot)
    nxt = jnp.minimum(i + 1, last)
    for r in range(tm):
        for k in range(TOP_K):
            fetch(nxt, 1 - slot, r, k).start()
    route = route_ref[...]
    moe = route[:, 2:3] * buf[slot, 0] + route[:, 3:4] * buf[slot, 1]
    y = _layer_norm(ALPHA * h2_ref[...] + moe, g_ref[...], b_ref[...])

    @pl.when(i == last)
    def _drain():
        wait_rows(1 - slot)

    @pl.when(i < n_prompt_tiles)
    def _prompt():
        yp_ref[...] = y

    @pl.when(i >= n_prompt_tiles)
    def _sample():
        ysm_ref[...] = y


def _combine(pos_flat, h2, route, ys, ln_g, ln_b, *, Np, Ns, tm):
    NT = h2.shape[0]
    tm = _tile(math.gcd(Np, Ns), tm)
    npt = Np // tm
    const = lambda i, pos: (0, 0)
    return pl.pallas_call(
        functools.partial(_combine_kernel, tm=tm, n_prompt_tiles=npt),
        grid_spec=pltpu.PrefetchScalarGridSpec(
            num_scalar_prefetch=1,
            grid=(NT // tm,),
            in_specs=[pl.BlockSpec((tm, D_MODEL), lambda i, pos: (i, 0)),
                      pl.BlockSpec((tm, LANES), lambda i, pos: (i, 0)),
                      pl.BlockSpec(memory_space=pl.ANY),
                      pl.BlockSpec((1, D_MODEL), const),
                      pl.BlockSpec((1, D_MODEL), const)],
            out_specs=[pl.BlockSpec((tm, D_MODEL), lambda i, pos: (jnp.minimum(i, npt - 1), 0)),
                       pl.BlockSpec((tm, D_MODEL), lambda i, pos: (jnp.maximum(i - npt, 0), 0))],
            scratch_shapes=[pltpu.VMEM((2, TOP_K, tm, D_MODEL), F32), pltpu.SemaphoreType.DMA((2,))],
        ),
        out_shape=[jax.ShapeDtypeStruct((Np, D_MODEL), F32),
                   jax.ShapeDtypeStruct((Ns, D_MODEL), F32)],
        compiler_params=_cparams(("arbitrary",)),
        name="moe_combine_ln3",
    )(pos_flat, h2, route, ys, ln_g, ln_b)


def kernel(x_prompt, x_sample, mem_prompt, cache_mem_k, cache_mem_v, state_mlstm_C, state_mlstm_n,
           state_mlstm_m, state_mlstm_conv, state_hgrn_S, w_in, b_in, conv_w, mlstm_gn, lb_logits, hgrn_gn,
           w_bm, w_bh, w_out, ln1_g, ln1_b, xa_wq, xa_wk, xa_wv, xa_wo, ln2_g, ln2_b,
           r1_w, r1_b, r2_w, r2_b, e_wg, e_wu, e_wd, ln3_g, ln3_b):
    Bp, Tp, _ = x_prompt.shape
    Bs, Ts, _ = x_sample.shape
    MEM = mem_prompt.shape[1]
    Np, Ns = Bp * Tp, Bs * Ts
    NT = Np + Ns
    Lp_m = math.gcd(Tp, 256)
    Lp_h = math.gcd(Tp, 64)
    Ls = Ts
    assert Ts % SUBLANES == 0 and Np % Ls == 0

    xp2 = x_prompt.reshape(Np, D_MODEL)
    xs2 = x_sample.reshape(Ns, D_MODEL)

    w = w_in[0]
    bi = b_in[0]
    w_t = jnp.swapaxes(w, 0, 1)
    xb_p = xp2.astype(BF16)
    xb_s = xs2.astype(BF16)
    b_main = jnp.concatenate([bi[:GATE_LO], bi[GATE_LO + 2 * M_H:]])[None]
    w_gate_t = lax.slice(w_t, (GATE_LO, 0), (GATE_LO + 2 * M_H, D_MODEL))
    w_gate = jnp.pad(w_gate_t, ((0, LANES - 2 * M_H), (0, 0)))
    b_gate = jnp.pad(bi[GATE_LO:GATE_LO + 2 * M_H], (0, LANES - 2 * M_H))[None]
    lb = jnp.cumsum(jax.nn.softmax(lb_logits.astype(F32), axis=0), axis=0)[0][None]
    wr = jnp.zeros((D_MODEL, LANES), F32)
    wr = wr.at[:, :N_GROUPS].set(r1_w[0])
    wr = wr.at[:, ROUTE_LANE0:ROUTE_LANE0 + N_EXPERTS].set(
        jnp.transpose(r2_w[0], (1, 0, 2)).reshape(D_MODEL, N_EXPERTS)).astype(BF16)
    br = jnp.zeros((1, LANES), F32)
    br = br.at[0, :N_GROUPS].set(r1_b[0])
    br = br.at[0, ROUTE_LANE0:ROUTE_LANE0 + N_EXPERTS].set(r2_b[0].reshape(N_EXPERTS))
    zeros_d = jnp.zeros((1, D_MODEL), F32)

    memb = mem_prompt.reshape(Bp * MEM, D_MODEL).astype(BF16)
    mk = _matmul_bias(memb, xa_wk[0], zeros_d, tm=1024, tn=1024, name="mem_k")
    mv = _matmul_bias(memb, xa_wv[0], zeros_d, tm=1024, tn=1024, name="mem_v")

    u_main = _in_proj(xb_p, xb_s, w_t, b_main, tm=1024, tn=1024)

    padc = lambda c: jnp.pad(c, ((0, 0), (SUBLANES - (CONV_K - 1), 0), (0, 0)))
    zC = jnp.zeros((Bp, M_H, M_DK, M_DV), F32)
    zn = jnp.zeros((Bp, M_H, M_DK), F32)
    zm = jnp.zeros((Bp, 1, M_H), F32)
    zconv = jnp.zeros((Bp, SUBLANES, 2 * M_W), F32)
    zS = jnp.zeros((Bp, HG_H, HG_DK, HG_DV), F32)
    cw = conv_w[0]
    mgn = mlstm_gn[0][None]
    hgn = hgrn_gn[0][None]
    hm_p, C_p, n_p, m_p, conv_p = _mlstm(u_main, xb_p, w_gate, b_gate, zconv, zC, zn, zm, cw, mgn,
                                         B=Bp, T=Tp, L=Lp_m, row0=0)
    hm_s, C_s, n_s, m_s, conv_s = _mlstm(u_main, xb_s, w_gate, b_gate, padc(state_mlstm_conv[0]), state_mlstm_C[0],
                                         state_mlstm_n[0], state_mlstm_m[0][:, None, :], cw, mgn,
                                         B=Bs, T=Ts, L=Ls, row0=Np, NB=SAMPLE_SEQS_PER_STEP)
    og_p, S_p = _hgrn(u_main, zS, lb, hgn, B=Bp, T=Tp, L=Lp_h, row0=0)
    og_s, S_s = _hgrn(u_main, state_hgrn_S[0], lb, hgn, B=Bs, T=Ts, L=Ls, row0=Np, NB=SAMPLE_SEQS_PER_STEP)

    h1, h1b = _merge(hm_p, hm_s, og_p, og_s, u_main, xp2, xs2, w_bm[0].astype(BF16), w_bh[0].astype(BF16),
                     w_out[0].astype(BF16), ln1_g, ln1_b, tm=256)

    q = _matmul_bias(h1b, xa_wq[0], zeros_d, tm=1024, tn=1024, name="xa_q")
    att_p = _attention(q, mk, mv, B=Bp, T=Tp, tq=512, row0=0)
    att_s = _attention_cache(q, cache_mem_k, cache_mem_v, B=Bs, T=Ts, row0=Np)
    h2, route, counts = _oln(att_p, att_s, h1, xa_wo[0].astype(BF16), ln2_g, ln2_b, wr, br, tm=512)

    pos_flat, items = _moe_schedule(route, counts, TOP_K * NT)
    ys_sorted = _experts(items, pos_flat, h2, e_wg[0], e_wu[0], e_wd[0])
    y_p, y_s = _combine(pos_flat, h2, route, ys_sorted, ln3_g, ln3_b, Np=Np, Ns=Ns, tm=256)

    kv5 = lambda a: a.reshape(1, Bp, MEM, XA_H, XA_D)
    return (y_p.reshape(Bp, Tp, D_MODEL), y_s.reshape(Bs, Ts, D_MODEL), kv5(mk), kv5(mv),
            C_p[None], n_p[None], m_p.reshape(1, Bp, M_H), conv_p[None], S_p[None],
            C_s[None], n_s[None], m_s.reshape(1, Bs, M_H), conv_s[None], S_s[None])
```

```python
import functools
import math

import jax
import jax.numpy as jnp
from jax import lax
from jax.experimental import pallas as pl
from jax.experimental.pallas import tpu as pltpu

F32 = jnp.float32
BF16 = jnp.bfloat16

D_MODEL = 2048
M_W = 1024
M_H = 4
M_DK = 256
M_DV = 256
CONV_K = 4
HG_W = 1024
HG_H = 8
HG_DK = 128
HG_DV = 128
XA_H = 4
XA_D = 512
N_GROUPS = 4
EXP_PER_GROUP = 8
N_EXPERTS = 32
TOP_K = 2
EXP_FF = 512
DEPTH = 1
ALPHA = (2 * DEPTH) ** 0.25
LN_EPS = 1e-5

COL_QK, COL_V, COL_O = 0, 2048, 3072
COL_QH, COL_FH, COL_IH, COL_GH = 4096, 5120, 6144, 7168
COL_GM, COL_GHH = 8192, 10240
N_MAIN = 12288
GATE_LO = 4 * M_W
LANES = 128
SUBLANES = 8
ROUTE_LANE0 = N_GROUPS

VMEM_LIMIT = 56 << 20
MOE_ROW_TILE = 256
SAMPLE_SEQS_PER_STEP = 4
N_XBUF = 3
DMA_UNROLL = 8
INVERT_GROUP = 16


def _cparams(sem, vmem=VMEM_LIMIT):
    return pltpu.CompilerParams(dimension_semantics=sem, vmem_limit_bytes=vmem)


def _tile(n, pref):
    t = math.gcd(n, pref)
    assert t % SUBLANES == 0, (n, pref)
    return t


def _bdot(a, b):
    return jnp.dot(a.astype(BF16), b.astype(BF16), preferred_element_type=F32)


def _bdot_nt(a, b):
    return lax.dot_general(a.astype(BF16), b.astype(BF16), (((1,), (1,)), ((), ())),
                           preferred_element_type=F32)


def _bdot_tn(a, b):
    return lax.dot_general(a.astype(BF16), b.astype(BF16), (((0,), (0,)), ((), ())),
                           preferred_element_type=F32)


def _sigmoid(x):
    return 1.0 / (1.0 + jnp.exp(-x))


def _cumsum_rows(x, seg=None):
    n = seg or x.shape[0]
    assert n & (n - 1) == 0
    row = lax.broadcasted_iota(jnp.int32, x.shape, 0) & (n - 1)
    d = 1
    while d < n:
        x = x + jnp.where(row >= d, pltpu.roll(x, d, axis=0), 0.0)
        d *= 2
    return x


def _col_to_row(col, eye):
    return jnp.sum(jnp.where(eye, col, 0.0), axis=0, keepdims=True)


def _row_to_col(row, eye):
    return jnp.sum(jnp.where(eye, row, 0.0), axis=1, keepdims=True)


def _layer_norm(x, g, b):
    mu = jnp.mean(x, axis=-1, keepdims=True)
    xc = x - mu
    var = jnp.mean(xc * xc, axis=-1, keepdims=True)
    return xc * lax.rsqrt(var + LN_EPS) * g + b


def _mm_kernel(x_ref, w_ref, b_ref, o_ref, wb_scr):
    @pl.when(pl.program_id(1) == 0)
    def _new_weight_tile():
        wb_scr[...] = w_ref[...].astype(BF16)

    acc = jnp.dot(x_ref[...], wb_scr[...], preferred_element_type=F32)
    o_ref[...] = (acc + b_ref[...]).astype(o_ref.dtype)


def _matmul_bias(x, w, b, *, tm, tn, out_dtype=F32, name):
    M, K = x.shape
    N = w.shape[1]
    tm = _tile(M, tm)
    tn = _tile(N, tn)
    return pl.pallas_call(
        _mm_kernel,
        grid=(N // tn, M // tm),
        in_specs=[pl.BlockSpec((tm, K), lambda j, i: (i, 0)),
                  pl.BlockSpec((K, tn), lambda j, i: (0, j)),
                  pl.BlockSpec((1, tn), lambda j, i: (0, j))],
        out_specs=pl.BlockSpec((tm, tn), lambda j, i: (i, j)),
        out_shape=jax.ShapeDtypeStruct((M, N), out_dtype),
        scratch_shapes=[pltpu.VMEM((K, tn), BF16)],
        compiler_params=_cparams(("parallel", "arbitrary")),
        name=name,
    )(x, w, b)


def _inproj_kernel(xp_ref, xs_ref, wt_ref, b_ref, wg_ref, bg_ref, u_ref, ug_ref, wb_scr, *, n_prompt_tiles):
    i = pl.program_id(1)

    @pl.when(i == 0)
    def _new_weight_tile():
        wb_scr[...] = wt_ref[...].astype(BF16)

    xb = jnp.where(i < n_prompt_tiles, xp_ref[...], xs_ref[...])
    u_ref[...] = _bdot_nt(xb, wb_scr[...]) + b_ref[...]

    @pl.when(pl.program_id(0) == 0)
    def _gates():
        ug_ref[...] = _bdot_nt(xb, wg_ref[...]) + bg_ref[...]


def _in_proj(xb_p, xb_s, w_t, b_main, w_gate, b_gate, *, tm, tn):
    Np, Ns = xb_p.shape[0], xb_s.shape[0]
    NT = Np + Ns
    K = w_t.shape[1]
    N = w_t.shape[0] - 2 * M_H
    tm = _tile(math.gcd(Np, Ns), tm)
    tn = _tile(math.gcd(GATE_LO, N), tn)
    assert (2 * M_H) % SUBLANES == 0
    npt = Np // tm
    na = GATE_LO // tn
    w_rows = lambda j, i: (pl.multiple_of(j * tn + jnp.where(j < na, 0, 2 * M_H), SUBLANES), 0)
    n_i = NT // tm
    gate_rows = lambda j, i: (jnp.where(j == 0, i, n_i - 1), 0)
    return pl.pallas_call(
        functools.partial(_inproj_kernel, n_prompt_tiles=npt),
        grid=(N // tn, n_i),
        in_specs=[pl.BlockSpec((tm, K), lambda j, i: (jnp.minimum(i, npt - 1), 0)),
                  pl.BlockSpec((tm, K), lambda j, i: (jnp.maximum(i - npt, 0), 0)),
                  pl.BlockSpec((pl.Element(tn), pl.Element(K)), w_rows),
                  pl.BlockSpec((1, tn), lambda j, i: (0, j)),
                  pl.BlockSpec((LANES, K), lambda j, i: (0, 0)),
                  pl.BlockSpec((1, LANES), lambda j, i: (0, 0))],
        out_specs=[pl.BlockSpec((tm, tn), lambda j, i: (i, j)),
                   pl.BlockSpec((tm, LANES), gate_rows)],
        out_shape=[jax.ShapeDtypeStruct((NT, N), F32), jax.ShapeDtypeStruct((NT, LANES), F32)],
        scratch_shapes=[pltpu.VMEM((tn, K), BF16)],
        compiler_params=_cparams(("arbitrary", "arbitrary")),
        name="in_proj",
    )(xb_p, xb_s, w_t, b_main, w_gate, b_gate)


def _mlstm_kernel(*refs, L, NC, NB):
    (qk_ref, v_ref, o_ref, g_ref, conv0_ref, C0_ref, n0_ref, m0_ref, cw_ref, gn_ref,
     h_ref, Co_ref, no_ref, mo_ref, convo_ref, C_scr, n_scr, m_scr, tail_scr) = refs
    c = pl.program_id(1)
    if NC > 1:
        @pl.when(c == 0)
        def _init():
            C_scr[...] = C0_ref[0]
            n_scr[...] = n0_ref[0]
            m_scr[...] = m0_ref[0]
            tail_scr[...] = conv0_ref[0]

    cw = cw_ref[...]
    gn = gn_ref[...]
    ti = lax.broadcasted_iota(jnp.int32, (L, L), 0)
    si = lax.broadcasted_iota(jnp.int32, (L, L), 1)
    eye = ti == si
    causal = si <= ti
    head_lane = lax.broadcasted_iota(jnp.int32, (1, M_H), 1)
    head_row = lax.broadcasted_iota(jnp.int32, (M_H, M_DK), 0)

    seqs = []
    for nb in range(NB):
        rows = slice(nb * L, (nb + 1) * L)
        if NC == 1:
            sq = dict(C_in=C0_ref.at[nb], C_out=Co_ref.at[nb], n_out=no_ref.at[nb], m_out=mo_ref.at[nb])
            n_all, m_all, tail = n0_ref[nb], m0_ref[nb], conv0_ref[nb]
        else:
            sq = dict(C_in=C_scr, C_out=C_scr, n_out=n_scr, m_out=m_scr)
            n_all, m_all, tail = n_scr[...], m_scr[...], tail_scr[...]
        qk_pre = qk_ref[rows, :]
        ext = jnp.concatenate([tail, qk_pre], axis=0)
        acc = qk_pre * cw[CONV_K - 1:CONV_K, :]
        for j in range(1, CONV_K):
            acc = acc + pltpu.roll(ext, j, axis=0)[SUBLANES:, :] * cw[CONV_K - 1 - j:CONV_K - j, :]
        if NC > 1:
            tail_scr[...] = qk_pre[L - SUBLANES:, :]
        qk = acc * _sigmoid(acc)
        g = g_ref[rows, :]
        lf_all = jnp.minimum(g, 0.0) - jnp.log(1.0 + jnp.exp(-jnp.abs(g)))
        F_all = _cumsum_rows(lf_all)
        heads = []
        for h in range(M_H):
            ks = slice(h * M_DK, (h + 1) * M_DK)
            q = qk[:, ks] * (M_DK ** -0.5)
            k = qk[:, M_W + h * M_DK:M_W + (h + 1) * M_DK]
            v = v_ref[rows, ks]
            ig = g[:, h:h + 1]
            F = F_all[:, M_H + h:M_H + h + 1]
            m_prev = m_all[:, h:h + 1]
            r_row = _col_to_row(ig - F, eye)
            Dm = jnp.where(causal, F + r_row, -jnp.inf)
            init_w = F + m_prev
            m_t = jnp.maximum(init_w, jnp.max(Dm, axis=1, keepdims=True))
            P = jnp.exp(Dm - m_t)
            a0 = jnp.exp(init_w - m_t)
            FL = F[L - 1:L, :]
            mL = m_t[L - 1:L, :]
            wL = jnp.exp(FL - F + ig - mL)
            decay = jnp.exp(FL + m_prev - mL)
            heads.append(dict(ks=ks, q=q, k=k, v=v, m_t=m_t, P=P, a0=a0, mL=mL, decay=decay, kw=wL * k))
        sq.update(rows=rows, heads=heads, n_all=n_all, m_all=m_all, qk_pre=qk_pre)
        seqs.append(sq)

    for sq in seqs:
        for h, d in enumerate(sq["heads"]):
            d["S"] = _bdot_nt(d["q"], d["k"])
            d["qC"] = _bdot(d["q"], sq["C_in"][h])
            d["kv"] = _bdot_tn(d["kw"], d["v"])

    for sq in seqs:
        for d in sq["heads"]:
            d["Sc"] = d["S"] * d["P"]
            d["num"] = _bdot(d["Sc"], d["v"])

    for nb, sq in enumerate(seqs):
        rows, n_new, m_new = sq["rows"], sq["n_all"], sq["m_all"]
        for h, d in enumerate(sq["heads"]):
            ks, q, a0, decay = d["ks"], d["q"], d["a0"], d["decay"]
            n_row = sq["n_all"][h:h + 1, :]
            num = d["num"] + a0 * d["qC"]
            den = jnp.sum(d["Sc"], axis=1, keepdims=True) + a0 * jnp.sum(q * n_row, axis=1, keepdims=True)
            hh = num * (1.0 / jnp.maximum(jnp.abs(den), jnp.exp(-d["m_t"])))
            sq["C_out"][h] = decay * sq["C_in"][h] + d["kv"]
            n_new = jnp.where(head_row == h, decay * n_row + jnp.sum(d["kw"], axis=0, keepdims=True), n_new)
            m_new = jnp.where(head_lane == h, d["mL"], m_new)

            hm = _sigmoid(o_ref[rows, ks]) * hh
            hm = hm - jnp.mean(hm, axis=1, keepdims=True)
            hm = hm * lax.rsqrt(jnp.mean(hm * hm, axis=1, keepdims=True) + LN_EPS) * gn[:, ks]
            h_ref[rows, ks] = hm
        sq["n_out"][...] = n_new
        sq["m_out"][...] = m_new
        conv_tail = sq["qk_pre"][L - (CONV_K - 1):, :]
        if NC == 1:
            convo_ref[nb] = conv_tail
        else:
            @pl.when(c == NC - 1)
            def _fin():
                Co_ref[0] = C_scr[...]
                no_ref[0] = n_new
                mo_ref[0] = m_new
                convo_ref[0] = conv_tail


def _mlstm(u_main, u_gate, conv0p, C0, n0, m0, conv_w, gn, *, B, T, L, row0, NB=1):
    NC = T // L
    assert NB == 1 or NC == 1
    assert B % NB == 0 and row0 % (NB * L) == 0
    R = NB * L
    rb0 = row0 // R
    tok = lambda b, c: rb0 + b * NC + c
    in_specs = [
        pl.BlockSpec((R, 2 * M_W), lambda b, c: (tok(b, c), COL_QK // (2 * M_W))),
        pl.BlockSpec((R, M_W), lambda b, c: (tok(b, c), COL_V // M_W)),
        pl.BlockSpec((R, M_W), lambda b, c: (tok(b, c), COL_O // M_W)),
        pl.BlockSpec((R, LANES), lambda b, c: (tok(b, c), 0)),
        pl.BlockSpec((NB, SUBLANES, 2 * M_W), lambda b, c: (b, 0, 0)),
        pl.BlockSpec((NB, M_H, M_DK, M_DV), lambda b, c: (b, 0, 0, 0)),
        pl.BlockSpec((NB, M_H, M_DK), lambda b, c: (b, 0, 0)),
        pl.BlockSpec((NB, 1, M_H), lambda b, c: (b, 0, 0)),
        pl.BlockSpec((CONV_K, 2 * M_W), lambda b, c: (0, 0)),
        pl.BlockSpec((1, M_W), lambda b, c: (0, 0)),
    ]
    args = [u_main, u_main, u_main, u_gate, conv0p, C0, n0, m0, conv_w, gn]
    out_specs = [
        pl.BlockSpec((R, M_W), lambda b, c: (b * NC + c, 0)),
        pl.BlockSpec((NB, M_H, M_DK, M_DV), lambda b, c: (b, 0, 0, 0)),
        pl.BlockSpec((NB, M_H, M_DK), lambda b, c: (b, 0, 0)),
        pl.BlockSpec((NB, 1, M_H), lambda b, c: (b, 0, 0)),
        pl.BlockSpec((NB, CONV_K - 1, 2 * M_W), lambda b, c: (b, 0, 0)),
    ]
    out_shape = [
        jax.ShapeDtypeStruct((B * T, M_W), F32),
        jax.ShapeDtypeStruct((B, M_H, M_DK, M_DV), F32),
        jax.ShapeDtypeStruct((B, M_H, M_DK), F32),
        jax.ShapeDtypeStruct((B, 1, M_H), F32),
        jax.ShapeDtypeStruct((B, CONV_K - 1, 2 * M_W), F32),
    ]
    return pl.pallas_call(
        functools.partial(_mlstm_kernel, L=L, NC=NC, NB=NB),
        grid=(B // NB, NC),
        in_specs=in_specs,
        out_specs=out_specs,
        out_shape=out_shape,
        scratch_shapes=[pltpu.VMEM((M_H, M_DK, M_DV), F32), pltpu.VMEM((M_H, M_DK), F32),
                        pltpu.VMEM((1, M_H), F32), pltpu.VMEM((SUBLANES, 2 * M_W), F32)],
        compiler_params=_cparams(("parallel", "arbitrary")),
        name=f"mlstm_L{L}",
    )(*args)


def _hgrn_kernel(*refs, L, NC, NB):
    (q_ref, f_ref, i_ref, g_ref, S0_ref, lb_ref, gn_ref, o_ref, So_ref, S_scr) = refs
    c = pl.program_id(1)
    if NC > 1:
        @pl.when(c == 0)
        def _init():
            S_scr[...] = S0_ref[0]

    lb = lb_ref[...]
    f = lb + (1.0 - lb) * _sigmoid(f_ref[...])
    kk = 1.0 - f
    b = _cumsum_rows(jnp.log(f), seg=L)
    qh = q_ref[...]
    q = qh * _sigmoid(qh)
    v = i_ref[...]
    gh = g_ref[...]
    gsilu = gh * _sigmoid(gh)
    gn = gn_ref[...]
    ti = lax.broadcasted_iota(jnp.int32, (L, L), 0)
    si = lax.broadcasted_iota(jnp.int32, (L, L), 1)
    causal = si <= ti
    ci = lax.broadcasted_iota(jnp.int32, (HG_DK, HG_DK), 0)
    cj = lax.broadcasted_iota(jnp.int32, (HG_DK, HG_DK), 1)
    eye = ci == cj
    mid = max(L // 2 - 1, 0)
    hsl = [slice(h * HG_DK, (h + 1) * HG_DK) for h in range(HG_H)]

    seqs = []
    for nb in range(NB):
        rows = slice(nb * L, (nb + 1) * L)
        bs, qs, ks = b[rows], q[rows], kk[rows]
        bL = bs[L - 1:L, :]
        bm = bs[mid:mid + 1, :]
        S_in, S_out = (S0_ref.at[nb], So_ref.at[nb]) if NC == 1 else (S_scr, S_scr)
        seqs.append(dict(rows=rows, bL=bL, v=v[rows], q_in=qs * jnp.exp(bs), q_t=qs * jnp.exp(bs - bm),
                         k_t=ks * jnp.exp(bm - bs), k_st=ks * jnp.exp(bL - bs), S_in=S_in, S_out=S_out))

    for sq in seqs:
        sq["A"] = [jnp.where(causal, _bdot_nt(sq["q_t"][:, hs], sq["k_t"][:, hs]), 0.0) for hs in hsl]
        sq["qS"] = [_bdot(sq["q_in"][:, hs], sq["S_in"][h]) for h, hs in enumerate(hsl)]
        sq["kv"] = [_bdot_tn(sq["k_st"][:, hs], sq["v"][:, hs]) for hs in hsl]
    for sq in seqs:
        sq["o_intra"] = [_bdot(sq["A"][h], sq["v"][:, hs]) for h, hs in enumerate(hsl)]
    for sq in seqs:
        rows = sq["rows"]
        for h, hs in enumerate(hsl):
            o = sq["o_intra"][h] + sq["qS"][h]
            dec = jnp.exp(_row_to_col(sq["bL"][:, hs], eye))
            sq["S_out"][h] = dec * sq["S_in"][h] + sq["kv"][h]
            o = o * lax.rsqrt(jnp.mean(o * o, axis=1, keepdims=True) + LN_EPS) * gn[:, hs]
            o_ref[rows, hs] = o * gsilu[rows, hs]

    if NC > 1:
        @pl.when(c == NC - 1)
        def _fin():
            So_ref[0] = S_scr[...]


def _hgrn(u_main, S0, lb, gn, *, B, T, L, row0, NB=1):
    NC = T // L
    assert NB == 1 or NC == 1
    assert B % NB == 0 and row0 % (NB * L) == 0
    R = NB * L
    rb0 = row0 // R
    tok = lambda b, c: rb0 + b * NC + c
    col = lambda off: (lambda b, c: (tok(b, c), off // HG_W))
    in_specs = [
        pl.BlockSpec((R, HG_W), col(COL_QH)),
        pl.BlockSpec((R, HG_W), col(COL_FH)),
        pl.BlockSpec((R, HG_W), col(COL_IH)),
        pl.BlockSpec((R, HG_W), col(COL_GH)),
        pl.BlockSpec((NB, HG_H, HG_DK, HG_DV), lambda b, c: (b, 0, 0, 0)),
        pl.BlockSpec((1, HG_W), lambda b, c: (0, 0)),
        pl.BlockSpec((1, HG_W), lambda b, c: (0, 0)),
    ]
    args = [u_main, u_main, u_main, u_main, S0, lb, gn]
    return pl.pallas_call(
        functools.partial(_hgrn_kernel, L=L, NC=NC, NB=NB),
        grid=(B // NB, NC),
        in_specs=in_specs,
        out_specs=[pl.BlockSpec((R, HG_W), lambda b, c: (b * NC + c, 0)),
                   pl.BlockSpec((NB, HG_H, HG_DK, HG_DV), lambda b, c: (b, 0, 0, 0))],
        out_shape=[jax.ShapeDtypeStruct((B * T, HG_W), F32),
                   jax.ShapeDtypeStruct((B, HG_H, HG_DK, HG_DV), F32)],
        scratch_shapes=[pltpu.VMEM((HG_H, HG_DK, HG_DV), F32)],
        compiler_params=_cparams(("parallel", "arbitrary")),
        name=f"hgrn_L{L}",
    )(*args)


def _merge_kernel(hmp_ref, hms_ref, ogp_ref, ogs_ref, gm_ref, gh_ref, xp_ref, xs_ref, wbm_ref, wbh_ref,
                  wo_ref, g_ref, b_ref, h1_ref, h1b_ref, *, n_prompt_tiles):
    is_prompt = pl.program_id(0) < n_prompt_tiles
    a = _bdot(jnp.where(is_prompt, hmp_ref[...], hms_ref[...]), wbm_ref[...])
    bb = _bdot(jnp.where(is_prompt, ogp_ref[...], ogs_ref[...]), wbh_ref[...])
    merged = _sigmoid(gm_ref[...]) * a + _sigmoid(gh_ref[...]) * bb
    mix = _bdot(merged, wo_ref[...])
    x = jnp.where(is_prompt, xp_ref[...], xs_ref[...])
    h1 = _layer_norm(ALPHA * x + mix, g_ref[...], b_ref[...])
    h1_ref[...] = h1
    h1b_ref[...] = h1.astype(BF16)


def _merge(hm_p, hm_s, og_p, og_s, u_main, x_p, x_s, w_bm, w_bh, w_out, ln_g, ln_b, *, tm):
    Np, Ns = x_p.shape[0], x_s.shape[0]
    NT = Np + Ns
    tm = _tile(math.gcd(Np, Ns), tm)
    npt = Np // tm
    const = lambda i: (0, 0)
    prompt_rows = lambda i: (jnp.minimum(i, npt - 1), 0)
    sample_rows = lambda i: (jnp.maximum(i - npt, 0), 0)
    return pl.pallas_call(
        functools.partial(_merge_kernel, n_prompt_tiles=npt),
        grid=(NT // tm,),
        in_specs=[
            pl.BlockSpec((tm, M_W), prompt_rows),
            pl.BlockSpec((tm, M_W), sample_rows),
            pl.BlockSpec((tm, HG_W), prompt_rows),
            pl.BlockSpec((tm, HG_W), sample_rows),
            pl.BlockSpec((tm, D_MODEL), lambda i: (i, COL_GM // D_MODEL)),
            pl.BlockSpec((tm, D_MODEL), lambda i: (i, COL_GHH // D_MODEL)),
            pl.BlockSpec((tm, D_MODEL), prompt_rows),
            pl.BlockSpec((tm, D_MODEL), sample_rows),
            pl.BlockSpec((M_W, D_MODEL), const, pipeline_mode=pl.Buffered(1)),
            pl.BlockSpec((HG_W, D_MODEL), const, pipeline_mode=pl.Buffered(1)),
            pl.BlockSpec((D_MODEL, D_MODEL), const, pipeline_mode=pl.Buffered(1)),
            pl.BlockSpec((1, D_MODEL), const),
            pl.BlockSpec((1, D_MODEL), const),
        ],
        out_specs=[pl.BlockSpec((tm, D_MODEL), lambda i: (i, 0)),
                   pl.BlockSpec((tm, D_MODEL), lambda i: (i, 0))],
        out_shape=[jax.ShapeDtypeStruct((NT, D_MODEL), F32),
                   jax.ShapeDtypeStruct((NT, D_MODEL), BF16)],
        compiler_params=_cparams(("parallel",)),
        name="merge_ln1",
    )(hm_p, hm_s, og_p, og_s, u_main, u_main, x_p, x_s, w_bm, w_bh, w_out, ln_g, ln_b)


def _attn_kernel(q_ref, *refs):
    k_refs, v_refs, o_ref = refs[:XA_H], refs[XA_H:2 * XA_H], refs[2 * XA_H]
    for h in range(XA_H):
        sl = slice(h * XA_D, (h + 1) * XA_D)
        s = _bdot_nt(q_ref[:, sl], k_refs[h][...]) * (XA_D ** -0.5)
        e = jnp.exp(s - jnp.max(s, axis=1, keepdims=True))
        p = e / jnp.sum(e, axis=1, keepdims=True)
        o_ref[:, sl] = _bdot(p, v_refs[h][...])


def _attn_cache_kernel(q_ref, k_ref, v_ref, o_ref):
    T = q_ref.shape[0]
    M = k_ref.shape[2]
    k_all = k_ref[0, 0].reshape(M * XA_H, XA_D)
    v_all = v_ref[0, 0].reshape(M * XA_H, XA_D)
    q_all = jnp.concatenate([q_ref[:, h * XA_D:(h + 1) * XA_D] for h in range(XA_H)], axis=0)
    s = _bdot_nt(q_all, k_all) * (XA_D ** -0.5)
    q_head = lax.broadcasted_iota(jnp.int32, s.shape, 0) // T
    k_head = lax.broadcasted_iota(jnp.int32, s.shape, 1) % XA_H
    s = jnp.where(q_head == k_head, s, -jnp.inf)
    e = jnp.exp(s - jnp.max(s, axis=1, keepdims=True))
    p = e / jnp.sum(e, axis=1, keepdims=True)
    o = _bdot(p, v_all)
    for h in range(XA_H):
        o_ref[:, h * XA_D:(h + 1) * XA_D] = o[h * T:(h + 1) * T]


def _attention_cache(q, cache_k, cache_v, *, B, T, row0):
    M = cache_k.shape[2]
    rb0 = row0 // T
    kv_spec = pl.BlockSpec((1, 1, M, XA_H, XA_D), lambda b: (0, b, 0, 0, 0))
    return pl.pallas_call(
        _attn_cache_kernel,
        grid=(B,),
        in_specs=[pl.BlockSpec((T, D_MODEL), lambda b: (rb0 + b, 0)), kv_spec, kv_spec],
        out_specs=pl.BlockSpec((T, D_MODEL), lambda b: (b, 0)),
        out_shape=jax.ShapeDtypeStruct((B * T, D_MODEL), F32),
        compiler_params=_cparams(("parallel",)),
        name="xattn_cache",
    )(q, cache_k, cache_v)


def _attention(q, mem_k, mem_v, *, B, T, tq, row0):
    tq = _tile(T, tq)
    nq = T // tq
    rb0 = row0 // tq
    M = mem_k.shape[0] // B
    kv_specs = [pl.BlockSpec((M, XA_D), lambda b, t, h=h: (b, h)) for h in range(XA_H)]
    return pl.pallas_call(
        _attn_kernel,
        grid=(B, nq),
        in_specs=[pl.BlockSpec((tq, D_MODEL), lambda b, t: (rb0 + b * nq + t, 0))] + kv_specs + kv_specs,
        out_specs=pl.BlockSpec((tq, D_MODEL), lambda b, t: (b * nq + t, 0)),
        out_shape=jax.ShapeDtypeStruct((B * T, D_MODEL), F32),
        compiler_params=_cparams(("parallel", "parallel")),
        name=f"xattn_T{T}",
    )(q, *([mem_k] * XA_H), *([mem_v] * XA_H))


def _oln_kernel(op_ref, os_ref, h1_ref, wo_ref, g_ref, b_ref, wr_ref, br_ref,
                h2_ref, route_ref, cnt_ref, carry_scr, *, n_prompt_tiles, n_sub):
    i = pl.program_id(0)

    @pl.when(i == 0)
    def _init():
        carry_scr[...] = jnp.zeros_like(carry_scr)

    ts = h1_ref.shape[0] // n_sub
    lane = lax.broadcasted_iota(jnp.int32, (ts, LANES), 1)
    lane_f = lane.astype(F32)
    neg = -jnp.inf
    ti = lax.broadcasted_iota(jnp.int32, (ts, ts), 0)
    si = lax.broadcasted_iota(jnp.int32, (ts, ts), 1)
    tri = jnp.where(si <= ti, 1.0, 0.0).astype(BF16)
    is_prompt = i < n_prompt_tiles

    def first_argmax(vals):
        mx = jnp.max(vals, axis=1, keepdims=True)
        idx = jnp.min(jnp.where(vals == mx, lane_f, float(LANES)), axis=1, keepdims=True)
        return mx, idx.astype(jnp.int32)

    carry = carry_scr[...]
    for sb in range(n_sub):
        rows = slice(sb * ts, (sb + 1) * ts)
        att = jnp.where(is_prompt, op_ref[rows, :], os_ref[rows, :])
        h2 = _layer_norm(ALPHA * h1_ref[rows, :] + _bdot(att, wo_ref[...]), g_ref[...], b_ref[...])
        h2_ref[rows, :] = h2

        logits = _bdot(h2, wr_ref[...]) + br_ref[...]
        gl = jnp.where(lane < N_GROUPS, logits, neg)
        gmax, grp = first_argmax(gl)
        p_grp = 1.0 / jnp.sum(jnp.exp(gl - gmax), axis=1, keepdims=True)
        lo = ROUTE_LANE0 + grp * EXP_PER_GROUP
        el = jnp.where((lane >= lo) & (lane < lo + EXP_PER_GROUP), logits, neg)
        v1, i1 = first_argmax(el)
        v2, i2 = first_argmax(jnp.where(lane == i1, neg, el))
        e21 = jnp.exp(v2 - v1)
        w1 = p_grp / (1.0 + e21)
        w2 = p_grp * e21 / (1.0 + e21)

        pick1 = lane == i1
        pick2 = lane == i2
        onehot = jnp.where(pick1 | pick2, 1.0, 0.0)
        cnt = _bdot(tri, onehot) + carry
        r1 = jnp.sum(jnp.where(pick1, cnt, 0.0), axis=1, keepdims=True) - 1.0
        r2 = jnp.sum(jnp.where(pick2, cnt, 0.0), axis=1, keepdims=True) - 1.0
        carry = cnt[ts - 1:ts, :]

        e1 = (i1 - ROUTE_LANE0).astype(F32)
        e2 = (i2 - ROUTE_LANE0).astype(F32)
        packed = jnp.zeros_like(logits)
        for idx, val in enumerate((e1, e2, w1, w2, r1, r2)):
            packed = jnp.where(lane == idx, val, packed)
        route_ref[rows, :] = packed

    carry_scr[...] = carry
    cnt_ref[...] = jnp.broadcast_to(carry, cnt_ref.shape)


def _oln(o_p, o_s, h1, wo, ln_g, ln_b, wr, br, *, tm):
    Np, Ns = o_p.shape[0], o_s.shape[0]
    NT = Np + Ns
    tm = _tile(math.gcd(Np, Ns), tm)
    npt = Np // tm
    const = lambda i: (0, 0)
    return pl.pallas_call(
        functools.partial(_oln_kernel, n_prompt_tiles=npt, n_sub=2 if tm % (2 * SUBLANES) == 0 else 1),
        grid=(NT // tm,),
        in_specs=[pl.BlockSpec((tm, D_MODEL), lambda i: (jnp.minimum(i, npt - 1), 0)),
                  pl.BlockSpec((tm, D_MODEL), lambda i: (jnp.maximum(i - npt, 0), 0)),
                  pl.BlockSpec((tm, D_MODEL), lambda i: (i, 0)),
                  pl.BlockSpec((D_MODEL, D_MODEL), const, pipeline_mode=pl.Buffered(1)),
                  pl.BlockSpec((1, D_MODEL), const),
                  pl.BlockSpec((1, D_MODEL), const),
                  pl.BlockSpec((D_MODEL, LANES), const),
                  pl.BlockSpec((1, LANES), const)],
        out_specs=[pl.BlockSpec((tm, D_MODEL), lambda i: (i, 0)),
                   pl.BlockSpec((tm, LANES), lambda i: (i, 0)),
                   pl.BlockSpec((SUBLANES, LANES), const)],
        out_shape=[jax.ShapeDtypeStruct((NT, D_MODEL), F32),
                   jax.ShapeDtypeStruct((NT, LANES), F32),
                   jax.ShapeDtypeStruct((SUBLANES, LANES), F32)],
        scratch_shapes=[pltpu.VMEM((1, LANES), F32)],
        compiler_params=_cparams(("arbitrary",)),
        name="oproj_ln2_router",
    )(o_p, o_s, h1, wo, ln_g, ln_b, wr, br)


def _row_copy(src_ref, src_row, dst_ref, dst_row, sem):
    return pltpu.make_async_copy(src_ref.at[pl.ds(src_row, 1), :], dst_ref.at[pl.ds(dst_row, 1), :], sem)


def _expert_weight_copies(e, slot, w_hbm, w_f32, sem):
    return [pltpu.make_async_copy(src.at[e], dst.at[slot], sem.at[slot, j])
            for j, (src, dst) in enumerate(zip(w_hbm, w_f32))]


def _expert_kernel(it_ref, ie_ref, lo_ref, hi_ref, first_ref, n_ref, ord_ref, nxt_ref, pos_ref,
                   h2_hbm, wg_hbm, wu_hbm, wd_hbm, y_ref,
                   xbuf, xsem, wg_f, wu_f, wd_f, wsem, wg_s, wu_s, wd_s, tok_ref):
    i = pl.program_id(0)
    n = n_ref[0]
    TR = xbuf.shape[1]
    w_hbm = (wg_hbm, wu_hbm, wd_hbm)
    w_f32 = (wg_f, wu_f, wd_f)

    def gather(base, slot, r):
        return _row_copy(h2_hbm, tok_ref[base + r], xbuf.at[slot], r, xsem.at[slot])

    def wait_rows(slot):
        def body(r, carry):
            gather(0, slot, r).wait()
            return carry
        lax.fori_loop(0, TR, body, 0, unroll=DMA_UNROLL)

    @pl.when(i == 0)
    def _prologue():
        for c in _expert_weight_copies(ie_ref[0], 0, w_hbm, w_f32, wsem):
            c.start()

        def invert(c, carry):
            p0 = pl.multiple_of(c * INVERT_GROUP, INVERT_GROUP)
            t0 = c * (INVERT_GROUP // TOP_K)
            rows = [pos_ref[p0 + u] for u in range(INVERT_GROUP)]
            for u in range(INVERT_GROUP):
                tok_ref[rows[u]] = t0 + u // TOP_K
            return carry
        lax.fori_loop(0, pos_ref.shape[0] // INVERT_GROUP, invert, 0)

        for ahead in range(N_XBUF - 1):
            base = it_ref[jnp.minimum(ahead, n - 1)] * TR

            def body(r, carry, base=base, ahead=ahead):
                gather(base, ahead, r).start()
                return carry
            lax.fori_loop(0, TR, body, 0, unroll=DMA_UNROLL)

    @pl.when(i < n)
    def _compute():
        slot = i % N_XBUF
        fill = (i + N_XBUF - 1) % N_XBUF
        wslot = ord_ref[i] % 2
        fresh = jnp.logical_or(i == 0, ie_ref[i] != ie_ref[jnp.maximum(i - 1, 0)])

        @pl.when(fresh)
        def _new_expert():
            for c in _expert_weight_copies(ie_ref[i], wslot, w_hbm, w_f32, wsem):
                c.wait()
            wg_s[...] = wg_f[wslot].astype(BF16)
            wu_s[...] = wu_f[wslot].astype(BF16)
            wd_s[...] = wd_f[wslot].astype(BF16)

            @pl.when(nxt_ref[i] >= 0)
            def _prefetch():
                for c in _expert_weight_copies(nxt_ref[i], 1 - wslot, w_hbm, w_f32, wsem):
                    c.start()

        wait_rows(slot)
        x = xbuf[slot].astype(BF16)
        base_next = it_ref[jnp.minimum(i + N_XBUF - 1, n - 1)] * TR
        for r in range(TR):
            gather(base_next, fill, r).start()
        gate = jnp.dot(x, wg_s[...], preferred_element_type=F32)
        up = jnp.dot(x, wu_s[...], preferred_element_type=F32)
        hid = gate * _sigmoid(gate) * up
        y = jnp.dot(hid.astype(BF16), wd_s[...], preferred_element_type=F32)
        rows = it_ref[i] * TR + lax.broadcasted_iota(jnp.int32, (TR, 1), 0)
        mine = (rows >= lo_ref[i]) & (rows < hi_ref[i])

        @pl.when(first_ref[i] == 1)
        def _first():
            y_ref[...] = jnp.where(mine, y, 0.0)

        @pl.when(first_ref[i] == 0)
        def _later():
            y_ref[...] = jnp.where(mine, y, y_ref[...])

        @pl.when(i == n - 1)
        def _drain():
            for ahead in range(1, N_XBUF):
                wait_rows((i + ahead) % N_XBUF)


def _experts(items, pos_flat, h2, e_wg, e_wu, e_wd):
    TR = MOE_ROW_TILE
    n_work = items[0].shape[0]
    n_rows = pos_flat.shape[0]
    assert n_rows % INVERT_GROUP == 0 and INVERT_GROUP % TOP_K == 0
    any_spec = pl.BlockSpec(memory_space=pl.ANY)
    return pl.pallas_call(
        _expert_kernel,
        grid_spec=pltpu.PrefetchScalarGridSpec(
            num_scalar_prefetch=9,
            grid=(n_work,),
            in_specs=[any_spec, any_spec, any_spec, any_spec],
            out_specs=pl.BlockSpec((TR, D_MODEL), lambda i, it, *_: (it[i], 0)),
            scratch_shapes=[pltpu.VMEM((N_XBUF, TR, D_MODEL), F32), pltpu.SemaphoreType.DMA((N_XBUF,)),
                            pltpu.VMEM((2, D_MODEL, EXP_FF), F32), pltpu.VMEM((2, D_MODEL, EXP_FF), F32),
                            pltpu.VMEM((2, EXP_FF, D_MODEL), F32), pltpu.SemaphoreType.DMA((2, 3)),
                            pltpu.VMEM((D_MODEL, EXP_FF), BF16), pltpu.VMEM((D_MODEL, EXP_FF), BF16),
                            pltpu.VMEM((EXP_FF, D_MODEL), BF16), pltpu.SMEM((n_rows,), jnp.int32)],
        ),
        out_shape=jax.ShapeDtypeStruct((n_rows, D_MODEL), F32),
        compiler_params=_cparams(("arbitrary",)),
        name="moe_experts",
    )(*items, pos_flat, h2, e_wg, e_wu, e_wd)


def _moe_schedule(route, counts, n_rows):
    TR = MOE_ROW_TILE
    assert n_rows % TR == 0
    n_work = n_rows // TR + N_EXPERTS - 1
    rt = route[:, :SUBLANES].T
    cnt = counts[0, ROUTE_LANE0:ROUTE_LANE0 + N_EXPERTS].astype(jnp.int32)
    g_end = jnp.cumsum(cnt)
    g_start = g_end - cnt
    onehot = rt[0:2, :, None] == jnp.arange(N_EXPERTS, dtype=F32)
    start_of = jnp.sum(jnp.where(onehot, g_start.astype(F32), 0.0), axis=-1)
    pos_flat = (start_of + rt[4:6]).astype(jnp.int32).T.reshape(-1)
    first_tile = g_start // TR
    n_items_e = jnp.where(cnt > 0, (g_end - 1) // TR - first_tile + 1, 0)
    item_end = jnp.cumsum(n_items_e)
    n_items = item_end[-1:]
    idx = jnp.minimum(jnp.arange(n_work, dtype=jnp.int32), n_items[0] - 1)
    ie = jnp.minimum(jnp.sum((item_end[None, :] <= idx[:, None]).astype(jnp.int32), axis=1), N_EXPERTS - 1)
    it = first_tile[ie] + idx - (item_end - n_items_e)[ie]
    first = jnp.concatenate([jnp.ones((1,), jnp.int32), (it[1:] != it[:-1]).astype(jnp.int32)])
    has = n_items_e > 0
    e_ids = jnp.arange(N_EXPERTS, dtype=jnp.int32)
    ordinal = jnp.cumsum(has.astype(jnp.int32)) - 1
    later = (e_ids[None, :] > e_ids[:, None]) & has[None, :]
    nxt = jnp.min(jnp.where(later, e_ids[None, :], N_EXPERTS), axis=1)
    nxt = jnp.where(nxt == N_EXPERTS, -1, nxt)
    items = (it.astype(jnp.int32), ie, g_start[ie], g_end[ie], first, n_items.astype(jnp.int32),
             ordinal[ie], nxt[ie])
    return pos_flat, items


def _combine_kernel(pos_ref, h2_ref, route_ref, ys_ref, g_ref, b_ref, yp_ref, ysm_ref,
                    buf, sem, *, tm, n_prompt_tiles):
    i = pl.program_id(0)
    last = pl.num_programs(0) - 1
    slot = i % 2

    def fetch(tile, slot, r, k):
        return _row_copy(ys_ref, pos_ref[TOP_K * (tile * tm + r) + k], buf.at[slot, k], r, sem.at[slot])

    def wait_rows(slot):
        def body(r, carry):
            for k in range(TOP_K):
                fetch(0, slot, r, k).wait()
            return carry
        lax.fori_loop(0, tm, body, 0, unroll=DMA_UNROLL)

    @pl.when(i == 0)
    def _prologue():
        def body(r, carry):
            for k in range(TOP_K):
                fetch(0, 0, r, k).start()
            return carry
        lax.fori_loop(0, tm, body, 0, unroll=DMA_UNROLL)

    wait_rows(slot)
    nxt = jnp.minimum(i + 1, last)
    for r in range(tm):
        for k in range(TOP_K):
            fetch(nxt, 1 - slot, r, k).start()
    route = route_ref[...]
    moe = route[:, 2:3] * buf[slot, 0] + route[:, 3:4] * buf[slot, 1]
    y = _layer_norm(ALPHA * h2_ref[...] + moe, g_ref[...], b_ref[...])

    @pl.when(i == last)
    def _drain():
        wait_rows(1 - slot)

    @pl.when(i < n_prompt_tiles)
    def _prompt():
        yp_ref[...] = y

    @pl.when(i >= n_prompt_tiles)
    def _sample():
        ysm_ref[...] = y


def _combine(pos_flat, h2, route, ys, ln_g, ln_b, *, Np, Ns, tm):
    NT = h2.shape[0]
    tm = _tile(math.gcd(Np, Ns), tm)
    npt = Np // tm
    const = lambda i, pos: (0, 0)
    return pl.pallas_call(
        functools.partial(_combine_kernel, tm=tm, n_prompt_tiles=npt),
        grid_spec=pltpu.PrefetchScalarGridSpec(
            num_scalar_prefetch=1,
            grid=(NT // tm,),
            in_specs=[pl.BlockSpec((tm, D_MODEL), lambda i, pos: (i, 0)),
                      pl.BlockSpec((tm, LANES), lambda i, pos: (i, 0)),
                      pl.BlockSpec(memory_space=pl.ANY),
                      pl.BlockSpec((1, D_MODEL), const),
                      pl.BlockSpec((1, D_MODEL), const)],
            out_specs=[pl.BlockSpec((tm, D_MODEL), lambda i, pos: (jnp.minimum(i, npt - 1), 0)),
                       pl.BlockSpec((tm, D_MODEL), lambda i, pos: (jnp.maximum(i - npt, 0), 0))],
            scratch_shapes=[pltpu.VMEM((2, TOP_K, tm, D_MODEL), F32), pltpu.SemaphoreType.DMA((2,))],
        ),
        out_shape=[jax.ShapeDtypeStruct((Np, D_MODEL), F32),
                   jax.ShapeDtypeStruct((Ns, D_MODEL), F32)],
        compiler_params=_cparams(("arbitrary",)),
        name="moe_combine_ln3",
    )(pos_flat, h2, route, ys, ln_g, ln_b)


def kernel(x_prompt, x_sample, mem_prompt, cache_mem_k, cache_mem_v, state_mlstm_C, state_mlstm_n,
           state_mlstm_m, state_mlstm_conv, state_hgrn_S, w_in, b_in, conv_w, mlstm_gn, lb_logits, hgrn_gn,
           w_bm, w_bh, w_out, ln1_g, ln1_b, xa_wq, xa_wk, xa_wv, xa_wo, ln2_g, ln2_b,
           r1_w, r1_b, r2_w, r2_b, e_wg, e_wu, e_wd, ln3_g, ln3_b):
    Bp, Tp, _ = x_prompt.shape
    Bs, Ts, _ = x_sample.shape
    MEM = mem_prompt.shape[1]
    Np, Ns = Bp * Tp, Bs * Ts
    NT = Np + Ns
    Lp_m = math.gcd(Tp, 256)
    Lp_h = math.gcd(Tp, 64)
    Ls = Ts
    assert Ts % SUBLANES == 0 and Np % Ls == 0

    xp2 = x_prompt.reshape(Np, D_MODEL)
    xs2 = x_sample.reshape(Ns, D_MODEL)

    w = w_in[0]
    bi = b_in[0]
    w_t = jnp.swapaxes(w, 0, 1)
    xb_p = xp2.astype(BF16)
    xb_s = xs2.astype(BF16)
    b_main = jnp.concatenate([bi[:GATE_LO], bi[GATE_LO + 2 * M_H:]])[None]
    w_gate_t = lax.slice(w_t, (GATE_LO, 0), (GATE_LO + 2 * M_H, D_MODEL))
    w_gate = jnp.pad(w_gate_t, ((0, LANES - 2 * M_H), (0, 0)))
    b_gate = jnp.pad(bi[GATE_LO:GATE_LO + 2 * M_H], (0, LANES - 2 * M_H))[None]
    lb = jnp.cumsum(jax.nn.softmax(lb_logits.astype(F32), axis=0), axis=0)[0][None]
    wr = jnp.zeros((D_MODEL, LANES), F32)
    wr = wr.at[:, :N_GROUPS].set(r1_w[0])
    wr = wr.at[:, ROUTE_LANE0:ROUTE_LANE0 + N_EXPERTS].set(
        jnp.transpose(r2_w[0], (1, 0, 2)).reshape(D_MODEL, N_EXPERTS)).astype(BF16)
    br = jnp.zeros((1, LANES), F32)
    br = br.at[0, :N_GROUPS].set(r1_b[0])
    br = br.at[0, ROUTE_LANE0:ROUTE_LANE0 + N_EXPERTS].set(r2_b[0].reshape(N_EXPERTS))
    zeros_d = jnp.zeros((1, D_MODEL), F32)

    memb = mem_prompt.reshape(Bp * MEM, D_MODEL).astype(BF16)
    mk = _matmul_bias(memb, xa_wk[0], zeros_d, tm=1024, tn=1024, name="mem_k")
    mv = _matmul_bias(memb, xa_wv[0], zeros_d, tm=1024, tn=1024, name="mem_v")

    u_main, u_gate = _in_proj(xb_p, xb_s, w_t, b_main, w_gate, b_gate, tm=1024, tn=1024)

    padc = lambda c: jnp.pad(c, ((0, 0), (SUBLANES - (CONV_K - 1), 0), (0, 0)))
    zC = jnp.zeros((Bp, M_H, M_DK, M_DV), F32)
    zn = jnp.zeros((Bp, M_H, M_DK), F32)
    zm = jnp.zeros((Bp, 1, M_H), F32)
    zconv = jnp.zeros((Bp, SUBLANES, 2 * M_W), F32)
    zS = jnp.zeros((Bp, HG_H, HG_DK, HG_DV), F32)
    cw = conv_w[0]
    mgn = mlstm_gn[0][None]
    hgn = hgrn_gn[0][None]
    hm_p, C_p, n_p, m_p, conv_p = _mlstm(u_main, u_gate, zconv, zC, zn, zm, cw, mgn,
                                         B=Bp, T=Tp, L=Lp_m, row0=0)
    hm_s, C_s, n_s, m_s, conv_s = _mlstm(u_main, u_gate, padc(state_mlstm_conv[0]), state_mlstm_C[0],
                                         state_mlstm_n[0], state_mlstm_m[0][:, None, :], cw, mgn,
                                         B=Bs, T=Ts, L=Ls, row0=Np, NB=SAMPLE_SEQS_PER_STEP)
    og_p, S_p = _hgrn(u_main, zS, lb, hgn, B=Bp, T=Tp, L=Lp_h, row0=0)
    og_s, S_s = _hgrn(u_main, state_hgrn_S[0], lb, hgn, B=Bs, T=Ts, L=Ls, row0=Np, NB=SAMPLE_SEQS_PER_STEP)

    h1, h1b = _merge(hm_p, hm_s, og_p, og_s, u_main, xp2, xs2, w_bm[0].astype(BF16), w_bh[0].astype(BF16),
                     w_out[0].astype(BF16), ln1_g, ln1_b, tm=256)

    q = _matmul_bias(h1b, xa_wq[0], zeros_d, tm=1024, tn=1024, name="xa_q")
    att_p = _attention(q, mk, mv, B=Bp, T=Tp, tq=1024, row0=0)
    att_s = _attention_cache(q, cache_mem_k, cache_mem_v, B=Bs, T=Ts, row0=Np)
    h2, route, counts = _oln(att_p, att_s, h1, xa_wo[0].astype(BF16), ln2_g, ln2_b, wr, br, tm=512)

    pos_flat, items = _moe_schedule(route, counts, TOP_K * NT)
    ys_sorted = _experts(items, pos_flat, h2, e_wg[0], e_wu[0], e_wd[0])
    y_p, y_s = _combine(pos_flat, h2, route, ys_sorted, ln3_g, ln3_b, Np=Np, Ns=Ns, tm=256)

    kv5 = lambda a: a.reshape(1, Bp, MEM, XA_H, XA_D)
    return (y_p.reshape(Bp, Tp, D_MODEL), y_s.reshape(Bs, Ts, D_MODEL), kv5(mk), kv5(mv),
            C_p[None], n_p[None], m_p.reshape(1, Bp, M_H), conv_p[None], S_p[None],
            C_s[None], n_s[None], m_s.reshape(1, Bs, M_H), conv_s[None], S_s[None])
```

```python
import functools
import math

import jax
import jax.numpy as jnp
from jax import lax
from jax.experimental import pallas as pl
from jax.experimental.pallas import tpu as pltpu

F32 = jnp.float32
BF16 = jnp.bfloat16

D_MODEL = 2048
M_W = 1024
M_H = 4
M_DK = 256
M_DV = 256
CONV_K = 4
HG_W = 1024
HG_H = 8
HG_DK = 128
HG_DV = 128
XA_H = 4
XA_D = 512
N_GROUPS = 4
EXP_PER_GROUP = 8
N_EXPERTS = 32
TOP_K = 2
EXP_FF = 512
DEPTH = 1
ALPHA = (2 * DEPTH) ** 0.25
LN_EPS = 1e-5

COL_QK, COL_V, COL_O = 0, 2048, 3072
COL_QH, COL_FH, COL_IH, COL_GH = 4096, 5120, 6144, 7168
COL_GM, COL_GHH = 8192, 10240
N_MAIN = 12288
GATE_LO = 4 * M_W
LANES = 128
SUBLANES = 8
ROUTE_LANE0 = N_GROUPS

VMEM_LIMIT = 56 << 20
MOE_ROW_TILE = 256
SAMPLE_SEQS_PER_STEP = 8
N_XBUF = 3
DMA_UNROLL = 8
INVERT_GROUP = 16


def _cparams(sem, vmem=VMEM_LIMIT):
    return pltpu.CompilerParams(dimension_semantics=sem, vmem_limit_bytes=vmem)


def _tile(n, pref):
    t = math.gcd(n, pref)
    assert t % SUBLANES == 0, (n, pref)
    return t


def _bdot(a, b):
    return jnp.dot(a.astype(BF16), b.astype(BF16), preferred_element_type=F32)


def _bdot_nt(a, b):
    return lax.dot_general(a.astype(BF16), b.astype(BF16), (((1,), (1,)), ((), ())),
                           preferred_element_type=F32)


def _bdot_tn(a, b):
    return lax.dot_general(a.astype(BF16), b.astype(BF16), (((0,), (0,)), ((), ())),
                           preferred_element_type=F32)


def _sigmoid(x):
    return 1.0 / (1.0 + jnp.exp(-x))


def _cumsum_rows(x, seg=None):
    n = seg or x.shape[0]
    assert n & (n - 1) == 0
    row = lax.broadcasted_iota(jnp.int32, x.shape, 0) & (n - 1)
    d = 1
    while d < n:
        x = x + jnp.where(row >= d, pltpu.roll(x, d, axis=0), 0.0)
        d *= 2
    return x


def _col_to_row(col, eye):
    return jnp.sum(jnp.where(eye, col, 0.0), axis=0, keepdims=True)


def _row_to_col(row, eye):
    return jnp.sum(jnp.where(eye, row, 0.0), axis=1, keepdims=True)


def _layer_norm(x, g, b):
    mu = jnp.mean(x, axis=-1, keepdims=True)
    xc = x - mu
    var = jnp.mean(xc * xc, axis=-1, keepdims=True)
    return xc * lax.rsqrt(var + LN_EPS) * g + b


def _mm_kernel(x_ref, w_ref, b_ref, o_ref, wb_scr):
    @pl.when(pl.program_id(1) == 0)
    def _new_weight_tile():
        wb_scr[...] = w_ref[...].astype(BF16)

    acc = jnp.dot(x_ref[...], wb_scr[...], preferred_element_type=F32)
    o_ref[...] = (acc + b_ref[...]).astype(o_ref.dtype)


def _matmul_bias(x, w, b, *, tm, tn, out_dtype=F32, name):
    M, K = x.shape
    N = w.shape[1]
    tm = _tile(M, tm)
    tn = _tile(N, tn)
    return pl.pallas_call(
        _mm_kernel,
        grid=(N // tn, M // tm),
        in_specs=[pl.BlockSpec((tm, K), lambda j, i: (i, 0)),
                  pl.BlockSpec((K, tn), lambda j, i: (0, j)),
                  pl.BlockSpec((1, tn), lambda j, i: (0, j))],
        out_specs=pl.BlockSpec((tm, tn), lambda j, i: (i, j)),
        out_shape=jax.ShapeDtypeStruct((M, N), out_dtype),
        scratch_shapes=[pltpu.VMEM((K, tn), BF16)],
        compiler_params=_cparams(("parallel", "arbitrary")),
        name=name,
    )(x, w, b)


def _inproj_kernel(xp_ref, xs_ref, wt_ref, b_ref, wg_ref, bg_ref, u_ref, ug_ref, wb_scr, *, n_prompt_tiles):
    i = pl.program_id(1)

    @pl.when(i == 0)
    def _new_weight_tile():
        wb_scr[...] = wt_ref[...].astype(BF16)

    xb = jnp.where(i < n_prompt_tiles, xp_ref[...], xs_ref[...])
    u_ref[...] = _bdot_nt(xb, wb_scr[...]) + b_ref[...]

    @pl.when(pl.program_id(0) == 0)
    def _gates():
        ug_ref[...] = _bdot_nt(xb, wg_ref[...]) + bg_ref[...]


def _in_proj(xb_p, xb_s, w_t, b_main, w_gate, b_gate, *, tm, tn):
    Np, Ns = xb_p.shape[0], xb_s.shape[0]
    NT = Np + Ns
    K = w_t.shape[1]
    N = w_t.shape[0] - 2 * M_H
    tm = _tile(math.gcd(Np, Ns), tm)
    tn = _tile(math.gcd(GATE_LO, N), tn)
    assert (2 * M_H) % SUBLANES == 0
    npt = Np // tm
    na = GATE_LO // tn
    w_rows = lambda j, i: (pl.multiple_of(j * tn + jnp.where(j < na, 0, 2 * M_H), SUBLANES), 0)
    n_i = NT // tm
    gate_rows = lambda j, i: (jnp.where(j == 0, i, n_i - 1), 0)
    return pl.pallas_call(
        functools.partial(_inproj_kernel, n_prompt_tiles=npt),
        grid=(N // tn, n_i),
        in_specs=[pl.BlockSpec((tm, K), lambda j, i: (jnp.minimum(i, npt - 1), 0)),
                  pl.BlockSpec((tm, K), lambda j, i: (jnp.maximum(i - npt, 0), 0)),
                  pl.BlockSpec((pl.Element(tn), pl.Element(K)), w_rows),
                  pl.BlockSpec((1, tn), lambda j, i: (0, j)),
                  pl.BlockSpec((LANES, K), lambda j, i: (0, 0)),
                  pl.BlockSpec((1, LANES), lambda j, i: (0, 0))],
        out_specs=[pl.BlockSpec((tm, tn), lambda j, i: (i, j)),
                   pl.BlockSpec((tm, LANES), gate_rows)],
        out_shape=[jax.ShapeDtypeStruct((NT, N), F32), jax.ShapeDtypeStruct((NT, LANES), F32)],
        scratch_shapes=[pltpu.VMEM((tn, K), BF16)],
        compiler_params=_cparams(("arbitrary", "arbitrary")),
        name="in_proj",
    )(xb_p, xb_s, w_t, b_main, w_gate, b_gate)


def _mlstm_kernel(*refs, L, NC, NB):
    (qk_ref, v_ref, o_ref, g_ref, conv0_ref, C0_ref, n0_ref, m0_ref, cw_ref, gn_ref,
     h_ref, Co_ref, no_ref, mo_ref, convo_ref, C_scr, n_scr, m_scr, tail_scr) = refs
    c = pl.program_id(1)
    if NC > 1:
        @pl.when(c == 0)
        def _init():
            C_scr[...] = C0_ref[0]
            n_scr[...] = n0_ref[0]
            m_scr[...] = m0_ref[0]
            tail_scr[...] = conv0_ref[0]

    cw = cw_ref[...]
    gn = gn_ref[...]
    ti = lax.broadcasted_iota(jnp.int32, (L, L), 0)
    si = lax.broadcasted_iota(jnp.int32, (L, L), 1)
    eye = ti == si
    causal = si <= ti
    head_lane = lax.broadcasted_iota(jnp.int32, (1, M_H), 1)
    head_row = lax.broadcasted_iota(jnp.int32, (M_H, M_DK), 0)

    seqs = []
    for nb in range(NB):
        rows = slice(nb * L, (nb + 1) * L)
        if NC == 1:
            sq = dict(C_in=C0_ref.at[nb], C_out=Co_ref.at[nb], n_out=no_ref.at[nb], m_out=mo_ref.at[nb])
            n_all, m_all, tail = n0_ref[nb], m0_ref[nb], conv0_ref[nb]
        else:
            sq = dict(C_in=C_scr, C_out=C_scr, n_out=n_scr, m_out=m_scr)
            n_all, m_all, tail = n_scr[...], m_scr[...], tail_scr[...]
        qk_pre = qk_ref[rows, :]
        ext = jnp.concatenate([tail, qk_pre], axis=0)
        acc = qk_pre * cw[CONV_K - 1:CONV_K, :]
        for j in range(1, CONV_K):
            acc = acc + pltpu.roll(ext, j, axis=0)[SUBLANES:, :] * cw[CONV_K - 1 - j:CONV_K - j, :]
        if NC > 1:
            tail_scr[...] = qk_pre[L - SUBLANES:, :]
        qk = acc * _sigmoid(acc)
        g = g_ref[rows, :]
        lf_all = jnp.minimum(g, 0.0) - jnp.log(1.0 + jnp.exp(-jnp.abs(g)))
        F_all = _cumsum_rows(lf_all)
        heads = []
        for h in range(M_H):
            ks = slice(h * M_DK, (h + 1) * M_DK)
            q = qk[:, ks] * (M_DK ** -0.5)
            k = qk[:, M_W + h * M_DK:M_W + (h + 1) * M_DK]
            v = v_ref[rows, ks]
            ig = g[:, h:h + 1]
            F = F_all[:, M_H + h:M_H + h + 1]
            m_prev = m_all[:, h:h + 1]
            r_row = _col_to_row(ig - F, eye)
            Dm = jnp.where(causal, F + r_row, -jnp.inf)
            init_w = F + m_prev
            m_t = jnp.maximum(init_w, jnp.max(Dm, axis=1, keepdims=True))
            P = jnp.exp(Dm - m_t)
            a0 = jnp.exp(init_w - m_t)
            FL = F[L - 1:L, :]
            mL = m_t[L - 1:L, :]
            wL = jnp.exp(FL - F + ig - mL)
            decay = jnp.exp(FL + m_prev - mL)
            heads.append(dict(ks=ks, q=q, k=k, v=v, m_t=m_t, P=P, a0=a0, mL=mL, decay=decay, kw=wL * k))
        sq.update(rows=rows, heads=heads, n_all=n_all, m_all=m_all, qk_pre=qk_pre)
        seqs.append(sq)

    for sq in seqs:
        for h, d in enumerate(sq["heads"]):
            d["S"] = _bdot_nt(d["q"], d["k"])
            d["qC"] = _bdot(d["q"], sq["C_in"][h])
            d["kv"] = _bdot_tn(d["kw"], d["v"])

    for sq in seqs:
        for d in sq["heads"]:
            d["Sc"] = d["S"] * d["P"]
            d["num"] = _bdot(d["Sc"], d["v"])

    for nb, sq in enumerate(seqs):
        rows, n_new, m_new = sq["rows"], sq["n_all"], sq["m_all"]
        for h, d in enumerate(sq["heads"]):
            ks, q, a0, decay = d["ks"], d["q"], d["a0"], d["decay"]
            n_row = sq["n_all"][h:h + 1, :]
            num = d["num"] + a0 * d["qC"]
            den = jnp.sum(d["Sc"], axis=1, keepdims=True) + a0 * jnp.sum(q * n_row, axis=1, keepdims=True)
            hh = num * (1.0 / jnp.maximum(jnp.abs(den), jnp.exp(-d["m_t"])))
            sq["C_out"][h] = decay * sq["C_in"][h] + d["kv"]
            n_new = jnp.where(head_row == h, decay * n_row + jnp.sum(d["kw"], axis=0, keepdims=True), n_new)
            m_new = jnp.where(head_lane == h, d["mL"], m_new)

            hm = _sigmoid(o_ref[rows, ks]) * hh
            hm = hm - jnp.mean(hm, axis=1, keepdims=True)
            hm = hm * lax.rsqrt(jnp.mean(hm * hm, axis=1, keepdims=True) + LN_EPS) * gn[:, ks]
            h_ref[rows, ks] = hm
        sq["n_out"][...] = n_new
        sq["m_out"][...] = m_new
        conv_tail = sq["qk_pre"][L - (CONV_K - 1):, :]
        if NC == 1:
            convo_ref[nb] = conv_tail
        else:
            @pl.when(c == NC - 1)
            def _fin():
                Co_ref[0] = C_scr[...]
                no_ref[0] = n_new
                mo_ref[0] = m_new
                convo_ref[0] = conv_tail


def _mlstm(u_main, u_gate, conv0p, C0, n0, m0, conv_w, gn, *, B, T, L, row0, NB=1):
    NC = T // L
    assert NB == 1 or NC == 1
    assert B % NB == 0 and row0 % (NB * L) == 0
    R = NB * L
    rb0 = row0 // R
    tok = lambda b, c: rb0 + b * NC + c
    in_specs = [
        pl.BlockSpec((R, 2 * M_W), lambda b, c: (tok(b, c), COL_QK // (2 * M_W))),
        pl.BlockSpec((R, M_W), lambda b, c: (tok(b, c), COL_V // M_W)),
        pl.BlockSpec((R, M_W), lambda b, c: (tok(b, c), COL_O // M_W)),
        pl.BlockSpec((R, LANES), lambda b, c: (tok(b, c), 0)),
        pl.BlockSpec((NB, SUBLANES, 2 * M_W), lambda b, c: (b, 0, 0)),
        pl.BlockSpec((NB, M_H, M_DK, M_DV), lambda b, c: (b, 0, 0, 0)),
        pl.BlockSpec((NB, M_H, M_DK), lambda b, c: (b, 0, 0)),
        pl.BlockSpec((NB, 1, M_H), lambda b, c: (b, 0, 0)),
        pl.BlockSpec((CONV_K, 2 * M_W), lambda b, c: (0, 0)),
        pl.BlockSpec((1, M_W), lambda b, c: (0, 0)),
    ]
    args = [u_main, u_main, u_main, u_gate, conv0p, C0, n0, m0, conv_w, gn]
    out_specs = [
        pl.BlockSpec((R, M_W), lambda b, c: (b * NC + c, 0)),
        pl.BlockSpec((NB, M_H, M_DK, M_DV), lambda b, c: (b, 0, 0, 0)),
        pl.BlockSpec((NB, M_H, M_DK), lambda b, c: (b, 0, 0)),
        pl.BlockSpec((NB, 1, M_H), lambda b, c: (b, 0, 0)),
        pl.BlockSpec((NB, CONV_K - 1, 2 * M_W), lambda b, c: (b, 0, 0)),
    ]
    out_shape = [
        jax.ShapeDtypeStruct((B * T, M_W), F32),
        jax.ShapeDtypeStruct((B, M_H, M_DK, M_DV), F32),
        jax.ShapeDtypeStruct((B, M_H, M_DK), F32),
        jax.ShapeDtypeStruct((B, 1, M_H), F32),
        jax.ShapeDtypeStruct((B, CONV_K - 1, 2 * M_W), F32),
    ]
    return pl.pallas_call(
        functools.partial(_mlstm_kernel, L=L, NC=NC, NB=NB),
        grid=(B // NB, NC),
        in_specs=in_specs,
        out_specs=out_specs,
        out_shape=out_shape,
        scratch_shapes=[pltpu.VMEM((M_H, M_DK, M_DV), F32), pltpu.VMEM((M_H, M_DK), F32),
                        pltpu.VMEM((1, M_H), F32), pltpu.VMEM((SUBLANES, 2 * M_W), F32)],
        compiler_params=_cparams(("parallel", "arbitrary")),
        name=f"mlstm_L{L}",
    )(*args)


def _hgrn_kernel(*refs, L, NC, NB):
    (q_ref, f_ref, i_ref, g_ref, S0_ref, lb_ref, gn_ref, o_ref, So_ref, S_scr) = refs
    c = pl.program_id(1)
    if NC > 1:
        @pl.when(c == 0)
        def _init():
            S_scr[...] = S0_ref[0]

    lb = lb_ref[...]
    f = lb + (1.0 - lb) * _sigmoid(f_ref[...])
    kk = 1.0 - f
    b = _cumsum_rows(jnp.log(f), seg=L)
    qh = q_ref[...]
    q = qh * _sigmoid(qh)
    v = i_ref[...]
    gh = g_ref[...]
    gsilu = gh * _sigmoid(gh)
    gn = gn_ref[...]
    ti = lax.broadcasted_iota(jnp.int32, (L, L), 0)
    si = lax.broadcasted_iota(jnp.int32, (L, L), 1)
    causal = si <= ti
    ci = lax.broadcasted_iota(jnp.int32, (HG_DK, HG_DK), 0)
    cj = lax.broadcasted_iota(jnp.int32, (HG_DK, HG_DK), 1)
    eye = ci == cj
    mid = max(L // 2 - 1, 0)
    hsl = [slice(h * HG_DK, (h + 1) * HG_DK) for h in range(HG_H)]

    seqs = []
    for nb in range(NB):
        rows = slice(nb * L, (nb + 1) * L)
        bs, qs, ks = b[rows], q[rows], kk[rows]
        bL = bs[L - 1:L, :]
        bm = bs[mid:mid + 1, :]
        S_in, S_out = (S0_ref.at[nb], So_ref.at[nb]) if NC == 1 else (S_scr, S_scr)
        seqs.append(dict(rows=rows, bL=bL, v=v[rows], q_in=qs * jnp.exp(bs), q_t=qs * jnp.exp(bs - bm),
                         k_t=ks * jnp.exp(bm - bs), k_st=ks * jnp.exp(bL - bs), S_in=S_in, S_out=S_out))

    for sq in seqs:
        sq["A"] = [jnp.where(causal, _bdot_nt(sq["q_t"][:, hs], sq["k_t"][:, hs]), 0.0) for hs in hsl]
        sq["qS"] = [_bdot(sq["q_in"][:, hs], sq["S_in"][h]) for h, hs in enumerate(hsl)]
        sq["kv"] = [_bdot_tn(sq["k_st"][:, hs], sq["v"][:, hs]) for hs in hsl]
    for sq in seqs:
        sq["o_intra"] = [_bdot(sq["A"][h], sq["v"][:, hs]) for h, hs in enumerate(hsl)]
    for sq in seqs:
        rows = sq["rows"]
        for h, hs in enumerate(hsl):
            o = sq["o_intra"][h] + sq["qS"][h]
            dec = jnp.exp(_row_to_col(sq["bL"][:, hs], eye))
            sq["S_out"][h] = dec * sq["S_in"][h] + sq["kv"][h]
            o = o * lax.rsqrt(jnp.mean(o * o, axis=1, keepdims=True) + LN_EPS) * gn[:, hs]
            o_ref[rows, hs] = o * gsilu[rows, hs]

    if NC > 1:
        @pl.when(c == NC - 1)
        def _fin():
            So_ref[0] = S_scr[...]


def _hgrn(u_main, S0, lb, gn, *, B, T, L, row0, NB=1):
    NC = T // L
    assert NB == 1 or NC == 1
    assert B % NB == 0 and row0 % (NB * L) == 0
    R = NB * L
    rb0 = row0 // R
    tok = lambda b, c: rb0 + b * NC + c
    col = lambda off: (lambda b, c: (tok(b, c), off // HG_W))
    in_specs = [
        pl.BlockSpec((R, HG_W), col(COL_QH)),
        pl.BlockSpec((R, HG_W), col(COL_FH)),
        pl.BlockSpec((R, HG_W), col(COL_IH)),
        pl.BlockSpec((R, HG_W), col(COL_GH)),
        pl.BlockSpec((NB, HG_H, HG_DK, HG_DV), lambda b, c: (b, 0, 0, 0)),
        pl.BlockSpec((1, HG_W), lambda b, c: (0, 0)),
        pl.BlockSpec((1, HG_W), lambda b, c: (0, 0)),
    ]
    args = [u_main, u_main, u_main, u_main, S0, lb, gn]
    return pl.pallas_call(
        functools.partial(_hgrn_kernel, L=L, NC=NC, NB=NB),
        grid=(B // NB, NC),
        in_specs=in_specs,
        out_specs=[pl.BlockSpec((R, HG_W), lambda b, c: (b * NC + c, 0)),
                   pl.BlockSpec((NB, HG_H, HG_DK, HG_DV), lambda b, c: (b, 0, 0, 0))],
        out_shape=[jax.ShapeDtypeStruct((B * T, HG_W), F32),
                   jax.ShapeDtypeStruct((B, HG_H, HG_DK, HG_DV), F32)],
        scratch_shapes=[pltpu.VMEM((HG_H, HG_DK, HG_DV), F32)],
        compiler_params=_cparams(("parallel", "arbitrary")),
        name=f"hgrn_L{L}",
    )(*args)


def _merge_kernel(hmp_ref, hms_ref, ogp_ref, ogs_ref, gm_ref, gh_ref, xp_ref, xs_ref, wbm_ref, wbh_ref,
                  wo_ref, g_ref, b_ref, h1_ref, h1b_ref, *, n_prompt_tiles):
    is_prompt = pl.program_id(0) < n_prompt_tiles
    a = _bdot(jnp.where(is_prompt, hmp_ref[...], hms_ref[...]), wbm_ref[...])
    bb = _bdot(jnp.where(is_prompt, ogp_ref[...], ogs_ref[...]), wbh_ref[...])
    merged = _sigmoid(gm_ref[...]) * a + _sigmoid(gh_ref[...]) * bb
    mix = _bdot(merged, wo_ref[...])
    x = jnp.where(is_prompt, xp_ref[...], xs_ref[...])
    h1 = _layer_norm(ALPHA * x + mix, g_ref[...], b_ref[...])
    h1_ref[...] = h1
    h1b_ref[...] = h1.astype(BF16)


def _merge(hm_p, hm_s, og_p, og_s, u_main, x_p, x_s, w_bm, w_bh, w_out, ln_g, ln_b, *, tm):
    Np, Ns = x_p.shape[0], x_s.shape[0]
    NT = Np + Ns
    tm = _tile(math.gcd(Np, Ns), tm)
    npt = Np // tm
    const = lambda i: (0, 0)
    prompt_rows = lambda i: (jnp.minimum(i, npt - 1), 0)
    sample_rows = lambda i: (jnp.maximum(i - npt, 0), 0)
    return pl.pallas_call(
        functools.partial(_merge_kernel, n_prompt_tiles=npt),
        grid=(NT // tm,),
        in_specs=[
            pl.BlockSpec((tm, M_W), prompt_rows),
            pl.BlockSpec((tm, M_W), sample_rows),
            pl.BlockSpec((tm, HG_W), prompt_rows),
            pl.BlockSpec((tm, HG_W), sample_rows),
            pl.BlockSpec((tm, D_MODEL), lambda i: (i, COL_GM // D_MODEL)),
            pl.BlockSpec((tm, D_MODEL), lambda i: (i, COL_GHH // D_MODEL)),
            pl.BlockSpec((tm, D_MODEL), prompt_rows),
            pl.BlockSpec((tm, D_MODEL), sample_rows),
            pl.BlockSpec((M_W, D_MODEL), const, pipeline_mode=pl.Buffered(1)),
            pl.BlockSpec((HG_W, D_MODEL), const, pipeline_mode=pl.Buffered(1)),
            pl.BlockSpec((D_MODEL, D_MODEL), const, pipeline_mode=pl.Buffered(1)),
            pl.BlockSpec((1, D_MODEL), const),
            pl.BlockSpec((1, D_MODEL), const),
        ],
        out_specs=[pl.BlockSpec((tm, D_MODEL), lambda i: (i, 0)),
                   pl.BlockSpec((tm, D_MODEL), lambda i: (i, 0))],
        out_shape=[jax.ShapeDtypeStruct((NT, D_MODEL), F32),
                   jax.ShapeDtypeStruct((NT, D_MODEL), BF16)],
        compiler_params=_cparams(("parallel",)),
        name="merge_ln1",
    )(hm_p, hm_s, og_p, og_s, u_main, u_main, x_p, x_s, w_bm, w_bh, w_out, ln_g, ln_b)


def _attn_kernel(q_ref, *refs):
    k_refs, v_refs, o_ref = refs[:XA_H], refs[XA_H:2 * XA_H], refs[2 * XA_H]
    for h in range(XA_H):
        sl = slice(h * XA_D, (h + 1) * XA_D)
        s = _bdot_nt(q_ref[:, sl], k_refs[h][...]) * (XA_D ** -0.5)
        e = jnp.exp(s - jnp.max(s, axis=1, keepdims=True))
        p = e / jnp.sum(e, axis=1, keepdims=True)
        o_ref[:, sl] = _bdot(p, v_refs[h][...])


def _attn_cache_kernel(q_ref, k_ref, v_ref, o_ref):
    T = q_ref.shape[0]
    M = k_ref.shape[2]
    k_all = k_ref[0, 0].reshape(M * XA_H, XA_D)
    v_all = v_ref[0, 0].reshape(M * XA_H, XA_D)
    q_all = jnp.concatenate([q_ref[:, h * XA_D:(h + 1) * XA_D] for h in range(XA_H)], axis=0)
    s = _bdot_nt(q_all, k_all) * (XA_D ** -0.5)
    q_head = lax.broadcasted_iota(jnp.int32, s.shape, 0) // T
    k_head = lax.broadcasted_iota(jnp.int32, s.shape, 1) % XA_H
    s = jnp.where(q_head == k_head, s, -jnp.inf)
    e = jnp.exp(s - jnp.max(s, axis=1, keepdims=True))
    p = e / jnp.sum(e, axis=1, keepdims=True)
    o = _bdot(p, v_all)
    for h in range(XA_H):
        o_ref[:, h * XA_D:(h + 1) * XA_D] = o[h * T:(h + 1) * T]


def _attention_cache(q, cache_k, cache_v, *, B, T, row0):
    M = cache_k.shape[2]
    rb0 = row0 // T
    kv_spec = pl.BlockSpec((1, 1, M, XA_H, XA_D), lambda b: (0, b, 0, 0, 0))
    return pl.pallas_call(
        _attn_cache_kernel,
        grid=(B,),
        in_specs=[pl.BlockSpec((T, D_MODEL), lambda b: (rb0 + b, 0)), kv_spec, kv_spec],
        out_specs=pl.BlockSpec((T, D_MODEL), lambda b: (b, 0)),
        out_shape=jax.ShapeDtypeStruct((B * T, D_MODEL), F32),
        compiler_params=_cparams(("parallel",)),
        name="xattn_cache",
    )(q, cache_k, cache_v)


def _attention(q, mem_k, mem_v, *, B, T, tq, row0):
    tq = _tile(T, tq)
    nq = T // tq
    rb0 = row0 // tq
    M = mem_k.shape[0] // B
    kv_specs = [pl.BlockSpec((M, XA_D), lambda b, t, h=h: (b, h)) for h in range(XA_H)]
    return pl.pallas_call(
        _attn_kernel,
        grid=(B, nq),
        in_specs=[pl.BlockSpec((tq, D_MODEL), lambda b, t: (rb0 + b * nq + t, 0))] + kv_specs + kv_specs,
        out_specs=pl.BlockSpec((tq, D_MODEL), lambda b, t: (b * nq + t, 0)),
        out_shape=jax.ShapeDtypeStruct((B * T, D_MODEL), F32),
        compiler_params=_cparams(("parallel", "parallel")),
        name=f"xattn_T{T}",
    )(q, *([mem_k] * XA_H), *([mem_v] * XA_H))


def _oln_kernel(op_ref, os_ref, h1_ref, wo_ref, g_ref, b_ref, wr_ref, br_ref,
                h2_ref, route_ref, cnt_ref, carry_scr, *, n_prompt_tiles, n_sub):
    i = pl.program_id(0)

    @pl.when(i == 0)
    def _init():
        carry_scr[...] = jnp.zeros_like(carry_scr)

    ts = h1_ref.shape[0] // n_sub
    lane = lax.broadcasted_iota(jnp.int32, (ts, LANES), 1)
    lane_f = lane.astype(F32)
    neg = -jnp.inf
    ti = lax.broadcasted_iota(jnp.int32, (ts, ts), 0)
    si = lax.broadcasted_iota(jnp.int32, (ts, ts), 1)
    tri = jnp.where(si <= ti, 1.0, 0.0).astype(BF16)
    is_prompt = i < n_prompt_tiles

    def first_argmax(vals):
        mx = jnp.max(vals, axis=1, keepdims=True)
        idx = jnp.min(jnp.where(vals == mx, lane_f, float(LANES)), axis=1, keepdims=True)
        return mx, idx.astype(jnp.int32)

    carry = carry_scr[...]
    for sb in range(n_sub):
        rows = slice(sb * ts, (sb + 1) * ts)
        att = jnp.where(is_prompt, op_ref[rows, :], os_ref[rows, :])
        h2 = _layer_norm(ALPHA * h1_ref[rows, :] + _bdot(att, wo_ref[...]), g_ref[...], b_ref[...])
        h2_ref[rows, :] = h2

        logits = _bdot(h2, wr_ref[...]) + br_ref[...]
        gl = jnp.where(lane < N_GROUPS, logits, neg)
        gmax, grp = first_argmax(gl)
        p_grp = 1.0 / jnp.sum(jnp.exp(gl - gmax), axis=1, keepdims=True)
        lo = ROUTE_LANE0 + grp * EXP_PER_GROUP
        el = jnp.where((lane >= lo) & (lane < lo + EXP_PER_GROUP), logits, neg)
        v1, i1 = first_argmax(el)
        v2, i2 = first_argmax(jnp.where(lane == i1, neg, el))
        e21 = jnp.exp(v2 - v1)
        w1 = p_grp / (1.0 + e21)
        w2 = p_grp * e21 / (1.0 + e21)

        pick1 = lane == i1
        pick2 = lane == i2
        onehot = jnp.where(pick1 | pick2, 1.0, 0.0)
        cnt = _bdot(tri, onehot) + carry
        r1 = jnp.sum(jnp.where(pick1, cnt, 0.0), axis=1, keepdims=True) - 1.0
        r2 = jnp.sum(jnp.where(pick2, cnt, 0.0), axis=1, keepdims=True) - 1.0
        carry = cnt[ts - 1:ts, :]

        e1 = (i1 - ROUTE_LANE0).astype(F32)
        e2 = (i2 - ROUTE_LANE0).astype(F32)
        packed = jnp.zeros_like(logits)
        for idx, val in enumerate((e1, e2, w1, w2, r1, r2)):
            packed = jnp.where(lane == idx, val, packed)
        route_ref[rows, :] = packed

    carry_scr[...] = carry
    cnt_ref[...] = jnp.broadcast_to(carry, cnt_ref.shape)


def _oln(o_p, o_s, h1, wo, ln_g, ln_b, wr, br, *, tm):
    Np, Ns = o_p.shape[0], o_s.shape[0]
    NT = Np + Ns
    tm = _tile(math.gcd(Np, Ns), tm)
    npt = Np // tm
    const = lambda i: (0, 0)
    return pl.pallas_call(
        functools.partial(_oln_kernel, n_prompt_tiles=npt, n_sub=2 if tm % (2 * SUBLANES) == 0 else 1),
        grid=(NT // tm,),
        in_specs=[pl.BlockSpec((tm, D_MODEL), lambda i: (jnp.minimum(i, npt - 1), 0)),
                  pl.BlockSpec((tm, D_MODEL), lambda i: (jnp.maximum(i - npt, 0), 0)),
                  pl.BlockSpec((tm, D_MODEL), lambda i: (i, 0)),
                  pl.BlockSpec((D_MODEL, D_MODEL), const, pipeline_mode=pl.Buffered(1)),
                  pl.BlockSpec((1, D_MODEL), const),
                  pl.BlockSpec((1, D_MODEL), const),
                  pl.BlockSpec((D_MODEL, LANES), const),
                  pl.BlockSpec((1, LANES), const)],
        out_specs=[pl.BlockSpec((tm, D_MODEL), lambda i: (i, 0)),
                   pl.BlockSpec((tm, LANES), lambda i: (i, 0)),
                   pl.BlockSpec((SUBLANES, LANES), const)],
        out_shape=[jax.ShapeDtypeStruct((NT, D_MODEL), F32),
                   jax.ShapeDtypeStruct((NT, LANES), F32),
                   jax.ShapeDtypeStruct((SUBLANES, LANES), F32)],
        scratch_shapes=[pltpu.VMEM((1, LANES), F32)],
        compiler_params=_cparams(("arbitrary",)),
        name="oproj_ln2_router",
    )(o_p, o_s, h1, wo, ln_g, ln_b, wr, br)


def _row_copy(src_ref, src_row, dst_ref, dst_row, sem):
    return pltpu.make_async_copy(src_ref.at[pl.ds(src_row, 1), :], dst_ref.at[pl.ds(dst_row, 1), :], sem)


def _expert_weight_copies(e, slot, w_hbm, w_f32, sem):
    return [pltpu.make_async_copy(src.at[e], dst.at[slot], sem.at[slot, j])
            for j, (src, dst) in enumerate(zip(w_hbm, w_f32))]


def _expert_kernel(it_ref, ie_ref, lo_ref, hi_ref, first_ref, n_ref, ord_ref, nxt_ref, pos_ref,
                   h2_hbm, wg_hbm, wu_hbm, wd_hbm, y_ref,
                   xbuf, xsem, wg_f, wu_f, wd_f, wsem, wg_s, wu_s, wd_s, tok_ref):
    i = pl.program_id(0)
    n = n_ref[0]
    TR = xbuf.shape[1]
    w_hbm = (wg_hbm, wu_hbm, wd_hbm)
    w_f32 = (wg_f, wu_f, wd_f)

    def gather(base, slot, r):
        return _row_copy(h2_hbm, tok_ref[base + r], xbuf.at[slot], r, xsem.at[slot])

    def wait_rows(slot):
        def body(r, carry):
            gather(0, slot, r).wait()
            return carry
        lax.fori_loop(0, TR, body, 0, unroll=DMA_UNROLL)

    @pl.when(i == 0)
    def _prologue():
        for c in _expert_weight_copies(ie_ref[0], 0, w_hbm, w_f32, wsem):
            c.start()

        def invert(c, carry):
            p0 = pl.multiple_of(c * INVERT_GROUP, INVERT_GROUP)
            t0 = c * (INVERT_GROUP // TOP_K)
            rows = [pos_ref[p0 + u] for u in range(INVERT_GROUP)]
            for u in range(INVERT_GROUP):
                tok_ref[rows[u]] = t0 + u // TOP_K
            return carry
        lax.fori_loop(0, pos_ref.shape[0] // INVERT_GROUP, invert, 0)

        for ahead in range(N_XBUF - 1):
            base = it_ref[jnp.minimum(ahead, n - 1)] * TR

            def body(r, carry, base=base, ahead=ahead):
                gather(base, ahead, r).start()
                return carry
            lax.fori_loop(0, TR, body, 0, unroll=DMA_UNROLL)

    @pl.when(i < n)
    def _compute():
        slot = i % N_XBUF
        fill = (i + N_XBUF - 1) % N_XBUF
        wslot = ord_ref[i] % 2
        fresh = jnp.logical_or(i == 0, ie_ref[i] != ie_ref[jnp.maximum(i - 1, 0)])

        @pl.when(fresh)
        def _new_expert():
            for c in _expert_weight_copies(ie_ref[i], wslot, w_hbm, w_f32, wsem):
                c.wait()
            wg_s[...] = wg_f[wslot].astype(BF16)
            wu_s[...] = wu_f[wslot].astype(BF16)
            wd_s[...] = wd_f[wslot].astype(BF16)

            @pl.when(nxt_ref[i] >= 0)
            def _prefetch():
                for c in _expert_weight_copies(nxt_ref[i], 1 - wslot, w_hbm, w_f32, wsem):
                    c.start()

        wait_rows(slot)
        x = xbuf[slot].astype(BF16)
        base_next = it_ref[jnp.minimum(i + N_XBUF - 1, n - 1)] * TR
        for r in range(TR):
            gather(base_next, fill, r).start()
        gate = jnp.dot(x, wg_s[...], preferred_element_type=F32)
        up = jnp.dot(x, wu_s[...], preferred_element_type=F32)
        hid = gate * _sigmoid(gate) * up
        y = jnp.dot(hid.astype(BF16), wd_s[...], preferred_element_type=F32)
        rows = it_ref[i] * TR + lax.broadcasted_iota(jnp.int32, (TR, 1), 0)
        mine = (rows >= lo_ref[i]) & (rows < hi_ref[i])

        @pl.when(first_ref[i] == 1)
        def _first():
            y_ref[...] = jnp.where(mine, y, 0.0)

        @pl.when(first_ref[i] == 0)
        def _later():
            y_ref[...] = jnp.where(mine, y, y_ref[...])

        @pl.when(i == n - 1)
        def _drain():
            for ahead in range(1, N_XBUF):
                wait_rows((i + ahead) % N_XBUF)


def _experts(items, pos_flat, h2, e_wg, e_wu, e_wd):
    TR = MOE_ROW_TILE
    n_work = items[0].shape[0]
    n_rows = pos_flat.shape[0]
    assert n_rows % INVERT_GROUP == 0 and INVERT_GROUP % TOP_K == 0
    any_spec = pl.BlockSpec(memory_space=pl.ANY)
    return pl.pallas_call(
        _expert_kernel,
        grid_spec=pltpu.PrefetchScalarGridSpec(
            num_scalar_prefetch=9,
            grid=(n_work,),
            in_specs=[any_spec, any_spec, any_spec, any_spec],
            out_specs=pl.BlockSpec((TR, D_MODEL), lambda i, it, *_: (it[i], 0)),
            scratch_shapes=[pltpu.VMEM((N_XBUF, TR, D_MODEL), F32), pltpu.SemaphoreType.DMA((N_XBUF,)),
                            pltpu.VMEM((2, D_MODEL, EXP_FF), F32), pltpu.VMEM((2, D_MODEL, EXP_FF), F32),
                            pltpu.VMEM((2, EXP_FF, D_MODEL), F32), pltpu.SemaphoreType.DMA((2, 3)),
                            pltpu.VMEM((D_MODEL, EXP_FF), BF16), pltpu.VMEM((D_MODEL, EXP_FF), BF16),
                            pltpu.VMEM((EXP_FF, D_MODEL), BF16), pltpu.SMEM((n_rows,), jnp.int32)],
        ),
        out_shape=jax.ShapeDtypeStruct((n_rows, D_MODEL), F32),
        compiler_params=_cparams(("arbitrary",)),
        name="moe_experts",
    )(*items, pos_flat, h2, e_wg, e_wu, e_wd)


def _moe_schedule(route, counts, n_rows):
    TR = MOE_ROW_TILE
    assert n_rows % TR == 0
    n_work = n_rows // TR + N_EXPERTS - 1
    rt = route[:, :SUBLANES].T
    cnt = counts[0, ROUTE_LANE0:ROUTE_LANE0 + N_EXPERTS].astype(jnp.int32)
    g_end = jnp.cumsum(cnt)
    g_start = g_end - cnt
    onehot = rt[0:2, :, None] == jnp.arange(N_EXPERTS, dtype=F32)
    start_of = jnp.sum(jnp.where(onehot, g_start.astype(F32), 0.0), axis=-1)
    pos_flat = (start_of + rt[4:6]).astype(jnp.int32).T.reshape(-1)
    first_tile = g_start // TR
    n_items_e = jnp.where(cnt > 0, (g_end - 1) // TR - first_tile + 1, 0)
    item_end = jnp.cumsum(n_items_e)
    n_items = item_end[-1:]
    idx = jnp.minimum(jnp.arange(n_work, dtype=jnp.int32), n_items[0] - 1)
    ie = jnp.minimum(jnp.sum((item_end[None, :] <= idx[:, None]).astype(jnp.int32), axis=1), N_EXPERTS - 1)
    it = first_tile[ie] + idx - (item_end - n_items_e)[ie]
    first = jnp.concatenate([jnp.ones((1,), jnp.int32), (it[1:] != it[:-1]).astype(jnp.int32)])
    has = n_items_e > 0
    e_ids = jnp.arange(N_EXPERTS, dtype=jnp.int32)
    ordinal = jnp.cumsum(has.astype(jnp.int32)) - 1
    later = (e_ids[None, :] > e_ids[:, None]) & has[None, :]
    nxt = jnp.min(jnp.where(later, e_ids[None, :], N_EXPERTS), axis=1)
    nxt = jnp.where(nxt == N_EXPERTS, -1, nxt)
    items = (it.astype(jnp.int32), ie, g_start[ie], g_end[ie], first, n_items.astype(jnp.int32),
             ordinal[ie], nxt[ie])
    return pos_flat, items


def _combine_kernel(pos_ref, h2_ref, route_ref, ys_ref, g_ref, b_ref, yp_ref, ysm_ref,
                    buf, sem, *, tm, n_prompt_tiles):
    i = pl.program_id(0)
    last = pl.num_programs(0) - 1
    slot = i % 2

    def fetch(tile, slot, r, k):
        return _row_copy(ys_ref, pos_ref[TOP_K * (tile * tm + r) + k], buf.at[slot, k], r, sem.at[slot])

    def wait_rows(slot):
        def body(r, carry):
            for k in range(TOP_K):
                fetch(0, slot, r, k).wait()
            return carry
        lax.fori_loop(0, tm, body, 0, unroll=DMA_UNROLL)

    @pl.when(i == 0)
    def _prologue():
        def body(r, carry):
            for k in range(TOP_K):
                fetch(0, 0, r, k).start()
            return carry
        lax.fori_loop(0, tm, body, 0, unroll=DMA_UNROLL)

    wait_rows(slot)
    nxt = jnp.minimum(i + 1, last)
    for r in range(tm):
        for k in range(TOP_K):
            fetch(nxt, 1 - slot, r, k).start()
    route = route_ref[...]
    moe = route[:, 2:3] * buf[slot, 0] + route[:, 3:4] * buf[slot, 1]
    y = _layer_norm(ALPHA * h2_ref[...] + moe, g_ref[...], b_ref[...])

    @pl.when(i == last)
    def _drain():
        wait_rows(1 - slot)

    @pl.when(i < n_prompt_tiles)
    def _prompt():
        yp_ref[...] = y

    @pl.when(i >= n_prompt_tiles)
    def _sample():
        ysm_ref[...] = y


def _combine(pos_flat, h2, route, ys, ln_g, ln_b, *, Np, Ns, tm):
    NT = h2.shape[0]
    tm = _tile(math.gcd(Np, Ns), tm)
    npt = Np // tm
    const = lambda i, pos: (0, 0)
    return pl.pallas_call(
        functools.partial(_combine_kernel, tm=tm, n_prompt_tiles=npt),
        grid_spec=pltpu.PrefetchScalarGridSpec(
            num_scalar_prefetch=1,
            grid=(NT // tm,),
            in_specs=[pl.BlockSpec((tm, D_MODEL), lambda i, pos: (i, 0)),
                      pl.BlockSpec((tm, LANES), lambda i, pos: (i, 0)),
                      pl.BlockSpec(memory_space=pl.ANY),
                      pl.BlockSpec((1, D_MODEL), const),
                      pl.BlockSpec((1, D_MODEL), const)],
            out_specs=[pl.BlockSpec((tm, D_MODEL), lambda i, pos: (jnp.minimum(i, npt - 1), 0)),
                       pl.BlockSpec((tm, D_MODEL), lambda i, pos: (jnp.maximum(i - npt, 0), 0))],
            scratch_shapes=[pltpu.VMEM((2, TOP_K, tm, D_MODEL), F32), pltpu.SemaphoreType.DMA((2,))],
        ),
        out_shape=[jax.ShapeDtypeStruct((Np, D_MODEL), F32),
                   jax.ShapeDtypeStruct((Ns, D_MODEL), F32)],
        compiler_params=_cparams(("arbitrary",)),
        name="moe_combine_ln3",
    )(pos_flat, h2, route, ys, ln_g, ln_b)


def kernel(x_prompt, x_sample, mem_prompt, cache_mem_k, cache_mem_v, state_mlstm_C, state_mlstm_n,
           state_mlstm_m, state_mlstm_conv, state_hgrn_S, w_in, b_in, conv_w, mlstm_gn, lb_logits, hgrn_gn,
           w_bm, w_bh, w_out, ln1_g, ln1_b, xa_wq, xa_wk, xa_wv, xa_wo, ln2_g, ln2_b,
           r1_w, r1_b, r2_w, r2_b, e_wg, e_wu, e_wd, ln3_g, ln3_b):
    Bp, Tp, _ = x_prompt.shape
    Bs, Ts, _ = x_sample.shape
    MEM = mem_prompt.shape[1]
    Np, Ns = Bp * Tp, Bs * Ts
    NT = Np + Ns
    Lp_m = math.gcd(Tp, 256)
    Lp_h = math.gcd(Tp, 64)
    Ls = Ts
    assert Ts % SUBLANES == 0 and Np % Ls == 0

    xp2 = x_prompt.reshape(Np, D_MODEL)
    xs2 = x_sample.reshape(Ns, D_MODEL)

    w = w_in[0]
    bi = b_in[0]
    w_t = jnp.swapaxes(w, 0, 1)
    xb_p = xp2.astype(BF16)
    xb_s = xs2.astype(BF16)
    b_main = jnp.concatenate([bi[:GATE_LO], bi[GATE_LO + 2 * M_H:]])[None]
    w_gate_t = lax.slice(w_t, (GATE_LO, 0), (GATE_LO + 2 * M_H, D_MODEL))
    w_gate = jnp.pad(w_gate_t, ((0, LANES - 2 * M_H), (0, 0)))
    b_gate = jnp.pad(bi[GATE_LO:GATE_LO + 2 * M_H], (0, LANES - 2 * M_H))[None]
    lb = jnp.cumsum(jax.nn.softmax(lb_logits.astype(F32), axis=0), axis=0)[0][None]
    wr = jnp.zeros((D_MODEL, LANES), F32)
    wr = wr.at[:, :N_GROUPS].set(r1_w[0])
    wr = wr.at[:, ROUTE_LANE0:ROUTE_LANE0 + N_EXPERTS].set(
        jnp.transpose(r2_w[0], (1, 0, 2)).reshape(D_MODEL, N_EXPERTS)).astype(BF16)
    br = jnp.zeros((1, LANES), F32)
    br = br.at[0, :N_GROUPS].set(r1_b[0])
    br = br.at[0, ROUTE_LANE0:ROUTE_LANE0 + N_EXPERTS].set(r2_b[0].reshape(N_EXPERTS))
    zeros_d = jnp.zeros((1, D_MODEL), F32)

    memb = mem_prompt.reshape(Bp * MEM, D_MODEL).astype(BF16)
    mk = _matmul_bias(memb, xa_wk[0], zeros_d, tm=1024, tn=1024, name="mem_k")
    mv = _matmul_bias(memb, xa_wv[0], zeros_d, tm=1024, tn=1024, name="mem_v")

    u_main, u_gate = _in_proj(xb_p, xb_s, w_t, b_main, w_gate, b_gate, tm=1024, tn=1024)

    padc = lambda c: jnp.pad(c, ((0, 0), (SUBLANES - (CONV_K - 1), 0), (0, 0)))
    zC = jnp.zeros((Bp, M_H, M_DK, M_DV), F32)
    zn = jnp.zeros((Bp, M_H, M_DK), F32)
    zm = jnp.zeros((Bp, 1, M_H), F32)
    zconv = jnp.zeros((Bp, SUBLANES, 2 * M_W), F32)
    zS = jnp.zeros((Bp, HG_H, HG_DK, HG_DV), F32)
    cw = conv_w[0]
    mgn = mlstm_gn[0][None]
    hgn = hgrn_gn[0][None]
    hm_p, C_p, n_p, m_p, conv_p = _mlstm(u_main, u_gate, zconv, zC, zn, zm, cw, mgn,
                                         B=Bp, T=Tp, L=Lp_m, row0=0)
    hm_s, C_s, n_s, m_s, conv_s = _mlstm(u_main, u_gate, padc(state_mlstm_conv[0]), state_mlstm_C[0],
                                         state_mlstm_n[0], state_mlstm_m[0][:, None, :], cw, mgn,
                                         B=Bs, T=Ts, L=Ls, row0=Np, NB=SAMPLE_SEQS_PER_STEP)
    og_p, S_p = _hgrn(u_main, zS, lb, hgn, B=Bp, T=Tp, L=Lp_h, row0=0)
    og_s, S_s = _hgrn(u_main, state_hgrn_S[0], lb, hgn, B=Bs, T=Ts, L=Ls, row0=Np, NB=SAMPLE_SEQS_PER_STEP)

    h1, h1b = _merge(hm_p, hm_s, og_p, og_s, u_main, xp2, xs2, w_bm[0].astype(BF16), w_bh[0].astype(BF16),
                     w_out[0].astype(BF16), ln1_g, ln1_b, tm=256)

    q = _matmul_bias(h1b, xa_wq[0], zeros_d, tm=1024, tn=1024, name="xa_q")
    att_p = _attention(q, mk, mv, B=Bp, T=Tp, tq=1024, row0=0)
    att_s = _attention_cache(q, cache_mem_k, cache_mem_v, B=Bs, T=Ts, row0=Np)
    h2, route, counts = _oln(att_p, att_s, h1, xa_wo[0].astype(BF16), ln2_g, ln2_b, wr, br, tm=512)

    pos_flat, items = _moe_schedule(route, counts, TOP_K * NT)
    ys_sorted = _experts(items, pos_flat, h2, e_wg[0], e_wu[0], e_wd[0])
    y_p, y_s = _combine(pos_flat, h2, route, ys_sorted, ln3_g, ln3_b, Np=Np, Ns=Ns, tm=256)

    kv5 = lambda a: a.reshape(1, Bp, MEM, XA_H, XA_D)
    return (y_p.reshape(Bp, Tp, D_MODEL), y_s.reshape(Bs, Ts, D_MODEL), kv5(mk), kv5(mv),
            C_p[None], n_p[None], m_p.reshape(1, Bp, M_H), conv_p[None], S_p[None],
            C_s[None], n_s[None], m_s.reshape(1, Bs, M_H), conv_s[None], S_s[None])
```

```python
import functools
import math

import jax
import jax.numpy as jnp
from jax import lax
from jax.experimental import pallas as pl
from jax.experimental.pallas import tpu as pltpu

F32 = jnp.float32
BF16 = jnp.bfloat16

D_MODEL = 2048
M_W = 1024
M_H = 4
M_DK = 256
M_DV = 256
CONV_K = 4
HG_W = 1024
HG_H = 8
HG_DK = 128
HG_DV = 128
XA_H = 4
XA_D = 512
N_GROUPS = 4
EXP_PER_GROUP = 8
N_EXPERTS = 32
TOP_K = 2
EXP_FF = 512
DEPTH = 1
ALPHA = (2 * DEPTH) ** 0.25
LN_EPS = 1e-5

COL_QK, COL_V, COL_O = 0, 2048, 3072
COL_QH, COL_FH, COL_IH, COL_GH = 4096, 5120, 6144, 7168
COL_GM, COL_GHH = 8192, 10240
N_MAIN = 12288
GATE_LO = 4 * M_W
LANES = 128
SUBLANES = 8
ROUTE_LANE0 = N_GROUPS

VMEM_LIMIT = 56 << 20
MOE_ROW_TILE = 256
SAMPLE_SEQS_PER_STEP = 8
N_XBUF = 3
DMA_UNROLL = 8
INVERT_GROUP = 16


def _cparams(sem, vmem=VMEM_LIMIT):
    return pltpu.CompilerParams(dimension_semantics=sem, vmem_limit_bytes=vmem)


def _tile(n, pref):
    t = math.gcd(n, pref)
    assert t % SUBLANES == 0, (n, pref)
    return t


def _bdot(a, b):
    return jnp.dot(a.astype(BF16), b.astype(BF16), preferred_element_type=F32)


def _bdot_nt(a, b):
    return lax.dot_general(a.astype(BF16), b.astype(BF16), (((1,), (1,)), ((), ())),
                           preferred_element_type=F32)


def _bdot_tn(a, b):
    return lax.dot_general(a.astype(BF16), b.astype(BF16), (((0,), (0,)), ((), ())),
                           preferred_element_type=F32)


def _sigmoid(x):
    return 1.0 / (1.0 + jnp.exp(-x))


def _cumsum_rows(x, seg=None):
    n = seg or x.shape[0]
    assert n & (n - 1) == 0
    row = lax.broadcasted_iota(jnp.int32, x.shape, 0) & (n - 1)
    d = 1
    while d < n:
        x = x + jnp.where(row >= d, pltpu.roll(x, d, axis=0), 0.0)
        d *= 2
    return x


def _col_to_row(col, eye):
    return jnp.sum(jnp.where(eye, col, 0.0), axis=0, keepdims=True)


def _row_to_col(row, eye):
    return jnp.sum(jnp.where(eye, row, 0.0), axis=1, keepdims=True)


def _layer_norm(x, g, b):
    mu = jnp.mean(x, axis=-1, keepdims=True)
    xc = x - mu
    var = jnp.mean(xc * xc, axis=-1, keepdims=True)
    return xc * lax.rsqrt(var + LN_EPS) * g + b


def _mm_kernel(x_ref, w_ref, b_ref, o_ref, wb_scr):
    @pl.when(pl.program_id(1) == 0)
    def _new_weight_tile():
        wb_scr[...] = w_ref[...].astype(BF16)

    acc = jnp.dot(x_ref[...], wb_scr[...], preferred_element_type=F32)
    o_ref[...] = (acc + b_ref[...]).astype(o_ref.dtype)


def _matmul_bias(x, w, b, *, tm, tn, out_dtype=F32, name):
    M, K = x.shape
    N = w.shape[1]
    tm = _tile(M, tm)
    tn = _tile(N, tn)
    return pl.pallas_call(
        _mm_kernel,
        grid=(N // tn, M // tm),
        in_specs=[pl.BlockSpec((tm, K), lambda j, i: (i, 0)),
                  pl.BlockSpec((K, tn), lambda j, i: (0, j)),
                  pl.BlockSpec((1, tn), lambda j, i: (0, j))],
        out_specs=pl.BlockSpec((tm, tn), lambda j, i: (i, j)),
        out_shape=jax.ShapeDtypeStruct((M, N), out_dtype),
        scratch_shapes=[pltpu.VMEM((K, tn), BF16)],
        compiler_params=_cparams(("parallel", "arbitrary")),
        name=name,
    )(x, w, b)


def _inproj_kernel(xp_ref, xs_ref, wt_ref, b_ref, wg_ref, bg_ref, u_ref, ug_ref, wb_scr, *, n_prompt_tiles):
    i = pl.program_id(1)

    @pl.when(i == 0)
    def _new_weight_tile():
        wb_scr[...] = wt_ref[...].astype(BF16)

    xb = jnp.where(i < n_prompt_tiles, xp_ref[...], xs_ref[...])
    u_ref[...] = _bdot_nt(xb, wb_scr[...]) + b_ref[...]

    @pl.when(pl.program_id(0) == 0)
    def _gates():
        ug_ref[...] = _bdot_nt(xb, wg_ref[...]) + bg_ref[...]


def _in_proj(xb_p, xb_s, w_t, b_main, w_gate, b_gate, *, tm, tn):
    Np, Ns = xb_p.shape[0], xb_s.shape[0]
    NT = Np + Ns
    K = w_t.shape[1]
    N = w_t.shape[0] - 2 * M_H
    tm = _tile(math.gcd(Np, Ns), tm)
    tn = _tile(math.gcd(GATE_LO, N), tn)
    assert (2 * M_H) % SUBLANES == 0
    npt = Np // tm
    na = GATE_LO // tn
    w_rows = lambda j, i: (pl.multiple_of(j * tn + jnp.where(j < na, 0, 2 * M_H), SUBLANES), 0)
    n_i = NT // tm
    gate_rows = lambda j, i: (jnp.where(j == 0, i, n_i - 1), 0)
    return pl.pallas_call(
        functools.partial(_inproj_kernel, n_prompt_tiles=npt),
        grid=(N // tn, n_i),
        in_specs=[pl.BlockSpec((tm, K), lambda j, i: (jnp.minimum(i, npt - 1), 0)),
                  pl.BlockSpec((tm, K), lambda j, i: (jnp.maximum(i - npt, 0), 0)),
                  pl.BlockSpec((pl.Element(tn), pl.Element(K)), w_rows),
                  pl.BlockSpec((1, tn), lambda j, i: (0, j)),
                  pl.BlockSpec((LANES, K), lambda j, i: (0, 0)),
                  pl.BlockSpec((1, LANES), lambda j, i: (0, 0))],
        out_specs=[pl.BlockSpec((tm, tn), lambda j, i: (i, j)),
                   pl.BlockSpec((tm, LANES), gate_rows)],
        out_shape=[jax.ShapeDtypeStruct((NT, N), F32), jax.ShapeDtypeStruct((NT, LANES), F32)],
        scratch_shapes=[pltpu.VMEM((tn, K), BF16)],
        compiler_params=_cparams(("arbitrary", "arbitrary")),
        name="in_proj",
    )(xb_p, xb_s, w_t, b_main, w_gate, b_gate)


def _mlstm_kernel(*refs, L, NC, NB):
    (qk_ref, v_ref, o_ref, g_ref, conv0_ref, C0_ref, n0_ref, m0_ref, cw_ref, gn_ref,
     h_ref, Co_ref, no_ref, mo_ref, convo_ref, C_scr, n_scr, m_scr, tail_scr) = refs
    c = pl.program_id(1)
    if NC > 1:
        @pl.when(c == 0)
        def _init():
            C_scr[...] = C0_ref[0]
            n_scr[...] = n0_ref[0]
            m_scr[...] = m0_ref[0]
            tail_scr[...] = conv0_ref[0]

    cw = cw_ref[...]
    gn = gn_ref[...]
    ti = lax.broadcasted_iota(jnp.int32, (L, L), 0)
    si = lax.broadcasted_iota(jnp.int32, (L, L), 1)
    eye = ti == si
    causal = si <= ti
    head_lane = lax.broadcasted_iota(jnp.int32, (1, M_H), 1)
    head_row = lax.broadcasted_iota(jnp.int32, (M_H, M_DK), 0)

    seqs = []
    for nb in range(NB):
        rows = slice(nb * L, (nb + 1) * L)
        if NC == 1:
            sq = dict(C_in=C0_ref.at[nb], C_out=Co_ref.at[nb], n_out=no_ref.at[nb], m_out=mo_ref.at[nb])
            n_all, m_all, tail = n0_ref[nb], m0_ref[nb], conv0_ref[nb]
        else:
            sq = dict(C_in=C_scr, C_out=C_scr, n_out=n_scr, m_out=m_scr)
            n_all, m_all, tail = n_scr[...], m_scr[...], tail_scr[...]
        qk_pre = qk_ref[rows, :]
        ext = jnp.concatenate([tail, qk_pre], axis=0)
        acc = qk_pre * cw[CONV_K - 1:CONV_K, :]
        for j in range(1, CONV_K):
            acc = acc + pltpu.roll(ext, j, axis=0)[SUBLANES:, :] * cw[CONV_K - 1 - j:CONV_K - j, :]
        if NC > 1:
            tail_scr[...] = qk_pre[L - SUBLANES:, :]
        qk = acc * _sigmoid(acc)
        g = g_ref[rows, :]
        lf_all = jnp.minimum(g, 0.0) - jnp.log(1.0 + jnp.exp(-jnp.abs(g)))
        F_all = _cumsum_rows(lf_all)
        heads = []
        for h in range(M_H):
            ks = slice(h * M_DK, (h + 1) * M_DK)
            q = qk[:, ks] * (M_DK ** -0.5)
            k = qk[:, M_W + h * M_DK:M_W + (h + 1) * M_DK]
            v = v_ref[rows, ks]
            ig = g[:, h:h + 1]
            F = F_all[:, M_H + h:M_H + h + 1]
            m_prev = m_all[:, h:h + 1]
            r_row = _col_to_row(ig - F, eye)
            Dm = jnp.where(causal, F + r_row, -jnp.inf)
            init_w = F + m_prev
            m_t = jnp.maximum(init_w, jnp.max(Dm, axis=1, keepdims=True))
            P = jnp.exp(Dm - m_t)
            a0 = jnp.exp(init_w - m_t)
            FL = F[L - 1:L, :]
            mL = m_t[L - 1:L, :]
            wL = jnp.exp(FL - F + ig - mL)
            decay = jnp.exp(FL + m_prev - mL)
            heads.append(dict(ks=ks, q=q, k=k, v=v, m_t=m_t, P=P, a0=a0, mL=mL, decay=decay, kw=wL * k))
        sq.update(rows=rows, heads=heads, n_all=n_all, m_all=m_all, qk_pre=qk_pre)
        seqs.append(sq)

    for sq in seqs:
        for h, d in enumerate(sq["heads"]):
            d["S"] = _bdot_nt(d["q"], d["k"])
            d["qC"] = _bdot(d["q"], sq["C_in"][h])
            d["kv"] = _bdot_tn(d["kw"], d["v"])

    for sq in seqs:
        for d in sq["heads"]:
            d["Sc"] = d["S"] * d["P"]
            d["num"] = _bdot(d["Sc"], d["v"])

    for nb, sq in enumerate(seqs):
        rows, n_new, m_new = sq["rows"], sq["n_all"], sq["m_all"]
        for h, d in enumerate(sq["heads"]):
            ks, q, a0, decay = d["ks"], d["q"], d["a0"], d["decay"]
            n_row = sq["n_all"][h:h + 1, :]
            num = d["num"] + a0 * d["qC"]
            den = jnp.sum(d["Sc"], axis=1, keepdims=True) + a0 * jnp.sum(q * n_row, axis=1, keepdims=True)
            hh = num * (1.0 / jnp.maximum(jnp.abs(den), jnp.exp(-d["m_t"])))
            sq["C_out"][h] = decay * sq["C_in"][h] + d["kv"]
            n_new = jnp.where(head_row == h, decay * n_row + jnp.sum(d["kw"], axis=0, keepdims=True), n_new)
            m_new = jnp.where(head_lane == h, d["mL"], m_new)

            hm = _sigmoid(o_ref[rows, ks]) * hh
            hm = hm - jnp.mean(hm, axis=1, keepdims=True)
            hm = hm * lax.rsqrt(jnp.mean(hm * hm, axis=1, keepdims=True) + LN_EPS) * gn[:, ks]
            h_ref[rows, ks] = hm
        sq["n_out"][...] = n_new
        sq["m_out"][...] = m_new
        conv_tail = sq["qk_pre"][L - (CONV_K - 1):, :]
        if NC == 1:
            convo_ref[nb] = conv_tail
        else:
            @pl.when(c == NC - 1)
            def _fin():
                Co_ref[0] = C_scr[...]
                no_ref[0] = n_new
                mo_ref[0] = m_new
                convo_ref[0] = conv_tail


def _mlstm(u_main, u_gate, conv0p, C0, n0, m0, conv_w, gn, *, B, T, L, row0, NB=1):
    NC = T // L
    assert NB == 1 or NC == 1
    assert B % NB == 0 and row0 % (NB * L) == 0
    R = NB * L
    rb0 = row0 // R
    tok = lambda b, c: rb0 + b * NC + c
    in_specs = [
        pl.BlockSpec((R, 2 * M_W), lambda b, c: (tok(b, c), COL_QK // (2 * M_W))),
        pl.BlockSpec((R, M_W), lambda b, c: (tok(b, c), COL_V // M_W)),
        pl.BlockSpec((R, M_W), lambda b, c: (tok(b, c), COL_O // M_W)),
        pl.BlockSpec((R, LANES), lambda b, c: (tok(b, c), 0)),
        pl.BlockSpec((NB, SUBLANES, 2 * M_W), lambda b, c: (b, 0, 0)),
        pl.BlockSpec((NB, M_H, M_DK, M_DV), lambda b, c: (b, 0, 0, 0)),
        pl.BlockSpec((NB, M_H, M_DK), lambda b, c: (b, 0, 0)),
        pl.BlockSpec((NB, 1, M_H), lambda b, c: (b, 0, 0)),
        pl.BlockSpec((CONV_K, 2 * M_W), lambda b, c: (0, 0)),
        pl.BlockSpec((1, M_W), lambda b, c: (0, 0)),
    ]
    args = [u_main, u_main, u_main, u_gate, conv0p, C0, n0, m0, conv_w, gn]
    out_specs = [
        pl.BlockSpec((R, M_W), lambda b, c: (b * NC + c, 0)),
        pl.BlockSpec((NB, M_H, M_DK, M_DV), lambda b, c: (b, 0, 0, 0)),
        pl.BlockSpec((NB, M_H, M_DK), lambda b, c: (b, 0, 0)),
        pl.BlockSpec((NB, 1, M_H), lambda b, c: (b, 0, 0)),
        pl.BlockSpec((NB, CONV_K - 1, 2 * M_W), lambda b, c: (b, 0, 0)),
    ]
    out_shape = [
        jax.ShapeDtypeStruct((B * T, M_W), F32),
        jax.ShapeDtypeStruct((B, M_H, M_DK, M_DV), F32),
        jax.ShapeDtypeStruct((B, M_H, M_DK), F32),
        jax.ShapeDtypeStruct((B, 1, M_H), F32),
        jax.ShapeDtypeStruct((B, CONV_K - 1, 2 * M_W), F32),
    ]
    return pl.pallas_call(
        functools.partial(_mlstm_kernel, L=L, NC=NC, NB=NB),
        grid=(B // NB, NC),
        in_specs=in_specs,
        out_specs=out_specs,
        out_shape=out_shape,
        scratch_shapes=[pltpu.VMEM((M_H, M_DK, M_DV), F32), pltpu.VMEM((M_H, M_DK), F32),
                        pltpu.VMEM((1, M_H), F32), pltpu.VMEM((SUBLANES, 2 * M_W), F32)],
        compiler_params=_cparams(("parallel", "arbitrary")),
        name=f"mlstm_L{L}",
    )(*args)


def _hgrn_kernel(*refs, L, NC, NB):
    (q_ref, f_ref, i_ref, g_ref, S0_ref, lb_ref, gn_ref, o_ref, So_ref, S_scr) = refs
    c = pl.program_id(1)
    if NC > 1:
        @pl.when(c == 0)
        def _init():
            S_scr[...] = S0_ref[0]

    lb = lb_ref[...]
    f = lb + (1.0 - lb) * _sigmoid(f_ref[...])
    kk = 1.0 - f
    b = _cumsum_rows(jnp.log(f), seg=L)
    qh = q_ref[...]
    q = qh * _sigmoid(qh)
    v = i_ref[...]
    gh = g_ref[...]
    gsilu = gh * _sigmoid(gh)
    gn = gn_ref[...]
    ti = lax.broadcasted_iota(jnp.int32, (L, L), 0)
    si = lax.broadcasted_iota(jnp.int32, (L, L), 1)
    causal = si <= ti
    ci = lax.broadcasted_iota(jnp.int32, (HG_DK, HG_DK), 0)
    cj = lax.broadcasted_iota(jnp.int32, (HG_DK, HG_DK), 1)
    eye = ci == cj
    mid = max(L // 2 - 1, 0)
    hsl = [slice(h * HG_DK, (h + 1) * HG_DK) for h in range(HG_H)]

    seqs = []
    for nb in range(NB):
        rows = slice(nb * L, (nb + 1) * L)
        bs, qs, ks = b[rows], q[rows], kk[rows]
        bL = bs[L - 1:L, :]
        bm = bs[mid:mid + 1, :]
        S_in, S_out = (S0_ref.at[nb], So_ref.at[nb]) if NC == 1 else (S_scr, S_scr)
        seqs.append(dict(rows=rows, bL=bL, v=v[rows], q_in=qs * jnp.exp(bs), q_t=qs * jnp.exp(bs - bm),
                         k_t=ks * jnp.exp(bm - bs), k_st=ks * jnp.exp(bL - bs), S_in=S_in, S_out=S_out))

    for sq in seqs:
        sq["A"] = [jnp.where(causal, _bdot_nt(sq["q_t"][:, hs], sq["k_t"][:, hs]), 0.0) for hs in hsl]
        sq["qS"] = [_bdot(sq["q_in"][:, hs], sq["S_in"][h]) for h, hs in enumerate(hsl)]
        sq["kv"] = [_bdot_tn(sq["k_st"][:, hs], sq["v"][:, hs]) for hs in hsl]
    for sq in seqs:
        sq["o_intra"] = [_bdot(sq["A"][h], sq["v"][:, hs]) for h, hs in enumerate(hsl)]
    for sq in seqs:
        rows = sq["rows"]
        for h, hs in enumerate(hsl):
            o = sq["o_intra"][h] + sq["qS"][h]
            dec = jnp.exp(_row_to_col(sq["bL"][:, hs], eye))
            sq["S_out"][h] = dec * sq["S_in"][h] + sq["kv"][h]
            o = o * lax.rsqrt(jnp.mean(o * o, axis=1, keepdims=True) + LN_EPS) * gn[:, hs]
            o_ref[rows, hs] = o * gsilu[rows, hs]

    if NC > 1:
        @pl.when(c == NC - 1)
        def _fin():
            So_ref[0] = S_scr[...]


def _hgrn(u_main, S0, lb, gn, *, B, T, L, row0, NB=1):
    NC = T // L
    assert NB == 1 or NC == 1
    assert B % NB == 0 and row0 % (NB * L) == 0
    R = NB * L
    rb0 = row0 // R
    tok = lambda b, c: rb0 + b * NC + c
    col = lambda off: (lambda b, c: (tok(b, c), off // HG_W))
    in_specs = [
        pl.BlockSpec((R, HG_W), col(COL_QH)),
        pl.BlockSpec((R, HG_W), col(COL_FH)),
        pl.BlockSpec((R, HG_W), col(COL_IH)),
        pl.BlockSpec((R, HG_W), col(COL_GH)),
        pl.BlockSpec((NB, HG_H, HG_DK, HG_DV), lambda b, c: (b, 0, 0, 0)),
        pl.BlockSpec((1, HG_W), lambda b, c: (0, 0)),
        pl.BlockSpec((1, HG_W), lambda b, c: (0, 0)),
    ]
    args = [u_main, u_main, u_main, u_main, S0, lb, gn]
    return pl.pallas_call(
        functools.partial(_hgrn_kernel, L=L, NC=NC, NB=NB),
        grid=(B // NB, NC),
        in_specs=in_specs,
        out_specs=[pl.BlockSpec((R, HG_W), lambda b, c: (b * NC + c, 0)),
                   pl.BlockSpec((NB, HG_H, HG_DK, HG_DV), lambda b, c: (b, 0, 0, 0))],
        out_shape=[jax.ShapeDtypeStruct((B * T, HG_W), F32),
                   jax.ShapeDtypeStruct((B, HG_H, HG_DK, HG_DV), F32)],
        scratch_shapes=[pltpu.VMEM((HG_H, HG_DK, HG_DV), F32)],
        compiler_params=_cparams(("parallel", "arbitrary")),
        name=f"hgrn_L{L}",
    )(*args)


def _merge_kernel(hmp_ref, hms_ref, ogp_ref, ogs_ref, gm_ref, gh_ref, xp_ref, xs_ref, wbm_ref, wbh_ref,
                  wo_ref, g_ref, b_ref, h1_ref, h1b_ref, *, n_prompt_tiles):
    is_prompt = pl.program_id(0) < n_prompt_tiles
    a = _bdot(jnp.where(is_prompt, hmp_ref[...], hms_ref[...]), wbm_ref[...])
    bb = _bdot(jnp.where(is_prompt, ogp_ref[...], ogs_ref[...]), wbh_ref[...])
    merged = _sigmoid(gm_ref[...]) * a + _sigmoid(gh_ref[...]) * bb
    mix = _bdot(merged, wo_ref[...])
    x = jnp.where(is_prompt, xp_ref[...], xs_ref[...])
    h1 = _layer_norm(ALPHA * x + mix, g_ref[...], b_ref[...])
    h1_ref[...] = h1
    h1b_ref[...] = h1.astype(BF16)


def _merge(hm_p, hm_s, og_p, og_s, u_main, x_p, x_s, w_bm, w_bh, w_out, ln_g, ln_b, *, tm):
    Np, Ns = x_p.shape[0], x_s.shape[0]
    NT = Np + Ns
    tm = _tile(math.gcd(Np, Ns), tm)
    npt = Np // tm
    const = lambda i: (0, 0)
    prompt_rows = lambda i: (jnp.minimum(i, npt - 1), 0)
    sample_rows = lambda i: (jnp.maximum(i - npt, 0), 0)
    return pl.pallas_call(
        functools.partial(_merge_kernel, n_prompt_tiles=npt),
        grid=(NT // tm,),
        in_specs=[
            pl.BlockSpec((tm, M_W), prompt_rows),
            pl.BlockSpec((tm, M_W), sample_rows),
            pl.BlockSpec((tm, HG_W), prompt_rows),
            pl.BlockSpec((tm, HG_W), sample_rows),
            pl.BlockSpec((tm, D_MODEL), lambda i: (i, COL_GM // D_MODEL)),
            pl.BlockSpec((tm, D_MODEL), lambda i: (i, COL_GHH // D_MODEL)),
            pl.BlockSpec((tm, D_MODEL), prompt_rows),
            pl.BlockSpec((tm, D_MODEL), sample_rows),
            pl.BlockSpec((M_W, D_MODEL), const, pipeline_mode=pl.Buffered(1)),
            pl.BlockSpec((HG_W, D_MODEL), const, pipeline_mode=pl.Buffered(1)),
            pl.BlockSpec((D_MODEL, D_MODEL), const, pipeline_mode=pl.Buffered(1)),
            pl.BlockSpec((1, D_MODEL), const),
            pl.BlockSpec((1, D_MODEL), const),
        ],
        out_specs=[pl.BlockSpec((tm, D_MODEL), lambda i: (i, 0)),
                   pl.BlockSpec((tm, D_MODEL), lambda i: (i, 0))],
        out_shape=[jax.ShapeDtypeStruct((NT, D_MODEL), F32),
                   jax.ShapeDtypeStruct((NT, D_MODEL), BF16)],
        compiler_params=_cparams(("parallel",)),
        name="merge_ln1",
    )(hm_p, hm_s, og_p, og_s, u_main, u_main, x_p, x_s, w_bm, w_bh, w_out, ln_g, ln_b)


def _attn_kernel(q_ref, *refs):
    k_refs, v_refs, o_ref = refs[:XA_H], refs[XA_H:2 * XA_H], refs[2 * XA_H]
    for h in range(XA_H):
        sl = slice(h * XA_D, (h + 1) * XA_D)
        s = _bdot_nt(q_ref[:, sl], k_refs[h][...]) * (XA_D ** -0.5)
        e = jnp.exp(s - jnp.max(s, axis=1, keepdims=True))
        p = e / jnp.sum(e, axis=1, keepdims=True)
        o_ref[:, sl] = _bdot(p, v_refs[h][...])


def _attn_cache_kernel(q_ref, k_ref, v_ref, o_ref, *, T):
    M = k_ref.shape[2]
    for nb in range(k_ref.shape[1]):
        rows = slice(nb * T, (nb + 1) * T)
        k_all = k_ref[0, nb].reshape(M * XA_H, XA_D)
        v_all = v_ref[0, nb].reshape(M * XA_H, XA_D)
        q_all = jnp.concatenate([q_ref[rows, h * XA_D:(h + 1) * XA_D] for h in range(XA_H)], axis=0)
        s = _bdot_nt(q_all, k_all) * (XA_D ** -0.5)
        q_head = lax.broadcasted_iota(jnp.int32, s.shape, 0) // T
        k_head = lax.broadcasted_iota(jnp.int32, s.shape, 1) % XA_H
        s = jnp.where(q_head == k_head, s, -jnp.inf)
        e = jnp.exp(s - jnp.max(s, axis=1, keepdims=True))
        p = e / jnp.sum(e, axis=1, keepdims=True)
        o = _bdot(p, v_all)
        for h in range(XA_H):
            o_ref[rows, h * XA_D:(h + 1) * XA_D] = o[h * T:(h + 1) * T]


def _attention_cache(q, cache_k, cache_v, *, B, T, row0, NB):
    M = cache_k.shape[2]
    assert B % NB == 0 and row0 % (NB * T) == 0
    rb0 = row0 // (NB * T)
    kv_spec = pl.BlockSpec((1, NB, M, XA_H, XA_D), lambda b: (0, b, 0, 0, 0))
    return pl.pallas_call(
        functools.partial(_attn_cache_kernel, T=T),
        grid=(B // NB,),
        in_specs=[pl.BlockSpec((NB * T, D_MODEL), lambda b: (rb0 + b, 0)), kv_spec, kv_spec],
        out_specs=pl.BlockSpec((NB * T, D_MODEL), lambda b: (b, 0)),
        out_shape=jax.ShapeDtypeStruct((B * T, D_MODEL), F32),
        compiler_params=_cparams(("parallel",)),
        name="xattn_cache",
    )(q, cache_k, cache_v)


def _attention(q, mem_k, mem_v, *, B, T, tq, row0):
    tq = _tile(T, tq)
    nq = T // tq
    rb0 = row0 // tq
    M = mem_k.shape[0] // B
    kv_specs = [pl.BlockSpec((M, XA_D), lambda b, t, h=h: (b, h)) for h in range(XA_H)]
    return pl.pallas_call(
        _attn_kernel,
        grid=(B, nq),
        in_specs=[pl.BlockSpec((tq, D_MODEL), lambda b, t: (rb0 + b * nq + t, 0))] + kv_specs + kv_specs,
        out_specs=pl.BlockSpec((tq, D_MODEL), lambda b, t: (b * nq + t, 0)),
        out_shape=jax.ShapeDtypeStruct((B * T, D_MODEL), F32),
        compiler_params=_cparams(("parallel", "parallel")),
        name=f"xattn_T{T}",
    )(q, *([mem_k] * XA_H), *([mem_v] * XA_H))


def _oln_kernel(op_ref, os_ref, h1_ref, wo_ref, g_ref, b_ref, wr_ref, br_ref,
                h2_ref, route_ref, cnt_ref, carry_scr, *, n_prompt_tiles, n_sub):
    i = pl.program_id(0)

    @pl.when(i == 0)
    def _init():
        carry_scr[...] = jnp.zeros_like(carry_scr)

    ts = h1_ref.shape[0] // n_sub
    lane = lax.broadcasted_iota(jnp.int32, (ts, LANES), 1)
    lane_f = lane.astype(F32)
    neg = -jnp.inf
    ti = lax.broadcasted_iota(jnp.int32, (ts, ts), 0)
    si = lax.broadcasted_iota(jnp.int32, (ts, ts), 1)
    tri = jnp.where(si <= ti, 1.0, 0.0).astype(BF16)
    is_prompt = i < n_prompt_tiles

    def first_argmax(vals):
        mx = jnp.max(vals, axis=1, keepdims=True)
        idx = jnp.min(jnp.where(vals == mx, lane_f, float(LANES)), axis=1, keepdims=True)
        return mx, idx.astype(jnp.int32)

    carry = carry_scr[...]
    for sb in range(n_sub):
        rows = slice(sb * ts, (sb + 1) * ts)
        att = jnp.where(is_prompt, op_ref[rows, :], os_ref[rows, :])
        h2 = _layer_norm(ALPHA * h1_ref[rows, :] + _bdot(att, wo_ref[...]), g_ref[...], b_ref[...])
        h2_ref[rows, :] = h2

        logits = _bdot(h2, wr_ref[...]) + br_ref[...]
        gl = jnp.where(lane < N_GROUPS, logits, neg)
        gmax, grp = first_argmax(gl)
        p_grp = 1.0 / jnp.sum(jnp.exp(gl - gmax), axis=1, keepdims=True)
        lo = ROUTE_LANE0 + grp * EXP_PER_GROUP
        el = jnp.where((lane >= lo) & (lane < lo + EXP_PER_GROUP), logits, neg)
        v1, i1 = first_argmax(el)
        v2, i2 = first_argmax(jnp.where(lane == i1, neg, el))
        e21 = jnp.exp(v2 - v1)
        w1 = p_grp / (1.0 + e21)
        w2 = p_grp * e21 / (1.0 + e21)

        pick1 = lane == i1
        pick2 = lane == i2
        onehot = jnp.where(pick1 | pick2, 1.0, 0.0)
        cnt = _bdot(tri, onehot) + carry
        r1 = jnp.sum(jnp.where(pick1, cnt, 0.0), axis=1, keepdims=True) - 1.0
        r2 = jnp.sum(jnp.where(pick2, cnt, 0.0), axis=1, keepdims=True) - 1.0
        carry = cnt[ts - 1:ts, :]

        e1 = (i1 - ROUTE_LANE0).astype(F32)
        e2 = (i2 - ROUTE_LANE0).astype(F32)
        packed = jnp.zeros_like(logits)
        for idx, val in enumerate((e1, e2, w1, w2, r1, r2)):
            packed = jnp.where(lane == idx, val, packed)
        route_ref[rows, :] = packed

    carry_scr[...] = carry
    cnt_ref[...] = jnp.broadcast_to(carry, cnt_ref.shape)


def _oln(o_p, o_s, h1, wo, ln_g, ln_b, wr, br, *, tm):
    Np, Ns = o_p.shape[0], o_s.shape[0]
    NT = Np + Ns
    tm = _tile(math.gcd(Np, Ns), tm)
    npt = Np // tm
    const = lambda i: (0, 0)
    return pl.pallas_call(
        functools.partial(_oln_kernel, n_prompt_tiles=npt, n_sub=2 if tm % (2 * SUBLANES) == 0 else 1),
        grid=(NT // tm,),
        in_specs=[pl.BlockSpec((tm, D_MODEL), lambda i: (jnp.minimum(i, npt - 1), 0)),
                  pl.BlockSpec((tm, D_MODEL), lambda i: (jnp.maximum(i - npt, 0), 0)),
                  pl.BlockSpec((tm, D_MODEL), lambda i: (i, 0)),
                  pl.BlockSpec((D_MODEL, D_MODEL), const, pipeline_mode=pl.Buffered(1)),
                  pl.BlockSpec((1, D_MODEL), const),
                  pl.BlockSpec((1, D_MODEL), const),
                  pl.BlockSpec((D_MODEL, LANES), const),
                  pl.BlockSpec((1, LANES), const)],
        out_specs=[pl.BlockSpec((tm, D_MODEL), lambda i: (i, 0)),
                   pl.BlockSpec((tm, LANES), lambda i: (i, 0)),
                   pl.BlockSpec((SUBLANES, LANES), const)],
        out_shape=[jax.ShapeDtypeStruct((NT, D_MODEL), F32),
                   jax.ShapeDtypeStruct((NT, LANES), F32),
                   jax.ShapeDtypeStruct((SUBLANES, LANES), F32)],
        scratch_shapes=[pltpu.VMEM((1, LANES), F32)],
        compiler_params=_cparams(("arbitrary",)),
        name="oproj_ln2_router",
    )(o_p, o_s, h1, wo, ln_g, ln_b, wr, br)


def _row_copy(src_ref, src_row, dst_ref, dst_row, sem):
    return pltpu.make_async_copy(src_ref.at[pl.ds(src_row, 1), :], dst_ref.at[pl.ds(dst_row, 1), :], sem)


def _expert_weight_copies(e, slot, w_hbm, w_f32, sem):
    return [pltpu.make_async_copy(src.at[e], dst.at[slot], sem.at[slot, j])
            for j, (src, dst) in enumerate(zip(w_hbm, w_f32))]


def _expert_kernel(it_ref, ie_ref, lo_ref, hi_ref, first_ref, n_ref, ord_ref, nxt_ref, pos_ref,
                   h2_hbm, wg_hbm, wu_hbm, wd_hbm, y_ref,
                   xbuf, xsem, wg_f, wu_f, wd_f, wsem, wg_s, wu_s, wd_s, tok_ref):
    i = pl.program_id(0)
    n = n_ref[0]
    TR = xbuf.shape[1]
    w_hbm = (wg_hbm, wu_hbm, wd_hbm)
    w_f32 = (wg_f, wu_f, wd_f)

    def gather(base, slot, r):
        return _row_copy(h2_hbm, tok_ref[base + r], xbuf.at[slot], r, xsem.at[slot])

    def wait_rows(slot):
        def body(r, carry):
            gather(0, slot, r).wait()
            return carry
        lax.fori_loop(0, TR, body, 0, unroll=DMA_UNROLL)

    @pl.when(i == 0)
    def _prologue():
        for c in _expert_weight_copies(ie_ref[0], 0, w_hbm, w_f32, wsem):
            c.start()

        def invert(c, carry):
            p0 = pl.multiple_of(c * INVERT_GROUP, INVERT_GROUP)
            t0 = c * (INVERT_GROUP // TOP_K)
            rows = [pos_ref[p0 + u] for u in range(INVERT_GROUP)]
            for u in range(INVERT_GROUP):
                tok_ref[rows[u]] = t0 + u // TOP_K
            return carry
        lax.fori_loop(0, pos_ref.shape[0] // INVERT_GROUP, invert, 0)

        for ahead in range(N_XBUF - 1):
            base = it_ref[jnp.minimum(ahead, n - 1)] * TR

            def body(r, carry, base=base, ahead=ahead):
                gather(base, ahead, r).start()
                return carry
            lax.fori_loop(0, TR, body, 0, unroll=DMA_UNROLL)

    @pl.when(i < n)
    def _compute():
        slot = i % N_XBUF
        fill = (i + N_XBUF - 1) % N_XBUF
        wslot = ord_ref[i] % 2
        fresh = jnp.logical_or(i == 0, ie_ref[i] != ie_ref[jnp.maximum(i - 1, 0)])

        @pl.when(fresh)
        def _new_expert():
            for c in _expert_weight_copies(ie_ref[i], wslot, w_hbm, w_f32, wsem):
                c.wait()
            wg_s[...] = wg_f[wslot].astype(BF16)
            wu_s[...] = wu_f[wslot].astype(BF16)
            wd_s[...] = wd_f[wslot].astype(BF16)

            @pl.when(nxt_ref[i] >= 0)
            def _prefetch():
                for c in _expert_weight_copies(nxt_ref[i], 1 - wslot, w_hbm, w_f32, wsem):
                    c.start()

        wait_rows(slot)
        x = xbuf[slot].astype(BF16)
        base_next = it_ref[jnp.minimum(i + N_XBUF - 1, n - 1)] * TR
        for r in range(TR):
            gather(base_next, fill, r).start()
        gate = jnp.dot(x, wg_s[...], preferred_element_type=F32)
        up = jnp.dot(x, wu_s[...], preferred_element_type=F32)
        hid = gate * _sigmoid(gate) * up
        y = jnp.dot(hid.astype(BF16), wd_s[...], preferred_element_type=F32)
        rows = it_ref[i] * TR + lax.broadcasted_iota(jnp.int32, (TR, 1), 0)
        mine = (rows >= lo_ref[i]) & (rows < hi_ref[i])

        @pl.when(first_ref[i] == 1)
        def _first():
            y_ref[...] = jnp.where(mine, y, 0.0)

        @pl.when(first_ref[i] == 0)
        def _later():
            y_ref[...] = jnp.where(mine, y, y_ref[...])

        @pl.when(i == n - 1)
        def _drain():
            for ahead in range(1, N_XBUF):
                wait_rows((i + ahead) % N_XBUF)


def _experts(items, pos_flat, h2, e_wg, e_wu, e_wd):
    TR = MOE_ROW_TILE
    n_work = items[0].shape[0]
    n_rows = pos_flat.shape[0]
    assert n_rows % INVERT_GROUP == 0 and INVERT_GROUP % TOP_K == 0
    any_spec = pl.BlockSpec(memory_space=pl.ANY)
    return pl.pallas_call(
        _expert_kernel,
        grid_spec=pltpu.PrefetchScalarGridSpec(
            num_scalar_prefetch=9,
            grid=(n_work,),
            in_specs=[any_spec, any_spec, any_spec, any_spec],
            out_specs=pl.BlockSpec((TR, D_MODEL), lambda i, it, *_: (it[i], 0)),
            scratch_shapes=[pltpu.VMEM((N_XBUF, TR, D_MODEL), F32), pltpu.SemaphoreType.DMA((N_XBUF,)),
                            pltpu.VMEM((2, D_MODEL, EXP_FF), F32), pltpu.VMEM((2, D_MODEL, EXP_FF), F32),
                            pltpu.VMEM((2, EXP_FF, D_MODEL), F32), pltpu.SemaphoreType.DMA((2, 3)),
                            pltpu.VMEM((D_MODEL, EXP_FF), BF16), pltpu.VMEM((D_MODEL, EXP_FF), BF16),
                            pltpu.VMEM((EXP_FF, D_MODEL), BF16), pltpu.SMEM((n_rows,), jnp.int32)],
        ),
        out_shape=jax.ShapeDtypeStruct((n_rows, D_MODEL), F32),
        compiler_params=_cparams(("arbitrary",)),
        name="moe_experts",
    )(*items, pos_flat, h2, e_wg, e_wu, e_wd)


def _moe_schedule(route, counts, n_rows):
    TR = MOE_ROW_TILE
    assert n_rows % TR == 0
    n_work = n_rows // TR + N_EXPERTS - 1
    rt = route[:, :SUBLANES].T
    cnt = counts[0, ROUTE_LANE0:ROUTE_LANE0 + N_EXPERTS].astype(jnp.int32)
    g_end = jnp.cumsum(cnt)
    g_start = g_end - cnt
    onehot = rt[0:2, :, None] == jnp.arange(N_EXPERTS, dtype=F32)
    start_of = jnp.sum(jnp.where(onehot, g_start.astype(F32), 0.0), axis=-1)
    pos_flat = (start_of + rt[4:6]).astype(jnp.int32).T.reshape(-1)
    first_tile = g_start // TR
    n_items_e = jnp.where(cnt > 0, (g_end - 1) // TR - first_tile + 1, 0)
    item_end = jnp.cumsum(n_items_e)
    n_items = item_end[-1:]
    idx = jnp.minimum(jnp.arange(n_work, dtype=jnp.int32), n_items[0] - 1)
    ie = jnp.minimum(jnp.sum((item_end[None, :] <= idx[:, None]).astype(jnp.int32), axis=1), N_EXPERTS - 1)
    it = first_tile[ie] + idx - (item_end - n_items_e)[ie]
    first = jnp.concatenate([jnp.ones((1,), jnp.int32), (it[1:] != it[:-1]).astype(jnp.int32)])
    has = n_items_e > 0
    e_ids = jnp.arange(N_EXPERTS, dtype=jnp.int32)
    ordinal = jnp.cumsum(has.astype(jnp.int32)) - 1
    later = (e_ids[None, :] > e_ids[:, None]) & has[None, :]
    nxt = jnp.min(jnp.where(later, e_ids[None, :], N_EXPERTS), axis=1)
    nxt = jnp.where(nxt == N_EXPERTS, -1, nxt)
    items = (it.astype(jnp.int32), ie, g_start[ie], g_end[ie], first, n_items.astype(jnp.int32),
             ordinal[ie], nxt[ie])
    return pos_flat, items


def _combine_kernel(pos_ref, h2_ref, route_ref, ys_ref, g_ref, b_ref, yp_ref, ysm_ref,
                    buf, sem, *, tm, n_prompt_tiles):
    i = pl.program_id(0)
    last = pl.num_programs(0) - 1
    slot = i % 2

    def fetch(tile, slot, r, k):
        return _row_copy(ys_ref, pos_ref[TOP_K * (tile * tm + r) + k], buf.at[slot, k], r, sem.at[slot])

    def wait_rows(slot):
        def body(r, carry):
            for k in range(TOP_K):
                fetch(0, slot, r, k).wait()
            return carry
        lax.fori_loop(0, tm, body, 0, unroll=DMA_UNROLL)

    @pl.when(i == 0)
    def _prologue():
        def body(r, carry):
            for k in range(TOP_K):
                fetch(0, 0, r, k).start()
            return carry
        lax.fori_loop(0, tm, body, 0, unroll=DMA_UNROLL)

    wait_rows(slot)
    nxt = jnp.minimum(i + 1, last)
    for r in range(tm):
        for k in range(TOP_K):
            fetch(nxt, 1 - slot, r, k).start()
    route = route_ref[...]
    moe = route[:, 2:3] * buf[slot, 0] + route[:, 3:4] * buf[slot, 1]
    y = _layer_norm(ALPHA * h2_ref[...] + moe, g_ref[...], b_ref[...])

    @pl.when(i == last)
    def _drain():
        wait_rows(1 - slot)

    @pl.when(i < n_prompt_tiles)
    def _prompt():
        yp_ref[...] = y

    @pl.when(i >= n_prompt_tiles)
    def _sample():
        ysm_ref[...] = y


def _combine(pos_flat, h2, route, ys, ln_g, ln_b, *, Np, Ns, tm):
    NT = h2.shape[0]
    tm = _tile(math.gcd(Np, Ns), tm)
    npt = Np // tm
    const = lambda i, pos: (0, 0)
    return pl.pallas_call(
        functools.partial(_combine_kernel, tm=tm, n_prompt_tiles=npt),
        grid_spec=pltpu.PrefetchScalarGridSpec(
            num_scalar_prefetch=1,
            grid=(NT // tm,),
            in_specs=[pl.BlockSpec((tm, D_MODEL), lambda i, pos: (i, 0)),
                      pl.BlockSpec((tm, LANES), lambda i, pos: (i, 0)),
                      pl.BlockSpec(memory_space=pl.ANY),
                      pl.BlockSpec((1, D_MODEL), const),
                      pl.BlockSpec((1, D_MODEL), const)],
            out_specs=[pl.BlockSpec((tm, D_MODEL), lambda i, pos: (jnp.minimum(i, npt - 1), 0)),
                       pl.BlockSpec((tm, D_MODEL), lambda i, pos: (jnp.maximum(i - npt, 0), 0))],
            scratch_shapes=[pltpu.VMEM((2, TOP_K, tm, D_MODEL), F32), pltpu.SemaphoreType.DMA((2,))],
        ),
        out_shape=[jax.ShapeDtypeStruct((Np, D_MODEL), F32),
                   jax.ShapeDtypeStruct((Ns, D_MODEL), F32)],
        compiler_params=_cparams(("arbitrary",)),
        name="moe_combine_ln3",
    )(pos_flat, h2, route, ys, ln_g, ln_b)


def kernel(x_prompt, x_sample, mem_prompt, cache_mem_k, cache_mem_v, state_mlstm_C, state_mlstm_n,
           state_mlstm_m, state_mlstm_conv, state_hgrn_S, w_in, b_in, conv_w, mlstm_gn, lb_logits, hgrn_gn,
           w_bm, w_bh, w_out, ln1_g, ln1_b, xa_wq, xa_wk, xa_wv, xa_wo, ln2_g, ln2_b,
           r1_w, r1_b, r2_w, r2_b, e_wg, e_wu, e_wd, ln3_g, ln3_b):
    Bp, Tp, _ = x_prompt.shape
    Bs, Ts, _ = x_sample.shape
    MEM = mem_prompt.shape[1]
    Np, Ns = Bp * Tp, Bs * Ts
    NT = Np + Ns
    Lp_m = math.gcd(Tp, 256)
    Lp_h = math.gcd(Tp, 64)
    Ls = Ts
    assert Ts % SUBLANES == 0 and Np % Ls == 0

    xp2 = x_prompt.reshape(Np, D_MODEL)
    xs2 = x_sample.reshape(Ns, D_MODEL)

    w = w_in[0]
    bi = b_in[0]
    w_t = jnp.swapaxes(w, 0, 1)
    xb_p = xp2.astype(BF16)
    xb_s = xs2.astype(BF16)
    b_main = jnp.concatenate([bi[:GATE_LO], bi[GATE_LO + 2 * M_H:]])[None]
    w_gate_t = lax.slice(w_t, (GATE_LO, 0), (GATE_LO + 2 * M_H, D_MODEL))
    w_gate = jnp.pad(w_gate_t, ((0, LANES - 2 * M_H), (0, 0)))
    b_gate = jnp.pad(bi[GATE_LO:GATE_LO + 2 * M_H], (0, LANES - 2 * M_H))[None]
    lb = jnp.cumsum(jax.nn.softmax(lb_logits.astype(F32), axis=0), axis=0)[0][None]
    wr = jnp.zeros((D_MODEL, LANES), F32)
    wr = wr.at[:, :N_GROUPS].set(r1_w[0])
    wr = wr.at[:, ROUTE_LANE0:ROUTE_LANE0 + N_EXPERTS].set(
        jnp.transpose(r2_w[0], (1, 0, 2)).reshape(D_MODEL, N_EXPERTS)).astype(BF16)
    br = jnp.zeros((1, LANES), F32)
    br = br.at[0, :N_GROUPS].set(r1_b[0])
    br = br.at[0, ROUTE_LANE0:ROUTE_LANE0 + N_EXPERTS].set(r2_b[0].reshape(N_EXPERTS))
    zeros_d = jnp.zeros((1, D_MODEL), F32)

    memb = mem_prompt.reshape(Bp * MEM, D_MODEL).astype(BF16)
    mk = _matmul_bias(memb, xa_wk[0], zeros_d, tm=1024, tn=1024, name="mem_k")
    mv = _matmul_bias(memb, xa_wv[0], zeros_d, tm=1024, tn=1024, name="mem_v")

    u_main, u_gate = _in_proj(xb_p, xb_s, w_t, b_main, w_gate, b_gate, tm=1024, tn=1024)

    padc = lambda c: jnp.pad(c, ((0, 0), (SUBLANES - (CONV_K - 1), 0), (0, 0)))
    zC = jnp.zeros((Bp, M_H, M_DK, M_DV), F32)
    zn = jnp.zeros((Bp, M_H, M_DK), F32)
    zm = jnp.zeros((Bp, 1, M_H), F32)
    zconv = jnp.zeros((Bp, SUBLANES, 2 * M_W), F32)
    zS = jnp.zeros((Bp, HG_H, HG_DK, HG_DV), F32)
    cw = conv_w[0]
    mgn = mlstm_gn[0][None]
    hgn = hgrn_gn[0][None]
    hm_p, C_p, n_p, m_p, conv_p = _mlstm(u_main, u_gate, zconv, zC, zn, zm, cw, mgn,
                                         B=Bp, T=Tp, L=Lp_m, row0=0)
    hm_s, C_s, n_s, m_s, conv_s = _mlstm(u_main, u_gate, padc(state_mlstm_conv[0]), state_mlstm_C[0],
                                         state_mlstm_n[0], state_mlstm_m[0][:, None, :], cw, mgn,
                                         B=Bs, T=Ts, L=Ls, row0=Np, NB=SAMPLE_SEQS_PER_STEP)
    og_p, S_p = _hgrn(u_main, zS, lb, hgn, B=Bp, T=Tp, L=Lp_h, row0=0)
    og_s, S_s = _hgrn(u_main, state_hgrn_S[0], lb, hgn, B=Bs, T=Ts, L=Ls, row0=Np, NB=SAMPLE_SEQS_PER_STEP)

    h1, h1b = _merge(hm_p, hm_s, og_p, og_s, u_main, xp2, xs2, w_bm[0].astype(BF16), w_bh[0].astype(BF16),
                     w_out[0].astype(BF16), ln1_g, ln1_b, tm=256)

    q = _matmul_bias(h1b, xa_wq[0], zeros_d, tm=1024, tn=1024, name="xa_q")
    att_p = _attention(q, mk, mv, B=Bp, T=Tp, tq=1024, row0=0)
    att_s = _attention_cache(q, cache_mem_k, cache_mem_v, B=Bs, T=Ts, row0=Np, NB=2)
    h2, route, counts = _oln(att_p, att_s, h1, xa_wo[0].astype(BF16), ln2_g, ln2_b, wr, br, tm=512)

    pos_flat, items = _moe_schedule(route, counts, TOP_K * NT)
    ys_sorted = _experts(items, pos_flat, h2, e_wg[0], e_wu[0], e_wd[0])
    y_p, y_s = _combine(pos_flat, h2, route, ys_sorted, ln3_g, ln3_b, Np=Np, Ns=Ns, tm=256)

    kv5 = lambda a: a.reshape(1, Bp, MEM, XA_H, XA_D)
    return (y_p.reshape(Bp, Tp, D_MODEL), y_s.reshape(Bs, Ts, D_MODEL), kv5(mk), kv5(mv),
            C_p[None], n_p[None], m_p.reshape(1, Bp, M_H), conv_p[None], S_p[None],
            C_s[None], n_s[None], m_s.reshape(1, Bs, M_H), conv_s[None], S_s[None])
```
